```python
import math
import jax
import jax.numpy as jnp
from jax import lax
import numpy as np

D_MODEL = 1024
BATCH = 8
SEQ = 2048
DEPTH = 2
DEC_BATCH = 16
DEC_SEQ = 16
PAST_LEN = 2048

CHUNK = 64
QBLK = 128
CONV_W = 4
FFN_CONV_W = 3
D_FF = 2816
ROPE_THETA = 10000.0
EPS = 1e-6

GDN_HEADS = 4
GDN_DK = 64
GDN_DV = 64
ATT_HEADS = 4
ATT_DH = 64
IDX_HEADS = 4
IDX_DH = 64
TOPK_MAX = 256
SSM_HEADS = 8
SSM_P = 64
SSM_GROUPS = 2
SSM_N = 128

GDN_W = GDN_HEADS * GDN_DV
ATT_W = ATT_HEADS * ATT_DH
SSM_W = SSM_HEADS * SSM_P
MIX_W = GDN_W + ATT_W + SSM_W
GDN_CONV_C = 2 * GDN_HEADS * GDN_DK + GDN_W
SSM_CONV_C = SSM_W + 2 * SSM_GROUPS * SSM_N
IN_SIZES = (GDN_CONV_C, GDN_HEADS, GDN_HEADS, GDN_W,
            ATT_W, ATT_W, ATT_W, IDX_HEADS * IDX_DH, IDX_DH, IDX_HEADS,
            SSM_W, SSM_CONV_C, SSM_HEADS)
IN_W = sum(IN_SIZES)

kernel_name = 'hybrid_stream_encoder_step'


def rms_norm(x, g):
    xf = x.astype(jnp.float32)
    y = xf * lax.rsqrt(jnp.mean(xf * xf, axis=-1, keepdims=True) + EPS)
    return (y * g.astype(jnp.float32)).astype(x.dtype)


def l2_normalize(x):
    return x * lax.rsqrt(jnp.sum(x * x, axis=-1, keepdims=True) + EPS)


def rotary(x, pos):
    half = x.shape[-1] // 2
    inv_freq = ROPE_THETA ** (-jnp.arange(half, dtype=jnp.float32) / half)
    ang = pos.astype(jnp.float32)[:, None] * inv_freq[None, :]
    cos = jnp.cos(ang)[:, None, :]
    sin = jnp.sin(ang)[:, None, :]
    xf = x.astype(jnp.float32)
    x1, x2 = xf[..., :half], xf[..., half:]
    return jnp.concatenate([x1 * cos - x2 * sin, x2 * cos + x1 * sin], axis=-1).astype(x.dtype)


def causal_dwconv(u, w, prev):
    K = w.shape[0]
    T = u.shape[1]
    ext = jnp.concatenate([prev.astype(u.dtype), u], axis=1)
    y = sum(ext[:, j:j + T] * w[j] for j in range(K))
    return y, ext[:, T:]


def to_chunks(z, c):
    B, T = z.shape[:2]
    return z.reshape(B, T // c, c, *z.shape[2:]).swapaxes(0, 1)


def from_chunks(z):
    n, B, c = z.shape[:3]
    return z.swapaxes(0, 1).reshape(B, n * c, *z.shape[3:])


def _gdn_chunk(S, inp):
    q, k, v, g, beta = inp
    q, k, v = [z.transpose(0, 2, 1, 3) for z in (q, k, v)]
    g = jnp.cumsum(g.transpose(0, 2, 1), axis=-1)
    beta = beta.transpose(0, 2, 1)
    c = q.shape[2]
    incl = jnp.tril(jnp.ones((c, c), dtype=bool))
    strict = jnp.tril(jnp.ones((c, c), dtype=bool), -1)
    decay = jnp.exp(jnp.where(incl, g[..., :, None] - g[..., None, :], -jnp.inf))
    kb = k * beta[..., None]
    a_mat = jnp.where(strict, jnp.einsum('bhid,bhjd->bhij', kb, k) * decay, 0.0)
    dv = v.shape[-1]
    rhs = jnp.concatenate([v * beta[..., None], kb * jnp.exp(g)[..., None]], axis=-1)
    sol = lax.linalg.triangular_solve(a_mat + jnp.eye(c, dtype=a_mat.dtype), rhs,
                                      left_side=True, lower=True, unit_diagonal=True)
    u, w = sol[..., :dv], sol[..., dv:]
    v_new = u - jnp.einsum('bhik,bhkv->bhiv', w, S)
    attn = jnp.where(incl, jnp.einsum('bhid,bhjd->bhij', q, k) * decay, 0.0)
    o = (jnp.einsum('bhik,bhkv->bhiv', q * jnp.exp(g)[..., None], S)
         + jnp.einsum('bhij,bhjv->bhiv', attn, v_new))
    g_last = g[..., -1]
    S = (S * jnp.exp(g_last)[..., None, None]
         + jnp.einsum('bhjk,bhjv->bhkv', k * jnp.exp(g_last[..., None] - g)[..., None], v_new))
    return S, o.transpose(0, 2, 1, 3)


def gdn_mixer(qkv, a, b, gate, conv_w, conv_prev, A_log, dt_bias, norm_g, S0):
    B, T, _ = qkv.shape
    f32 = jnp.float32
    y, conv_new = causal_dwconv(qkv, conv_w, conv_prev)
    y = jax.nn.silu(y.astype(f32))
    nkd = GDN_HEADS * GDN_DK
    q = l2_normalize(y[..., :nkd].reshape(B, T, GDN_HEADS, GDN_DK)) * (GDN_DK ** -0.5)
    k = l2_normalize(y[..., nkd:2 * nkd].reshape(B, T, GDN_HEADS, GDN_DK))
    v = y[..., 2 * nkd:].reshape(B, T, GDN_HEADS, GDN_DV)
    g = -jnp.exp(A_log.astype(f32)) * jax.nn.softplus(a.astype(f32) + dt_bias.astype(f32))
    beta = jax.nn.sigmoid(b.astype(f32))
    c = math.gcd(T, CHUNK)
    S, o = lax.scan(_gdn_chunk, S0.astype(f32), tuple(to_chunks(z, c) for z in (q, k, v, g, beta)))
    o = from_chunks(o)
    o = o * lax.rsqrt(jnp.mean(o * o, axis=-1, keepdims=True) + EPS) * norm_g.astype(f32)
    o = o * jax.nn.silu(gate.astype(f32).reshape(B, T, GDN_HEADS, GDN_DV))
    return o.reshape(B, T, GDN_W).astype(qkv.dtype), conv_new, S.astype(qkv.dtype)


def _ssd_chunk(h, inp):
    x, dt, a, Bm, Cm = inp
    c = x.shape[1]
    acum = jnp.cumsum(a, axis=1)
    incl = jnp.tril(jnp.ones((c, c), dtype=bool))
    seg = jnp.exp(jnp.where(incl[None, :, :, None], acum[:, :, None, :] - acum[:, None, :, :], -jnp.inf))
    scores = jnp.einsum('bihn,bjhn->bijh', Cm, Bm) * seg
    y = jnp.einsum('bijh,bjhp->bihp', scores, x * dt[..., None])
    y = y + jnp.einsum('bihn,bhpn->bihp', Cm, h) * jnp.exp(acum)[..., None]
    a_last = acum[:, -1]
    w_end = jnp.exp(a_last[:, None] - acum) * dt
    h = h * jnp.exp(a_last)[..., None, None] + jnp.einsum('bjhn,bjhp->bhpn', Bm * w_end[..., None], x)
    return h, y


def ssd_mixer(z, xbc, dt, conv_w, conv_b, conv_prev, A_log, dt_bias, D_skip, norm_g, h0):
    B, T, _ = xbc.shape
    f32 = jnp.float32
    y, conv_new = causal_dwconv(xbc, conv_w, conv_prev)
    y = jax.nn.silu((y + conv_b).astype(f32))
    gn = SSM_GROUPS * SSM_N
    rep = SSM_HEADS // SSM_GROUPS
    xs = y[..., :SSM_W].reshape(B, T, SSM_HEADS, SSM_P)
    Bm = jnp.repeat(y[..., SSM_W:SSM_W + gn].reshape(B, T, SSM_GROUPS, SSM_N), rep, axis=2)
    Cm = jnp.repeat(y[..., SSM_W + gn:].reshape(B, T, SSM_GROUPS, SSM_N), rep, axis=2)
    dts = jax.nn.softplus(dt.astype(f32) + dt_bias.astype(f32))
    a = dts * (-jnp.exp(A_log.astype(f32)))
    c = math.gcd(T, CHUNK)
    h, ys = lax.scan(_ssd_chunk, h0.astype(f32), tuple(to_chunks(u, c) for u in (xs, dts, a, Bm, Cm)))
    ys = from_chunks(ys) + D_skip.astype(f32)[:, None] * xs
    ys = ys.reshape(B, T, SSM_W) * jax.nn.silu(z.astype(f32))
    ys = ys.reshape(B, T, SSM_GROUPS, SSM_W // SSM_GROUPS)
    ys = ys * lax.rsqrt(jnp.mean(ys * ys, axis=-1, keepdims=True) + EPS)
    ys = ys.reshape(B, T, SSM_W) * norm_g.astype(f32)
    return ys.astype(xbc.dtype), conv_new, h.astype(xbc.dtype)


def dsa_attend(q, qi, wi, k, v, ki, qpos, kpos, topk):
    f32 = jnp.float32
    rel = jax.nn.relu(jnp.einsum('bqhd,bsd->bqhs', qi.astype(f32), ki.astype(f32))) * (IDX_DH ** -0.5)
    score = jnp.einsum('bqhs,bqh->bqs', rel, wi.astype(f32)) * (IDX_HEADS ** -0.5)
    admissible = (kpos[None, :] // CHUNK) <= (qpos[:, None] // CHUNK)
    score = jnp.where(admissible[None], score, -jnp.inf)
    _, sel = lax.top_k(score, topk)
    valid = (kpos[sel] // CHUNK) <= (qpos[None, :, None] // CHUNK)
    gather = jax.vmap(lambda arr, idx: arr[idx])
    kg = gather(k, sel)
    vg = gather(v, sel)
    logits = jnp.einsum('bqhd,bqkhd->bqhk', q, kg).astype(f32) * (ATT_DH ** -0.5)
    logits = jnp.where(valid[:, :, None, :], logits, -jnp.inf)
    p = jax.nn.softmax(logits, axis=-1).astype(v.dtype)
    return jnp.einsum('bqhk,bqkhd->bqhd', p, vg)


def dsa_prompt(q, qi, wi, k, v, ki, pos, topk):
    B, T = q.shape[:2]
    nb = T // QBLK

    def blocks(a):
        return a.reshape(B, nb, QBLK, *a.shape[2:]).swapaxes(0, 1)

    def body(args):
        qb, qib, wib, pb = args
        return dsa_attend(qb, qib, wib, k, v, ki, pb, pos, topk)

    out = lax.map(body, (blocks(q), blocks(qi), blocks(wi), pos.reshape(nb, QBLK)))
    return out.swapaxes(0, 1).reshape(B, T, ATT_HEADS, ATT_DH)


def trunk_layer(x, c, pos, attn_cache, gdn_conv_prev, gdn_S0, ssm_conv_prev, ssm_h0, ffn_conv_prev,
                w_ada, b_ada, norm1_g, w_in, gdn_conv_w, gdn_A_log, gdn_dt_bias, gdn_norm_g,
                ssm_conv_w, ssm_conv_b, ssm_A_log, ssm_dt_bias, ssm_D, ssm_norm_g, w_out,
                norm2_g, w_gate, w_up, ffn_conv_w, ffn_conv_b, w_down):
    B, T, D = x.shape
    mod = (jax.nn.silu(c) @ w_ada + b_ada).reshape(B, 6, 1, D)
    shift1, scale1, gate1, shift2, scale2, gate2 = [mod[:, i] for i in range(6)]

    h = rms_norm(x, norm1_g) * (1 + scale1) + shift1
    u = h @ w_in
    split_points = [int(s) for s in np.cumsum(IN_SIZES)[:-1]]
    (g_qkv, g_a, g_b, g_gate, a_q, a_k, a_v, a_qi, a_ki, a_wi,
     s_z, s_xbc, s_dt) = jnp.split(u, split_points, axis=-1)

    gdn_out, gdn_conv_new, gdn_S = gdn_mixer(g_qkv, g_a, g_b, g_gate, gdn_conv_w, gdn_conv_prev,
                                             gdn_A_log, gdn_dt_bias, gdn_norm_g, gdn_S0)

    q = rotary(a_q.reshape(B, T, ATT_HEADS, ATT_DH), pos)
    k = rotary(a_k.reshape(B, T, ATT_HEADS, ATT_DH), pos)
    v = a_v.reshape(B, T, ATT_HEADS, ATT_DH)
    qi = rotary(a_qi.reshape(B, T, IDX_HEADS, IDX_DH), pos)
    ki = rotary(a_ki.reshape(B, T, 1, IDX_DH), pos)[:, :, 0]
    if attn_cache is None:
        att = dsa_prompt(q, qi, a_wi, k, v, ki, pos, min(TOPK_MAX, T // 4))
    else:
        ck, cv, cki = attn_cache
        L = ck.shape[1] + T
        k_all = jnp.concatenate([ck.astype(k.dtype), k], axis=1)
        v_all = jnp.concatenate([cv.astype(v.dtype), v], axis=1)
        ki_all = jnp.concatenate([cki.astype(ki.dtype), ki], axis=1)
        att = dsa_attend(q, qi, a_wi, k_all, v_all, ki_all, pos, jnp.arange(L), min(TOPK_MAX, L // 4))
    att = att.reshape(B, T, ATT_W)

    ssm_out, ssm_conv_new, ssm_h = ssd_mixer(s_z, s_xbc, s_dt, ssm_conv_w, ssm_conv_b, ssm_conv_prev,
                                             ssm_A_log, ssm_dt_bias, ssm_D, ssm_norm_g, ssm_h0)

    mix = jnp.concatenate([gdn_out, att, ssm_out], axis=-1)
    x = x + gate1 * (mix @ w_out)

    h2 = rms_norm(x, norm2_g) * (1 + scale2) + shift2
    ag = h2 @ w_gate
    ag_c, ffn_conv_new = causal_dwconv(ag, ffn_conv_w, ffn_conv_prev)
    y = (jax.nn.silu(ag_c + ffn_conv_b) * (h2 @ w_up)) @ w_down
    x = x + gate2 * y
    return x, (k, v, ki, gdn_conv_new, gdn_S, ssm_conv_new, ssm_h, ffn_conv_new)


def setup_inputs(seed: int = 0) -> dict:
    key = jax.random.key(seed)
    keys = jax.random.split(key, 48)
    counter = [0]
    f32 = jnp.float32

    def nk():
        kk = keys[counter[0]]
        counter[0] += 1
        return kk

    def normal(shape, scale):
        return jax.random.normal(nk(), shape, f32) * scale

    def gain(shape):
        return 1.0 + normal(shape, 0.02)

    def dt_bias(n):
        dt = jnp.exp(jax.random.uniform(nk(), (DEPTH, n), f32, math.log(1e-3), math.log(1e-1)))
        return dt + jnp.log(-jnp.expm1(-dt))

    def a_log(n):
        return jnp.log(jax.random.uniform(nk(), (DEPTH, n), f32, 1.0, 16.0))

    return {
        'x_prompt': normal((BATCH, SEQ, D_MODEL), 1.0),
        'x_sample': normal((DEC_BATCH, DEC_SEQ, D_MODEL), 1.0),
        'c_prompt': normal((BATCH, D_MODEL), 1.0),
        'c_sample': normal((DEC_BATCH, D_MODEL), 1.0),
        'cache_k': normal((DEPTH, DEC_BATCH, PAST_LEN, ATT_HEADS, ATT_DH), 1.0),
        'cache_v': normal((DEPTH, DEC_BATCH, PAST_LEN, ATT_HEADS, ATT_DH), 1.0),
        'cache_kidx': normal((DEPTH, DEC_BATCH, PAST_LEN, IDX_DH), 1.0),
        'state_gdn_conv': normal((DEPTH, DEC_BATCH, CONV_W - 1, GDN_CONV_C), 1.0),
        'state_gdn': normal((DEPTH, DEC_BATCH, GDN_HEADS, GDN_DK, GDN_DV), 0.5),
        'state_ssm_conv': normal((DEPTH, DEC_BATCH, CONV_W - 1, SSM_CONV_C), 1.0),
        'state_ssm': normal((DEPTH, DEC_BATCH, SSM_HEADS, SSM_P, SSM_N), 0.1),
        'state_ffn_conv': normal((DEPTH, DEC_BATCH, FFN_CONV_W - 1, D_FF), 1.0),
        'w_ada': normal((DEPTH, D_MODEL, 6 * D_MODEL), 0.5 * D_MODEL ** -0.5),
        'b_ada': normal((DEPTH, 6 * D_MODEL), 0.02),
        'norm1_g': gain((DEPTH, D_MODEL)),
        'w_in': normal((DEPTH, D_MODEL, IN_W), D_MODEL ** -0.5),
        'gdn_conv_w': normal((DEPTH, CONV_W, GDN_CONV_C), 0.5),
        'gdn_A_log': a_log(GDN_HEADS),
        'gdn_dt_bias': dt_bias(GDN_HEADS),
        'gdn_norm_g': gain((DEPTH, GDN_DV)),
        'ssm_conv_w': normal((DEPTH, CONV_W, SSM_CONV_C), 0.5),
        'ssm_conv_b': normal((DEPTH, SSM_CONV_C), 0.02),
        'ssm_A_log': a_log(SSM_HEADS),
        'ssm_dt_bias': dt_bias(SSM_HEADS),
        'ssm_D': 1.0 + normal((DEPTH, SSM_HEADS), 0.1),
        'ssm_norm_g': gain((DEPTH, SSM_W)),
        'w_out': normal((DEPTH, MIX_W, D_MODEL), MIX_W ** -0.5),
        'norm2_g': gain((DEPTH, D_MODEL)),
        'w_gate': normal((DEPTH, D_MODEL, D_FF), D_MODEL ** -0.5),
        'w_up': normal((DEPTH, D_MODEL, D_FF), D_MODEL ** -0.5),
        'ffn_conv_w': normal((DEPTH, FFN_CONV_W, D_FF), 0.5),
        'ffn_conv_b': normal((DEPTH, D_FF), 0.02),
        'w_down': normal((DEPTH, D_FF, D_MODEL), D_FF ** -0.5),
        'final_g': gain((D_MODEL,)),
    }


def reference(x_prompt, x_sample, c_prompt, c_sample, cache_k, cache_v, cache_kidx,
              state_gdn_conv, state_gdn, state_ssm_conv, state_ssm, state_ffn_conv,
              w_ada, b_ada, norm1_g, w_in, gdn_conv_w, gdn_A_log, gdn_dt_bias, gdn_norm_g,
              ssm_conv_w, ssm_conv_b, ssm_A_log, ssm_dt_bias, ssm_D, ssm_norm_g, w_out,
              norm2_g, w_gate, w_up, ffn_conv_w, ffn_conv_b, w_down, final_g):
    dtype = x_prompt.dtype
    Bp, T, _ = x_prompt.shape
    Ts = x_sample.shape[1]
    P = cache_k.shape[2]
    pos_p = jnp.arange(T)
    pos_s = P + jnp.arange(Ts)
    xp, xs = x_prompt, x_sample
    new_p, new_s = [], []
    for l in range(DEPTH):
        lw = (w_ada[l], b_ada[l], norm1_g[l], w_in[l], gdn_conv_w[l], gdn_A_log[l], gdn_dt_bias[l],
              gdn_norm_g[l], ssm_conv_w[l], ssm_conv_b[l], ssm_A_log[l], ssm_dt_bias[l], ssm_D[l],
              ssm_norm_g[l], w_out[l], norm2_g[l], w_gate[l], w_up[l], ffn_conv_w[l], ffn_conv_b[l],
              w_down[l])
        xp, st_p = trunk_layer(
            xp, c_prompt, pos_p, None,
            jnp.zeros((Bp, CONV_W - 1, GDN_CONV_C), dtype),
            jnp.zeros((Bp, GDN_HEADS, GDN_DK, GDN_DV), dtype),
            jnp.zeros((Bp, CONV_W - 1, SSM_CONV_C), dtype),
            jnp.zeros((Bp, SSM_HEADS, SSM_P, SSM_N), dtype),
            jnp.zeros((Bp, FFN_CONV_W - 1, D_FF), dtype),
            *lw)
        xs, st_s = trunk_layer(
            xs, c_sample, pos_s, (cache_k[l], cache_v[l], cache_kidx[l]),
            state_gdn_conv[l], state_gdn[l], state_ssm_conv[l], state_ssm[l], state_ffn_conv[l],
            *lw)
        new_p.append(st_p)
        new_s.append(st_s)
    y_prompt = rms_norm(xp, final_g)
    y_sample = rms_norm(xs, final_g)
    p_k, p_v, p_kidx, p_gdn_conv, p_gdn, p_ssm_conv, p_ssm, p_ffn_conv = [
        jnp.stack([st[i] for st in new_p]) for i in range(8)]
    s_k, s_v, s_kidx, s_gdn_conv, s_gdn, s_ssm_conv, s_ssm, s_ffn_conv = [
        jnp.stack([st[i] for st in new_s]) for i in range(8)]
    return (y_prompt, y_sample,
            p_k, p_v, p_kidx, p_gdn_conv, p_gdn, p_ssm_conv, p_ssm, p_ffn_conv,
            s_k, s_v, s_kidx, s_gdn_conv, s_gdn, s_ssm_conv, s_ssm, s_ffn_conv)
```

```python
import functools
import math

import jax
import jax.numpy as jnp
import numpy as np
from jax import lax
from jax.experimental import pallas as pl
from jax.experimental.pallas import tpu as pltpu

F32 = jnp.float32
BF16 = jnp.bfloat16
HI = lax.Precision.HIGHEST

D_MODEL = 1024
DEPTH = 2
CHUNK = 64
CONV_W = 4
FFN_CONV_W = 3
D_FF = 2816
ROPE_THETA = 10000.0
EPS = 1e-6
GDN_HEADS = 4
GDN_DK = 64
GDN_DV = 64
ATT_HEADS = 4
ATT_DH = 64
IDX_HEADS = 4
IDX_DH = 64
TOPK_MAX = 256
SSM_HEADS = 8
SSM_P = 64
SSM_GROUPS = 2
SSM_N = 128
GDN_W = GDN_HEADS * GDN_DV
ATT_W = ATT_HEADS * ATT_DH
SSM_W = SSM_HEADS * SSM_P
MIX_W = GDN_W + ATT_W + SSM_W
GDN_CONV_C = 2 * GDN_HEADS * GDN_DK + GDN_W
SSM_CONV_C = SSM_W + 2 * SSM_GROUPS * SSM_N
IN_SIZES = (GDN_CONV_C, GDN_HEADS, GDN_HEADS, GDN_W,
            ATT_W, ATT_W, ATT_W, IDX_HEADS * IDX_DH, IDX_DH, IDX_HEADS,
            SSM_W, SSM_CONV_C, SSM_HEADS)
IN_W = sum(IN_SIZES)

LANES = 128
SUBLANES = 8
VMEM_LIMIT = 56 * 1024 * 1024

KIS_A = IDX_DH
KIS_B = KIS_A + GDN_HEADS
KIS_WI = KIS_B + GDN_HEADS
KIS_DT = KIS_WI + IDX_HEADS
KIS_END = KIS_DT + SSM_HEADS
OUT_SEGS = (("gqkv", GDN_CONV_C), ("ggate", GDN_W), ("q", ATT_W), ("k", ATT_W), ("v", ATT_W),
            ("qi", IDX_HEADS * IDX_DH), ("kis", LANES), ("z", SSM_W), ("xbc", SSM_CONV_C))
PERM_W = sum(w for _, w in OUT_SEGS)
ROT_W = ATT_W + LANES


def _perm_columns():
    starts = np.concatenate([[0], np.cumsum(IN_SIZES)])
    (s_gqkv, s_ga, s_gb, s_gg, s_q, s_k, s_v, s_qi, s_ki, s_wi, s_z, s_xbc, s_dt) = starts[:-1]
    cols = []
    cols += list(range(s_gqkv, s_gqkv + GDN_CONV_C))
    cols += list(range(s_gg, s_gg + GDN_W))
    cols += list(range(s_q, s_q + ATT_W))
    cols += list(range(s_k, s_k + ATT_W))
    cols += list(range(s_v, s_v + ATT_W))
    cols += list(range(s_qi, s_qi + IDX_HEADS * IDX_DH))
    kis = (list(range(s_ki, s_ki + IDX_DH)) + list(range(s_ga, s_ga + GDN_HEADS))
           + list(range(s_gb, s_gb + GDN_HEADS)) + list(range(s_wi, s_wi + IDX_HEADS))
           + list(range(s_dt, s_dt + SSM_HEADS)))
    cols += kis + [-1] * (LANES - len(kis))
    cols += list(range(s_z, s_z + SSM_W))
    cols += list(range(s_xbc, s_xbc + SSM_CONV_C))
    assert len(cols) == PERM_W
    return np.asarray(cols, np.int32)


_PERM_COLS = _perm_columns()


def _silu(x):
    return x * jax.nn.sigmoid(x)


def _softplus(x):
    return jnp.maximum(x, 0.0) + jnp.log1p(jnp.exp(-jnp.abs(x)))


def _dot(a, b, precision=None):
    return jnp.dot(a, b, preferred_element_type=F32, precision=precision)


def _dot_nt(a, b, precision=None):
    return lax.dot_general(a, b, (((1,), (1,)), ((), ())), preferred_element_type=F32,
                           precision=precision)


def _dot_tn(a, b, precision=None):
    return lax.dot_general(a, b, (((0,), (0,)), ((), ())), preferred_element_type=F32,
                           precision=precision)


def _cumsum_rows(x):
    c = x.shape[0]
    row = lax.broadcasted_iota(jnp.int32, x.shape, 0)
    s = 1
    while s < c:
        x = x + jnp.where(row >= s, pltpu.roll(x, s, axis=0), 0.0)
        s *= 2
    return x


def _row_form(vals, lane, c):
    onehot = (lax.broadcasted_iota(jnp.int32, (c, LANES), 1) == lane).astype(F32)
    return _dot_nt(onehot, vals, HI)


def _lane_vector(vals, start):
    return jnp.zeros((1, LANES), F32).at[0, start:start + vals.shape[0]].set(vals.astype(F32))


def _params(sem):
    return pltpu.CompilerParams(dimension_semantics=sem, vmem_limit_bytes=VMEM_LIMIT)


def _ada_kernel(c_ref, w_ref, b_ref, o_ref):
    s = _silu(c_ref[...])
    o_ref[0] = _dot(s.astype(BF16), w_ref[0].astype(BF16)) + b_ref[0]


def _ada(c_all, w_ada, b_ada):
    rows = c_all.shape[0]
    n = w_ada.shape[2]
    tn = 1536
    return pl.pallas_call(
        _ada_kernel,
        grid=(DEPTH, n // tn),
        in_specs=[pl.BlockSpec((rows, D_MODEL), lambda l, j: (0, 0)),
                  pl.BlockSpec((1, D_MODEL, tn), lambda l, j: (l, 0, j)),
                  pl.BlockSpec((1, 1, tn), lambda l, j: (l, 0, j))],
        out_specs=pl.BlockSpec((1, rows, tn), lambda l, j: (l, 0, j)),
        out_shape=jax.ShapeDtypeStruct((DEPTH, rows, n), F32),
        compiler_params=_params(("arbitrary", "arbitrary")),
        name="ada",
    )(c_all, w_ada, b_ada.reshape(DEPTH, 1, n))


def _rotate(x, cos, sin_signed):
    w = x.shape[-1]
    lane = lax.broadcasted_iota(jnp.int32, x.shape, x.ndim - 1)
    first = (lane % ATT_DH) < (ATT_DH // 2)
    swapped = jnp.where(first, pltpu.roll(x, w - ATT_DH // 2, axis=x.ndim - 1),
                        pltpu.roll(x, ATT_DH // 2, axis=x.ndim - 1))
    return x * cos + swapped * sin_signed


def _inproj_kernel(x_ref, mod_ref, g_ref, w_ref, cos_ref, sin_ref, *out_refs, nb, tt):
    x = x_ref[...]
    ms = jnp.mean(x * x, axis=-1, keepdims=True)
    xn = x * lax.rsqrt(ms + EPS) * g_ref[...]
    mod = mod_ref[...]
    h = xn * (1.0 + mod[:, 1:2, :]) + mod[:, 0:1, :]
    u = _dot(h.reshape(nb * tt, D_MODEL).astype(BF16), w_ref[...])
    cos = cos_ref[...]
    sin = sin_ref[...]
    off = 0
    for (name, width), o_ref in zip(OUT_SEGS, out_refs):
        seg = u[:, off:off + width]
        if name in ("q", "k", "qi", "kis"):
            t0 = ATT_W if name == "kis" else 0
            c3 = cos[:, t0:t0 + width][None]
            s3 = sin[:, t0:t0 + width][None]
            if nb > 1:
                c3 = jnp.broadcast_to(c3, (nb, tt, width)).reshape(nb * tt, width)
                s3 = jnp.broadcast_to(s3, (nb, tt, width)).reshape(nb * tt, width)
            else:
                c3 = c3[0]
                s3 = s3[0]
            seg = _rotate(seg, c3, s3)
        o_ref[...] = seg.reshape(nb, tt, width)
        off += width


def _inproj(x, mod, g1, w_perm, cos, sin, nb, tt):
    B, T, _ = x.shape
    grid = (B // nb, T // tt)
    out_shape = tuple(jax.ShapeDtypeStruct((B, T, w), F32) for _, w in OUT_SEGS)
    out_specs = tuple(pl.BlockSpec((nb, tt, w), lambda b, t: (b, t, 0)) for _, w in OUT_SEGS)
    return pl.pallas_call(
        functools.partial(_inproj_kernel, nb=nb, tt=tt),
        grid=grid,
        in_specs=[pl.BlockSpec((nb, tt, D_MODEL), lambda b, t: (b, t, 0)),
                  pl.BlockSpec((nb, 6, D_MODEL), lambda b, t: (b, 0, 0)),
                  pl.BlockSpec((1, D_MODEL), lambda b, t: (0, 0)),
                  pl.BlockSpec((D_MODEL, PERM_W), lambda b, t: (0, 0)),
                  pl.BlockSpec((tt, ROT_W), lambda b, t: (t, 0)),
                  pl.BlockSpec((tt, ROT_W), lambda b, t: (t, 0))],
        out_specs=out_specs,
        out_shape=out_shape,
        compiler_params=_params(("arbitrary", "arbitrary")),
        name="inproj",
    )(x, mod, g1, w_perm, cos, sin)


def _short_conv(ext_ref, u, w, prev_ref, new_ref, first, c):
    lo = SUBLANES - (CONV_W - 1)

    @pl.when(first)
    def _():
        ext_ref[lo:SUBLANES, :] = prev_ref[0]

    ext_ref[SUBLANES:SUBLANES + c, :] = u
    y = ext_ref[lo:lo + c, :] * w[0:1, :]
    for j in range(1, CONV_W):
        y = y + ext_ref[lo + j:lo + j + c, :] * w[j:j + 1, :]
    tail = ext_ref[c + lo:c + SUBLANES, :]
    ext_ref[lo:SUBLANES, :] = tail
    new_ref[0] = tail
    return y


def _neumann_inverse(a, c):
    eye = (lax.broadcasted_iota(jnp.int32, (c, c), 0)
           == lax.broadcasted_iota(jnp.int32, (c, c), 1)).astype(F32)
    p = eye - a
    pw = _dot(a, a, HI)
    n = 2
    while n < c:
        p = p + _dot(p, pw, HI)
        n *= 2
        if n < c:
            pw = _dot(pw, pw, HI)
    return p


def _gdn_kernel(qkv_ref, kis_ref, gate_ref, w_ref, prev_ref, s0_ref, alog_ref, dtb_ref, ng_ref,
                o_ref, new_ref, s_ref, ext_ref, s_sc, *, c):
    j = pl.program_id(1)
    first = j == 0

    @pl.when(first)
    def _():
        s_sc[...] = s0_ref[0]

    y = _short_conv(ext_ref, qkv_ref[0], w_ref[...], prev_ref, new_ref, first, c)
    y = _silu(y)
    small = kis_ref[0]
    g = -jnp.exp(alog_ref[...]) * _softplus(small + dtb_ref[...])
    beta = jax.nn.sigmoid(small)
    gc = _cumsum_rows(g)
    gate = gate_ref[0]
    ng = ng_ref[...]

    ri = lax.broadcasted_iota(jnp.int32, (c, c), 0)
    ci = lax.broadcasted_iota(jnp.int32, (c, c), 1)
    incl = ri >= ci
    strict = ri > ci
    nkd = GDN_HEADS * GDN_DK
    for h in range(GDN_HEADS):
        qh = y[:, h * GDN_DK:(h + 1) * GDN_DK]
        kh = y[:, nkd + h * GDN_DK:nkd + (h + 1) * GDN_DK]
        vh = y[:, 2 * nkd + h * GDN_DV:2 * nkd + (h + 1) * GDN_DV]
        qh = qh * lax.rsqrt(jnp.sum(qh * qh, axis=-1, keepdims=True) + EPS) * (GDN_DK ** -0.5)
        kh = kh * lax.rsqrt(jnp.sum(kh * kh, axis=-1, keepdims=True) + EPS)
        gcol = gc[:, KIS_A + h:KIS_A + h + 1]
        bcol = beta[:, KIS_B + h:KIS_B + h + 1]
        diff = jnp.broadcast_to(gcol, (c, c)) - _row_form(gc, KIS_A + h, c)
        decay = jnp.where(incl, jnp.exp(jnp.where(incl, diff, 0.0)), 0.0)
        kb = kh * bcol
        a_mat = jnp.where(strict, _dot_nt(kb, kh, HI) * decay, 0.0)
        t_inv = _neumann_inverse(a_mat, c)
        eg = jnp.exp(gcol)
        rhs = jnp.concatenate([vh * bcol, kb * eg], axis=1)
        sol = _dot(t_inv, rhs, HI)
        u = sol[:, :GDN_DV]
        w = sol[:, GDN_DV:]
        s = s_sc[h]
        v_new = u - _dot(w, s, HI)
        attn = jnp.where(incl, _dot_nt(qh, kh, HI) * decay, 0.0)
        o = _dot(qh * eg, s, HI) + _dot(attn, v_new, HI)
        g_last = gcol[c - 1:c, :]
        kd = kh * jnp.exp(g_last - gcol)
        s_new = s * jnp.exp(g_last) + _dot_tn(kd, v_new, HI)
        s_sc[h] = s_new
        s_ref[0, h] = s_new
        o = o * lax.rsqrt(jnp.mean(o * o, axis=-1, keepdims=True) + EPS) * ng
        o = o * _silu(gate[:, h * GDN_DV:(h + 1) * GDN_DV])
        o_ref[0, :, h * GDN_DV:(h + 1) * GDN_DV] = o


def _gdn(gqkv, kis, ggate, conv_w, conv_prev, s0, a_log, dt_bias, norm_g):
    B, T, _ = gqkv.shape
    c = math.gcd(T, CHUNK)
    return pl.pallas_call(
        functools.partial(_gdn_kernel, c=c),
        grid=(B, T // c),
        in_specs=[pl.BlockSpec((1, c, GDN_CONV_C), lambda b, j: (b, j, 0)),
                  pl.BlockSpec((1, c, LANES), lambda b, j: (b, j, 0)),
                  pl.BlockSpec((1, c, GDN_W), lambda b, j: (b, j, 0)),
                  pl.BlockSpec((CONV_W, GDN_CONV_C), lambda b, j: (0, 0)),
                  pl.BlockSpec((1, CONV_W - 1, GDN_CONV_C), lambda b, j: (b, 0, 0)),
                  pl.BlockSpec((1, GDN_HEADS, GDN_DK, GDN_DV), lambda b, j: (b, 0, 0, 0)),
                  pl.BlockSpec((1, LANES), lambda b, j: (0, 0)),
                  pl.BlockSpec((1, LANES), lambda b, j: (0, 0)),
                  pl.BlockSpec((1, GDN_DV), lambda b, j: (0, 0))],
        out_specs=(pl.BlockSpec((1, c, GDN_W), lambda b, j: (b, j, 0)),
                   pl.BlockSpec((1, CONV_W - 1, GDN_CONV_C), lambda b, j: (b, 0, 0)),
                   pl.BlockSpec((1, GDN_HEADS, GDN_DK, GDN_DV), lambda b, j: (b, 0, 0, 0))),
        out_shape=(jax.ShapeDtypeStruct((B, T, GDN_W), F32),
                   jax.ShapeDtypeStruct((B, CONV_W - 1, GDN_CONV_C), F32),
                   jax.ShapeDtypeStruct((B, GDN_HEADS, GDN_DK, GDN_DV), F32)),
        scratch_shapes=[pltpu.VMEM((SUBLANES + c, GDN_CONV_C), F32),
                        pltpu.VMEM((GDN_HEADS, GDN_DK, GDN_DV), F32)],
        compiler_params=_params(("arbitrary", "arbitrary")),
        name="gdn",
    )(gqkv, kis, ggate, conv_w, conv_prev, s0, _lane_vector(a_log, KIS_A), _lane_vector(dt_bias, KIS_A),
      norm_g.reshape(1, -1))


def _ssd_kernel(xbc_ref, z_ref, kis_ref, w_ref, cb_ref, prev_ref, h0_ref, alog_ref, dtb_ref, dsk_ref,
                ng_ref, o_ref, new_ref, h_ref, ext_ref, h_sc, *, c):
    j = pl.program_id(1)
    first = j == 0

    @pl.when(first)
    def _():
        h_sc[...] = h0_ref[0]

    y = _short_conv(ext_ref, xbc_ref[0], w_ref[...], prev_ref, new_ref, first, c)
    y = _silu(y + cb_ref[...])
    dts = _softplus(kis_ref[0] + dtb_ref[...])
    a = dts * (-jnp.exp(alog_ref[...]))
    acum = _cumsum_rows(a)
    dsk = dsk_ref[...]
    zg = _silu(z_ref[0])
    ng = ng_ref[...]

    ri = lax.broadcasted_iota(jnp.int32, (c, c), 0)
    ci = lax.broadcasted_iota(jnp.int32, (c, c), 1)
    incl = ri >= ci
    gn = SSM_GROUPS * SSM_N
    rep = SSM_HEADS // SSM_GROUPS
    gw = SSM_W // SSM_GROUPS
    for grp in range(SSM_GROUPS):
        bg = y[:, SSM_W + grp * SSM_N:SSM_W + (grp + 1) * SSM_N]
        cg = y[:, SSM_W + gn + grp * SSM_N:SSM_W + gn + (grp + 1) * SSM_N]
        cb = _dot_nt(cg, bg, HI)
        outs = []
        for hh in range(rep):
            h = grp * rep + hh
            acol = acum[:, KIS_DT + h:KIS_DT + h + 1]
            diff = jnp.broadcast_to(acol, (c, c)) - _row_form(acum, KIS_DT + h, c)
            seg = jnp.where(incl, jnp.exp(jnp.where(incl, diff, 0.0)), 0.0)
            xh = y[:, h * SSM_P:(h + 1) * SSM_P]
            dtcol = dts[:, KIS_DT + h:KIS_DT + h + 1]
            hs = h_sc[h]
            yh = _dot(cb * seg, xh * dtcol, HI) + _dot_nt(cg, hs, HI) * jnp.exp(acol)
            a_last = acol[c - 1:c, :]
            w_end = jnp.exp(a_last - acol) * dtcol
            h_new = hs * jnp.exp(a_last) + _dot_tn(xh, bg * w_end, HI)
            h_sc[h] = h_new
            h_ref[0, h] = h_new
            outs.append(yh + dsk[:, h:h + 1] * xh)
        yg = jnp.concatenate(outs, axis=1) * zg[:, grp * gw:(grp + 1) * gw]
        yg = yg * lax.rsqrt(jnp.mean(yg * yg, axis=-1, keepdims=True) + EPS)
        o_ref[0, :, grp * gw:(grp + 1) * gw] = yg * ng[:, grp * gw:(grp + 1) * gw]


def _ssd(xbc, z, kis, conv_w, conv_b, conv_prev, h0, a_log, dt_bias, d_skip, norm_g):
    B, T, _ = xbc.shape
    c = math.gcd(T, CHUNK)
    return pl.pallas_call(
        functools.partial(_ssd_kernel, c=c),
        grid=(B, T // c),
        in_specs=[pl.BlockSpec((1, c, SSM_CONV_C), lambda b, j: (b, j, 0)),
                  pl.BlockSpec((1, c, SSM_W), lambda b, j: (b, j, 0)),
                  pl.BlockSpec((1, c, LANES), lambda b, j: (b, j, 0)),
                  pl.BlockSpec((CONV_W, SSM_CONV_C), lambda b, j: (0, 0)),
                  pl.BlockSpec((1, SSM_CONV_C), lambda b, j: (0, 0)),
                  pl.BlockSpec((1, CONV_W - 1, SSM_CONV_C), lambda b, j: (b, 0, 0)),
                  pl.BlockSpec((1, SSM_HEADS, SSM_P, SSM_N), lambda b, j: (b, 0, 0, 0)),
                  pl.BlockSpec((1, LANES), lambda b, j: (0, 0)),
                  pl.BlockSpec((1, LANES), lambda b, j: (0, 0)),
                  pl.BlockSpec((1, SSM_HEADS), lambda b, j: (0, 0)),
                  pl.BlockSpec((1, SSM_W), lambda b, j: (0, 0))],
        out_specs=(pl.BlockSpec((1, c, SSM_W), lambda b, j: (b, j, 0)),
                   pl.BlockSpec((1, CONV_W - 1, SSM_CONV_C), lambda b, j: (b, 0, 0)),
                   pl.BlockSpec((1, SSM_HEADS, SSM_P, SSM_N), lambda b, j: (b, 0, 0, 0))),
        out_shape=(jax.ShapeDtypeStruct((B, T, SSM_W), F32),
                   jax.ShapeDtypeStruct((B, CONV_W - 1, SSM_CONV_C), F32),
                   jax.ShapeDtypeStruct((B, SSM_HEADS, SSM_P, SSM_N), F32)),
        scratch_shapes=[pltpu.VMEM((SUBLANES + c, SSM_CONV_C), F32),
                        pltpu.VMEM((SSM_HEADS, SSM_P, SSM_N), F32)],
        compiler_params=_params(("arbitrary", "arbitrary")),
        name="ssd",
    )(xbc, z, kis, conv_w, conv_b.reshape(1, -1), conv_prev, h0, _lane_vector(a_log, KIS_DT),
      _lane_vector(dt_bias, KIS_DT), d_skip.reshape(1, -1), norm_g.reshape(1, -1))


IDX_BITS = 12


def _count(mask):
    return jnp.sum(mask.astype(F32), axis=1, keepdims=True)


def _dsa_kernel(q_ref, qi_ref, kis_ref, k_ref, v_ref, ki_ref, o_ref, *, tq, s_pad, s_valid, q_offset, topk):
    j = pl.program_id(1)
    qpos = q_offset + j * tq + lax.broadcasted_iota(jnp.int32, (tq, 1), 0)
    lim = jnp.minimum((qpos // CHUNK + 1) * CHUNK, s_valid)
    col = lax.broadcasted_iota(jnp.int32, (tq, s_pad), 1)
    adm = col < lim

    qi = qi_ref[0]
    ki = ki_ref[0]
    wi = kis_ref[0][:, KIS_WI:KIS_DT]
    score = jnp.zeros((tq, s_pad), F32)
    for h in range(IDX_HEADS):
        rel = jnp.maximum(_dot_nt(qi[:, h * IDX_DH:(h + 1) * IDX_DH], ki, HI), 0.0) * (IDX_DH ** -0.5)
        score = score + rel * wi[:, h:h + 1]
    score = score * (IDX_HEADS ** -0.5)
    s = jnp.where(adm, score, -jnp.inf)

    few = _count(adm) < topk
    lo = jnp.where(few, -jnp.inf, jnp.min(jnp.where(adm, score, jnp.inf), axis=1, keepdims=True))
    hi = jnp.where(few, -jnp.inf, jnp.max(s, axis=1, keepdims=True))

    def search_cond(c):
        return jnp.any(c[0] < c[1])

    def search_body(c):
        lo, hi = c
        mid = 0.5 * lo + 0.5 * hi
        mid = jnp.where(mid > lo, jnp.minimum(mid, hi), hi)
        ge = s >= mid
        take = _count(ge) >= topk
        above = jnp.min(jnp.where(ge, s, jnp.inf), axis=1, keepdims=True)
        below = jnp.max(jnp.where(ge, -jnp.inf, s), axis=1, keepdims=True)
        active = lo < hi
        return (jnp.where(active & take, above, lo), jnp.where(active & jnp.logical_not(take), below, hi))

    t, _ = lax.while_loop(search_cond, search_body, (lo, hi))
    gt = s > t
    need = topk - _count(gt)
    tie = s == t

    def ibody(i, m):
        cand = m | (jnp.int32(1) << (IDX_BITS - 1 - i))
        return jnp.where(_count(tie & (col < cand)) <= need, cand, m)

    m = lax.fori_loop(0, IDX_BITS, ibody, jnp.zeros((tq, 1), jnp.int32))
    sel = adm & (gt | (tie & (col < m)))

    q = q_ref[0]
    kb = k_ref[0].astype(BF16)
    vb = v_ref[0].astype(BF16)
    lane = lax.broadcasted_iota(jnp.int32, (tq, ATT_W), 1)
    out = jnp.zeros((tq, ATT_W), F32)
    for h in range(ATT_HEADS):
        head = (lane // ATT_DH) == h
        qh = jnp.where(head, q, 0.0).astype(BF16)
        logits = _dot_nt(qh, kb) * (ATT_DH ** -0.5)
        logits = jnp.where(sel, logits, -jnp.inf)
        mx = jnp.max(logits, axis=1, keepdims=True)
        p = jnp.exp(logits - mx)
        den = jnp.sum(p, axis=1, keepdims=True)
        oh = _dot(p.astype(BF16), vb) / den
        out = jnp.where(head, oh, out)
    o_ref[0] = out


def _dsa(q, qi, kis, k, v, ki, tq, s_valid, q_offset, topk):
    B, T, _ = q.shape
    s_pad = k.shape[1]
    assert s_pad < 2 ** IDX_BITS
    return pl.pallas_call(
        functools.partial(_dsa_kernel, tq=tq, s_pad=s_pad, s_valid=s_valid, q_offset=q_offset, topk=topk),
        grid=(B, T // tq),
        in_specs=[pl.BlockSpec((1, tq, ATT_W), lambda b, j: (b, j, 0)),
                  pl.BlockSpec((1, tq, IDX_HEADS * IDX_DH), lambda b, j: (b, j, 0)),
                  pl.BlockSpec((1, tq, LANES), lambda b, j: (b, j, 0)),
                  pl.BlockSpec((1, s_pad, ATT_W), lambda b, j: (b, 0, 0)),
                  pl.BlockSpec((1, s_pad, ATT_W), lambda b, j: (b, 0, 0)),
                  pl.BlockSpec((1, s_pad, IDX_DH), lambda b, j: (b, 0, 0))],
        out_specs=pl.BlockSpec((1, tq, ATT_W), lambda b, j: (b, j, 0)),
        out_shape=jax.ShapeDtypeStruct((B, T, ATT_W), F32),
        compiler_params=_params(("arbitrary", "arbitrary")),
        name="dsa",
    )(q, qi, kis, k, v, ki)


def _ffn_kernel(x_ref, gdn_ref, att_ref, ssm_ref, mod_ref, wo_ref, g2_ref, wg_ref, wu_ref, cw_ref, cb_ref,
                wd_ref, prev_ref, fg_ref, *rest, nb, tt, nk, final):
    if final:
        xo_ref, y_ref, new_ref, h2_sc, acc_sc, buf_sc, carry_sc = rest
    else:
        xo_ref, new_ref, h2_sc, acc_sc, buf_sc, carry_sc = rest
        y_ref = None
    t = pl.program_id(1)
    k = pl.program_id(2)
    rows = nb * tt
    lo = SUBLANES - (FFN_CONV_W - 1)
    mod = mod_ref[...]

    @pl.when(k == 0)
    def _():
        mix = jnp.concatenate([gdn_ref[...], att_ref[...], ssm_ref[...]], axis=-1)
        proj = _dot(mix.reshape(rows, MIX_W).astype(BF16), wo_ref[...]).reshape(nb, tt, D_MODEL)
        x1 = x_ref[...] + mod[:, 2:3, :] * proj
        acc_sc[...] = x1
        ms = jnp.mean(x1 * x1, axis=-1, keepdims=True)
        h2 = x1 * lax.rsqrt(ms + EPS) * g2_ref[...]
        h2 = h2 * (1.0 + mod[:, 4:5, :]) + mod[:, 3:4, :]
        h2_sc[...] = h2.reshape(rows, D_MODEL).astype(BF16)

    @pl.when(t == 0)
    def _():
        carry_sc[k, :, lo:SUBLANES, :] = prev_ref[...]

    h2 = h2_sc[...]
    ag = _dot(h2, wg_ref[...])
    up = _dot(h2, wu_ref[...])
    tf = ag.shape[-1]
    buf_sc[:, lo:SUBLANES, :] = carry_sc[k, :, lo:SUBLANES, :]
    buf_sc[:, SUBLANES:SUBLANES + tt, :] = ag.reshape(nb, tt, tf)
    cw = cw_ref[...]
    conv = buf_sc[:, lo:lo + tt, :] * cw[0:1, :][None]
    for jj in range(1, FFN_CONV_W):
        conv = conv + buf_sc[:, lo + jj:lo + jj + tt, :] * cw[jj:jj + 1, :][None]
    tail = buf_sc[:, tt + lo:tt + SUBLANES, :]
    carry_sc[k, :, lo:SUBLANES, :] = tail
    for kk in range(nk):
        @pl.when(k == kk)
        def _(kk=kk):
            new_ref[:, :, kk * tf:(kk + 1) * tf] = tail
    act = _silu(conv + cb_ref[...][None]).reshape(rows, tf) * up
    y = _dot(act.astype(BF16), wd_ref[...]).reshape(nb, tt, D_MODEL)
    acc_sc[...] += mod[:, 5:6, :] * y

    @pl.when(k == nk - 1)
    def _():
        xo = acc_sc[...]
        xo_ref[...] = xo
        if final:
            ms = jnp.mean(xo * xo, axis=-1, keepdims=True)
            y_ref[...] = xo * lax.rsqrt(ms + EPS) * fg_ref[...]


def _ffn(x, gdn_o, att_o, ssm_o, mod, w_out, g2, w_gate, w_up, conv_w, conv_b, w_down, conv_prev, final_g,
         nb, tt, tf, final):
    B, T, _ = x.shape
    nk = D_FF // tf
    grid = (B // nb, T // tt, nk)
    row_spec = lambda w: pl.BlockSpec((nb, tt, w), lambda b, t, k: (b, t, 0))
    in_specs = [row_spec(D_MODEL), row_spec(GDN_W), row_spec(ATT_W), row_spec(SSM_W),
                pl.BlockSpec((nb, 6, D_MODEL), lambda b, t, k: (b, 0, 0)),
                pl.BlockSpec((MIX_W, D_MODEL), lambda b, t, k: (0, 0)),
                pl.BlockSpec((1, D_MODEL), lambda b, t, k: (0, 0)),
                pl.BlockSpec((D_MODEL, tf), lambda b, t, k: (0, k)),
                pl.BlockSpec((D_MODEL, tf), lambda b, t, k: (0, k)),
                pl.BlockSpec((FFN_CONV_W, tf), lambda b, t, k: (0, k)),
                pl.BlockSpec((1, tf), lambda b, t, k: (0, k)),
                pl.BlockSpec((tf, D_MODEL), lambda b, t, k: (k, 0)),
                pl.BlockSpec((nb, FFN_CONV_W - 1, tf), lambda b, t, k: (b, 0, k)),
                pl.BlockSpec((1, D_MODEL), lambda b, t, k: (0, 0))]
    out_shape = [jax.ShapeDtypeStruct((B, T, D_MODEL), F32)]
    out_specs = [row_spec(D_MODEL)]
    if final:
        out_shape.append(jax.ShapeDtypeStruct((B, T, D_MODEL), F32))
        out_specs.append(row_spec(D_MODEL))
    out_shape.append(jax.ShapeDtypeStruct((B, FFN_CONV_W - 1, D_FF), F32))
    out_specs.append(pl.BlockSpec((nb, FFN_CONV_W - 1, D_FF), lambda b, t, k: (b, 0, 0)))
    return pl.pallas_call(
        functools.partial(_ffn_kernel, nb=nb, tt=tt, nk=nk, final=final),
        grid=grid,
        in_specs=in_specs,
        out_specs=tuple(out_specs),
        out_shape=tuple(out_shape),
        scratch_shapes=[pltpu.VMEM((nb * tt, D_MODEL), BF16),
                        pltpu.VMEM((nb, tt, D_MODEL), F32),
                        pltpu.VMEM((nb, SUBLANES + tt, tf), F32),
                        pltpu.VMEM((nk, nb, SUBLANES, tf), F32)],
        compiler_params=_params(("arbitrary", "arbitrary", "arbitrary")),
        name="ffn",
    )(x, gdn_o, att_o, ssm_o, mod, w_out, g2, w_gate, w_up, conv_w, conv_b.reshape(1, -1), w_down,
      conv_prev, final_g)


def _rope_tables(pos):
    half = ATT_DH // 2
    inv_freq = ROPE_THETA ** (-jnp.arange(half, dtype=F32) / half)
    ang = pos.astype(F32)[:, None] * inv_freq[None, :]
    cos = jnp.cos(ang)
    sin = jnp.sin(ang)
    cos_h = jnp.concatenate([cos, cos], axis=-1)
    sin_h = jnp.concatenate([-sin, sin], axis=-1)
    ones = jnp.ones((pos.shape[0], LANES - ATT_DH), F32)
    cos_t = jnp.concatenate([jnp.tile(cos_h, (1, ATT_HEADS)), cos_h, ones], axis=-1)
    sin_t = jnp.concatenate([jnp.tile(sin_h, (1, ATT_HEADS)), sin_h, 0.0 * ones], axis=-1)
    return cos_t, sin_t


def _layer(x, mod, pos_tables, cache, states, lw, tiles, final, final_g):
    (norm1_g, w_perm, gdn_conv_w, gdn_A_log, gdn_dt_bias, gdn_norm_g, ssm_conv_w, ssm_conv_b, ssm_A_log,
     ssm_dt_bias, ssm_D, ssm_norm_g, w_out, norm2_g, w_gate, w_up, ffn_conv_w, ffn_conv_b, w_down) = lw
    gdn_conv_prev, gdn_s0, ssm_conv_prev, ssm_h0, ffn_conv_prev = states
    B, T, _ = x.shape
    nb, tt, tq, ffn_tt, tf = tiles
    cos_t, sin_t = pos_tables
    gqkv, ggate, q, k, v, qi, kis, z, xbc = _inproj(x, mod, norm1_g.reshape(1, -1), w_perm, cos_t, sin_t, nb, tt)
    ki = kis[..., :IDX_DH]

    gdn_o, gdn_conv_new, gdn_s = _gdn(gqkv, kis, ggate, gdn_conv_w, gdn_conv_prev, gdn_s0, gdn_A_log,
                                      gdn_dt_bias, gdn_norm_g)
    ssm_o, ssm_conv_new, ssm_h = _ssd(xbc, z, kis, ssm_conv_w, ssm_conv_b, ssm_conv_prev, ssm_h0, ssm_A_log,
                                      ssm_dt_bias, ssm_D, ssm_norm_g)
    if cache is None:
        att_o = _dsa(q, qi, kis, k, v, ki, tq, T, 0, min(TOPK_MAX, T // 4))
    else:
        ck, cv, cki = cache
        P = ck.shape[1]
        L = P + T
        s_pad = -(-L // LANES) * LANES
        pad = s_pad - L
        k_all = jnp.concatenate([ck.reshape(B, P, ATT_W), k, jnp.zeros((B, pad, ATT_W), F32)], axis=1)
        v_all = jnp.concatenate([cv.reshape(B, P, ATT_W), v, jnp.zeros((B, pad, ATT_W), F32)], axis=1)
        ki_all = jnp.concatenate([cki, ki, jnp.zeros((B, pad, IDX_DH), F32)], axis=1)
        att_o = _dsa(q, qi, kis, k_all, v_all, ki_all, tq, L, P, min(TOPK_MAX, L // 4))

    res = _ffn(x, gdn_o, att_o, ssm_o, mod, w_out, norm2_g.reshape(1, -1), w_gate, w_up, ffn_conv_w, ffn_conv_b,
               w_down, ffn_conv_prev, final_g.reshape(1, -1), ffn_tt[0], ffn_tt[1], tf, final)
    if final:
        x_new, y, ffn_conv_new = res
    else:
        (x_new, ffn_conv_new), y = res, None
    st = (k.reshape(B, T, ATT_HEADS, ATT_DH), v.reshape(B, T, ATT_HEADS, ATT_DH), ki,
          gdn_conv_new, gdn_s, ssm_conv_new, ssm_h, ffn_conv_new)
    return x_new, y, st


def kernel(x_prompt, x_sample, c_prompt, c_sample, cache_k, cache_v, cache_kidx, state_gdn_conv, state_gdn,
           state_ssm_conv, state_ssm, state_ffn_conv, w_ada, b_ada, norm1_g, w_in, gdn_conv_w, gdn_A_log,
           gdn_dt_bias, gdn_norm_g, ssm_conv_w, ssm_conv_b, ssm_A_log, ssm_dt_bias, ssm_D, ssm_norm_g, w_out,
           norm2_g, w_gate, w_up, ffn_conv_w, ffn_conv_b, w_down, final_g):
    Bp, T, _ = x_prompt.shape
    Bs, Ts, _ = x_sample.shape
    P = cache_k.shape[2]

    c_all = jnp.concatenate([c_prompt, c_sample], axis=0)
    mod_all = _ada(c_all, w_ada, b_ada).reshape(DEPTH, Bp + Bs, 6, D_MODEL)

    tables_p = _rope_tables(jnp.arange(T))
    tables_s = _rope_tables(P + jnp.arange(Ts))

    perm = jnp.asarray(np.maximum(_PERM_COLS, 0))
    keep = jnp.asarray((_PERM_COLS >= 0).astype(np.float32))
    w_perm = (jnp.take(w_in, perm, axis=2) * keep).astype(BF16)
    w_out_b = w_out.astype(BF16)
    w_gate_b = w_gate.astype(BF16)
    w_up_b = w_up.astype(BF16)
    w_down_b = w_down.astype(BF16)

    zeros_p = (jnp.zeros((Bp, CONV_W - 1, GDN_CONV_C), F32),
               jnp.zeros((Bp, GDN_HEADS, GDN_DK, GDN_DV), F32),
               jnp.zeros((Bp, CONV_W - 1, SSM_CONV_C), F32),
               jnp.zeros((Bp, SSM_HEADS, SSM_P, SSM_N), F32),
               jnp.zeros((Bp, FFN_CONV_W - 1, D_FF), F32))

    tf = D_FF // 2
    tiles_p = (1, min(256, T), min(128, T), (1, min(512, T)), tf)
    tiles_s = (Bs, Ts, Ts, (Bs, Ts), tf)

    xp, xs = x_prompt, x_sample
    new_p, new_s = [], []
    yp = ys = None
    for l in range(DEPTH):
        lw = (norm1_g[l], w_perm[l], gdn_conv_w[l], gdn_A_log[l], gdn_dt_bias[l], gdn_norm_g[l], ssm_conv_w[l],
              ssm_conv_b[l], ssm_A_log[l], ssm_dt_bias[l], ssm_D[l], ssm_norm_g[l], w_out_b[l], norm2_g[l],
              w_gate_b[l], w_up_b[l], ffn_conv_w[l], ffn_conv_b[l], w_down_b[l])
        final = l == DEPTH - 1
        xp, yp, st_p = _layer(xp, mod_all[l, :Bp], tables_p, None, zeros_p, lw, tiles_p, final, final_g)
        states_s = (state_gdn_conv[l], state_gdn[l], state_ssm_conv[l], state_ssm[l], state_ffn_conv[l])
        xs, ys, st_s = _layer(xs, mod_all[l, Bp:], tables_s, (cache_k[l], cache_v[l], cache_kidx[l]), states_s,
                              lw, tiles_s, final, final_g)
        new_p.append(st_p)
        new_s.append(st_s)
    outs_p = [jnp.stack([st[i] for st in new_p]) for i in range(8)]
    outs_s = [jnp.stack([st[i] for st in new_s]) for i in range(8)]
    return (yp, ys, *outs_p, *outs_s)
```

```python
import functools
import math

import jax
import jax.numpy as jnp
import numpy as np
from jax import lax
from jax.experimental import pallas as pl
from jax.experimental.pallas import tpu as pltpu

F32 = jnp.float32
BF16 = jnp.bfloat16
HI = lax.Precision.HIGHEST

D_MODEL = 1024
DEPTH = 2
CHUNK = 64
CONV_W = 4
FFN_CONV_W = 3
D_FF = 2816
ROPE_THETA = 10000.0
EPS = 1e-6
GDN_HEADS = 4
GDN_DK = 64
GDN_DV = 64
ATT_HEADS = 4
ATT_DH = 64
IDX_HEADS = 4
IDX_DH = 64
TOPK_MAX = 256
SSM_HEADS = 8
SSM_P = 64
SSM_GROUPS = 2
SSM_N = 128
GDN_W = GDN_HEADS * GDN_DV
ATT_W = ATT_HEADS * ATT_DH
SSM_W = SSM_HEADS * SSM_P
MIX_W = GDN_W + ATT_W + SSM_W
GDN_CONV_C = 2 * GDN_HEADS * GDN_DK + GDN_W
SSM_CONV_C = SSM_W + 2 * SSM_GROUPS * SSM_N
IN_SIZES = (GDN_CONV_C, GDN_HEADS, GDN_HEADS, GDN_W,
            ATT_W, ATT_W, ATT_W, IDX_HEADS * IDX_DH, IDX_DH, IDX_HEADS,
            SSM_W, SSM_CONV_C, SSM_HEADS)
IN_W = sum(IN_SIZES)

LANES = 128
SUBLANES = 8
VMEM_LIMIT = 56 * 1024 * 1024

KIS_A = IDX_DH
KIS_B = KIS_A + GDN_HEADS
KIS_WI = KIS_B + GDN_HEADS
KIS_DT = KIS_WI + IDX_HEADS
KIS_END = KIS_DT + SSM_HEADS
OUT_SEGS = (("gqkv", GDN_CONV_C), ("ggate", GDN_W), ("q", ATT_W), ("k", ATT_W), ("v", ATT_W),
            ("qi", IDX_HEADS * IDX_DH), ("kis", LANES), ("z", SSM_W), ("xbc", SSM_CONV_C))
PERM_W = sum(w for _, w in OUT_SEGS)
ROT_W = ATT_W + LANES


def _perm_columns():
    starts = np.concatenate([[0], np.cumsum(IN_SIZES)])
    (s_gqkv, s_ga, s_gb, s_gg, s_q, s_k, s_v, s_qi, s_ki, s_wi, s_z, s_xbc, s_dt) = starts[:-1]
    cols = []
    cols += list(range(s_gqkv, s_gqkv + GDN_CONV_C))
    cols += list(range(s_gg, s_gg + GDN_W))
    cols += list(range(s_q, s_q + ATT_W))
    cols += list(range(s_k, s_k + ATT_W))
    cols += list(range(s_v, s_v + ATT_W))
    cols += list(range(s_qi, s_qi + IDX_HEADS * IDX_DH))
    kis = (list(range(s_ki, s_ki + IDX_DH)) + list(range(s_ga, s_ga + GDN_HEADS))
           + list(range(s_gb, s_gb + GDN_HEADS)) + list(range(s_wi, s_wi + IDX_HEADS))
           + list(range(s_dt, s_dt + SSM_HEADS)))
    cols += kis + [-1] * (LANES - len(kis))
    cols += list(range(s_z, s_z + SSM_W))
    cols += list(range(s_xbc, s_xbc + SSM_CONV_C))
    assert len(cols) == PERM_W
    return np.asarray(cols, np.int32)


_PERM_COLS = _perm_columns()


def _silu(x):
    return x * jax.nn.sigmoid(x)


def _softplus(x):
    return jnp.maximum(x, 0.0) + jnp.log1p(jnp.exp(-jnp.abs(x)))


def _dot(a, b, precision=None):
    return jnp.dot(a, b, preferred_element_type=F32, precision=precision)


def _dot_nt(a, b, precision=None):
    return lax.dot_general(a, b, (((1,), (1,)), ((), ())), preferred_element_type=F32,
                           precision=precision)


def _dot_tn(a, b, precision=None):
    return lax.dot_general(a, b, (((0,), (0,)), ((), ())), preferred_element_type=F32,
                           precision=precision)


def _cumsum_rows(x):
    c = x.shape[0]
    row = lax.broadcasted_iota(jnp.int32, x.shape, 0)
    s = 1
    while s < c:
        x = x + jnp.where(row >= s, pltpu.roll(x, s, axis=0), 0.0)
        s *= 2
    return x


def _mm(a, b):
    return _dot(a.astype(BF16), b.astype(BF16))


def _mm_nt(a, b):
    return _dot_nt(a.astype(BF16), b.astype(BF16))


def _split(a):
    hi = a.astype(BF16)
    return hi, (a - hi.astype(F32)).astype(BF16)


def _mm3(a, b):
    (ah, al), (bh, bl) = a, b
    return _dot(ah, bh) + (_dot(ah, bl) + _dot(al, bh))


def _transpose_rows(vals):
    c = vals.shape[0]
    if c < LANES:
        vals = jnp.concatenate([vals, jnp.zeros((LANES - c, LANES), vals.dtype)], axis=0)
    return vals.T[:, :c]


def _lane_vector(vals, start):
    return jnp.zeros((1, LANES), F32).at[0, start:start + vals.shape[0]].set(vals.astype(F32))


def _params(sem):
    return pltpu.CompilerParams(dimension_semantics=sem, vmem_limit_bytes=VMEM_LIMIT)


def _ada_kernel(c_ref, w_ref, b_ref, o_ref):
    s = _silu(c_ref[...])
    o_ref[0] = _dot(s.astype(BF16), w_ref[0].astype(BF16)) + b_ref[0]


def _ada(c_all, w_ada, b_ada):
    rows = c_all.shape[0]
    n = w_ada.shape[2]
    tn = 1536
    return pl.pallas_call(
        _ada_kernel,
        grid=(DEPTH, n // tn),
        in_specs=[pl.BlockSpec((rows, D_MODEL), lambda l, j: (0, 0)),
                  pl.BlockSpec((1, D_MODEL, tn), lambda l, j: (l, 0, j)),
                  pl.BlockSpec((1, 1, tn), lambda l, j: (l, 0, j))],
        out_specs=pl.BlockSpec((1, rows, tn), lambda l, j: (l, 0, j)),
        out_shape=jax.ShapeDtypeStruct((DEPTH, rows, n), F32),
        compiler_params=_params(("arbitrary", "arbitrary")),
        name="ada",
    )(c_all, w_ada, b_ada.reshape(DEPTH, 1, n))


def _rotate(x, cos, sin_signed):
    w = x.shape[-1]
    lane = lax.broadcasted_iota(jnp.int32, x.shape, x.ndim - 1)
    first = (lane % ATT_DH) < (ATT_DH // 2)
    swapped = jnp.where(first, pltpu.roll(x, w - ATT_DH // 2, axis=x.ndim - 1),
                        pltpu.roll(x, ATT_DH // 2, axis=x.ndim - 1))
    return x * cos + swapped * sin_signed


def _inproj_kernel(x_ref, mod_ref, g_ref, w_ref, cos_ref, sin_ref, *out_refs, nb, tt):
    x = x_ref[...]
    ms = jnp.mean(x * x, axis=-1, keepdims=True)
    xn = x * lax.rsqrt(ms + EPS) * g_ref[...]
    mod = mod_ref[...]
    h = xn * (1.0 + mod[:, 1:2, :]) + mod[:, 0:1, :]
    u = _dot(h.reshape(nb * tt, D_MODEL).astype(BF16), w_ref[...])
    cos = cos_ref[...]
    sin = sin_ref[...]
    off = 0
    for (name, width), o_ref in zip(OUT_SEGS, out_refs):
        seg = u[:, off:off + width]
        if name in ("q", "k", "qi", "kis"):
            t0 = ATT_W if name == "kis" else 0
            c3 = cos[:, t0:t0 + width][None]
            s3 = sin[:, t0:t0 + width][None]
            if nb > 1:
                c3 = jnp.broadcast_to(c3, (nb, tt, width)).reshape(nb * tt, width)
                s3 = jnp.broadcast_to(s3, (nb, tt, width)).reshape(nb * tt, width)
            else:
                c3 = c3[0]
                s3 = s3[0]
            seg = _rotate(seg, c3, s3)
        o_ref[...] = seg.reshape(nb, tt, width)
        off += width


def _inproj(x, mod, g1, w_perm, cos, sin, nb, tt):
    B, T, _ = x.shape
    grid = (B // nb, T // tt)
    out_shape = tuple(jax.ShapeDtypeStruct((B, T, w), F32) for _, w in OUT_SEGS)
    out_specs = tuple(pl.BlockSpec((nb, tt, w), lambda b, t: (b, t, 0)) for _, w in OUT_SEGS)
    return pl.pallas_call(
        functools.partial(_inproj_kernel, nb=nb, tt=tt),
        grid=grid,
        in_specs=[pl.BlockSpec((nb, tt, D_MODEL), lambda b, t: (b, t, 0)),
                  pl.BlockSpec((nb, 6, D_MODEL), lambda b, t: (b, 0, 0)),
                  pl.BlockSpec((1, D_MODEL), lambda b, t: (0, 0)),
                  pl.BlockSpec((D_MODEL, PERM_W), lambda b, t: (0, 0)),
                  pl.BlockSpec((tt, ROT_W), lambda b, t: (t, 0)),
                  pl.BlockSpec((tt, ROT_W), lambda b, t: (t, 0))],
        out_specs=out_specs,
        out_shape=out_shape,
        compiler_params=_params(("arbitrary", "arbitrary")),
        name="inproj",
    )(x, mod, g1, w_perm, cos, sin)


def _short_conv(ext_ref, bi, u, w, c):
    lo = SUBLANES - (CONV_W - 1)
    ext_ref[bi, SUBLANES:SUBLANES + c, :] = u
    y = ext_ref[bi, lo:lo + c, :] * w[0:1, :]
    for j in range(1, CONV_W):
        y = y + ext_ref[bi, lo + j:lo + j + c, :] * w[j:j + 1, :]
    tail = ext_ref[bi, c + lo:c + SUBLANES, :]
    ext_ref[bi, lo:SUBLANES, :] = tail
    return y, tail


def _neumann_inverses(mats, c):
    eye = (lax.broadcasted_iota(jnp.int32, (c, c), 0)
           == lax.broadcasted_iota(jnp.int32, (c, c), 1)).astype(F32)
    ps = [eye - a for a in mats]
    sp = [_split(a) for a in mats]
    pws = [_mm3(s, s) for s in sp]
    n = 2
    while n < c:
        sp = [_split(pw) for pw in pws]
        ps = [p + _mm3(_split(p), s) for p, s in zip(ps, sp)]
        n *= 2
        if n < c:
            pws = [_mm3(s, s) for s in sp]
    return ps


def _gdn_kernel(qkv_ref, kis_ref, gate_ref, w_ref, prev_ref, s0_ref, alog_ref, dtb_ref, ng_ref,
                o_ref, new_ref, s_ref, ext_ref, s_sc, *, c, nbb):
    j = pl.program_id(1)
    lo = SUBLANES - (CONV_W - 1)

    @pl.when(j == 0)
    def _():
        s_sc[...] = s0_ref[...]
        ext_ref[:, lo:SUBLANES, :] = prev_ref[...]

    ng = ng_ref[...]
    w_conv = w_ref[...]
    ri = lax.broadcasted_iota(jnp.int32, (c, c), 0)
    ci = lax.broadcasted_iota(jnp.int32, (c, c), 1)
    incl = ri >= ci
    strict = ri > ci
    nkd = GDN_HEADS * GDN_DK
    ch = []
    for bi in range(nbb):
        y, tail = _short_conv(ext_ref, bi, qkv_ref[bi], w_conv, c)
        new_ref[bi] = tail
        y = _silu(y)
        small = kis_ref[bi]
        g = -jnp.exp(alog_ref[...]) * _softplus(small + dtb_ref[...])
        beta = jax.nn.sigmoid(small)
        gc = _cumsum_rows(g)
        gc_t = _transpose_rows(gc)
        gate = gate_ref[bi]
        for h in range(GDN_HEADS):
            qh = y[:, h * GDN_DK:(h + 1) * GDN_DK]
            kh = y[:, nkd + h * GDN_DK:nkd + (h + 1) * GDN_DK]
            vh = y[:, 2 * nkd + h * GDN_DV:2 * nkd + (h + 1) * GDN_DV]
            qh = qh * lax.rsqrt(jnp.sum(qh * qh, axis=-1, keepdims=True) + EPS) * (GDN_DK ** -0.5)
            kh = kh * lax.rsqrt(jnp.sum(kh * kh, axis=-1, keepdims=True) + EPS)
            ch.append((qh, kh, vh, gc[:, KIS_A + h:KIS_A + h + 1], gc_t[KIS_A + h:KIS_A + h + 1, :],
                       beta[:, KIS_B + h:KIS_B + h + 1], gate[:, h * GDN_DV:(h + 1) * GDN_DV]))
    ids = [(bi, h) for bi in range(nbb) for h in range(GDN_HEADS)]
    decay = [jnp.where(incl, jnp.exp(jnp.where(incl, gcol - grow, 0.0)), 0.0)
             for (_, _, _, gcol, grow, _, _) in ch]
    kb = [kh * bcol for (_, kh, _, _, _, bcol, _) in ch]
    eg = [jnp.exp(gcol) for (_, _, _, gcol, _, _, _) in ch]
    kk = [_mm_nt(kbi, x[1]) for kbi, x in zip(kb, ch)]
    qk = [_mm_nt(x[0], x[1]) for x in ch]
    t_inv = _neumann_inverses([jnp.where(strict, m * d, 0.0) for m, d in zip(kk, decay)], c)
    attn = [jnp.where(incl, m * d, 0.0) for m, d in zip(qk, decay)]
    rhs = [jnp.concatenate([x[2] * x[5], kbi * e], axis=1) for x, kbi, e in zip(ch, kb, eg)]
    sol = [_mm(t, r) for t, r in zip(t_inv, rhs)]
    st = [s_sc[bi, h] for bi, h in ids]
    ws = [_mm(sl[:, GDN_DV:], s) for sl, s in zip(sol, st)]
    qs = [_mm(x[0] * e, s) for x, e, s in zip(ch, eg, st)]
    v_new = [sl[:, :GDN_DV] - w for sl, w in zip(sol, ws)]
    av = [_mm(a, v) for a, v in zip(attn, v_new)]
    g_last = [x[3][c - 1:c, :] for x in ch]
    kv = [_dot_tn(x[1] * jnp.exp(gl - x[3]), v) for x, gl, v in zip(ch, g_last, v_new)]
    for (bi, h), x, s, gl, kvi, qsi, avi in zip(ids, ch, st, g_last, kv, qs, av):
        s_new = s * jnp.exp(gl) + kvi
        s_sc[bi, h] = s_new
        s_ref[bi, h] = s_new
        o = qsi + avi
        o = o * lax.rsqrt(jnp.mean(o * o, axis=-1, keepdims=True) + EPS) * ng
        o_ref[bi, :, h * GDN_DV:(h + 1) * GDN_DV] = o * _silu(x[6])


def _gdn(gqkv, kis, ggate, conv_w, conv_prev, s0, a_log, dt_bias, norm_g, nbb):
    B, T, _ = gqkv.shape
    c = math.gcd(T, CHUNK)
    return pl.pallas_call(
        functools.partial(_gdn_kernel, c=c, nbb=nbb),
        grid=(B // nbb, T // c),
        in_specs=[pl.BlockSpec((nbb, c, GDN_CONV_C), lambda b, j: (b, j, 0)),
                  pl.BlockSpec((nbb, c, LANES), lambda b, j: (b, j, 0)),
                  pl.BlockSpec((nbb, c, GDN_W), lambda b, j: (b, j, 0)),
                  pl.BlockSpec((CONV_W, GDN_CONV_C), lambda b, j: (0, 0)),
                  pl.BlockSpec((nbb, CONV_W - 1, GDN_CONV_C), lambda b, j: (b, 0, 0)),
                  pl.BlockSpec((nbb, GDN_HEADS, GDN_DK, GDN_DV), lambda b, j: (b, 0, 0, 0)),
                  pl.BlockSpec((1, LANES), lambda b, j: (0, 0)),
                  pl.BlockSpec((1, LANES), lambda b, j: (0, 0)),
                  pl.BlockSpec((1, GDN_DV), lambda b, j: (0, 0))],
        out_specs=(pl.BlockSpec((nbb, c, GDN_W), lambda b, j: (b, j, 0)),
                   pl.BlockSpec((nbb, CONV_W - 1, GDN_CONV_C), lambda b, j: (b, 0, 0)),
                   pl.BlockSpec((nbb, GDN_HEADS, GDN_DK, GDN_DV), lambda b, j: (b, 0, 0, 0))),
        out_shape=(jax.ShapeDtypeStruct((B, T, GDN_W), F32),
                   jax.ShapeDtypeStruct((B, CONV_W - 1, GDN_CONV_C), F32),
                   jax.ShapeDtypeStruct((B, GDN_HEADS, GDN_DK, GDN_DV), F32)),
        scratch_shapes=[pltpu.VMEM((nbb, SUBLANES + c, GDN_CONV_C), F32),
                        pltpu.VMEM((nbb, GDN_HEADS, GDN_DK, GDN_DV), F32)],
        compiler_params=_params(("arbitrary", "arbitrary")),
        name="gdn",
    )(gqkv, kis, ggate, conv_w, conv_prev, s0, _lane_vector(a_log, KIS_A), _lane_vector(dt_bias, KIS_A),
      norm_g.reshape(1, -1))


def _ssd_kernel(xbc_ref, z_ref, kis_ref, w_ref, cb_ref, prev_ref, h0_ref, alog_ref, dtb_ref, dsk_ref,
                ng_ref, o_ref, new_ref, h_ref, ext_ref, h_sc, *, c, nbb):
    j = pl.program_id(1)
    lo = SUBLANES - (CONV_W - 1)

    @pl.when(j == 0)
    def _():
        h_sc[...] = h0_ref[...]
        ext_ref[:, lo:SUBLANES, :] = prev_ref[...]

    dsk = dsk_ref[...]
    ng = ng_ref[...]
    w_conv = w_ref[...]
    ri = lax.broadcasted_iota(jnp.int32, (c, c), 0)
    ci = lax.broadcasted_iota(jnp.int32, (c, c), 1)
    incl = ri >= ci
    gn = SSM_GROUPS * SSM_N
    rep = SSM_HEADS // SSM_GROUPS
    gw = SSM_W // SSM_GROUPS
    grp_in = []
    ch = []
    ids = []
    for bi in range(nbb):
        y, tail = _short_conv(ext_ref, bi, xbc_ref[bi], w_conv, c)
        new_ref[bi] = tail
        y = _silu(y + cb_ref[...])
        dts = _softplus(kis_ref[bi] + dtb_ref[...])
        a = dts * (-jnp.exp(alog_ref[...]))
        acum = _cumsum_rows(a)
        acum_t = _transpose_rows(acum)
        zg = _silu(z_ref[bi])
        for grp in range(SSM_GROUPS):
            grp_in.append((y[:, SSM_W + grp * SSM_N:SSM_W + (grp + 1) * SSM_N],
                           y[:, SSM_W + gn + grp * SSM_N:SSM_W + gn + (grp + 1) * SSM_N],
                           zg[:, grp * gw:(grp + 1) * gw]))
            for hh in range(rep):
                h = grp * rep + hh
                ch.append((y[:, h * SSM_P:(h + 1) * SSM_P], acum[:, KIS_DT + h:KIS_DT + h + 1],
                           acum_t[KIS_DT + h:KIS_DT + h + 1, :], dts[:, KIS_DT + h:KIS_DT + h + 1],
                           len(grp_in) - 1))
                ids.append((bi, h))
    cb = [_mm_nt(cg, bg) for bg, cg, _ in grp_in]
    seg = [jnp.where(incl, jnp.exp(jnp.where(incl, acol - arow, 0.0)), 0.0) for (_, acol, arow, _, _) in ch]
    hs = [h_sc[bi, h] for bi, h in ids]
    y_in = [_mm(cb[x[4]] * sg, x[0] * x[3]) for x, sg in zip(ch, seg)]
    y_st = [_mm_nt(grp_in[x[4]][1], hst) for x, hst in zip(ch, hs)]
    a_last = [x[1][c - 1:c, :] for x in ch]
    upd = [_dot_tn(x[0], grp_in[x[4]][0] * (jnp.exp(al - x[1]) * x[3])) for x, al in zip(ch, a_last)]
    outs = []
    for (bi, h), x, hst, al, yi, ys, up in zip(ids, ch, hs, a_last, y_in, y_st, upd):
        h_new = hst * jnp.exp(al) + up
        h_sc[bi, h] = h_new
        h_ref[bi, h] = h_new
        outs.append(yi + ys * jnp.exp(x[1]) + dsk[:, h:h + 1] * x[0])
    for gi, (_, _, zslice) in enumerate(grp_in):
        bi, grp = divmod(gi, SSM_GROUPS)
        yg = jnp.concatenate(outs[gi * rep:(gi + 1) * rep], axis=1) * zslice
        yg = yg * lax.rsqrt(jnp.mean(yg * yg, axis=-1, keepdims=True) + EPS)
        o_ref[bi, :, grp * gw:(grp + 1) * gw] = yg * ng[:, grp * gw:(grp + 1) * gw]


def _ssd(xbc, z, kis, conv_w, conv_b, conv_prev, h0, a_log, dt_bias, d_skip, norm_g, nbb):
    B, T, _ = xbc.shape
    c = math.gcd(T, CHUNK)
    return pl.pallas_call(
        functools.partial(_ssd_kernel, c=c, nbb=nbb),
        grid=(B // nbb, T // c),
        in_specs=[pl.BlockSpec((nbb, c, SSM_CONV_C), lambda b, j: (b, j, 0)),
                  pl.BlockSpec((nbb, c, SSM_W), lambda b, j: (b, j, 0)),
                  pl.BlockSpec((nbb, c, LANES), lambda b, j: (b, j, 0)),
                  pl.BlockSpec((CONV_W, SSM_CONV_C), lambda b, j: (0, 0)),
                  pl.BlockSpec((1, SSM_CONV_C), lambda b, j: (0, 0)),
                  pl.BlockSpec((nbb, CONV_W - 1, SSM_CONV_C), lambda b, j: (b, 0, 0)),
                  pl.BlockSpec((nbb, SSM_HEADS, SSM_P, SSM_N), lambda b, j: (b, 0, 0, 0)),
                  pl.BlockSpec((1, LANES), lambda b, j: (0, 0)),
                  pl.BlockSpec((1, LANES), lambda b, j: (0, 0)),
                  pl.BlockSpec((1, SSM_HEADS), lambda b, j: (0, 0)),
                  pl.BlockSpec((1, SSM_W), lambda b, j: (0, 0))],
        out_specs=(pl.BlockSpec((nbb, c, SSM_W), lambda b, j: (b, j, 0)),
                   pl.BlockSpec((nbb, CONV_W - 1, SSM_CONV_C), lambda b, j: (b, 0, 0)),
                   pl.BlockSpec((nbb, SSM_HEADS, SSM_P, SSM_N), lambda b, j: (b, 0, 0, 0))),
        out_shape=(jax.ShapeDtypeStruct((B, T, SSM_W), F32),
                   jax.ShapeDtypeStruct((B, CONV_W - 1, SSM_CONV_C), F32),
                   jax.ShapeDtypeStruct((B, SSM_HEADS, SSM_P, SSM_N), F32)),
        scratch_shapes=[pltpu.VMEM((nbb, SUBLANES + c, SSM_CONV_C), F32),
                        pltpu.VMEM((nbb, SSM_HEADS, SSM_P, SSM_N), F32)],
        compiler_params=_params(("arbitrary", "arbitrary")),
        name="ssd",
    )(xbc, z, kis, conv_w, conv_b.reshape(1, -1), conv_prev, h0, _lane_vector(a_log, KIS_DT),
      _lane_vector(dt_bias, KIS_DT), d_skip.reshape(1, -1), norm_g.reshape(1, -1))


IDX_BITS = 12


SEARCH_STEPS = 5
SEARCH_ROUNDS = 64


def _count(mask):
    return jnp.sum(mask.astype(F32), axis=0, keepdims=True)


def _dsa_kernel(q_ref, qi_ref, kis_ref, k_ref, vt_ref, ki_ref, o_ref, *, tq, s_len, s_valid, q_offset, j0,
                n_valid_q, topk):
    j = pl.program_id(1)
    lane = lax.broadcasted_iota(jnp.int32, (1, tq), 1)
    qpos = q_offset + (j0 + j) * tq + lane
    lim = jnp.minimum((qpos // CHUNK + 1) * CHUNK, s_valid)
    row = lax.broadcasted_iota(jnp.int32, (s_len, tq), 0)
    adm = row < lim

    qi_t = qi_ref[0].T.astype(BF16)
    kis_t = kis_ref[0].T
    ki = ki_ref[0]
    score = jnp.zeros((s_len, tq), F32)
    for h in range(IDX_HEADS):
        rel = jnp.maximum(_dot(ki, qi_t[h * IDX_DH:(h + 1) * IDX_DH, :]), 0.0) * (IDX_DH ** -0.5)
        score = score + rel * kis_t[KIS_WI + h:KIS_WI + h + 1, :]
    score = score * (IDX_HEADS ** -0.5)
    s = jnp.where(adm, score, -jnp.inf)

    few = (_count(adm) < topk) | (lane >= n_valid_q)
    smax = jnp.max(s, axis=0, keepdims=True)
    smin = jnp.min(jnp.where(adm, score, jnp.inf), axis=0, keepdims=True)
    top_full = _count(s >= smax) >= topk
    lo0 = jnp.where(few, -jnp.inf, jnp.where(top_full, smax, smin))
    hi0 = jnp.where(few, -jnp.inf, smax)
    done0 = (few | top_full).astype(F32)

    def search_cond(c):
        return (jnp.min(c[2]) < 0.5) & (c[3] < SEARCH_ROUNDS)

    def search_body(c):
        lo, hi, done, it = c
        for _ in range(SEARCH_STEPS):
            mid = 0.5 * lo + 0.5 * hi
            ok = (mid > lo) & (mid < hi) & (done < 0.5)
            take = _count(s >= mid) >= topk
            lo = jnp.where(ok & take, mid, lo)
            hi = jnp.where(ok & jnp.logical_not(take), mid, hi)
        first = jnp.min(jnp.where(s >= lo, s, jnp.inf), axis=0, keepdims=True)
        last = jnp.max(jnp.where(s < hi, s, -jnp.inf), axis=0, keepdims=True)
        lo = jnp.where(done < 0.5, first, lo)
        done = jnp.maximum(done, (first >= last).astype(F32))
        return lo, hi, done, it + 1

    t, _, _, _ = lax.while_loop(search_cond, search_body, (lo0, hi0, done0, jnp.int32(0)))
    gt = s > t
    tie = s == t
    need = topk - _count(gt)
    excess = jnp.where(few, 0.0, _count(tie) - need)

    def tie_search():
        def ibody(i, m):
            cand = m | (jnp.int32(1) << (IDX_BITS - 1 - i))
            return jnp.where(_count(tie & (row < cand)) <= need, cand, m)

        return lax.fori_loop(0, IDX_BITS, ibody, jnp.zeros((1, tq), jnp.int32))

    m = lax.cond(jnp.max(excess) > 0.0, tie_search, lambda: jnp.full((1, tq), 2 ** IDX_BITS - 1, jnp.int32))
    sel = adm & (gt | (tie & (row < m)))

    q_t = q_ref[0].T
    head_of_row = lax.broadcasted_iota(jnp.int32, (ATT_W, tq), 0) // ATT_DH
    k = k_ref[0]
    outs = []
    for h in range(ATT_HEADS):
        qm = jnp.where(head_of_row == h, q_t, 0.0).astype(BF16)
        logits = _dot(k, qm) * (ATT_DH ** -0.5)
        logits = jnp.where(sel, logits, -jnp.inf)
        mx = jnp.max(logits, axis=0, keepdims=True)
        p = jnp.exp(logits - mx)
        den = jnp.sum(p, axis=0, keepdims=True)
        oh = _dot(vt_ref[0, h * ATT_DH:(h + 1) * ATT_DH, :], p.astype(BF16))
        outs.append(oh * (1.0 / den))
    o_ref[0] = jnp.concatenate(outs, axis=0).T


def _dsa(q, qi, kis, k_b, vt_b, ki_b, *, tq, j0, nj, s_len, s_valid, q_offset, n_valid_q, topk):
    B = q.shape[0]
    assert s_len < 2 ** IDX_BITS and s_len <= k_b.shape[1]
    return pl.pallas_call(
        functools.partial(_dsa_kernel, tq=tq, s_len=s_len, s_valid=s_valid, q_offset=q_offset, j0=j0,
                          n_valid_q=n_valid_q, topk=topk),
        grid=(B, nj),
        in_specs=[pl.BlockSpec((1, tq, ATT_W), lambda b, j: (b, j0 + j, 0)),
                  pl.BlockSpec((1, tq, IDX_HEADS * IDX_DH), lambda b, j: (b, j0 + j, 0)),
                  pl.BlockSpec((1, tq, LANES), lambda b, j: (b, j0 + j, 0)),
                  pl.BlockSpec((1, s_len, ATT_W), lambda b, j: (b, 0, 0)),
                  pl.BlockSpec((1, ATT_W, s_len), lambda b, j: (b, 0, 0)),
                  pl.BlockSpec((1, s_len, IDX_DH), lambda b, j: (b, 0, 0))],
        out_specs=pl.BlockSpec((1, tq, ATT_W), lambda b, j: (b, j, 0)),
        out_shape=jax.ShapeDtypeStruct((B, nj * tq, ATT_W), F32),
        compiler_params=_params(("arbitrary", "arbitrary")),
        name="dsa",
    )(q, qi, kis, k_b, vt_b, ki_b)


def _dsa_group(q, qi, kis, k_b, vt_b, ki_b, *, s_valid, q_offset, topk):
    B, T, _ = q.shape
    tq = LANES
    n_valid_q = min(T, tq)
    if T < tq:
        padq = lambda a: jnp.pad(a, ((0, 0), (0, tq - T), (0, 0)))
        q, qi, kis = padq(q), padq(qi), padq(kis)
    nq = q.shape[1] // tq
    s_total = k_b.shape[1]
    outs = []
    j0 = 0
    while j0 < nq:
        nj = min(2, nq - j0)
        reach = -(-(q_offset + (j0 + nj) * tq) // CHUNK) * CHUNK
        s_len = min(s_total, -(-min(reach, s_valid) // LANES) * LANES)
        outs.append(_dsa(q, qi, kis, k_b, vt_b, ki_b, tq=tq, j0=j0, nj=nj, s_len=s_len, s_valid=s_valid,
                         q_offset=q_offset, n_valid_q=n_valid_q, topk=topk))
        j0 += nj
    out = outs[0] if len(outs) == 1 else jnp.concatenate(outs, axis=1)
    return out[:, :T]


def _ffn_kernel(x_ref, gdn_ref, att_ref, ssm_ref, mod_ref, wo_ref, g2_ref, wg_ref, wu_ref, cw_ref, cb_ref,
                wd_ref, prev_ref, fg_ref, *rest, nb, tt, nk, final):
    if final:
        xo_ref, y_ref, new_ref, h2_sc, acc_sc, buf_sc, carry_sc = rest
    else:
        xo_ref, new_ref, h2_sc, acc_sc, buf_sc, carry_sc = rest
        y_ref = None
    t = pl.program_id(1)
    k = pl.program_id(2)
    rows = nb * tt
    lo = SUBLANES - (FFN_CONV_W - 1)
    mod = mod_ref[...]

    @pl.when(k == 0)
    def _():
        mix = jnp.concatenate([gdn_ref[...], att_ref[...], ssm_ref[...]], axis=-1)
        proj = _dot(mix.reshape(rows, MIX_W).astype(BF16), wo_ref[...]).reshape(nb, tt, D_MODEL)
        x1 = x_ref[...] + mod[:, 2:3, :] * proj
        acc_sc[...] = x1
        ms = jnp.mean(x1 * x1, axis=-1, keepdims=True)
        h2 = x1 * lax.rsqrt(ms + EPS) * g2_ref[...]
        h2 = h2 * (1.0 + mod[:, 4:5, :]) + mod[:, 3:4, :]
        h2_sc[...] = h2.reshape(rows, D_MODEL).astype(BF16)

    @pl.when(t == 0)
    def _():
        carry_sc[k, :, lo:SUBLANES, :] = prev_ref[...]

    h2 = h2_sc[...]
    ag = _dot(h2, wg_ref[...])
    up = _dot(h2, wu_ref[...])
    tf = ag.shape[-1]
    buf_sc[:, lo:SUBLANES, :] = carry_sc[k, :, lo:SUBLANES, :]
    buf_sc[:, SUBLANES:SUBLANES + tt, :] = ag.reshape(nb, tt, tf)
    cw = cw_ref[...]
    conv = buf_sc[:, lo:lo + tt, :] * cw[0:1, :][None]
    for jj in range(1, FFN_CONV_W):
        conv = conv + buf_sc[:, lo + jj:lo + jj + tt, :] * cw[jj:jj + 1, :][None]
    tail = buf_sc[:, tt + lo:tt + SUBLANES, :]
    carry_sc[k, :, lo:SUBLANES, :] = tail
    new_ref[:, pl.ds(k, 1), :, :] = tail[:, None]
    act = _silu(conv + cb_ref[...][None]).reshape(rows, tf) * up
    y = _dot(act.astype(BF16), wd_ref[...]).reshape(nb, tt, D_MODEL)
    acc_sc[...] += mod[:, 5:6, :] * y

    @pl.when(k == nk - 1)
    def _():
        xo = acc_sc[...]
        xo_ref[...] = xo
        if final:
            ms = jnp.mean(xo * xo, axis=-1, keepdims=True)
            y_ref[...] = xo * lax.rsqrt(ms + EPS) * fg_ref[...]


def _ffn(x, gdn_o, att_o, ssm_o, mod, w_out, g2, w_gate, w_up, conv_w, conv_b, w_down, conv_prev, final_g,
         nb, tt, tf, final):
    B, T, _ = x.shape
    nk = D_FF // tf
    grid = (B // nb, T // tt, nk)
    row_spec = lambda w: pl.BlockSpec((nb, tt, w), lambda b, t, k: (b, t, 0))
    in_specs = [row_spec(D_MODEL), row_spec(GDN_W), row_spec(ATT_W), row_spec(SSM_W),
                pl.BlockSpec((nb, 6, D_MODEL), lambda b, t, k: (b, 0, 0)),
                pl.BlockSpec((MIX_W, D_MODEL), lambda b, t, k: (0, 0)),
                pl.BlockSpec((1, D_MODEL), lambda b, t, k: (0, 0)),
                pl.BlockSpec((D_MODEL, tf), lambda b, t, k: (0, k)),
                pl.BlockSpec((D_MODEL, tf), lambda b, t, k: (0, k)),
                pl.BlockSpec((FFN_CONV_W, tf), lambda b, t, k: (0, k)),
                pl.BlockSpec((1, tf), lambda b, t, k: (0, k)),
                pl.BlockSpec((tf, D_MODEL), lambda b, t, k: (k, 0)),
                pl.BlockSpec((nb, FFN_CONV_W - 1, tf), lambda b, t, k: (b, 0, k)),
                pl.BlockSpec((1, D_MODEL), lambda b, t, k: (0, 0))]
    out_shape = [jax.ShapeDtypeStruct((B, T, D_MODEL), F32)]
    out_specs = [row_spec(D_MODEL)]
    if final:
        out_shape.append(jax.ShapeDtypeStruct((B, T, D_MODEL), F32))
        out_specs.append(row_spec(D_MODEL))
    out_shape.append(jax.ShapeDtypeStruct((B, nk, FFN_CONV_W - 1, tf), F32))
    out_specs.append(pl.BlockSpec((nb, nk, FFN_CONV_W - 1, tf), lambda b, t, k: (b, 0, 0, 0)))
    res = pl.pallas_call(
        functools.partial(_ffn_kernel, nb=nb, tt=tt, nk=nk, final=final),
        grid=grid,
        in_specs=in_specs,
        out_specs=tuple(out_specs),
        out_shape=tuple(out_shape),
        scratch_shapes=[pltpu.VMEM((nb * tt, D_MODEL), BF16),
                        pltpu.VMEM((nb, tt, D_MODEL), F32),
                        pltpu.VMEM((nb, SUBLANES + tt, tf), F32),
                        pltpu.VMEM((nk, nb, SUBLANES, tf), F32)],
        compiler_params=_params(("arbitrary", "arbitrary", "arbitrary")),
        name="ffn",
    )(x, gdn_o, att_o, ssm_o, mod, w_out, g2, w_gate, w_up, conv_w, conv_b.reshape(1, -1), w_down,
      conv_prev, final_g)
    tail = res[-1].transpose(0, 2, 1, 3).reshape(B, FFN_CONV_W - 1, D_FF)
    return (*res[:-1], tail)


def _rope_tables(pos):
    half = ATT_DH // 2
    inv_freq = ROPE_THETA ** (-jnp.arange(half, dtype=F32) / half)
    ang = pos.astype(F32)[:, None] * inv_freq[None, :]
    cos = jnp.cos(ang)
    sin = jnp.sin(ang)
    cos_h = jnp.concatenate([cos, cos], axis=-1)
    sin_h = jnp.concatenate([-sin, sin], axis=-1)
    ones = jnp.ones((pos.shape[0], LANES - ATT_DH), F32)
    cos_t = jnp.concatenate([jnp.tile(cos_h, (1, ATT_HEADS)), cos_h, ones], axis=-1)
    sin_t = jnp.concatenate([jnp.tile(sin_h, (1, ATT_HEADS)), sin_h, 0.0 * ones], axis=-1)
    return cos_t, sin_t


def _layer(x, mod, pos_tables, cache, states, lw, tiles, final, final_g):
    (norm1_g, w_perm, gdn_conv_w, gdn_A_log, gdn_dt_bias, gdn_norm_g, ssm_conv_w, ssm_conv_b, ssm_A_log,
     ssm_dt_bias, ssm_D, ssm_norm_g, w_out, norm2_g, w_gate, w_up, ffn_conv_w, ffn_conv_b, w_down) = lw
    gdn_conv_prev, gdn_s0, ssm_conv_prev, ssm_h0, ffn_conv_prev = states
    B, T, _ = x.shape
    nb, tt, nbb, ffn_tt, tf = tiles
    cos_t, sin_t = pos_tables
    gqkv, ggate, q, k, v, qi, kis, z, xbc = _inproj(x, mod, norm1_g.reshape(1, -1), w_perm, cos_t, sin_t, nb, tt)
    ki = kis[..., :IDX_DH]

    gdn_o, gdn_conv_new, gdn_s = _gdn(gqkv, kis, ggate, gdn_conv_w, gdn_conv_prev, gdn_s0, gdn_A_log,
                                      gdn_dt_bias, gdn_norm_g, nbb)
    ssm_o, ssm_conv_new, ssm_h = _ssd(xbc, z, kis, ssm_conv_w, ssm_conv_b, ssm_conv_prev, ssm_h0, ssm_A_log,
                                      ssm_dt_bias, ssm_D, ssm_norm_g, nbb)
    if cache is None:
        k_b = k.astype(BF16)
        vt_b = v.transpose(0, 2, 1).astype(BF16)
        ki_b = ki.astype(BF16)
        att_o = _dsa_group(q, qi, kis, k_b, vt_b, ki_b, s_valid=T, q_offset=0, topk=min(TOPK_MAX, T // 4))
    else:
        ck, cv, cki = cache
        P = ck.shape[1]
        L = P + T
        pad = -(-L // LANES) * LANES - L
        k_b = jnp.concatenate([ck.reshape(B, P, ATT_W), k, jnp.zeros((B, pad, ATT_W), F32)], axis=1).astype(BF16)
        vt_b = jnp.concatenate([cv.reshape(B, P, ATT_W), v, jnp.zeros((B, pad, ATT_W), F32)],
                               axis=1).transpose(0, 2, 1).astype(BF16)
        ki_b = jnp.concatenate([cki, ki, jnp.zeros((B, pad, IDX_DH), F32)], axis=1).astype(BF16)
        att_o = _dsa_group(q, qi, kis, k_b, vt_b, ki_b, s_valid=L, q_offset=P, topk=min(TOPK_MAX, L // 4))

    res = _ffn(x, gdn_o, att_o, ssm_o, mod, w_out, norm2_g.reshape(1, -1), w_gate, w_up, ffn_conv_w, ffn_conv_b,
               w_down, ffn_conv_prev, final_g.reshape(1, -1), ffn_tt[0], ffn_tt[1], tf, final)
    if final:
        x_new, y, ffn_conv_new = res
    else:
        (x_new, ffn_conv_new), y = res, None
    st = (k.reshape(B, T, ATT_HEADS, ATT_DH), v.reshape(B, T, ATT_HEADS, ATT_DH), ki,
          gdn_conv_new, gdn_s, ssm_conv_new, ssm_h, ffn_conv_new)
    return x_new, y, st


def kernel(x_prompt, x_sample, c_prompt, c_sample, cache_k, cache_v, cache_kidx, state_gdn_conv, state_gdn,
           state_ssm_conv, state_ssm, state_ffn_conv, w_ada, b_ada, norm1_g, w_in, gdn_conv_w, gdn_A_log,
           gdn_dt_bias, gdn_norm_g, ssm_conv_w, ssm_conv_b, ssm_A_log, ssm_dt_bias, ssm_D, ssm_norm_g, w_out,
           norm2_g, w_gate, w_up, ffn_conv_w, ffn_conv_b, w_down, final_g):
    Bp, T, _ = x_prompt.shape
    Bs, Ts, _ = x_sample.shape
    P = cache_k.shape[2]

    c_all = jnp.concatenate([c_prompt, c_sample], axis=0)
    mod_all = _ada(c_all, w_ada, b_ada).reshape(DEPTH, Bp + Bs, 6, D_MODEL)

    tables_p = _rope_tables(jnp.arange(T))
    tables_s = _rope_tables(P + jnp.arange(Ts))

    perm = jnp.asarray(np.maximum(_PERM_COLS, 0))
    keep = jnp.asarray((_PERM_COLS >= 0).astype(np.float32))
    w_perm = (jnp.take(w_in, perm, axis=2) * keep).astype(BF16)
    w_out_b = w_out.astype(BF16)
    w_gate_b = w_gate.astype(BF16)
    w_up_b = w_up.astype(BF16)
    w_down_b = w_down.astype(BF16)

    zeros_p = (jnp.zeros((Bp, CONV_W - 1, GDN_CONV_C), F32),
               jnp.zeros((Bp, GDN_HEADS, GDN_DK, GDN_DV), F32),
               jnp.zeros((Bp, CONV_W - 1, SSM_CONV_C), F32),
               jnp.zeros((Bp, SSM_HEADS, SSM_P, SSM_N), F32),
               jnp.zeros((Bp, FFN_CONV_W - 1, D_FF), F32))

    tf = D_FF // 2
    tiles_p = (1, min(256, T), 2, (1, min(512, T)), tf)
    tiles_s = (Bs, Ts, 2, (Bs, Ts), tf)

    xp, xs = x_prompt, x_sample
    new_p, new_s = [], []
    yp = ys = None
    for l in range(DEPTH):
        lw = (norm1_g[l], w_perm[l], gdn_conv_w[l], gdn_A_log[l], gdn_dt_bias[l], gdn_norm_g[l], ssm_conv_w[l],
              ssm_conv_b[l], ssm_A_log[l], ssm_dt_bias[l], ssm_D[l], ssm_norm_g[l], w_out_b[l], norm2_g[l],
              w_gate_b[l], w_up_b[l], ffn_conv_w[l], ffn_conv_b[l], w_down_b[l])
        final = l == DEPTH - 1
        xp, yp, st_p = _layer(xp, mod_all[l, :Bp], tables_p, None, zeros_p, lw, tiles_p, final, final_g)
        states_s = (state_gdn_conv[l], state_gdn[l], state_ssm_conv[l], state_ssm[l], state_ffn_conv[l])
        xs, ys, st_s = _layer(xs, mod_all[l, Bp:], tables_s, (cache_k[l], cache_v[l], cache_kidx[l]), states_s,
                              lw, tiles_s, final, final_g)
        new_p.append(st_p)
        new_s.append(st_s)
    outs_p = [jnp.stack([st[i] for st in new_p]) for i in range(8)]
    outs_s = [jnp.stack([st[i] for st in new_s]) for i in range(8)]
    return (yp, ys, *outs_p, *outs_s)
```

```python
import functools
import math

import jax
import jax.numpy as jnp
import numpy as np
from jax import lax
from jax.experimental import pallas as pl
from jax.experimental.pallas import tpu as pltpu

F32 = jnp.float32
BF16 = jnp.bfloat16
HI = lax.Precision.HIGHEST

D_MODEL = 1024
DEPTH = 2
CHUNK = 64
CONV_W = 4
FFN_CONV_W = 3
D_FF = 2816
ROPE_THETA = 10000.0
EPS = 1e-6
GDN_HEADS = 4
GDN_DK = 64
GDN_DV = 64
ATT_HEADS = 4
ATT_DH = 64
IDX_HEADS = 4
IDX_DH = 64
TOPK_MAX = 256
SSM_HEADS = 8
SSM_P = 64
SSM_GROUPS = 2
SSM_N = 128
GDN_W = GDN_HEADS * GDN_DV
ATT_W = ATT_HEADS * ATT_DH
SSM_W = SSM_HEADS * SSM_P
MIX_W = GDN_W + ATT_W + SSM_W
GDN_CONV_C = 2 * GDN_HEADS * GDN_DK + GDN_W
SSM_CONV_C = SSM_W + 2 * SSM_GROUPS * SSM_N
IN_SIZES = (GDN_CONV_C, GDN_HEADS, GDN_HEADS, GDN_W,
            ATT_W, ATT_W, ATT_W, IDX_HEADS * IDX_DH, IDX_DH, IDX_HEADS,
            SSM_W, SSM_CONV_C, SSM_HEADS)
IN_W = sum(IN_SIZES)

LANES = 128
SUBLANES = 8
VMEM_LIMIT = 56 * 1024 * 1024

KIS_A = IDX_DH
KIS_B = KIS_A + GDN_HEADS
KIS_WI = KIS_B + GDN_HEADS
KIS_DT = KIS_WI + IDX_HEADS
KIS_END = KIS_DT + SSM_HEADS
OUT_SEGS = (("gqkv", GDN_CONV_C), ("ggate", GDN_W), ("q", ATT_W), ("k", ATT_W), ("v", ATT_W),
            ("qi", IDX_HEADS * IDX_DH), ("kis", LANES), ("z", SSM_W), ("xbc", SSM_CONV_C))
PERM_W = sum(w for _, w in OUT_SEGS)
ROT_W = ATT_W + LANES


def _perm_columns():
    starts = np.concatenate([[0], np.cumsum(IN_SIZES)])
    (s_gqkv, s_ga, s_gb, s_gg, s_q, s_k, s_v, s_qi, s_ki, s_wi, s_z, s_xbc, s_dt) = starts[:-1]
    cols = []
    cols += list(range(s_gqkv, s_gqkv + GDN_CONV_C))
    cols += list(range(s_gg, s_gg + GDN_W))
    cols += list(range(s_q, s_q + ATT_W))
    cols += list(range(s_k, s_k + ATT_W))
    cols += list(range(s_v, s_v + ATT_W))
    cols += list(range(s_qi, s_qi + IDX_HEADS * IDX_DH))
    kis = (list(range(s_ki, s_ki + IDX_DH)) + list(range(s_ga, s_ga + GDN_HEADS))
           + list(range(s_gb, s_gb + GDN_HEADS)) + list(range(s_wi, s_wi + IDX_HEADS))
           + list(range(s_dt, s_dt + SSM_HEADS)))
    cols += kis + [-1] * (LANES - len(kis))
    cols += list(range(s_z, s_z + SSM_W))
    cols += list(range(s_xbc, s_xbc + SSM_CONV_C))
    assert len(cols) == PERM_W
    return np.asarray(cols, np.int32)


_PERM_COLS = _perm_columns()


def _silu(x):
    return x * jax.nn.sigmoid(x)


def _softplus(x):
    return jnp.maximum(x, 0.0) + jnp.log1p(jnp.exp(-jnp.abs(x)))


def _dot(a, b, precision=None):
    return jnp.dot(a, b, preferred_element_type=F32, precision=precision)


def _dot_nt(a, b, precision=None):
    return lax.dot_general(a, b, (((1,), (1,)), ((), ())), preferred_element_type=F32,
                           precision=precision)


def _dot_tn(a, b, precision=None):
    return lax.dot_general(a, b, (((0,), (0,)), ((), ())), preferred_element_type=F32,
                           precision=precision)


def _cumsum_rows(x):
    c = x.shape[0]
    row = lax.broadcasted_iota(jnp.int32, x.shape, 0)
    s = 1
    while s < c:
        x = x + jnp.where(row >= s, pltpu.roll(x, s, axis=0), 0.0)
        s *= 2
    return x


def _mm(a, b):
    return _dot(a.astype(BF16), b.astype(BF16))


def _mm_nt(a, b):
    return _dot_nt(a.astype(BF16), b.astype(BF16))


def _split(a):
    hi = a.astype(BF16)
    return hi, (a - hi.astype(F32)).astype(BF16)


def _mm3(a, b):
    (ah, al), (bh, bl) = a, b
    return _dot(ah, bh) + (_dot(ah, bl) + _dot(al, bh))


def _transpose_rows(vals):
    c = vals.shape[0]
    if c < LANES:
        vals = jnp.concatenate([vals, jnp.zeros((LANES - c, LANES), vals.dtype)], axis=0)
    return vals.T[:, :c]


def _lane_vector(vals, start):
    return jnp.zeros((1, LANES), F32).at[0, start:start + vals.shape[0]].set(vals.astype(F32))


def _params(sem):
    return pltpu.CompilerParams(dimension_semantics=sem, vmem_limit_bytes=VMEM_LIMIT)


def _ada_kernel(c_ref, w_ref, b_ref, o_ref):
    s = _silu(c_ref[...])
    o_ref[0] = _dot(s.astype(BF16), w_ref[0].astype(BF16)) + b_ref[0]


def _ada(c_all, w_ada, b_ada):
    rows = c_all.shape[0]
    n = w_ada.shape[2]
    tn = 1536
    return pl.pallas_call(
        _ada_kernel,
        grid=(DEPTH, n // tn),
        in_specs=[pl.BlockSpec((rows, D_MODEL), lambda l, j: (0, 0)),
                  pl.BlockSpec((1, D_MODEL, tn), lambda l, j: (l, 0, j)),
                  pl.BlockSpec((1, 1, tn), lambda l, j: (l, 0, j))],
        out_specs=pl.BlockSpec((1, rows, tn), lambda l, j: (l, 0, j)),
        out_shape=jax.ShapeDtypeStruct((DEPTH, rows, n), F32),
        compiler_params=_params(("arbitrary", "arbitrary")),
        name="ada",
    )(c_all, w_ada, b_ada.reshape(DEPTH, 1, n))


def _rotate(x, cos, sin_signed):
    w = x.shape[-1]
    lane = lax.broadcasted_iota(jnp.int32, x.shape, x.ndim - 1)
    first = (lane % ATT_DH) < (ATT_DH // 2)
    swapped = jnp.where(first, pltpu.roll(x, w - ATT_DH // 2, axis=x.ndim - 1),
                        pltpu.roll(x, ATT_DH // 2, axis=x.ndim - 1))
    return x * cos + swapped * sin_signed


def _inproj_kernel(x_ref, mod_ref, g_ref, w_ref, cos_ref, sin_ref, *out_refs, nb, tt):
    x = x_ref[...]
    ms = jnp.mean(x * x, axis=-1, keepdims=True)
    xn = x * lax.rsqrt(ms + EPS) * g_ref[...]
    mod = mod_ref[...]
    h = xn * (1.0 + mod[:, 1:2, :]) + mod[:, 0:1, :]
    u = _dot(h.reshape(nb * tt, D_MODEL).astype(BF16), w_ref[...])
    cos = cos_ref[...]
    sin = sin_ref[...]
    off = 0
    for (name, width), o_ref in zip(OUT_SEGS, out_refs):
        seg = u[:, off:off + width]
        if name in ("q", "k", "qi", "kis"):
            t0 = ATT_W if name == "kis" else 0
            c3 = cos[:, t0:t0 + width][None]
            s3 = sin[:, t0:t0 + width][None]
            if nb > 1:
                c3 = jnp.broadcast_to(c3, (nb, tt, width)).reshape(nb * tt, width)
                s3 = jnp.broadcast_to(s3, (nb, tt, width)).reshape(nb * tt, width)
            else:
                c3 = c3[0]
                s3 = s3[0]
            seg = _rotate(seg, c3, s3)
        o_ref[...] = seg.reshape(nb, tt, width)
        off += width


def _inproj(x, mod, g1, w_perm, cos, sin, nb, tt):
    B, T, _ = x.shape
    grid = (B // nb, T // tt)
    out_shape = tuple(jax.ShapeDtypeStruct((B, T, w), F32) for _, w in OUT_SEGS)
    out_specs = tuple(pl.BlockSpec((nb, tt, w), lambda b, t: (b, t, 0)) for _, w in OUT_SEGS)
    return pl.pallas_call(
        functools.partial(_inproj_kernel, nb=nb, tt=tt),
        grid=grid,
        in_specs=[pl.BlockSpec((nb, tt, D_MODEL), lambda b, t: (b, t, 0)),
                  pl.BlockSpec((nb, 6, D_MODEL), lambda b, t: (b, 0, 0)),
                  pl.BlockSpec((1, D_MODEL), lambda b, t: (0, 0)),
                  pl.BlockSpec((D_MODEL, PERM_W), lambda b, t: (0, 0)),
                  pl.BlockSpec((tt, ROT_W), lambda b, t: (t, 0)),
                  pl.BlockSpec((tt, ROT_W), lambda b, t: (t, 0))],
        out_specs=out_specs,
        out_shape=out_shape,
        compiler_params=_params(("arbitrary", "arbitrary")),
        name="inproj",
    )(x, mod, g1, w_perm, cos, sin)


def _short_conv(ext_ref, bi, u, w, c):
    lo = SUBLANES - (CONV_W - 1)
    ext_ref[bi, SUBLANES:SUBLANES + c, :] = u
    y = ext_ref[bi, lo:lo + c, :] * w[0:1, :]
    for j in range(1, CONV_W):
        y = y + ext_ref[bi, lo + j:lo + j + c, :] * w[j:j + 1, :]
    tail = ext_ref[bi, c + lo:c + SUBLANES, :]
    ext_ref[bi, lo:SUBLANES, :] = tail
    return y, tail


def _neumann_inverses(mats, c):
    eye = (lax.broadcasted_iota(jnp.int32, (c, c), 0)
           == lax.broadcasted_iota(jnp.int32, (c, c), 1)).astype(F32)
    ps = [eye - a for a in mats]
    sp = [_split(a) for a in mats]
    pws = [_mm3(s, s) for s in sp]
    n = 2
    while n < c:
        sp = [_split(pw) for pw in pws]
        ps = [p + _mm3(_split(p), s) for p, s in zip(ps, sp)]
        n *= 2
        if n < c:
            pws = [_mm3(s, s) for s in sp]
    return ps


def _gdn_kernel(qkv_ref, kis_ref, gate_ref, w_ref, prev_ref, s0_ref, alog_ref, dtb_ref, ng_ref,
                o_ref, new_ref, s_ref, ext_ref, s_sc, *, c, nbb):
    j = pl.program_id(1)
    lo = SUBLANES - (CONV_W - 1)

    @pl.when(j == 0)
    def _():
        s_sc[...] = s0_ref[...]
        ext_ref[:, lo:SUBLANES, :] = prev_ref[...]

    ng = ng_ref[...]
    w_conv = w_ref[...]
    ri = lax.broadcasted_iota(jnp.int32, (c, c), 0)
    ci = lax.broadcasted_iota(jnp.int32, (c, c), 1)
    incl = ri >= ci
    strict = ri > ci
    nkd = GDN_HEADS * GDN_DK
    ch = []
    for bi in range(nbb):
        y, tail = _short_conv(ext_ref, bi, qkv_ref[bi], w_conv, c)
        new_ref[bi] = tail
        y = _silu(y)
        small = kis_ref[bi]
        g = -jnp.exp(alog_ref[...]) * _softplus(small + dtb_ref[...])
        beta = jax.nn.sigmoid(small)
        gc = _cumsum_rows(g)
        gc_t = _transpose_rows(gc)
        gate = gate_ref[bi]
        for h in range(GDN_HEADS):
            qh = y[:, h * GDN_DK:(h + 1) * GDN_DK]
            kh = y[:, nkd + h * GDN_DK:nkd + (h + 1) * GDN_DK]
            vh = y[:, 2 * nkd + h * GDN_DV:2 * nkd + (h + 1) * GDN_DV]
            qh = qh * lax.rsqrt(jnp.sum(qh * qh, axis=-1, keepdims=True) + EPS) * (GDN_DK ** -0.5)
            kh = kh * lax.rsqrt(jnp.sum(kh * kh, axis=-1, keepdims=True) + EPS)
            ch.append((qh, kh, vh, gc[:, KIS_A + h:KIS_A + h + 1], gc_t[KIS_A + h:KIS_A + h + 1, :],
                       beta[:, KIS_B + h:KIS_B + h + 1], gate[:, h * GDN_DV:(h + 1) * GDN_DV]))
    ids = [(bi, h) for bi in range(nbb) for h in range(GDN_HEADS)]
    decay = [jnp.where(incl, jnp.exp(jnp.where(incl, gcol - grow, 0.0)), 0.0)
             for (_, _, _, gcol, grow, _, _) in ch]
    kb = [kh * bcol for (_, kh, _, _, _, bcol, _) in ch]
    eg = [jnp.exp(gcol) for (_, _, _, gcol, _, _, _) in ch]
    kk = [_mm_nt(kbi, x[1]) for kbi, x in zip(kb, ch)]
    qk = [_mm_nt(x[0], x[1]) for x in ch]
    t_inv = _neumann_inverses([jnp.where(strict, m * d, 0.0) for m, d in zip(kk, decay)], c)
    attn = [jnp.where(incl, m * d, 0.0) for m, d in zip(qk, decay)]
    rhs = [jnp.concatenate([x[2] * x[5], kbi * e], axis=1) for x, kbi, e in zip(ch, kb, eg)]
    sol = [_mm(t, r) for t, r in zip(t_inv, rhs)]
    st = [s_sc[bi, h] for bi, h in ids]
    ws = [_mm(sl[:, GDN_DV:], s) for sl, s in zip(sol, st)]
    qs = [_mm(x[0] * e, s) for x, e, s in zip(ch, eg, st)]
    v_new = [sl[:, :GDN_DV] - w for sl, w in zip(sol, ws)]
    av = [_mm(a, v) for a, v in zip(attn, v_new)]
    g_last = [x[3][c - 1:c, :] for x in ch]
    kv = [_dot_tn(x[1] * jnp.exp(gl - x[3]), v) for x, gl, v in zip(ch, g_last, v_new)]
    for (bi, h), x, s, gl, kvi, qsi, avi in zip(ids, ch, st, g_last, kv, qs, av):
        s_new = s * jnp.exp(gl) + kvi
        s_sc[bi, h] = s_new
        s_ref[bi, h] = s_new
        o = qsi + avi
        o = o * lax.rsqrt(jnp.mean(o * o, axis=-1, keepdims=True) + EPS) * ng
        o_ref[bi, :, h * GDN_DV:(h + 1) * GDN_DV] = o * _silu(x[6])


def _gdn(gqkv, kis, ggate, conv_w, conv_prev, s0, a_log, dt_bias, norm_g, nbb):
    B, T, _ = gqkv.shape
    c = math.gcd(T, CHUNK)
    return pl.pallas_call(
        functools.partial(_gdn_kernel, c=c, nbb=nbb),
        grid=(B // nbb, T // c),
        in_specs=[pl.BlockSpec((nbb, c, GDN_CONV_C), lambda b, j: (b, j, 0)),
                  pl.BlockSpec((nbb, c, LANES), lambda b, j: (b, j, 0)),
                  pl.BlockSpec((nbb, c, GDN_W), lambda b, j: (b, j, 0)),
                  pl.BlockSpec((CONV_W, GDN_CONV_C), lambda b, j: (0, 0)),
                  pl.BlockSpec((nbb, CONV_W - 1, GDN_CONV_C), lambda b, j: (b, 0, 0)),
                  pl.BlockSpec((nbb, GDN_HEADS, GDN_DK, GDN_DV), lambda b, j: (b, 0, 0, 0)),
                  pl.BlockSpec((1, LANES), lambda b, j: (0, 0)),
                  pl.BlockSpec((1, LANES), lambda b, j: (0, 0)),
                  pl.BlockSpec((1, GDN_DV), lambda b, j: (0, 0))],
        out_specs=(pl.BlockSpec((nbb, c, GDN_W), lambda b, j: (b, j, 0)),
                   pl.BlockSpec((nbb, CONV_W - 1, GDN_CONV_C), lambda b, j: (b, 0, 0)),
                   pl.BlockSpec((nbb, GDN_HEADS, GDN_DK, GDN_DV), lambda b, j: (b, 0, 0, 0))),
        out_shape=(jax.ShapeDtypeStruct((B, T, GDN_W), F32),
                   jax.ShapeDtypeStruct((B, CONV_W - 1, GDN_CONV_C), F32),
                   jax.ShapeDtypeStruct((B, GDN_HEADS, GDN_DK, GDN_DV), F32)),
        scratch_shapes=[pltpu.VMEM((nbb, SUBLANES + c, GDN_CONV_C), F32),
                        pltpu.VMEM((nbb, GDN_HEADS, GDN_DK, GDN_DV), F32)],
        compiler_params=_params(("arbitrary", "arbitrary")),
        name="gdn",
    )(gqkv, kis, ggate, conv_w, conv_prev, s0, _lane_vector(a_log, KIS_A), _lane_vector(dt_bias, KIS_A),
      norm_g.reshape(1, -1))


def _ssd_kernel(xbc_ref, z_ref, kis_ref, w_ref, cb_ref, prev_ref, h0_ref, alog_ref, dtb_ref, dsk_ref,
                ng_ref, o_ref, new_ref, h_ref, ext_ref, h_sc, *, c, nbb):
    j = pl.program_id(1)
    lo = SUBLANES - (CONV_W - 1)

    @pl.when(j == 0)
    def _():
        h_sc[...] = h0_ref[...]
        ext_ref[:, lo:SUBLANES, :] = prev_ref[...]

    dsk = dsk_ref[...]
    ng = ng_ref[...]
    w_conv = w_ref[...]
    ri = lax.broadcasted_iota(jnp.int32, (c, c), 0)
    ci = lax.broadcasted_iota(jnp.int32, (c, c), 1)
    incl = ri >= ci
    gn = SSM_GROUPS * SSM_N
    rep = SSM_HEADS // SSM_GROUPS
    gw = SSM_W // SSM_GROUPS
    grp_in = []
    ch = []
    ids = []
    for bi in range(nbb):
        y, tail = _short_conv(ext_ref, bi, xbc_ref[bi], w_conv, c)
        new_ref[bi] = tail
        y = _silu(y + cb_ref[...])
        dts = _softplus(kis_ref[bi] + dtb_ref[...])
        a = dts * (-jnp.exp(alog_ref[...]))
        acum = _cumsum_rows(a)
        acum_t = _transpose_rows(acum)
        zg = _silu(z_ref[bi])
        for grp in range(SSM_GROUPS):
            grp_in.append((y[:, SSM_W + grp * SSM_N:SSM_W + (grp + 1) * SSM_N],
                           y[:, SSM_W + gn + grp * SSM_N:SSM_W + gn + (grp + 1) * SSM_N],
                           zg[:, grp * gw:(grp + 1) * gw]))
            for hh in range(rep):
                h = grp * rep + hh
                ch.append((y[:, h * SSM_P:(h + 1) * SSM_P], acum[:, KIS_DT + h:KIS_DT + h + 1],
                           acum_t[KIS_DT + h:KIS_DT + h + 1, :], dts[:, KIS_DT + h:KIS_DT + h + 1],
                           len(grp_in) - 1))
                ids.append((bi, h))
    cb = [_mm_nt(cg, bg) for bg, cg, _ in grp_in]
    seg = [jnp.where(incl, jnp.exp(jnp.where(incl, acol - arow, 0.0)), 0.0) for (_, acol, arow, _, _) in ch]
    hs = [h_sc[bi, h] for bi, h in ids]
    y_in = [_mm(cb[x[4]] * sg, x[0] * x[3]) for x, sg in zip(ch, seg)]
    y_st = [_mm_nt(grp_in[x[4]][1], hst) for x, hst in zip(ch, hs)]
    a_last = [x[1][c - 1:c, :] for x in ch]
    upd = [_dot_tn(x[0], grp_in[x[4]][0] * (jnp.exp(al - x[1]) * x[3])) for x, al in zip(ch, a_last)]
    outs = []
    for (bi, h), x, hst, al, yi, ys, up in zip(ids, ch, hs, a_last, y_in, y_st, upd):
        h_new = hst * jnp.exp(al) + up
        h_sc[bi, h] = h_new
        h_ref[bi, h] = h_new
        outs.append(yi + ys * jnp.exp(x[1]) + dsk[:, h:h + 1] * x[0])
    for gi, (_, _, zslice) in enumerate(grp_in):
        bi, grp = divmod(gi, SSM_GROUPS)
        yg = jnp.concatenate(outs[gi * rep:(gi + 1) * rep], axis=1) * zslice
        yg = yg * lax.rsqrt(jnp.mean(yg * yg, axis=-1, keepdims=True) + EPS)
        o_ref[bi, :, grp * gw:(grp + 1) * gw] = yg * ng[:, grp * gw:(grp + 1) * gw]


def _ssd(xbc, z, kis, conv_w, conv_b, conv_prev, h0, a_log, dt_bias, d_skip, norm_g, nbb):
    B, T, _ = xbc.shape
    c = math.gcd(T, CHUNK)
    return pl.pallas_call(
        functools.partial(_ssd_kernel, c=c, nbb=nbb),
        grid=(B // nbb, T // c),
        in_specs=[pl.BlockSpec((nbb, c, SSM_CONV_C), lambda b, j: (b, j, 0)),
                  pl.BlockSpec((nbb, c, SSM_W), lambda b, j: (b, j, 0)),
                  pl.BlockSpec((nbb, c, LANES), lambda b, j: (b, j, 0)),
                  pl.BlockSpec((CONV_W, SSM_CONV_C), lambda b, j: (0, 0)),
                  pl.BlockSpec((1, SSM_CONV_C), lambda b, j: (0, 0)),
                  pl.BlockSpec((nbb, CONV_W - 1, SSM_CONV_C), lambda b, j: (b, 0, 0)),
                  pl.BlockSpec((nbb, SSM_HEADS, SSM_P, SSM_N), lambda b, j: (b, 0, 0, 0)),
                  pl.BlockSpec((1, LANES), lambda b, j: (0, 0)),
                  pl.BlockSpec((1, LANES), lambda b, j: (0, 0)),
                  pl.BlockSpec((1, SSM_HEADS), lambda b, j: (0, 0)),
                  pl.BlockSpec((1, SSM_W), lambda b, j: (0, 0))],
        out_specs=(pl.BlockSpec((nbb, c, SSM_W), lambda b, j: (b, j, 0)),
                   pl.BlockSpec((nbb, CONV_W - 1, SSM_CONV_C), lambda b, j: (b, 0, 0)),
                   pl.BlockSpec((nbb, SSM_HEADS, SSM_P, SSM_N), lambda b, j: (b, 0, 0, 0))),
        out_shape=(jax.ShapeDtypeStruct((B, T, SSM_W), F32),
                   jax.ShapeDtypeStruct((B, CONV_W - 1, SSM_CONV_C), F32),
                   jax.ShapeDtypeStruct((B, SSM_HEADS, SSM_P, SSM_N), F32)),
        scratch_shapes=[pltpu.VMEM((nbb, SUBLANES + c, SSM_CONV_C), F32),
                        pltpu.VMEM((nbb, SSM_HEADS, SSM_P, SSM_N), F32)],
        compiler_params=_params(("arbitrary", "arbitrary")),
        name="ssd",
    )(xbc, z, kis, conv_w, conv_b.reshape(1, -1), conv_prev, h0, _lane_vector(a_log, KIS_DT),
      _lane_vector(dt_bias, KIS_DT), d_skip.reshape(1, -1), norm_g.reshape(1, -1))


IDX_BITS = 12


SEARCH_STEPS = 4
SEARCH_ROUNDS = 128
FOLD_ROWS = 128


def _fold_rows(x, op):
    s = x.shape[0]
    if s % FOLD_ROWS == 0 and s > FOLD_ROWS:
        x = op(x.reshape(s // FOLD_ROWS, FOLD_ROWS, x.shape[1]), axis=0)
    return op(x, axis=0, keepdims=True)


def _count(mask):
    return _fold_rows(mask.astype(F32), jnp.sum)


def _dsa_kernel(q_ref, qi_ref, kis_ref, k_ref, vt_ref, ki_ref, o_ref, *, tq, s_len, s_valid, q_offset, j0,
                n_valid_q, topk):
    j = pl.program_id(1)
    lane = lax.broadcasted_iota(jnp.int32, (1, tq), 1)
    qpos = q_offset + (j0 + j) * tq + lane
    lim = jnp.minimum((qpos // CHUNK + 1) * CHUNK, s_valid)
    row = lax.broadcasted_iota(jnp.int32, (s_len, tq), 0)
    adm = row < lim

    qi_t = qi_ref[0].T.astype(BF16)
    kis_t = kis_ref[0].T
    ki = ki_ref[0]
    assert IDX_DH == 64 and IDX_HEADS == 4 and ATT_DH == 64
    wi = kis_t[KIS_WI:KIS_WI + IDX_HEADS, :] * (IDX_DH ** -0.5 * IDX_HEADS ** -0.5)
    qi_all = jnp.concatenate([qi_t[h * IDX_DH:(h + 1) * IDX_DH, :] for h in range(IDX_HEADS)], axis=1)
    rel_all = _dot(ki, qi_all)
    score = jnp.zeros((s_len, tq), F32)
    for h in range(IDX_HEADS):
        score = score + jnp.maximum(rel_all[:, h * tq:(h + 1) * tq], 0.0) * wi[h:h + 1, :]
    s = jnp.where(adm, score, -jnp.inf)

    n_adm = _count(adm)
    few = (n_adm < topk) | (lane >= n_valid_q)
    smax = _fold_rows(s, jnp.max)
    smin = _fold_rows(jnp.where(adm, score, jnp.inf), jnp.min)
    c_top = _count(s >= smax)
    top_full = c_top >= topk
    lo0 = jnp.where(few, 0.0, jnp.where(top_full, smax, smin))
    hi0 = jnp.where(few, 0.0, smax)
    done0 = (few | top_full).astype(F32)

    def search_cond(c):
        return (jnp.min(c[4]) < 0.5) & (c[5] < SEARCH_ROUNDS)

    def search_body(c):
        lo, hi, c_lo, c_hi, done, it = c
        for step in range(SEARCH_STEPS):
            if step % 2 == 0:
                frac = jnp.clip((c_lo - topk + 0.5) / (c_lo - c_hi), 1.0 / 32, 31.0 / 32)
                mid = lo + (hi - lo) * frac
            else:
                mid = 0.5 * lo + 0.5 * hi
            ok = (mid > lo) & (mid < hi) & (done < 0.5)
            cnt = _count(s >= mid)
            up = ok & (cnt >= topk)
            dn = ok & (cnt < topk)
            lo, c_lo = jnp.where(up, mid, lo), jnp.where(up, cnt, c_lo)
            hi, c_hi = jnp.where(dn, mid, hi), jnp.where(dn, cnt, c_hi)
        first = _fold_rows(jnp.where(s >= lo, s, jnp.inf), jnp.min)
        last = _fold_rows(jnp.where(s < hi, s, -jnp.inf), jnp.max)
        lo = jnp.where(done < 0.5, first, lo)
        done = jnp.maximum(done, (first >= last).astype(F32))
        return lo, hi, c_lo, c_hi, done, it + 1

    lo, _, _, _, _, _ = lax.while_loop(search_cond, search_body, (lo0, hi0, n_adm, c_top, done0, jnp.int32(0)))
    t = jnp.where(few, -jnp.inf, lo)
    gt = s > t
    tie = s == t
    need = topk - _count(gt)
    excess = jnp.where(few, 0.0, _count(tie) - need)

    def tie_search():
        def ibody(i, m):
            cand = m | (jnp.int32(1) << (IDX_BITS - 1 - i))
            return jnp.where(_count(tie & (row < cand)) <= need, cand, m)

        return lax.fori_loop(0, IDX_BITS, ibody, jnp.zeros((1, tq), jnp.int32))

    m = lax.cond(jnp.max(excess) > 0.0, tie_search, lambda: jnp.full((1, tq), 2 ** IDX_BITS - 1, jnp.int32))
    sel = adm & (gt | (tie & (row < m)))

    q_t = q_ref[0].T * (ATT_DH ** -0.5)
    pair = LANES // ATT_DH
    head_in_pair = lax.broadcasted_iota(jnp.int32, (LANES, tq), 0) // ATT_DH
    logits = []
    for g in range(ATT_HEADS // pair):
        qg = q_t[g * LANES:(g + 1) * LANES, :]
        qm = jnp.concatenate([jnp.where(head_in_pair == i, qg, 0.0) for i in range(pair)], axis=1)
        lg = _dot(k_ref[0, :, g * LANES:(g + 1) * LANES], qm.astype(BF16))
        logits += [lg[:, i * tq:(i + 1) * tq] for i in range(pair)]
    probs, scales = [], []
    for lg in logits:
        lg = jnp.where(sel, lg, -jnp.inf)
        p = jnp.exp(lg - _fold_rows(lg, jnp.max))
        scales.append(1.0 / _fold_rows(p, jnp.sum))
        probs.append(p.astype(BF16))
    outs = [_dot(vt_ref[0, h * ATT_DH:(h + 1) * ATT_DH, :], probs[h]) * scales[h] for h in range(ATT_HEADS)]
    o_ref[0] = jnp.concatenate(outs, axis=0).T


def _dsa(q, qi, kis, k_b, vt_b, ki_b, *, tq, j0, nj, s_len, s_valid, q_offset, n_valid_q, topk):
    B = q.shape[0]
    assert s_len < 2 ** IDX_BITS and s_len <= k_b.shape[1]
    return pl.pallas_call(
        functools.partial(_dsa_kernel, tq=tq, s_len=s_len, s_valid=s_valid, q_offset=q_offset, j0=j0,
                          n_valid_q=n_valid_q, topk=topk),
        grid=(B, nj),
        in_specs=[pl.BlockSpec((1, tq, ATT_W), lambda b, j: (b, j0 + j, 0)),
                  pl.BlockSpec((1, tq, IDX_HEADS * IDX_DH), lambda b, j: (b, j0 + j, 0)),
                  pl.BlockSpec((1, tq, LANES), lambda b, j: (b, j0 + j, 0)),
                  pl.BlockSpec((1, s_len, ATT_W), lambda b, j: (b, 0, 0)),
                  pl.BlockSpec((1, ATT_W, s_len), lambda b, j: (b, 0, 0)),
                  pl.BlockSpec((1, s_len, IDX_DH), lambda b, j: (b, 0, 0))],
        out_specs=pl.BlockSpec((1, tq, ATT_W), lambda b, j: (b, j, 0)),
        out_shape=jax.ShapeDtypeStruct((B, nj * tq, ATT_W), F32),
        compiler_params=_params(("arbitrary", "arbitrary")),
        name="dsa",
    )(q, qi, kis, k_b, vt_b, ki_b)


def _dsa_group(q, qi, kis, k_b, vt_b, ki_b, *, s_valid, q_offset, topk):
    B, T, _ = q.shape
    tq = LANES
    n_valid_q = min(T, tq)
    if T < tq:
        padq = lambda a: jnp.pad(a, ((0, 0), (0, tq - T), (0, 0)))
        q, qi, kis = padq(q), padq(qi), padq(kis)
    nq = q.shape[1] // tq
    s_total = k_b.shape[1]
    outs = []
    j0 = 0
    while j0 < nq:
        nj = min(2, nq - j0)
        reach = -(-(q_offset + (j0 + nj) * tq) // CHUNK) * CHUNK
        s_len = min(s_total, -(-min(reach, s_valid) // LANES) * LANES)
        outs.append(_dsa(q, qi, kis, k_b, vt_b, ki_b, tq=tq, j0=j0, nj=nj, s_len=s_len, s_valid=s_valid,
                         q_offset=q_offset, n_valid_q=n_valid_q, topk=topk))
        j0 += nj
    out = outs[0] if len(outs) == 1 else jnp.concatenate(outs, axis=1)
    return out[:, :T]


def _ffn_kernel(x_ref, gdn_ref, att_ref, ssm_ref, mod_ref, wo_ref, g2_ref, wg_ref, wu_ref, cw_ref, cb_ref,
                wd_ref, prev_ref, fg_ref, *rest, nb, tt, nk, final):
    if final:
        xo_ref, y_ref, new_ref, h2_sc, acc_sc, buf_sc, carry_sc = rest
    else:
        xo_ref, new_ref, h2_sc, acc_sc, buf_sc, carry_sc = rest
        y_ref = None
    t = pl.program_id(1)
    k = pl.program_id(2)
    rows = nb * tt
    lo = SUBLANES - (FFN_CONV_W - 1)
    mod = mod_ref[...]

    @pl.when(k == 0)
    def _():
        mix = jnp.concatenate([gdn_ref[...], att_ref[...], ssm_ref[...]], axis=-1)
        proj = _dot(mix.reshape(rows, MIX_W).astype(BF16), wo_ref[...]).reshape(nb, tt, D_MODEL)
        x1 = x_ref[...] + mod[:, 2:3, :] * proj
        acc_sc[...] = x1
        ms = jnp.mean(x1 * x1, axis=-1, keepdims=True)
        h2 = x1 * lax.rsqrt(ms + EPS) * g2_ref[...]
        h2 = h2 * (1.0 + mod[:, 4:5, :]) + mod[:, 3:4, :]
        h2_sc[...] = h2.reshape(rows, D_MODEL).astype(BF16)

    @pl.when(t == 0)
    def _():
        carry_sc[k, :, lo:SUBLANES, :] = prev_ref[...]

    h2 = h2_sc[...]
    ag = _dot(h2, wg_ref[...])
    up = _dot(h2, wu_ref[...])
    tf = ag.shape[-1]
    buf_sc[:, lo:SUBLANES, :] = carry_sc[k, :, lo:SUBLANES, :]
    buf_sc[:, SUBLANES:SUBLANES + tt, :] = ag.reshape(nb, tt, tf)
    cw = cw_ref[...]
    conv = buf_sc[:, lo:lo + tt, :] * cw[0:1, :][None]
    for jj in range(1, FFN_CONV_W):
        conv = conv + buf_sc[:, lo + jj:lo + jj + tt, :] * cw[jj:jj + 1, :][None]
    tail = buf_sc[:, tt + lo:tt + SUBLANES, :]
    carry_sc[k, :, lo:SUBLANES, :] = tail
    new_ref[:, pl.ds(k, 1), :, :] = tail[:, None]
    act = _silu(conv + cb_ref[...][None]).reshape(rows, tf) * up
    y = _dot(act.astype(BF16), wd_ref[...]).reshape(nb, tt, D_MODEL)
    acc_sc[...] += mod[:, 5:6, :] * y

    @pl.when(k == nk - 1)
    def _():
        xo = acc_sc[...]
        xo_ref[...] = xo
        if final:
            ms = jnp.mean(xo * xo, axis=-1, keepdims=True)
            y_ref[...] = xo * lax.rsqrt(ms + EPS) * fg_ref[...]


def _ffn(x, gdn_o, att_o, ssm_o, mod, w_out, g2, w_gate, w_up, conv_w, conv_b, w_down, conv_prev, final_g,
         nb, tt, tf, final):
    B, T, _ = x.shape
    nk = D_FF // tf
    grid = (B // nb, T // tt, nk)
    row_spec = lambda w: pl.BlockSpec((nb, tt, w), lambda b, t, k: (b, t, 0))
    in_specs = [row_spec(D_MODEL), row_spec(GDN_W), row_spec(ATT_W), row_spec(SSM_W),
                pl.BlockSpec((nb, 6, D_MODEL), lambda b, t, k: (b, 0, 0)),
                pl.BlockSpec((MIX_W, D_MODEL), lambda b, t, k: (0, 0)),
                pl.BlockSpec((1, D_MODEL), lambda b, t, k: (0, 0)),
                pl.BlockSpec((D_MODEL, tf), lambda b, t, k: (0, k)),
                pl.BlockSpec((D_MODEL, tf), lambda b, t, k: (0, k)),
                pl.BlockSpec((FFN_CONV_W, tf), lambda b, t, k: (0, k)),
                pl.BlockSpec((1, tf), lambda b, t, k: (0, k)),
                pl.BlockSpec((tf, D_MODEL), lambda b, t, k: (k, 0)),
                pl.BlockSpec((nb, FFN_CONV_W - 1, tf), lambda b, t, k: (b, 0, k)),
                pl.BlockSpec((1, D_MODEL), lambda b, t, k: (0, 0))]
    out_shape = [jax.ShapeDtypeStruct((B, T, D_MODEL), F32)]
    out_specs = [row_spec(D_MODEL)]
    if final:
        out_shape.append(jax.ShapeDtypeStruct((B, T, D_MODEL), F32))
        out_specs.append(row_spec(D_MODEL))
    out_shape.append(jax.ShapeDtypeStruct((B, nk, FFN_CONV_W - 1, tf), F32))
    out_specs.append(pl.BlockSpec((nb, nk, FFN_CONV_W - 1, tf), lambda b, t, k: (b, 0, 0, 0)))
    res = pl.pallas_call(
        functools.partial(_ffn_kernel, nb=nb, tt=tt, nk=nk, final=final),
        grid=grid,
        in_specs=in_specs,
        out_specs=tuple(out_specs),
        out_shape=tuple(out_shape),
        scratch_shapes=[pltpu.VMEM((nb * tt, D_MODEL), BF16),
                        pltpu.VMEM((nb, tt, D_MODEL), F32),
                        pltpu.VMEM((nb, SUBLANES + tt, tf), F32),
                        pltpu.VMEM((nk, nb, SUBLANES, tf), F32)],
        compiler_params=_params(("arbitrary", "arbitrary", "arbitrary")),
        name="ffn",
    )(x, gdn_o, att_o, ssm_o, mod, w_out, g2, w_gate, w_up, conv_w, conv_b.reshape(1, -1), w_down,
      conv_prev, final_g)
    tail = res[-1].transpose(0, 2, 1, 3).reshape(B, FFN_CONV_W - 1, D_FF)
    return (*res[:-1], tail)


def _rope_tables(pos):
    half = ATT_DH // 2
    inv_freq = ROPE_THETA ** (-jnp.arange(half, dtype=F32) / half)
    ang = pos.astype(F32)[:, None] * inv_freq[None, :]
    cos = jnp.cos(ang)
    sin = jnp.sin(ang)
    cos_h = jnp.concatenate([cos, cos], axis=-1)
    sin_h = jnp.concatenate([-sin, sin], axis=-1)
    ones = jnp.ones((pos.shape[0], LANES - ATT_DH), F32)
    cos_t = jnp.concatenate([jnp.tile(cos_h, (1, ATT_HEADS)), cos_h, ones], axis=-1)
    sin_t = jnp.concatenate([jnp.tile(sin_h, (1, ATT_HEADS)), sin_h, 0.0 * ones], axis=-1)
    return cos_t, sin_t


def _layer(x, mod, pos_tables, cache, states, lw, tiles, final, final_g):
    (norm1_g, w_perm, gdn_conv_w, gdn_A_log, gdn_dt_bias, gdn_norm_g, ssm_conv_w, ssm_conv_b, ssm_A_log,
     ssm_dt_bias, ssm_D, ssm_norm_g, w_out, norm2_g, w_gate, w_up, ffn_conv_w, ffn_conv_b, w_down) = lw
    gdn_conv_prev, gdn_s0, ssm_conv_prev, ssm_h0, ffn_conv_prev = states
    B, T, _ = x.shape
    nb, tt, nbb, ffn_tt, tf = tiles
    cos_t, sin_t = pos_tables
    gqkv, ggate, q, k, v, qi, kis, z, xbc = _inproj(x, mod, norm1_g.reshape(1, -1), w_perm, cos_t, sin_t, nb, tt)
    ki = kis[..., :IDX_DH]

    gdn_o, gdn_conv_new, gdn_s = _gdn(gqkv, kis, ggate, gdn_conv_w, gdn_conv_prev, gdn_s0, gdn_A_log,
                                      gdn_dt_bias, gdn_norm_g, nbb)
    ssm_o, ssm_conv_new, ssm_h = _ssd(xbc, z, kis, ssm_conv_w, ssm_conv_b, ssm_conv_prev, ssm_h0, ssm_A_log,
                                      ssm_dt_bias, ssm_D, ssm_norm_g, nbb)
    if cache is None:
        k_b = k.astype(BF16)
        vt_b = v.transpose(0, 2, 1).astype(BF16)
        ki_b = ki.astype(BF16)
        att_o = _dsa_group(q, qi, kis, k_b, vt_b, ki_b, s_valid=T, q_offset=0, topk=min(TOPK_MAX, T // 4))
    else:
        ck, cv, cki = cache
        P = ck.shape[1]
        L = P + T
        pad = -(-L // LANES) * LANES - L
        k_b = jnp.concatenate([ck.reshape(B, P, ATT_W), k, jnp.zeros((B, pad, ATT_W), F32)], axis=1).astype(BF16)
        vt_b = jnp.concatenate([cv.reshape(B, P, ATT_W), v, jnp.zeros((B, pad, ATT_W), F32)],
                               axis=1).transpose(0, 2, 1).astype(BF16)
        ki_b = jnp.concatenate([cki, ki, jnp.zeros((B, pad, IDX_DH), F32)], axis=1).astype(BF16)
        att_o = _dsa_group(q, qi, kis, k_b, vt_b, ki_b, s_valid=L, q_offset=P, topk=min(TOPK_MAX, L // 4))

    res = _ffn(x, gdn_o, att_o, ssm_o, mod, w_out, norm2_g.reshape(1, -1), w_gate, w_up, ffn_conv_w, ffn_conv_b,
               w_down, ffn_conv_prev, final_g.reshape(1, -1), ffn_tt[0], ffn_tt[1], tf, final)
    if final:
        x_new, y, ffn_conv_new = res
    else:
        (x_new, ffn_conv_new), y = res, None
    st = (k.reshape(B, T, ATT_HEADS, ATT_DH), v.reshape(B, T, ATT_HEADS, ATT_DH), ki,
          gdn_conv_new, gdn_s, ssm_conv_new, ssm_h, ffn_conv_new)
    return x_new, y, st


def kernel(x_prompt, x_sample, c_prompt, c_sample, cache_k, cache_v, cache_kidx, state_gdn_conv, state_gdn,
           state_ssm_conv, state_ssm, state_ffn_conv, w_ada, b_ada, norm1_g, w_in, gdn_conv_w, gdn_A_log,
           gdn_dt_bias, gdn_norm_g, ssm_conv_w, ssm_conv_b, ssm_A_log, ssm_dt_bias, ssm_D, ssm_norm_g, w_out,
           norm2_g, w_gate, w_up, ffn_conv_w, ffn_conv_b, w_down, final_g):
    Bp, T, _ = x_prompt.shape
    Bs, Ts, _ = x_sample.shape
    P = cache_k.shape[2]

    c_all = jnp.concatenate([c_prompt, c_sample], axis=0)
    mod_all = _ada(c_all, w_ada, b_ada).reshape(DEPTH, Bp + Bs, 6, D_MODEL)

    tables_p = _rope_tables(jnp.arange(T))
    tables_s = _rope_tables(P + jnp.arange(Ts))

    perm = jnp.asarray(np.maximum(_PERM_COLS, 0))
    keep = jnp.asarray((_PERM_COLS >= 0).astype(np.float32))
    w_perm = (jnp.take(w_in, perm, axis=2) * keep).astype(BF16)
    w_out_b = w_out.astype(BF16)
    w_gate_b = w_gate.astype(BF16)
    w_up_b = w_up.astype(BF16)
    w_down_b = w_down.astype(BF16)

    zeros_p = (jnp.zeros((Bp, CONV_W - 1, GDN_CONV_C), F32),
               jnp.zeros((Bp, GDN_HEADS, GDN_DK, GDN_DV), F32),
               jnp.zeros((Bp, CONV_W - 1, SSM_CONV_C), F32),
               jnp.zeros((Bp, SSM_HEADS, SSM_P, SSM_N), F32),
               jnp.zeros((Bp, FFN_CONV_W - 1, D_FF), F32))

    tf = D_FF // 2
    tiles_p = (1, min(256, T), 4, (1, min(512, T)), tf)
    tiles_s = (Bs, Ts, 4, (Bs, Ts), tf)

    xp, xs = x_prompt, x_sample
    new_p, new_s = [], []
    yp = ys = None
    for l in range(DEPTH):
        lw = (norm1_g[l], w_perm[l], gdn_conv_w[l], gdn_A_log[l], gdn_dt_bias[l], gdn_norm_g[l], ssm_conv_w[l],
              ssm_conv_b[l], ssm_A_log[l], ssm_dt_bias[l], ssm_D[l], ssm_norm_g[l], w_out_b[l], norm2_g[l],
              w_gate_b[l], w_up_b[l], ffn_conv_w[l], ffn_conv_b[l], w_down_b[l])
        final = l == DEPTH - 1
        xp, yp, st_p = _layer(xp, mod_all[l, :Bp], tables_p, None, zeros_p, lw, tiles_p, final, final_g)
        states_s = (state_gdn_conv[l], state_gdn[l], state_ssm_conv[l], state_ssm[l], state_ffn_conv[l])
        xs, ys, st_s = _layer(xs, mod_all[l, Bp:], tables_s, (cache_k[l], cache_v[l], cache_kidx[l]), states_s,
                              lw, tiles_s, final, final_g)
        new_p.append(st_p)
        new_s.append(st_s)
    outs_p = [jnp.stack([st[i] for st in new_p]) for i in range(8)]
    outs_s = [jnp.stack([st[i] for st in new_s]) for i in range(8)]
    return (yp, ys, *outs_p, *outs_s)
```

```python
import functools
import math

import jax
import jax.numpy as jnp
import numpy as np
from jax import lax
from jax.experimental import pallas as pl
from jax.experimental.pallas import tpu as pltpu

F32 = jnp.float32
BF16 = jnp.bfloat16
HI = lax.Precision.HIGHEST

D_MODEL = 1024
DEPTH = 2
CHUNK = 64
CONV_W = 4
FFN_CONV_W = 3
D_FF = 2816
ROPE_THETA = 10000.0
EPS = 1e-6
GDN_HEADS = 4
GDN_DK = 64
GDN_DV = 64
ATT_HEADS = 4
ATT_DH = 64
IDX_HEADS = 4
IDX_DH = 64
TOPK_MAX = 256
SSM_HEADS = 8
SSM_P = 64
SSM_GROUPS = 2
SSM_N = 128
GDN_W = GDN_HEADS * GDN_DV
ATT_W = ATT_HEADS * ATT_DH
SSM_W = SSM_HEADS * SSM_P
MIX_W = GDN_W + ATT_W + SSM_W
GDN_CONV_C = 2 * GDN_HEADS * GDN_DK + GDN_W
SSM_CONV_C = SSM_W + 2 * SSM_GROUPS * SSM_N
IN_SIZES = (GDN_CONV_C, GDN_HEADS, GDN_HEADS, GDN_W,
            ATT_W, ATT_W, ATT_W, IDX_HEADS * IDX_DH, IDX_DH, IDX_HEADS,
            SSM_W, SSM_CONV_C, SSM_HEADS)
IN_W = sum(IN_SIZES)

LANES = 128
SUBLANES = 8
VMEM_LIMIT = 56 * 1024 * 1024

KIS_A = IDX_DH
KIS_B = KIS_A + GDN_HEADS
KIS_WI = KIS_B + GDN_HEADS
KIS_DT = KIS_WI + IDX_HEADS
KIS_END = KIS_DT + SSM_HEADS
OUT_SEGS = (("gqkv", GDN_CONV_C), ("ggate", GDN_W), ("q", ATT_W), ("k", ATT_W), ("v", ATT_W),
            ("qi", IDX_HEADS * IDX_DH), ("kis", LANES), ("z", SSM_W), ("xbc", SSM_CONV_C))
PERM_W = sum(w for _, w in OUT_SEGS)
ROT_W = ATT_W + LANES


def _perm_columns():
    starts = np.concatenate([[0], np.cumsum(IN_SIZES)])
    (s_gqkv, s_ga, s_gb, s_gg, s_q, s_k, s_v, s_qi, s_ki, s_wi, s_z, s_xbc, s_dt) = starts[:-1]
    cols = []
    cols += list(range(s_gqkv, s_gqkv + GDN_CONV_C))
    cols += list(range(s_gg, s_gg + GDN_W))
    cols += list(range(s_q, s_q + ATT_W))
    cols += list(range(s_k, s_k + ATT_W))
    cols += list(range(s_v, s_v + ATT_W))
    cols += list(range(s_qi, s_qi + IDX_HEADS * IDX_DH))
    kis = (list(range(s_ki, s_ki + IDX_DH)) + list(range(s_ga, s_ga + GDN_HEADS))
           + list(range(s_gb, s_gb + GDN_HEADS)) + list(range(s_wi, s_wi + IDX_HEADS))
           + list(range(s_dt, s_dt + SSM_HEADS)))
    cols += kis + [-1] * (LANES - len(kis))
    cols += list(range(s_z, s_z + SSM_W))
    cols += list(range(s_xbc, s_xbc + SSM_CONV_C))
    assert len(cols) == PERM_W
    return np.asarray(cols, np.int32)


_PERM_COLS = _perm_columns()


def _silu(x):
    return x * jax.nn.sigmoid(x)


def _softplus(x):
    return jnp.maximum(x, 0.0) + jnp.log1p(jnp.exp(-jnp.abs(x)))


def _dot(a, b, precision=None):
    return jnp.dot(a, b, preferred_element_type=F32, precision=precision)


def _dot_nt(a, b, precision=None):
    return lax.dot_general(a, b, (((1,), (1,)), ((), ())), preferred_element_type=F32,
                           precision=precision)


def _dot_tn(a, b, precision=None):
    return lax.dot_general(a, b, (((0,), (0,)), ((), ())), preferred_element_type=F32,
                           precision=precision)


def _cumsum_rows(x):
    c = x.shape[0]
    row = lax.broadcasted_iota(jnp.int32, x.shape, 0)
    s = 1
    while s < c:
        x = x + jnp.where(row >= s, pltpu.roll(x, s, axis=0), 0.0)
        s *= 2
    return x


def _mm(a, b):
    return _dot(a.astype(BF16), b.astype(BF16))


def _mm_nt(a, b):
    return _dot_nt(a.astype(BF16), b.astype(BF16))


def _split(a):
    hi = a.astype(BF16)
    return hi, (a - hi.astype(F32)).astype(BF16)


def _mm3(a, b):
    (ah, al), (bh, bl) = a, b
    return _dot(ah, bh) + (_dot(ah, bl) + _dot(al, bh))


def _head_sums(x):
    r = lax.broadcasted_iota(jnp.int32, (LANES, LANES), 0) // GDN_DV
    cidx = lax.broadcasted_iota(jnp.int32, (LANES, LANES), 1) // GDN_DV
    ones = (r == cidx).astype(BF16)
    hi, lo = _split(x)
    cols = []
    for g in range(x.shape[1] // LANES):
        sl = slice(g * LANES, (g + 1) * LANES)
        cols.append(_dot(hi[:, sl], ones) + _dot(lo[:, sl], ones))
    return jnp.concatenate(cols, axis=1)


def _transpose_rows(vals):
    c = vals.shape[0]
    if c < LANES:
        vals = jnp.concatenate([vals, jnp.zeros((LANES - c, LANES), vals.dtype)], axis=0)
    return vals.T[:, :c]


def _lane_vector(vals, start):
    return jnp.zeros((1, LANES), F32).at[0, start:start + vals.shape[0]].set(vals.astype(F32))


def _params(sem):
    return pltpu.CompilerParams(dimension_semantics=sem, vmem_limit_bytes=VMEM_LIMIT)


def _ada_kernel(c_ref, w_ref, b_ref, o_ref):
    s = _silu(c_ref[...])
    o_ref[0] = _dot(s.astype(BF16), w_ref[0].astype(BF16)) + b_ref[0]


def _ada(c_all, w_ada, b_ada):
    rows = c_all.shape[0]
    n = w_ada.shape[2]
    tn = 1536
    return pl.pallas_call(
        _ada_kernel,
        grid=(DEPTH, n // tn),
        in_specs=[pl.BlockSpec((rows, D_MODEL), lambda l, j: (0, 0)),
                  pl.BlockSpec((1, D_MODEL, tn), lambda l, j: (l, 0, j)),
                  pl.BlockSpec((1, 1, tn), lambda l, j: (l, 0, j))],
        out_specs=pl.BlockSpec((1, rows, tn), lambda l, j: (l, 0, j)),
        out_shape=jax.ShapeDtypeStruct((DEPTH, rows, n), F32),
        compiler_params=_params(("arbitrary", "arbitrary")),
        name="ada",
    )(c_all, w_ada, b_ada.reshape(DEPTH, 1, n))


def _rotate(x, cos, sin_signed):
    w = x.shape[-1]
    lane = lax.broadcasted_iota(jnp.int32, x.shape, x.ndim - 1)
    first = (lane % ATT_DH) < (ATT_DH // 2)
    swapped = jnp.where(first, pltpu.roll(x, w - ATT_DH // 2, axis=x.ndim - 1),
                        pltpu.roll(x, ATT_DH // 2, axis=x.ndim - 1))
    return x * cos + swapped * sin_signed


def _inproj_kernel(x_ref, mod_ref, g_ref, w_ref, cos_ref, sin_ref, *out_refs, nb, tt, emit_keys):
    x = x_ref[...]
    ms = jnp.mean(x * x, axis=-1, keepdims=True)
    xn = x * lax.rsqrt(ms + EPS) * g_ref[...]
    mod = mod_ref[...]
    h = xn * (1.0 + mod[:, 1:2, :]) + mod[:, 0:1, :]
    u = _dot(h.reshape(nb * tt, D_MODEL).astype(BF16), w_ref[...])
    cos = cos_ref[...]
    sin = sin_ref[...]
    off = 0
    segs = {}
    for (name, width), o_ref in zip(OUT_SEGS, out_refs):
        seg = u[:, off:off + width]
        if name in ("q", "k", "qi", "kis"):
            t0 = ATT_W if name == "kis" else 0
            c3 = cos[:, t0:t0 + width][None]
            s3 = sin[:, t0:t0 + width][None]
            if nb > 1:
                c3 = jnp.broadcast_to(c3, (nb, tt, width)).reshape(nb * tt, width)
                s3 = jnp.broadcast_to(s3, (nb, tt, width)).reshape(nb * tt, width)
            else:
                c3 = c3[0]
                s3 = s3[0]
            seg = _rotate(seg, c3, s3)
        o_ref[...] = seg.reshape(nb, tt, width)
        segs[name] = seg
        off += width
    if emit_keys:
        k_b_ref, ki_b_ref, vt_b_ref = out_refs[len(OUT_SEGS):]
        k_b_ref[0] = segs["k"].astype(BF16)
        ki_b_ref[0] = segs["kis"][:, :IDX_DH].astype(BF16)
        vt_b_ref[0] = segs["v"].T.astype(BF16)


def _inproj(x, mod, g1, w_perm, cos, sin, nb, tt, emit_keys):
    B, T, _ = x.shape
    grid = (B // nb, T // tt)
    out_shape = tuple(jax.ShapeDtypeStruct((B, T, w), F32) for _, w in OUT_SEGS)
    out_specs = tuple(pl.BlockSpec((nb, tt, w), lambda b, t: (b, t, 0)) for _, w in OUT_SEGS)
    if emit_keys:
        assert nb == 1
        out_shape += (jax.ShapeDtypeStruct((B, T, ATT_W), BF16), jax.ShapeDtypeStruct((B, T, IDX_DH), BF16),
                      jax.ShapeDtypeStruct((B, ATT_W, T), BF16))
        out_specs += (pl.BlockSpec((1, tt, ATT_W), lambda b, t: (b, t, 0)),
                      pl.BlockSpec((1, tt, IDX_DH), lambda b, t: (b, t, 0)),
                      pl.BlockSpec((1, ATT_W, tt), lambda b, t: (b, 0, t)))
    return pl.pallas_call(
        functools.partial(_inproj_kernel, nb=nb, tt=tt, emit_keys=emit_keys),
        grid=grid,
        in_specs=[pl.BlockSpec((nb, tt, D_MODEL), lambda b, t: (b, t, 0)),
                  pl.BlockSpec((nb, 6, D_MODEL), lambda b, t: (b, 0, 0)),
                  pl.BlockSpec((1, D_MODEL), lambda b, t: (0, 0)),
                  pl.BlockSpec((D_MODEL, PERM_W), lambda b, t: (0, 0)),
                  pl.BlockSpec((tt, ROT_W), lambda b, t: (t, 0)),
                  pl.BlockSpec((tt, ROT_W), lambda b, t: (t, 0))],
        out_specs=out_specs,
        out_shape=out_shape,
        compiler_params=_params(("arbitrary", "arbitrary")),
        name="inproj",
    )(x, mod, g1, w_perm, cos, sin)


def _short_conv(ext_ref, bi, u, w, c):
    lo = SUBLANES - (CONV_W - 1)
    ext_ref[bi, SUBLANES:SUBLANES + c, :] = u
    y = ext_ref[bi, lo:lo + c, :] * w[0:1, :]
    for j in range(1, CONV_W):
        y = y + ext_ref[bi, lo + j:lo + j + c, :] * w[j:j + 1, :]
    tail = ext_ref[bi, c + lo:c + SUBLANES, :]
    ext_ref[bi, lo:SUBLANES, :] = tail
    return y, tail


def _neumann_inverses(mats, c):
    eye = (lax.broadcasted_iota(jnp.int32, (c, c), 0)
           == lax.broadcasted_iota(jnp.int32, (c, c), 1)).astype(F32)
    ps = [eye - a for a in mats]
    sp = [_split(a) for a in mats]
    pws = [_mm3(s, s) for s in sp]
    n = 2
    while n < c:
        sp = [_split(pw) for pw in pws]
        ps = [p + _mm3(_split(p), s) for p, s in zip(ps, sp)]
        n *= 2
        if n < c:
            pws = [_mm3(s, s) for s in sp]
    return ps


def _gdn_kernel(qkv_ref, kis_ref, gate_ref, w_ref, prev_ref, s0_ref, alog_ref, dtb_ref, ng_ref,
                o_ref, new_ref, s_ref, ext_ref, s_sc, *, c, nbb):
    j = pl.program_id(1)
    lo = SUBLANES - (CONV_W - 1)

    @pl.when(j == 0)
    def _():
        s_sc[...] = s0_ref[...]
        ext_ref[:, lo:SUBLANES, :] = prev_ref[...]

    ng = ng_ref[...]
    w_conv = w_ref[...]
    ri = lax.broadcasted_iota(jnp.int32, (c, c), 0)
    ci = lax.broadcasted_iota(jnp.int32, (c, c), 1)
    incl = ri >= ci
    strict = ri > ci
    nkd = GDN_HEADS * GDN_DK
    ch = []
    for bi in range(nbb):
        y, tail = _short_conv(ext_ref, bi, qkv_ref[bi], w_conv, c)
        new_ref[bi] = tail
        y = _silu(y)
        small = kis_ref[bi]
        g = -jnp.exp(alog_ref[...]) * _softplus(small + dtb_ref[...])
        beta = jax.nn.sigmoid(small)
        gc = _cumsum_rows(g)
        gc_t = _transpose_rows(gc)
        qk_n = y[:, :2 * nkd]
        qk_n = qk_n * lax.rsqrt(_head_sums(qk_n * qk_n) + EPS)
        for h in range(GDN_HEADS):
            qh = qk_n[:, h * GDN_DK:(h + 1) * GDN_DK] * (GDN_DK ** -0.5)
            kh = qk_n[:, nkd + h * GDN_DK:nkd + (h + 1) * GDN_DK]
            vh = y[:, 2 * nkd + h * GDN_DV:2 * nkd + (h + 1) * GDN_DV]
            ch.append((qh, kh, vh, gc[:, KIS_A + h:KIS_A + h + 1], gc_t[KIS_A + h:KIS_A + h + 1, :],
                       beta[:, KIS_B + h:KIS_B + h + 1]))
    ids = [(bi, h) for bi in range(nbb) for h in range(GDN_HEADS)]
    decay = [jnp.where(incl, jnp.exp(jnp.where(incl, gcol - grow, 0.0)), 0.0)
             for (_, _, _, gcol, grow, _) in ch]
    kb = [kh * bcol for (_, kh, _, _, _, bcol) in ch]
    eg = [jnp.exp(gcol) for (_, _, _, gcol, _, _) in ch]
    grams = [_mm_nt(jnp.concatenate([kbi, x[0]], axis=0), x[1]) for kbi, x in zip(kb, ch)]
    t_inv = _neumann_inverses([jnp.where(strict, m[:c] * d, 0.0) for m, d in zip(grams, decay)], c)
    attn = [jnp.where(incl, m[c:] * d, 0.0) for m, d in zip(grams, decay)]
    rhs = [jnp.concatenate([x[2] * x[5], kbi * e], axis=1) for x, kbi, e in zip(ch, kb, eg)]
    sol = [_mm(t, r) for t, r in zip(t_inv, rhs)]
    st = [s_sc[bi, h] for bi, h in ids]
    ws = [_mm(sl[:, GDN_DV:], s) for sl, s in zip(sol, st)]
    qs = [_mm(x[0] * e, s) for x, e, s in zip(ch, eg, st)]
    v_new = [sl[:, :GDN_DV] - w for sl, w in zip(sol, ws)]
    av = [_mm(a, v) for a, v in zip(attn, v_new)]
    g_last = [x[3][c - 1:c, :] for x in ch]
    kv = [_dot_tn(x[1] * jnp.exp(gl - x[3]), v) for x, gl, v in zip(ch, g_last, v_new)]
    for (bi, h), s, gl, kvi in zip(ids, st, g_last, kv):
        s_new = s * jnp.exp(gl) + kvi
        s_sc[bi, h] = s_new
        s_ref[bi, h] = s_new
    for bi in range(nbb):
        o = jnp.concatenate([qs[bi * GDN_HEADS + h] + av[bi * GDN_HEADS + h] for h in range(GDN_HEADS)], axis=1)
        o = o * lax.rsqrt(_head_sums(o * o) * (1.0 / GDN_DV) + EPS) * ng
        o_ref[bi] = o * _silu(gate_ref[bi])


def _gdn(gqkv, kis, ggate, conv_w, conv_prev, s0, a_log, dt_bias, norm_g, nbb):
    B, T, _ = gqkv.shape
    c = math.gcd(T, CHUNK)
    return pl.pallas_call(
        functools.partial(_gdn_kernel, c=c, nbb=nbb),
        grid=(B // nbb, T // c),
        in_specs=[pl.BlockSpec((nbb, c, GDN_CONV_C), lambda b, j: (b, j, 0)),
                  pl.BlockSpec((nbb, c, LANES), lambda b, j: (b, j, 0)),
                  pl.BlockSpec((nbb, c, GDN_W), lambda b, j: (b, j, 0)),
                  pl.BlockSpec((CONV_W, GDN_CONV_C), lambda b, j: (0, 0)),
                  pl.BlockSpec((nbb, CONV_W - 1, GDN_CONV_C), lambda b, j: (b, 0, 0)),
                  pl.BlockSpec((nbb, GDN_HEADS, GDN_DK, GDN_DV), lambda b, j: (b, 0, 0, 0)),
                  pl.BlockSpec((1, LANES), lambda b, j: (0, 0)),
                  pl.BlockSpec((1, LANES), lambda b, j: (0, 0)),
                  pl.BlockSpec((1, GDN_W), lambda b, j: (0, 0))],
        out_specs=(pl.BlockSpec((nbb, c, GDN_W), lambda b, j: (b, j, 0)),
                   pl.BlockSpec((nbb, CONV_W - 1, GDN_CONV_C), lambda b, j: (b, 0, 0)),
                   pl.BlockSpec((nbb, GDN_HEADS, GDN_DK, GDN_DV), lambda b, j: (b, 0, 0, 0))),
        out_shape=(jax.ShapeDtypeStruct((B, T, GDN_W), F32),
                   jax.ShapeDtypeStruct((B, CONV_W - 1, GDN_CONV_C), F32),
                   jax.ShapeDtypeStruct((B, GDN_HEADS, GDN_DK, GDN_DV), F32)),
        scratch_shapes=[pltpu.VMEM((nbb, SUBLANES + c, GDN_CONV_C), F32),
                        pltpu.VMEM((nbb, GDN_HEADS, GDN_DK, GDN_DV), F32)],
        compiler_params=_params(("arbitrary", "arbitrary")),
        name="gdn",
    )(gqkv, kis, ggate, conv_w, conv_prev, s0, _lane_vector(a_log, KIS_A), _lane_vector(dt_bias, KIS_A),
      jnp.tile(norm_g, GDN_HEADS).reshape(1, -1))


def _ssd_kernel(xbc_ref, z_ref, kis_ref, w_ref, cb_ref, prev_ref, h0_ref, alog_ref, dtb_ref, dsk_ref,
                ng_ref, o_ref, new_ref, h_ref, ext_ref, h_sc, *, c, nbb):
    j = pl.program_id(1)
    lo = SUBLANES - (CONV_W - 1)

    @pl.when(j == 0)
    def _():
        h_sc[...] = h0_ref[...]
        ext_ref[:, lo:SUBLANES, :] = prev_ref[...]

    dsk = dsk_ref[...]
    ng = ng_ref[...]
    w_conv = w_ref[...]
    ri = lax.broadcasted_iota(jnp.int32, (c, c), 0)
    ci = lax.broadcasted_iota(jnp.int32, (c, c), 1)
    incl = ri >= ci
    gn = SSM_GROUPS * SSM_N
    rep = SSM_HEADS // SSM_GROUPS
    gw = SSM_W // SSM_GROUPS
    grp_in = []
    ch = []
    ids = []
    for bi in range(nbb):
        y, tail = _short_conv(ext_ref, bi, xbc_ref[bi], w_conv, c)
        new_ref[bi] = tail
        y = _silu(y + cb_ref[...])
        dts = _softplus(kis_ref[bi] + dtb_ref[...])
        a = dts * (-jnp.exp(alog_ref[...]))
        acum = _cumsum_rows(a)
        acum_t = _transpose_rows(acum)
        zg = _silu(z_ref[bi])
        for grp in range(SSM_GROUPS):
            grp_in.append((y[:, SSM_W + grp * SSM_N:SSM_W + (grp + 1) * SSM_N],
                           y[:, SSM_W + gn + grp * SSM_N:SSM_W + gn + (grp + 1) * SSM_N],
                           zg[:, grp * gw:(grp + 1) * gw]))
            for hh in range(rep):
                h = grp * rep + hh
                ch.append((y[:, h * SSM_P:(h + 1) * SSM_P], acum[:, KIS_DT + h:KIS_DT + h + 1],
                           acum_t[KIS_DT + h:KIS_DT + h + 1, :], dts[:, KIS_DT + h:KIS_DT + h + 1],
                           len(grp_in) - 1))
                ids.append((bi, h))
    cb = [_mm_nt(cg, bg) for bg, cg, _ in grp_in]
    seg = [jnp.where(incl, jnp.exp(jnp.where(incl, acol - arow, 0.0)), 0.0) for (_, acol, arow, _, _) in ch]
    hs = [h_sc[bi, h] for bi, h in ids]
    y_in = [_mm(cb[x[4]] * sg, x[0] * x[3]) for x, sg in zip(ch, seg)]
    y_st = [_mm_nt(grp_in[x[4]][1], hst) for x, hst in zip(ch, hs)]
    a_last = [x[1][c - 1:c, :] for x in ch]
    upd = [_dot_tn(x[0], grp_in[x[4]][0] * (jnp.exp(al - x[1]) * x[3])) for x, al in zip(ch, a_last)]
    outs = []
    for (bi, h), x, hst, al, yi, ys, up in zip(ids, ch, hs, a_last, y_in, y_st, upd):
        h_new = hst * jnp.exp(al) + up
        h_sc[bi, h] = h_new
        h_ref[bi, h] = h_new
        outs.append(yi + ys * jnp.exp(x[1]) + dsk[:, h:h + 1] * x[0])
    for gi, (_, _, zslice) in enumerate(grp_in):
        bi, grp = divmod(gi, SSM_GROUPS)
        yg = jnp.concatenate(outs[gi * rep:(gi + 1) * rep], axis=1) * zslice
        yg = yg * lax.rsqrt(jnp.mean(yg * yg, axis=-1, keepdims=True) + EPS)
        o_ref[bi, :, grp * gw:(grp + 1) * gw] = yg * ng[:, grp * gw:(grp + 1) * gw]


def _ssd(xbc, z, kis, conv_w, conv_b, conv_prev, h0, a_log, dt_bias, d_skip, norm_g, nbb):
    B, T, _ = xbc.shape
    c = math.gcd(T, CHUNK)
    return pl.pallas_call(
        functools.partial(_ssd_kernel, c=c, nbb=nbb),
        grid=(B // nbb, T // c),
        in_specs=[pl.BlockSpec((nbb, c, SSM_CONV_C), lambda b, j: (b, j, 0)),
                  pl.BlockSpec((nbb, c, SSM_W), lambda b, j: (b, j, 0)),
                  pl.BlockSpec((nbb, c, LANES), lambda b, j: (b, j, 0)),
                  pl.BlockSpec((CONV_W, SSM_CONV_C), lambda b, j: (0, 0)),
                  pl.BlockSpec((1, SSM_CONV_C), lambda b, j: (0, 0)),
                  pl.BlockSpec((nbb, CONV_W - 1, SSM_CONV_C), lambda b, j: (b, 0, 0)),
                  pl.BlockSpec((nbb, SSM_HEADS, SSM_P, SSM_N), lambda b, j: (b, 0, 0, 0)),
                  pl.BlockSpec((1, LANES), lambda b, j: (0, 0)),
                  pl.BlockSpec((1, LANES), lambda b, j: (0, 0)),
                  pl.BlockSpec((1, SSM_HEADS), lambda b, j: (0, 0)),
                  pl.BlockSpec((1, SSM_W), lambda b, j: (0, 0))],
        out_specs=(pl.BlockSpec((nbb, c, SSM_W), lambda b, j: (b, j, 0)),
                   pl.BlockSpec((nbb, CONV_W - 1, SSM_CONV_C), lambda b, j: (b, 0, 0)),
                   pl.BlockSpec((nbb, SSM_HEADS, SSM_P, SSM_N), lambda b, j: (b, 0, 0, 0))),
        out_shape=(jax.ShapeDtypeStruct((B, T, SSM_W), F32),
                   jax.ShapeDtypeStruct((B, CONV_W - 1, SSM_CONV_C), F32),
                   jax.ShapeDtypeStruct((B, SSM_HEADS, SSM_P, SSM_N), F32)),
        scratch_shapes=[pltpu.VMEM((nbb, SUBLANES + c, SSM_CONV_C), F32),
                        pltpu.VMEM((nbb, SSM_HEADS, SSM_P, SSM_N), F32)],
        compiler_params=_params(("arbitrary", "arbitrary")),
        name="ssd",
    )(xbc, z, kis, conv_w, conv_b.reshape(1, -1), conv_prev, h0, _lane_vector(a_log, KIS_DT),
      _lane_vector(dt_bias, KIS_DT), d_skip.reshape(1, -1), norm_g.reshape(1, -1))


IDX_BITS = 12


SEARCH_STEPS = 6
SEARCH_ROUNDS = 128
FOLD_ROWS = 128


def _fold_rows(x, op):
    s = x.shape[0]
    if s % FOLD_ROWS == 0 and s > FOLD_ROWS:
        x = op(x.reshape(s // FOLD_ROWS, FOLD_ROWS, x.shape[1]), axis=0)
    return op(x, axis=0, keepdims=True)


def _count(mask):
    return _fold_rows(mask.astype(F32), jnp.sum)


def _dsa_kernel(q_ref, qi_ref, kis_ref, k_ref, vt_ref, ki_ref, o_ref, *, tq, s_len, s_valid, q_offset, j0,
                n_valid_q, topk):
    j = pl.program_id(1)
    lane = lax.broadcasted_iota(jnp.int32, (1, tq), 1)
    qpos = q_offset + (j0 + j) * tq + lane
    lim = jnp.minimum((qpos // CHUNK + 1) * CHUNK, s_valid)
    row = lax.broadcasted_iota(jnp.int32, (s_len, tq), 0)
    adm = row < lim

    qi_t = qi_ref[0].T.astype(BF16)
    kis_t = kis_ref[0].T
    ki = ki_ref[0]
    assert IDX_DH == 64 and IDX_HEADS == 4 and ATT_DH == 64
    wi = kis_t[KIS_WI:KIS_WI + IDX_HEADS, :] * (IDX_DH ** -0.5 * IDX_HEADS ** -0.5)
    qi_all = jnp.concatenate([qi_t[h * IDX_DH:(h + 1) * IDX_DH, :] for h in range(IDX_HEADS)], axis=1)
    rel_all = _dot(ki, qi_all)
    score = jnp.zeros((s_len, tq), F32)
    for h in range(IDX_HEADS):
        score = score + jnp.maximum(rel_all[:, h * tq:(h + 1) * tq], 0.0) * wi[h:h + 1, :]
    s = jnp.where(adm, score, -jnp.inf)

    n_adm = _count(adm)
    few = (n_adm < topk) | (lane >= n_valid_q)
    smax = _fold_rows(s, jnp.max)
    smin = _fold_rows(jnp.where(adm, score, jnp.inf), jnp.min)
    c_top = _count(s >= smax)
    top_full = c_top >= topk
    lo0 = jnp.where(few, 0.0, jnp.where(top_full, smax, smin))
    hi0 = jnp.where(few, 0.0, smax)
    done0 = (few | top_full).astype(F32)

    def search_cond(c):
        return (jnp.min(c[4]) < 0.5) & (c[5] < SEARCH_ROUNDS)

    def search_body(c):
        lo, hi, c_lo, c_hi, done, it = c
        for step in range(SEARCH_STEPS):
            if step % 2 == 0:
                frac = jnp.clip((c_lo - topk + 0.5) / (c_lo - c_hi), 1.0 / 32, 31.0 / 32)
                mid = lo + (hi - lo) * frac
            else:
                mid = 0.5 * lo + 0.5 * hi
            ok = (mid > lo) & (mid < hi) & (done < 0.5)
            cnt = _count(s >= mid)
            up = ok & (cnt >= topk)
            dn = ok & (cnt < topk)
            lo, c_lo = jnp.where(up, mid, lo), jnp.where(up, cnt, c_lo)
            hi, c_hi = jnp.where(dn, mid, hi), jnp.where(dn, cnt, c_hi)
        first = _fold_rows(jnp.where(s >= lo, s, jnp.inf), jnp.min)
        last = _fold_rows(jnp.where(s < hi, s, -jnp.inf), jnp.max)
        lo = jnp.where(done < 0.5, first, lo)
        done = jnp.maximum(done, (first >= last).astype(F32))
        return lo, hi, c_lo, c_hi, done, it + 1

    lo, _, _, _, _, _ = lax.while_loop(search_cond, search_body, (lo0, hi0, n_adm, c_top, done0, jnp.int32(0)))
    t = jnp.where(few, -jnp.inf, lo)
    gt = s > t
    tie = s == t
    need = topk - _count(gt)
    excess = jnp.where(few, 0.0, _count(tie) - need)

    def tie_search():
        def ibody(i, m):
            cand = m | (jnp.int32(1) << (IDX_BITS - 1 - i))
            return jnp.where(_count(tie & (row < cand)) <= need, cand, m)

        return lax.fori_loop(0, IDX_BITS, ibody, jnp.zeros((1, tq), jnp.int32))

    m = lax.cond(jnp.max(excess) > 0.0, tie_search, lambda: jnp.full((1, tq), 2 ** IDX_BITS - 1, jnp.int32))
    sel = adm & (gt | (tie & (row < m)))

    q_t = q_ref[0].T * (ATT_DH ** -0.5)
    pair = LANES // ATT_DH
    head_in_pair = lax.broadcasted_iota(jnp.int32, (LANES, tq), 0) // ATT_DH
    logits = []
    for g in range(ATT_HEADS // pair):
        qg = q_t[g * LANES:(g + 1) * LANES, :]
        qm = jnp.concatenate([jnp.where(head_in_pair == i, qg, 0.0) for i in range(pair)], axis=1)
        lg = _dot(k_ref[0, :, g * LANES:(g + 1) * LANES], qm.astype(BF16))
        logits += [lg[:, i * tq:(i + 1) * tq] for i in range(pair)]
    probs, scales = [], []
    for lg in logits:
        lg = jnp.where(sel, lg, -jnp.inf)
        p = jnp.exp(lg - _fold_rows(lg, jnp.max))
        scales.append(1.0 / _fold_rows(p, jnp.sum))
        probs.append(p.astype(BF16))
    outs = [_dot(vt_ref[0, h * ATT_DH:(h + 1) * ATT_DH, :], probs[h]) * scales[h] for h in range(ATT_HEADS)]
    o_ref[0] = jnp.concatenate(outs, axis=0).T


def _dsa(q, qi, kis, k_b, vt_b, ki_b, *, tq, j0, nj, s_len, s_valid, q_offset, n_valid_q, topk):
    B = q.shape[0]
    assert s_len < 2 ** IDX_BITS and s_len <= k_b.shape[1]
    return pl.pallas_call(
        functools.partial(_dsa_kernel, tq=tq, s_len=s_len, s_valid=s_valid, q_offset=q_offset, j0=j0,
                          n_valid_q=n_valid_q, topk=topk),
        grid=(B, nj),
        in_specs=[pl.BlockSpec((1, tq, ATT_W), lambda b, j: (b, j0 + j, 0)),
                  pl.BlockSpec((1, tq, IDX_HEADS * IDX_DH), lambda b, j: (b, j0 + j, 0)),
                  pl.BlockSpec((1, tq, LANES), lambda b, j: (b, j0 + j, 0)),
                  pl.BlockSpec((1, s_len, ATT_W), lambda b, j: (b, 0, 0)),
                  pl.BlockSpec((1, ATT_W, s_len), lambda b, j: (b, 0, 0)),
                  pl.BlockSpec((1, s_len, IDX_DH), lambda b, j: (b, 0, 0))],
        out_specs=pl.BlockSpec((1, tq, ATT_W), lambda b, j: (b, j, 0)),
        out_shape=jax.ShapeDtypeStruct((B, nj * tq, ATT_W), F32),
        compiler_params=_params(("arbitrary", "arbitrary")),
        name="dsa",
    )(q, qi, kis, k_b, vt_b, ki_b)


def _dsa_group(q, qi, kis, k_b, vt_b, ki_b, *, s_valid, q_offset, topk):
    B, T, _ = q.shape
    tq = 2 * LANES if T % (2 * LANES) == 0 else LANES
    n_valid_q = min(T, tq)
    if T < tq:
        padq = lambda a: jnp.pad(a, ((0, 0), (0, tq - T), (0, 0)))
        q, qi, kis = padq(q), padq(qi), padq(kis)
    nq = q.shape[1] // tq
    s_total = k_b.shape[1]
    outs = []
    j0 = 0
    while j0 < nq:
        nj = 1
        reach = -(-(q_offset + (j0 + nj) * tq) // CHUNK) * CHUNK
        s_len = min(s_total, -(-min(reach, s_valid) // LANES) * LANES)
        outs.append(_dsa(q, qi, kis, k_b, vt_b, ki_b, tq=tq, j0=j0, nj=nj, s_len=s_len, s_valid=s_valid,
                         q_offset=q_offset, n_valid_q=n_valid_q, topk=topk))
        j0 += nj
    out = outs[0] if len(outs) == 1 else jnp.concatenate(outs, axis=1)
    return out[:, :T]


def _ffn_kernel(x_ref, gdn_ref, att_ref, ssm_ref, mod_ref, wo_ref, g2_ref, wg_ref, wu_ref, cw_ref, cb_ref,
                wd_ref, prev_ref, fg_ref, *rest, nb, tt, nk, final):
    if final:
        xo_ref, y_ref, new_ref, h2_sc, acc_sc, buf_sc, carry_sc = rest
    else:
        xo_ref, new_ref, h2_sc, acc_sc, buf_sc, carry_sc = rest
        y_ref = None
    t = pl.program_id(1)
    k = pl.program_id(2)
    rows = nb * tt
    lo = SUBLANES - (FFN_CONV_W - 1)
    mod = mod_ref[...]

    @pl.when(k == 0)
    def _():
        mix = jnp.concatenate([gdn_ref[...], att_ref[...], ssm_ref[...]], axis=-1)
        proj = _dot(mix.reshape(rows, MIX_W).astype(BF16), wo_ref[...]).reshape(nb, tt, D_MODEL)
        x1 = x_ref[...] + mod[:, 2:3, :] * proj
        acc_sc[...] = x1
        ms = jnp.mean(x1 * x1, axis=-1, keepdims=True)
        h2 = x1 * lax.rsqrt(ms + EPS) * g2_ref[...]
        h2 = h2 * (1.0 + mod[:, 4:5, :]) + mod[:, 3:4, :]
        h2_sc[...] = h2.reshape(rows, D_MODEL).astype(BF16)

    @pl.when(t == 0)
    def _():
        carry_sc[k, :, lo:SUBLANES, :] = prev_ref[...]

    h2 = h2_sc[...]
    ag = _dot(h2, wg_ref[...])
    up = _dot(h2, wu_ref[...])
    tf = ag.shape[-1]
    buf_sc[:, lo:SUBLANES, :] = carry_sc[k, :, lo:SUBLANES, :]
    buf_sc[:, SUBLANES:SUBLANES + tt, :] = ag.reshape(nb, tt, tf)
    cw = cw_ref[...]
    conv = buf_sc[:, lo:lo + tt, :] * cw[0:1, :][None]
    for jj in range(1, FFN_CONV_W):
        conv = conv + buf_sc[:, lo + jj:lo + jj + tt, :] * cw[jj:jj + 1, :][None]
    tail = buf_sc[:, tt + lo:tt + SUBLANES, :]
    carry_sc[k, :, lo:SUBLANES, :] = tail
    new_ref[:, pl.ds(k, 1), :, :] = tail[:, None]
    act = _silu(conv + cb_ref[...][None]).reshape(rows, tf) * up
    y = _dot(act.astype(BF16), wd_ref[...]).reshape(nb, tt, D_MODEL)
    acc_sc[...] += mod[:, 5:6, :] * y

    @pl.when(k == nk - 1)
    def _():
        xo = acc_sc[...]
        xo_ref[...] = xo
        if final:
            ms = jnp.mean(xo * xo, axis=-1, keepdims=True)
            y_ref[...] = xo * lax.rsqrt(ms + EPS) * fg_ref[...]


def _ffn(x, gdn_o, att_o, ssm_o, mod, w_out, g2, w_gate, w_up, conv_w, conv_b, w_down, conv_prev, final_g,
         nb, tt, tf, final):
    B, T, _ = x.shape
    nk = D_FF // tf
    grid = (B // nb, T // tt, nk)
    row_spec = lambda w: pl.BlockSpec((nb, tt, w), lambda b, t, k: (b, t, 0))
    in_specs = [row_spec(D_MODEL), row_spec(GDN_W), row_spec(ATT_W), row_spec(SSM_W),
                pl.BlockSpec((nb, 6, D_MODEL), lambda b, t, k: (b, 0, 0)),
                pl.BlockSpec((MIX_W, D_MODEL), lambda b, t, k: (0, 0)),
                pl.BlockSpec((1, D_MODEL), lambda b, t, k: (0, 0)),
                pl.BlockSpec((D_MODEL, tf), lambda b, t, k: (0, k)),
                pl.BlockSpec((D_MODEL, tf), lambda b, t, k: (0, k)),
                pl.BlockSpec((FFN_CONV_W, tf), lambda b, t, k: (0, k)),
                pl.BlockSpec((1, tf), lambda b, t, k: (0, k)),
                pl.BlockSpec((tf, D_MODEL), lambda b, t, k: (k, 0)),
                pl.BlockSpec((nb, FFN_CONV_W - 1, tf), lambda b, t, k: (b, 0, k)),
                pl.BlockSpec((1, D_MODEL), lambda b, t, k: (0, 0))]
    out_shape = [jax.ShapeDtypeStruct((B, T, D_MODEL), F32)]
    out_specs = [row_spec(D_MODEL)]
    if final:
        out_shape.append(jax.ShapeDtypeStruct((B, T, D_MODEL), F32))
        out_specs.append(row_spec(D_MODEL))
    out_shape.append(jax.ShapeDtypeStruct((B, nk, FFN_CONV_W - 1, tf), F32))
    out_specs.append(pl.BlockSpec((nb, nk, FFN_CONV_W - 1, tf), lambda b, t, k: (b, 0, 0, 0)))
    res = pl.pallas_call(
        functools.partial(_ffn_kernel, nb=nb, tt=tt, nk=nk, final=final),
        grid=grid,
        in_specs=in_specs,
        out_specs=tuple(out_specs),
        out_shape=tuple(out_shape),
        scratch_shapes=[pltpu.VMEM((nb * tt, D_MODEL), BF16),
                        pltpu.VMEM((nb, tt, D_MODEL), F32),
                        pltpu.VMEM((nb, SUBLANES + tt, tf), F32),
                        pltpu.VMEM((nk, nb, SUBLANES, tf), F32)],
        compiler_params=_params(("arbitrary", "arbitrary", "arbitrary")),
        name="ffn",
    )(x, gdn_o, att_o, ssm_o, mod, w_out, g2, w_gate, w_up, conv_w, conv_b.reshape(1, -1), w_down,
      conv_prev, final_g)
    tail = res[-1].transpose(0, 2, 1, 3).reshape(B, FFN_CONV_W - 1, D_FF)
    return (*res[:-1], tail)


def _rope_tables(pos):
    half = ATT_DH // 2
    inv_freq = ROPE_THETA ** (-jnp.arange(half, dtype=F32) / half)
    ang = pos.astype(F32)[:, None] * inv_freq[None, :]
    cos = jnp.cos(ang)
    sin = jnp.sin(ang)
    cos_h = jnp.concatenate([cos, cos], axis=-1)
    sin_h = jnp.concatenate([-sin, sin], axis=-1)
    ones = jnp.ones((pos.shape[0], LANES - ATT_DH), F32)
    cos_t = jnp.concatenate([jnp.tile(cos_h, (1, ATT_HEADS)), cos_h, ones], axis=-1)
    sin_t = jnp.concatenate([jnp.tile(sin_h, (1, ATT_HEADS)), sin_h, 0.0 * ones], axis=-1)
    return cos_t, sin_t


def _layer(x, mod, pos_tables, cache, states, lw, tiles, final, final_g):
    (norm1_g, w_perm, gdn_conv_w, gdn_A_log, gdn_dt_bias, gdn_norm_g, ssm_conv_w, ssm_conv_b, ssm_A_log,
     ssm_dt_bias, ssm_D, ssm_norm_g, w_out, norm2_g, w_gate, w_up, ffn_conv_w, ffn_conv_b, w_down) = lw
    gdn_conv_prev, gdn_s0, ssm_conv_prev, ssm_h0, ffn_conv_prev = states
    B, T, _ = x.shape
    nb, tt, nbb, ffn_tt, tf = tiles
    cos_t, sin_t = pos_tables
    proj = _inproj(x, mod, norm1_g.reshape(1, -1), w_perm, cos_t, sin_t, nb, tt, emit_keys=cache is None)
    gqkv, ggate, q, k, v, qi, kis, z, xbc = proj[:len(OUT_SEGS)]
    ki = kis[..., :IDX_DH]

    gdn_o, gdn_conv_new, gdn_s = _gdn(gqkv, kis, ggate, gdn_conv_w, gdn_conv_prev, gdn_s0, gdn_A_log,
                                      gdn_dt_bias, gdn_norm_g, nbb)
    ssm_o, ssm_conv_new, ssm_h = _ssd(xbc, z, kis, ssm_conv_w, ssm_conv_b, ssm_conv_prev, ssm_h0, ssm_A_log,
                                      ssm_dt_bias, ssm_D, ssm_norm_g, nbb)
    if cache is None:
        k_b, ki_b, vt_b = proj[len(OUT_SEGS):]
        att_o = _dsa_group(q, qi, kis, k_b, vt_b, ki_b, s_valid=T, q_offset=0, topk=min(TOPK_MAX, T // 4))
    else:
        ck, cv, cki = cache
        P = ck.shape[1]
        L = P + T
        pad = -(-L // LANES) * LANES - L
        k_b = jnp.concatenate([ck.reshape(B, P, ATT_W), k, jnp.zeros((B, pad, ATT_W), F32)], axis=1).astype(BF16)
        vt_b = jnp.concatenate([cv.reshape(B, P, ATT_W), v, jnp.zeros((B, pad, ATT_W), F32)],
                               axis=1).transpose(0, 2, 1).astype(BF16)
        ki_b = jnp.concatenate([cki, ki, jnp.zeros((B, pad, IDX_DH), F32)], axis=1).astype(BF16)
        att_o = _dsa_group(q, qi, kis, k_b, vt_b, ki_b, s_valid=L, q_offset=P, topk=min(TOPK_MAX, L // 4))

    res = _ffn(x, gdn_o, att_o, ssm_o, mod, w_out, norm2_g.reshape(1, -1), w_gate, w_up, ffn_conv_w, ffn_conv_b,
               w_down, ffn_conv_prev, final_g.reshape(1, -1), ffn_tt[0], ffn_tt[1], tf, final)
    if final:
        x_new, y, ffn_conv_new = res
    else:
        (x_new, ffn_conv_new), y = res, None
    st = (k.reshape(B, T, ATT_HEADS, ATT_DH), v.reshape(B, T, ATT_HEADS, ATT_DH), ki,
          gdn_conv_new, gdn_s, ssm_conv_new, ssm_h, ffn_conv_new)
    return x_new, y, st


def kernel(x_prompt, x_sample, c_prompt, c_sample, cache_k, cache_v, cache_kidx, state_gdn_conv, state_gdn,
           state_ssm_conv, state_ssm, state_ffn_conv, w_ada, b_ada, norm1_g, w_in, gdn_conv_w, gdn_A_log,
           gdn_dt_bias, gdn_norm_g, ssm_conv_w, ssm_conv_b, ssm_A_log, ssm_dt_bias, ssm_D, ssm_norm_g, w_out,
           norm2_g, w_gate, w_up, ffn_conv_w, ffn_conv_b, w_down, final_g):
    Bp, T, _ = x_prompt.shape
    Bs, Ts, _ = x_sample.shape
    P = cache_k.shape[2]

    c_all = jnp.concatenate([c_prompt, c_sample], axis=0)
    mod_all = _ada(c_all, w_ada, b_ada).reshape(DEPTH, Bp + Bs, 6, D_MODEL)

    tables_p = _rope_tables(jnp.arange(T))
    tables_s = _rope_tables(P + jnp.arange(Ts))

    perm = jnp.asarray(np.maximum(_PERM_COLS, 0))
    keep = jnp.asarray((_PERM_COLS >= 0).astype(np.float32))
    w_perm = (jnp.take(w_in, perm, axis=2) * keep).astype(BF16)
    w_out_b = w_out.astype(BF16)
    w_gate_b = w_gate.astype(BF16)
    w_up_b = w_up.astype(BF16)
    w_down_b = w_down.astype(BF16)

    zeros_p = (jnp.zeros((Bp, CONV_W - 1, GDN_CONV_C), F32),
               jnp.zeros((Bp, GDN_HEADS, GDN_DK, GDN_DV), F32),
               jnp.zeros((Bp, CONV_W - 1, SSM_CONV_C), F32),
               jnp.zeros((Bp, SSM_HEADS, SSM_P, SSM_N), F32),
               jnp.zeros((Bp, FFN_CONV_W - 1, D_FF), F32))

    tf = D_FF // 2
    tiles_p = (1, min(256, T), math.gcd(Bp, 4), (1, min(512, T)), tf)
    tiles_s = (Bs, Ts, math.gcd(Bs, 4), (Bs, Ts), tf)

    xp, xs = x_prompt, x_sample
    new_p, new_s = [], []
    yp = ys = None
    for l in range(DEPTH):
        lw = (norm1_g[l], w_perm[l], gdn_conv_w[l], gdn_A_log[l], gdn_dt_bias[l], gdn_norm_g[l], ssm_conv_w[l],
              ssm_conv_b[l], ssm_A_log[l], ssm_dt_bias[l], ssm_D[l], ssm_norm_g[l], w_out_b[l], norm2_g[l],
              w_gate_b[l], w_up_b[l], ffn_conv_w[l], ffn_conv_b[l], w_down_b[l])
        final = l == DEPTH - 1
        xp, yp, st_p = _layer(xp, mod_all[l, :Bp], tables_p, None, zeros_p, lw, tiles_p, final, final_g)
        states_s = (state_gdn_conv[l], state_gdn[l], state_ssm_conv[l], state_ssm[l], state_ffn_conv[l])
        xs, ys, st_s = _layer(xs, mod_all[l, Bp:], tables_s, (cache_k[l], cache_v[l], cache_kidx[l]), states_s,
                              lw, tiles_s, final, final_g)
        new_p.append(st_p)
        new_s.append(st_s)
    outs_p = [jnp.stack([st[i] for st in new_p]) for i in range(8)]
    outs_s = [jnp.stack([st[i] for st in new_s]) for i in range(8)]
    return (yp, ys, *outs_p, *outs_s)
```

```python
import functools
import math

import jax
import jax.numpy as jnp
import numpy as np
from jax import lax
from jax.experimental import pallas as pl
from jax.experimental.pallas import tpu as pltpu

F32 = jnp.float32
BF16 = jnp.bfloat16
HI = lax.Precision.HIGHEST

D_MODEL = 1024
DEPTH = 2
CHUNK = 64
CONV_W = 4
FFN_CONV_W = 3
D_FF = 2816
ROPE_THETA = 10000.0
EPS = 1e-6
GDN_HEADS = 4
GDN_DK = 64
GDN_DV = 64
ATT_HEADS = 4
ATT_DH = 64
IDX_HEADS = 4
IDX_DH = 64
TOPK_MAX = 256
SSM_HEADS = 8
SSM_P = 64
SSM_GROUPS = 2
SSM_N = 128
GDN_W = GDN_HEADS * GDN_DV
ATT_W = ATT_HEADS * ATT_DH
SSM_W = SSM_HEADS * SSM_P
MIX_W = GDN_W + ATT_W + SSM_W
GDN_CONV_C = 2 * GDN_HEADS * GDN_DK + GDN_W
SSM_CONV_C = SSM_W + 2 * SSM_GROUPS * SSM_N
IN_SIZES = (GDN_CONV_C, GDN_HEADS, GDN_HEADS, GDN_W,
            ATT_W, ATT_W, ATT_W, IDX_HEADS * IDX_DH, IDX_DH, IDX_HEADS,
            SSM_W, SSM_CONV_C, SSM_HEADS)
IN_W = sum(IN_SIZES)

LANES = 128
SUBLANES = 8
VMEM_LIMIT = 56 * 1024 * 1024

KIS_A = IDX_DH
KIS_B = KIS_A + GDN_HEADS
KIS_WI = KIS_B + GDN_HEADS
KIS_DT = KIS_WI + IDX_HEADS
KIS_END = KIS_DT + SSM_HEADS
OUT_SEGS = (("gqkv", GDN_CONV_C), ("ggate", GDN_W), ("q", ATT_W), ("k", ATT_W), ("v", ATT_W),
            ("qi", IDX_HEADS * IDX_DH), ("kis", LANES), ("z", SSM_W), ("xbc", SSM_CONV_C))
PERM_W = sum(w for _, w in OUT_SEGS)
ROT_W = ATT_W + LANES


def _perm_columns():
    starts = np.concatenate([[0], np.cumsum(IN_SIZES)])
    (s_gqkv, s_ga, s_gb, s_gg, s_q, s_k, s_v, s_qi, s_ki, s_wi, s_z, s_xbc, s_dt) = starts[:-1]
    cols = []
    cols += list(range(s_gqkv, s_gqkv + GDN_CONV_C))
    cols += list(range(s_gg, s_gg + GDN_W))
    cols += list(range(s_q, s_q + ATT_W))
    cols += list(range(s_k, s_k + ATT_W))
    cols += list(range(s_v, s_v + ATT_W))
    cols += list(range(s_qi, s_qi + IDX_HEADS * IDX_DH))
    kis = (list(range(s_ki, s_ki + IDX_DH)) + list(range(s_ga, s_ga + GDN_HEADS))
           + list(range(s_gb, s_gb + GDN_HEADS)) + list(range(s_wi, s_wi + IDX_HEADS))
           + list(range(s_dt, s_dt + SSM_HEADS)))
    cols += kis + [-1] * (LANES - len(kis))
    cols += list(range(s_z, s_z + SSM_W))
    cols += list(range(s_xbc, s_xbc + SSM_CONV_C))
    assert len(cols) == PERM_W
    return np.asarray(cols, np.int32)


_PERM_COLS = _perm_columns()


def _permute_columns(w_in):
    pieces = []
    start = 0
    for i in range(1, PERM_W + 1):
        prev = int(_PERM_COLS[i - 1])
        if i == PERM_W or not ((prev < 0 and _PERM_COLS[i] < 0) or (prev >= 0 and _PERM_COLS[i] == prev + 1)):
            if _PERM_COLS[start] < 0:
                pieces.append(jnp.zeros(w_in.shape[:-1] + (i - start,), w_in.dtype))
            else:
                pieces.append(w_in[..., int(_PERM_COLS[start]):int(_PERM_COLS[start]) + i - start])
            start = i
    return jnp.concatenate(pieces, axis=-1)


def _silu(x):
    return x * jax.nn.sigmoid(x)


def _softplus(x):
    return jnp.maximum(x, 0.0) + jnp.log1p(jnp.exp(-jnp.abs(x)))


def _dot(a, b, precision=None):
    return jnp.dot(a, b, preferred_element_type=F32, precision=precision)


def _dot_nt(a, b, precision=None):
    return lax.dot_general(a, b, (((1,), (1,)), ((), ())), preferred_element_type=F32,
                           precision=precision)


def _dot_tn(a, b, precision=None):
    return lax.dot_general(a, b, (((0,), (0,)), ((), ())), preferred_element_type=F32,
                           precision=precision)


def _cumsum_rows(x):
    c = x.shape[0]
    row = lax.broadcasted_iota(jnp.int32, x.shape, 0)
    s = 1
    while s < c:
        x = x + jnp.where(row >= s, pltpu.roll(x, s, axis=0), 0.0)
        s *= 2
    return x


def _mm(a, b):
    return _dot(a.astype(BF16), b.astype(BF16))


def _mm_nt(a, b):
    return _dot_nt(a.astype(BF16), b.astype(BF16))


def _split(a):
    hi = a.astype(BF16)
    return hi, (a - hi.astype(F32)).astype(BF16)


def _mm3(a, b):
    (ah, al), (bh, bl) = a, b
    return _dot(ah, bh) + (_dot(ah, bl) + _dot(al, bh))


def _head_sums(x):
    r = lax.broadcasted_iota(jnp.int32, (LANES, LANES), 0) // GDN_DV
    cidx = lax.broadcasted_iota(jnp.int32, (LANES, LANES), 1) // GDN_DV
    ones = (r == cidx).astype(BF16)
    hi, lo = _split(x)
    cols = []
    for g in range(x.shape[1] // LANES):
        sl = slice(g * LANES, (g + 1) * LANES)
        cols.append(_dot(hi[:, sl], ones) + _dot(lo[:, sl], ones))
    return jnp.concatenate(cols, axis=1)


def _transpose_rows(vals):
    c = vals.shape[0]
    if c < LANES:
        vals = jnp.concatenate([vals, jnp.zeros((LANES - c, LANES), vals.dtype)], axis=0)
    return vals.T[:, :c]


def _lane_vector(vals, start):
    return jnp.zeros((1, LANES), F32).at[0, start:start + vals.shape[0]].set(vals.astype(F32))


def _params(sem):
    return pltpu.CompilerParams(dimension_semantics=sem, vmem_limit_bytes=VMEM_LIMIT)


def _ada_kernel(c_ref, w_ref, b_ref, o_ref):
    s = _silu(c_ref[...])
    o_ref[0] = _dot(s.astype(BF16), w_ref[0].astype(BF16)) + b_ref[0]


def _ada(c_all, w_ada, b_ada):
    rows = c_all.shape[0]
    n = w_ada.shape[2]
    tn = 1536
    return pl.pallas_call(
        _ada_kernel,
        grid=(DEPTH, n // tn),
        in_specs=[pl.BlockSpec((rows, D_MODEL), lambda l, j: (0, 0)),
                  pl.BlockSpec((1, D_MODEL, tn), lambda l, j: (l, 0, j)),
                  pl.BlockSpec((1, 1, tn), lambda l, j: (l, 0, j))],
        out_specs=pl.BlockSpec((1, rows, tn), lambda l, j: (l, 0, j)),
        out_shape=jax.ShapeDtypeStruct((DEPTH, rows, n), F32),
        compiler_params=_params(("arbitrary", "arbitrary")),
        name="ada",
    )(c_all, w_ada, b_ada.reshape(DEPTH, 1, n))


def _rotate(x, cos, sin_signed):
    w = x.shape[-1]
    lane = lax.broadcasted_iota(jnp.int32, x.shape, x.ndim - 1)
    first = (lane % ATT_DH) < (ATT_DH // 2)
    swapped = jnp.where(first, pltpu.roll(x, w - ATT_DH // 2, axis=x.ndim - 1),
                        pltpu.roll(x, ATT_DH // 2, axis=x.ndim - 1))
    return x * cos + swapped * sin_signed


def _inproj_kernel(x_ref, mod_ref, g_ref, w_ref, cos_ref, sin_ref, *out_refs, nb, tt, emit_keys):
    x = x_ref[...]
    ms = jnp.mean(x * x, axis=-1, keepdims=True)
    xn = x * lax.rsqrt(ms + EPS) * g_ref[...]
    mod = mod_ref[...]
    h = xn * (1.0 + mod[:, 1:2, :]) + mod[:, 0:1, :]
    u = _dot(h.reshape(nb * tt, D_MODEL).astype(BF16), w_ref[...])
    cos = cos_ref[...]
    sin = sin_ref[...]
    off = 0
    segs = {}
    for (name, width), o_ref in zip(OUT_SEGS, out_refs):
        seg = u[:, off:off + width]
        if name in ("q", "k", "qi", "kis"):
            t0 = ATT_W if name == "kis" else 0
            c3 = cos[:, t0:t0 + width][None]
            s3 = sin[:, t0:t0 + width][None]
            if nb > 1:
                c3 = jnp.broadcast_to(c3, (nb, tt, width)).reshape(nb * tt, width)
                s3 = jnp.broadcast_to(s3, (nb, tt, width)).reshape(nb * tt, width)
            else:
                c3 = c3[0]
                s3 = s3[0]
            seg = _rotate(seg, c3, s3)
        o_ref[...] = seg.reshape(nb, tt, width)
        segs[name] = seg
        off += width
    if emit_keys:
        k_b_ref, ki_b_ref, vt_b_ref = out_refs[len(OUT_SEGS):]
        k_b_ref[0] = segs["k"].astype(BF16)
        ki_b_ref[0] = segs["kis"][:, :IDX_DH].astype(BF16)
        vt_b_ref[0] = segs["v"].T.astype(BF16)


def _inproj(x, mod, g1, w_perm, cos, sin, nb, tt, emit_keys):
    B, T, _ = x.shape
    grid = (B // nb, T // tt)
    out_shape = tuple(jax.ShapeDtypeStruct((B, T, w), F32) for _, w in OUT_SEGS)
    out_specs = tuple(pl.BlockSpec((nb, tt, w), lambda b, t: (b, t, 0)) for _, w in OUT_SEGS)
    if emit_keys:
        assert nb == 1
        out_shape += (jax.ShapeDtypeStruct((B, T, ATT_W), BF16), jax.ShapeDtypeStruct((B, T, IDX_DH), BF16),
                      jax.ShapeDtypeStruct((B, ATT_W, T), BF16))
        out_specs += (pl.BlockSpec((1, tt, ATT_W), lambda b, t: (b, t, 0)),
                      pl.BlockSpec((1, tt, IDX_DH), lambda b, t: (b, t, 0)),
                      pl.BlockSpec((1, ATT_W, tt), lambda b, t: (b, 0, t)))
    return pl.pallas_call(
        functools.partial(_inproj_kernel, nb=nb, tt=tt, emit_keys=emit_keys),
        grid=grid,
        in_specs=[pl.BlockSpec((nb, tt, D_MODEL), lambda b, t: (b, t, 0)),
                  pl.BlockSpec((nb, 6, D_MODEL), lambda b, t: (b, 0, 0)),
                  pl.BlockSpec((1, D_MODEL), lambda b, t: (0, 0)),
                  pl.BlockSpec((D_MODEL, PERM_W), lambda b, t: (0, 0)),
                  pl.BlockSpec((tt, ROT_W), lambda b, t: (t, 0)),
                  pl.BlockSpec((tt, ROT_W), lambda b, t: (t, 0))],
        out_specs=out_specs,
        out_shape=out_shape,
        compiler_params=_params(("arbitrary", "arbitrary")),
        name="inproj",
    )(x, mod, g1, w_perm, cos, sin)


def _short_conv(ext_ref, bi, u, w, c):
    lo = SUBLANES - (CONV_W - 1)
    ext_ref[bi, SUBLANES:SUBLANES + c, :] = u
    y = ext_ref[bi, lo:lo + c, :] * w[0:1, :]
    for j in range(1, CONV_W):
        y = y + ext_ref[bi, lo + j:lo + j + c, :] * w[j:j + 1, :]
    tail = ext_ref[bi, c + lo:c + SUBLANES, :]
    ext_ref[bi, lo:SUBLANES, :] = tail
    return y, tail


def _neumann_inverses(mats, c):
    eye = (lax.broadcasted_iota(jnp.int32, (c, c), 0)
           == lax.broadcasted_iota(jnp.int32, (c, c), 1)).astype(F32)
    ps = [eye - a for a in mats]
    sp = [_split(a) for a in mats]
    pws = [_mm3(s, s) for s in sp]
    n = 2
    while n < c:
        sp = [_split(pw) for pw in pws]
        ps = [p + _mm3(_split(p), s) for p, s in zip(ps, sp)]
        n *= 2
        if n < c:
            pws = [_mm3(s, s) for s in sp]
    return ps


def _gdn_kernel(qkv_ref, kis_ref, gate_ref, w_ref, prev_ref, s0_ref, alog_ref, dtb_ref, ng_ref,
                o_ref, new_ref, s_ref, ext_ref, s_sc, *, c, nbb):
    j = pl.program_id(1)
    lo = SUBLANES - (CONV_W - 1)

    @pl.when(j == 0)
    def _():
        s_sc[...] = s0_ref[...]
        ext_ref[:, lo:SUBLANES, :] = prev_ref[...]

    ng = ng_ref[...]
    w_conv = w_ref[...]
    ri = lax.broadcasted_iota(jnp.int32, (c, c), 0)
    ci = lax.broadcasted_iota(jnp.int32, (c, c), 1)
    incl = ri >= ci
    strict = ri > ci
    nkd = GDN_HEADS * GDN_DK
    ch = []
    for bi in range(nbb):
        y, tail = _short_conv(ext_ref, bi, qkv_ref[bi], w_conv, c)
        new_ref[bi] = tail
        y = _silu(y)
        small = kis_ref[bi]
        g = -jnp.exp(alog_ref[...]) * _softplus(small + dtb_ref[...])
        beta = jax.nn.sigmoid(small)
        gc = _cumsum_rows(g)
        gc_t = _transpose_rows(gc)
        qk_n = y[:, :2 * nkd]
        qk_n = qk_n * lax.rsqrt(_head_sums(qk_n * qk_n) + EPS)
        for h in range(GDN_HEADS):
            qh = qk_n[:, h * GDN_DK:(h + 1) * GDN_DK] * (GDN_DK ** -0.5)
            kh = qk_n[:, nkd + h * GDN_DK:nkd + (h + 1) * GDN_DK]
            vh = y[:, 2 * nkd + h * GDN_DV:2 * nkd + (h + 1) * GDN_DV]
            ch.append((qh, kh, vh, gc[:, KIS_A + h:KIS_A + h + 1], gc_t[KIS_A + h:KIS_A + h + 1, :],
                       beta[:, KIS_B + h:KIS_B + h + 1]))
    ids = [(bi, h) for bi in range(nbb) for h in range(GDN_HEADS)]
    decay = [jnp.where(incl, jnp.exp(jnp.where(incl, gcol - grow, 0.0)), 0.0)
             for (_, _, _, gcol, grow, _) in ch]
    kb = [kh * bcol for (_, kh, _, _, _, bcol) in ch]
    eg = [jnp.exp(gcol) for (_, _, _, gcol, _, _) in ch]
    grams = [_mm_nt(jnp.concatenate([kbi, x[0]], axis=0), x[1]) for kbi, x in zip(kb, ch)]
    t_inv = _neumann_inverses([jnp.where(strict, m[:c] * d, 0.0) for m, d in zip(grams, decay)], c)
    attn = [jnp.where(incl, m[c:] * d, 0.0) for m, d in zip(grams, decay)]
    rhs = [jnp.concatenate([x[2] * x[5], kbi * e], axis=1) for x, kbi, e in zip(ch, kb, eg)]
    sol = [_mm(t, r) for t, r in zip(t_inv, rhs)]
    st = [s_sc[bi, h] for bi, h in ids]
    ws = [_mm(sl[:, GDN_DV:], s) for sl, s in zip(sol, st)]
    qs = [_mm(x[0] * e, s) for x, e, s in zip(ch, eg, st)]
    v_new = [sl[:, :GDN_DV] - w for sl, w in zip(sol, ws)]
    av = [_mm(a, v) for a, v in zip(attn, v_new)]
    g_last = [x[3][c - 1:c, :] for x in ch]
    kv = [_dot_tn(x[1] * jnp.exp(gl - x[3]), v) for x, gl, v in zip(ch, g_last, v_new)]
    for (bi, h), s, gl, kvi in zip(ids, st, g_last, kv):
        s_new = s * jnp.exp(gl) + kvi
        s_sc[bi, h] = s_new
        s_ref[bi, h] = s_new
    for bi in range(nbb):
        o = jnp.concatenate([qs[bi * GDN_HEADS + h] + av[bi * GDN_HEADS + h] for h in range(GDN_HEADS)], axis=1)
        o = o * lax.rsqrt(_head_sums(o * o) * (1.0 / GDN_DV) + EPS) * ng
        o_ref[bi] = o * _silu(gate_ref[bi])


def _gdn(gqkv, kis, ggate, conv_w, conv_prev, s0, a_log, dt_bias, norm_g, nbb):
    B, T, _ = gqkv.shape
    c = math.gcd(T, CHUNK)
    return pl.pallas_call(
        functools.partial(_gdn_kernel, c=c, nbb=nbb),
        grid=(B // nbb, T // c),
        in_specs=[pl.BlockSpec((nbb, c, GDN_CONV_C), lambda b, j: (b, j, 0)),
                  pl.BlockSpec((nbb, c, LANES), lambda b, j: (b, j, 0)),
                  pl.BlockSpec((nbb, c, GDN_W), lambda b, j: (b, j, 0)),
                  pl.BlockSpec((CONV_W, GDN_CONV_C), lambda b, j: (0, 0)),
                  pl.BlockSpec((nbb, CONV_W - 1, GDN_CONV_C), lambda b, j: (b, 0, 0)),
                  pl.BlockSpec((nbb, GDN_HEADS, GDN_DK, GDN_DV), lambda b, j: (b, 0, 0, 0)),
                  pl.BlockSpec((1, LANES), lambda b, j: (0, 0)),
                  pl.BlockSpec((1, LANES), lambda b, j: (0, 0)),
                  pl.BlockSpec((1, GDN_W), lambda b, j: (0, 0))],
        out_specs=(pl.BlockSpec((nbb, c, GDN_W), lambda b, j: (b, j, 0)),
                   pl.BlockSpec((nbb, CONV_W - 1, GDN_CONV_C), lambda b, j: (b, 0, 0)),
                   pl.BlockSpec((nbb, GDN_HEADS, GDN_DK, GDN_DV), lambda b, j: (b, 0, 0, 0))),
        out_shape=(jax.ShapeDtypeStruct((B, T, GDN_W), F32),
                   jax.ShapeDtypeStruct((B, CONV_W - 1, GDN_CONV_C), F32),
                   jax.ShapeDtypeStruct((B, GDN_HEADS, GDN_DK, GDN_DV), F32)),
        scratch_shapes=[pltpu.VMEM((nbb, SUBLANES + c, GDN_CONV_C), F32),
                        pltpu.VMEM((nbb, GDN_HEADS, GDN_DK, GDN_DV), F32)],
        compiler_params=_params(("arbitrary", "arbitrary")),
        name="gdn",
    )(gqkv, kis, ggate, conv_w, conv_prev, s0, _lane_vector(a_log, KIS_A), _lane_vector(dt_bias, KIS_A),
      jnp.tile(norm_g, GDN_HEADS).reshape(1, -1))


def _ssd_kernel(xbc_ref, z_ref, kis_ref, w_ref, cb_ref, prev_ref, h0_ref, alog_ref, dtb_ref, dsk_ref,
                ng_ref, o_ref, new_ref, h_ref, ext_ref, h_sc, *, c, nbb):
    j = pl.program_id(1)
    lo = SUBLANES - (CONV_W - 1)

    @pl.when(j == 0)
    def _():
        h_sc[...] = h0_ref[...]
        ext_ref[:, lo:SUBLANES, :] = prev_ref[...]

    dsk = dsk_ref[...]
    ng = ng_ref[...]
    w_conv = w_ref[...]
    ri = lax.broadcasted_iota(jnp.int32, (c, c), 0)
    ci = lax.broadcasted_iota(jnp.int32, (c, c), 1)
    incl = ri >= ci
    gn = SSM_GROUPS * SSM_N
    rep = SSM_HEADS // SSM_GROUPS
    gw = SSM_W // SSM_GROUPS
    grp_in = []
    ch = []
    ids = []
    for bi in range(nbb):
        y, tail = _short_conv(ext_ref, bi, xbc_ref[bi], w_conv, c)
        new_ref[bi] = tail
        y = _silu(y + cb_ref[...])
        dts = _softplus(kis_ref[bi] + dtb_ref[...])
        a = dts * (-jnp.exp(alog_ref[...]))
        acum = _cumsum_rows(a)
        acum_t = _transpose_rows(acum)
        zg = _silu(z_ref[bi])
        for grp in range(SSM_GROUPS):
            grp_in.append((y[:, SSM_W + grp * SSM_N:SSM_W + (grp + 1) * SSM_N],
                           y[:, SSM_W + gn + grp * SSM_N:SSM_W + gn + (grp + 1) * SSM_N],
                           zg[:, grp * gw:(grp + 1) * gw]))
            for hh in range(rep):
                h = grp * rep + hh
                ch.append((y[:, h * SSM_P:(h + 1) * SSM_P], acum[:, KIS_DT + h:KIS_DT + h + 1],
                           acum_t[KIS_DT + h:KIS_DT + h + 1, :], dts[:, KIS_DT + h:KIS_DT + h + 1],
                           len(grp_in) - 1))
                ids.append((bi, h))
    cb = [_mm_nt(cg, bg) for bg, cg, _ in grp_in]
    seg = [jnp.where(incl, jnp.exp(jnp.where(incl, acol - arow, 0.0)), 0.0) for (_, acol, arow, _, _) in ch]
    hs = [h_sc[bi, h] for bi, h in ids]
    y_in = [_mm(cb[x[4]] * sg, x[0] * x[3]) for x, sg in zip(ch, seg)]
    y_st = [_mm_nt(grp_in[x[4]][1], hst) for x, hst in zip(ch, hs)]
    a_last = [x[1][c - 1:c, :] for x in ch]
    upd = [_dot_tn(x[0], grp_in[x[4]][0] * (jnp.exp(al - x[1]) * x[3])) for x, al in zip(ch, a_last)]
    outs = []
    for (bi, h), x, hst, al, yi, ys, up in zip(ids, ch, hs, a_last, y_in, y_st, upd):
        h_new = hst * jnp.exp(al) + up
        h_sc[bi, h] = h_new
        h_ref[bi, h] = h_new
        outs.append(yi + ys * jnp.exp(x[1]) + dsk[:, h:h + 1] * x[0])
    for gi, (_, _, zslice) in enumerate(grp_in):
        bi, grp = divmod(gi, SSM_GROUPS)
        yg = jnp.concatenate(outs[gi * rep:(gi + 1) * rep], axis=1) * zslice
        yg = yg * lax.rsqrt(jnp.mean(yg * yg, axis=-1, keepdims=True) + EPS)
        o_ref[bi, :, grp * gw:(grp + 1) * gw] = yg * ng[:, grp * gw:(grp + 1) * gw]


def _ssd(xbc, z, kis, conv_w, conv_b, conv_prev, h0, a_log, dt_bias, d_skip, norm_g, nbb):
    B, T, _ = xbc.shape
    c = math.gcd(T, CHUNK)
    return pl.pallas_call(
        functools.partial(_ssd_kernel, c=c, nbb=nbb),
        grid=(B // nbb, T // c),
        in_specs=[pl.BlockSpec((nbb, c, SSM_CONV_C), lambda b, j: (b, j, 0)),
                  pl.BlockSpec((nbb, c, SSM_W), lambda b, j: (b, j, 0)),
                  pl.BlockSpec((nbb, c, LANES), lambda b, j: (b, j, 0)),
                  pl.BlockSpec((CONV_W, SSM_CONV_C), lambda b, j: (0, 0)),
                  pl.BlockSpec((1, SSM_CONV_C), lambda b, j: (0, 0)),
                  pl.BlockSpec((nbb, CONV_W - 1, SSM_CONV_C), lambda b, j: (b, 0, 0)),
                  pl.BlockSpec((nbb, SSM_HEADS, SSM_P, SSM_N), lambda b, j: (b, 0, 0, 0)),
                  pl.BlockSpec((1, LANES), lambda b, j: (0, 0)),
                  pl.BlockSpec((1, LANES), lambda b, j: (0, 0)),
                  pl.BlockSpec((1, SSM_HEADS), lambda b, j: (0, 0)),
                  pl.BlockSpec((1, SSM_W), lambda b, j: (0, 0))],
        out_specs=(pl.BlockSpec((nbb, c, SSM_W), lambda b, j: (b, j, 0)),
                   pl.BlockSpec((nbb, CONV_W - 1, SSM_CONV_C), lambda b, j: (b, 0, 0)),
                   pl.BlockSpec((nbb, SSM_HEADS, SSM_P, SSM_N), lambda b, j: (b, 0, 0, 0))),
        out_shape=(jax.ShapeDtypeStruct((B, T, SSM_W), F32),
                   jax.ShapeDtypeStruct((B, CONV_W - 1, SSM_CONV_C), F32),
                   jax.ShapeDtypeStruct((B, SSM_HEADS, SSM_P, SSM_N), F32)),
        scratch_shapes=[pltpu.VMEM((nbb, SUBLANES + c, SSM_CONV_C), F32),
                        pltpu.VMEM((nbb, SSM_HEADS, SSM_P, SSM_N), F32)],
        compiler_params=_params(("arbitrary", "arbitrary")),
        name="ssd",
    )(xbc, z, kis, conv_w, conv_b.reshape(1, -1), conv_prev, h0, _lane_vector(a_log, KIS_DT),
      _lane_vector(dt_bias, KIS_DT), d_skip.reshape(1, -1), norm_g.reshape(1, -1))


IDX_BITS = 12


OPENING_PROBES = 14
ROUND_PROBES = 2
SEARCH_ROUNDS = 4096
FOLD_ROWS = 128


def _fold_rows(x, op):
    s = x.shape[0]
    if s % FOLD_ROWS == 0 and s > FOLD_ROWS:
        x = op(x.reshape(s // FOLD_ROWS, FOLD_ROWS, x.shape[1]), axis=0)
    return op(x, axis=0, keepdims=True)


def _count(mask):
    return _fold_rows(mask.astype(F32), jnp.sum)


def _dsa_kernel(q_ref, qi_ref, kis_ref, k_ref, vt_ref, ki_ref, o_ref, *, tq, s_len, s_valid, q_offset, j0,
                n_valid_q, topk):
    j = pl.program_id(1)
    lane = lax.broadcasted_iota(jnp.int32, (1, tq), 1)
    qpos = q_offset + (j0 + j) * tq + lane
    lim = jnp.minimum((qpos // CHUNK + 1) * CHUNK, s_valid)
    row = lax.broadcasted_iota(jnp.int32, (s_len, tq), 0)
    adm = row < lim

    qi_t = qi_ref[0].T.astype(BF16)
    kis_t = kis_ref[0].T
    ki = ki_ref[0]
    assert IDX_DH == 64 and IDX_HEADS == 4 and ATT_DH == 64
    wi = kis_t[KIS_WI:KIS_WI + IDX_HEADS, :] * (IDX_DH ** -0.5 * IDX_HEADS ** -0.5)
    qi_all = jnp.concatenate([qi_t[h * IDX_DH:(h + 1) * IDX_DH, :] for h in range(IDX_HEADS)], axis=1)
    rel_all = _dot(ki, qi_all)
    score = jnp.zeros((s_len, tq), F32)
    for h in range(IDX_HEADS):
        score = score + jnp.maximum(rel_all[:, h * tq:(h + 1) * tq], 0.0) * wi[h:h + 1, :]
    s = jnp.where(adm, score, -jnp.inf)

    few = (_count(adm) <= topk) | (lane >= n_valid_q)
    smax = _fold_rows(s, jnp.max)
    smin = _fold_rows(jnp.where(adm, score, jnp.inf), jnp.min)
    top_full = _count(s >= smax) >= topk
    lo0 = jnp.where(few, 0.0, jnp.where(top_full, smax, smin))
    hi0 = jnp.where(few, 0.0, smax)
    done0 = (few | top_full).astype(F32)

    def probe(lo, hi, done):
        mid = 0.5 * lo + 0.5 * hi
        ok = (mid > lo) & (mid < hi) & (done < 0.5)
        take = _count(s >= mid) >= topk
        return jnp.where(ok & take, mid, lo), jnp.where(ok & jnp.logical_not(take), mid, hi)

    def opening(lo, hi):
        for _ in range(OPENING_PROBES):
            lo, hi = probe(lo, hi, done0)
        return lo, hi

    lo1, hi1 = lax.cond(jnp.min(done0) < 0.5, opening, lambda lo, hi: (lo, hi), lo0, hi0)

    def search_cond(c):
        return (jnp.min(c[2]) < 0.5) & (c[3] < SEARCH_ROUNDS)

    def search_body(c):
        lo, hi, done, it = c
        last = _fold_rows(jnp.where(s < hi, s, -jnp.inf), jnp.max)
        reached = _count(s >= last) >= topk
        active = done < 0.5
        lo = jnp.where(active & reached, last, lo)
        hi = jnp.where(active & jnp.logical_not(reached), last, hi)
        done = jnp.maximum(done, reached.astype(F32))
        for _ in range(ROUND_PROBES):
            lo, hi = probe(lo, hi, done)
        return lo, hi, done, it + 1

    lo, _, _, _ = lax.while_loop(search_cond, search_body, (lo1, hi1, done0, jnp.int32(0)))
    t = jnp.where(few, -jnp.inf, lo)
    gt = s > t
    tie = s == t
    need = topk - _count(gt)
    excess = jnp.where(few, 0.0, _count(tie) - need)

    def tie_search():
        def ibody(i, m):
            cand = m | (jnp.int32(1) << (IDX_BITS - 1 - i))
            return jnp.where(_count(tie & (row < cand)) <= need, cand, m)

        return lax.fori_loop(0, IDX_BITS, ibody, jnp.zeros((1, tq), jnp.int32))

    m = lax.cond(jnp.max(excess) > 0.0, tie_search, lambda: jnp.full((1, tq), 2 ** IDX_BITS - 1, jnp.int32))
    sel = adm & (gt | (tie & (row < m)))

    q_t = q_ref[0].T * (ATT_DH ** -0.5)
    pair = LANES // ATT_DH
    head_in_pair = lax.broadcasted_iota(jnp.int32, (LANES, tq), 0) // ATT_DH
    logits = []
    for g in range(ATT_HEADS // pair):
        qg = q_t[g * LANES:(g + 1) * LANES, :]
        qm = jnp.concatenate([jnp.where(head_in_pair == i, qg, 0.0) for i in range(pair)], axis=1)
        lg = _dot(k_ref[0, :, g * LANES:(g + 1) * LANES], qm.astype(BF16))
        logits += [lg[:, i * tq:(i + 1) * tq] for i in range(pair)]
    probs, scales = [], []
    for lg in logits:
        lg = jnp.where(sel, lg, -jnp.inf)
        p = jnp.exp(lg - _fold_rows(lg, jnp.max))
        scales.append(1.0 / _fold_rows(p, jnp.sum))
        probs.append(p.astype(BF16))
    outs = [_dot(vt_ref[0, h * ATT_DH:(h + 1) * ATT_DH, :], probs[h]) * scales[h] for h in range(ATT_HEADS)]
    o_ref[0] = jnp.concatenate(outs, axis=0).T


def _dsa(q, qi, kis, k_b, vt_b, ki_b, *, tq, j0, nj, s_len, s_valid, q_offset, n_valid_q, topk):
    B = q.shape[0]
    assert s_len < 2 ** IDX_BITS and s_len <= k_b.shape[1]
    return pl.pallas_call(
        functools.partial(_dsa_kernel, tq=tq, s_len=s_len, s_valid=s_valid, q_offset=q_offset, j0=j0,
                          n_valid_q=n_valid_q, topk=topk),
        grid=(B, nj),
        in_specs=[pl.BlockSpec((1, tq, ATT_W), lambda b, j: (b, j0 + j, 0)),
                  pl.BlockSpec((1, tq, IDX_HEADS * IDX_DH), lambda b, j: (b, j0 + j, 0)),
                  pl.BlockSpec((1, tq, LANES), lambda b, j: (b, j0 + j, 0)),
                  pl.BlockSpec((1, s_len, ATT_W), lambda b, j: (b, 0, 0)),
                  pl.BlockSpec((1, ATT_W, s_len), lambda b, j: (b, 0, 0)),
                  pl.BlockSpec((1, s_len, IDX_DH), lambda b, j: (b, 0, 0))],
        out_specs=pl.BlockSpec((1, tq, ATT_W), lambda b, j: (b, j, 0)),
        out_shape=jax.ShapeDtypeStruct((B, nj * tq, ATT_W), F32),
        compiler_params=_params(("arbitrary", "arbitrary")),
        name="dsa",
    )(q, qi, kis, k_b, vt_b, ki_b)


def _dsa_group(q, qi, kis, k_b, vt_b, ki_b, *, s_valid, q_offset, topk):
    B, T, _ = q.shape
    tq = 2 * LANES if T % (2 * LANES) == 0 else LANES
    n_valid_q = min(T, tq)
    if T < tq:
        padq = lambda a: jnp.pad(a, ((0, 0), (0, tq - T), (0, 0)))
        q, qi, kis = padq(q), padq(qi), padq(kis)
    nq = q.shape[1] // tq
    s_total = k_b.shape[1]
    outs = []
    j0 = 0
    while j0 < nq:
        nj = 1
        reach = -(-(q_offset + (j0 + nj) * tq) // CHUNK) * CHUNK
        s_len = min(s_total, -(-min(reach, s_valid) // LANES) * LANES)
        outs.append(_dsa(q, qi, kis, k_b, vt_b, ki_b, tq=tq, j0=j0, nj=nj, s_len=s_len, s_valid=s_valid,
                         q_offset=q_offset, n_valid_q=n_valid_q, topk=topk))
        j0 += nj
    out = outs[0] if len(outs) == 1 else jnp.concatenate(outs, axis=1)
    return out[:, :T]


def _ffn_kernel(x_ref, gdn_ref, att_ref, ssm_ref, mod_ref, wo_ref, g2_ref, wg_ref, wu_ref, cw_ref, cb_ref,
                wd_ref, prev_ref, fg_ref, *rest, nb, tt, nk, final):
    if final:
        xo_ref, y_ref, new_ref, h2_sc, acc_sc, buf_sc, carry_sc = rest
    else:
        xo_ref, new_ref, h2_sc, acc_sc, buf_sc, carry_sc = rest
        y_ref = None
    t = pl.program_id(1)
    k = pl.program_id(2)
    rows = nb * tt
    lo = SUBLANES - (FFN_CONV_W - 1)
    mod = mod_ref[...]

    @pl.when(k == 0)
    def _():
        mix = jnp.concatenate([gdn_ref[...], att_ref[...], ssm_ref[...]], axis=-1)
        proj = _dot(mix.reshape(rows, MIX_W).astype(BF16), wo_ref[...]).reshape(nb, tt, D_MODEL)
        x1 = x_ref[...] + mod[:, 2:3, :] * proj
        acc_sc[...] = x1
        ms = jnp.mean(x1 * x1, axis=-1, keepdims=True)
        h2 = x1 * lax.rsqrt(ms + EPS) * g2_ref[...]
        h2 = h2 * (1.0 + mod[:, 4:5, :]) + mod[:, 3:4, :]
        h2_sc[...] = h2.reshape(rows, D_MODEL).astype(BF16)

    @pl.when(t == 0)
    def _():
        carry_sc[k, :, lo:SUBLANES, :] = prev_ref[...]

    h2 = h2_sc[...]
    ag = _dot(h2, wg_ref[...])
    up = _dot(h2, wu_ref[...])
    tf = ag.shape[-1]
    buf_sc[:, lo:SUBLANES, :] = carry_sc[k, :, lo:SUBLANES, :]
    buf_sc[:, SUBLANES:SUBLANES + tt, :] = ag.reshape(nb, tt, tf)
    cw = cw_ref[...]
    conv = buf_sc[:, lo:lo + tt, :] * cw[0:1, :][None]
    for jj in range(1, FFN_CONV_W):
        conv = conv + buf_sc[:, lo + jj:lo + jj + tt, :] * cw[jj:jj + 1, :][None]
    tail = buf_sc[:, tt + lo:tt + SUBLANES, :]
    carry_sc[k, :, lo:SUBLANES, :] = tail
    new_ref[:, pl.ds(k, 1), :, :] = tail[:, None]
    act = _silu(conv + cb_ref[...][None]).reshape(rows, tf) * up
    y = _dot(act.astype(BF16), wd_ref[...]).reshape(nb, tt, D_MODEL)
    acc_sc[...] += mod[:, 5:6, :] * y

    @pl.when(k == nk - 1)
    def _():
        xo = acc_sc[...]
        xo_ref[...] = xo
        if final:
            ms = jnp.mean(xo * xo, axis=-1, keepdims=True)
            y_ref[...] = xo * lax.rsqrt(ms + EPS) * fg_ref[...]


def _ffn(x, gdn_o, att_o, ssm_o, mod, w_out, g2, w_gate, w_up, conv_w, conv_b, w_down, conv_prev, final_g,
         nb, tt, tf, final):
    B, T, _ = x.shape
    nk = D_FF // tf
    grid = (B // nb, T // tt, nk)
    row_spec = lambda w: pl.BlockSpec((nb, tt, w), lambda b, t, k: (b, t, 0))
    in_specs = [row_spec(D_MODEL), row_spec(GDN_W), row_spec(ATT_W), row_spec(SSM_W),
                pl.BlockSpec((nb, 6, D_MODEL), lambda b, t, k: (b, 0, 0)),
                pl.BlockSpec((MIX_W, D_MODEL), lambda b, t, k: (0, 0)),
                pl.BlockSpec((1, D_MODEL), lambda b, t, k: (0, 0)),
                pl.BlockSpec((D_MODEL, tf), lambda b, t, k: (0, k)),
                pl.BlockSpec((D_MODEL, tf), lambda b, t, k: (0, k)),
                pl.BlockSpec((FFN_CONV_W, tf), lambda b, t, k: (0, k)),
                pl.BlockSpec((1, tf), lambda b, t, k: (0, k)),
                pl.BlockSpec((tf, D_MODEL), lambda b, t, k: (k, 0)),
                pl.BlockSpec((nb, FFN_CONV_W - 1, tf), lambda b, t, k: (b, 0, k)),
                pl.BlockSpec((1, D_MODEL), lambda b, t, k: (0, 0))]
    out_shape = [jax.ShapeDtypeStruct((B, T, D_MODEL), F32)]
    out_specs = [row_spec(D_MODEL)]
    if final:
        out_shape.append(jax.ShapeDtypeStruct((B, T, D_MODEL), F32))
        out_specs.append(row_spec(D_MODEL))
    out_shape.append(jax.ShapeDtypeStruct((B, nk, FFN_CONV_W - 1, tf), F32))
    out_specs.append(pl.BlockSpec((nb, nk, FFN_CONV_W - 1, tf), lambda b, t, k: (b, 0, 0, 0)))
    res = pl.pallas_call(
        functools.partial(_ffn_kernel, nb=nb, tt=tt, nk=nk, final=final),
        grid=grid,
        in_specs=in_specs,
        out_specs=tuple(out_specs),
        out_shape=tuple(out_shape),
        scratch_shapes=[pltpu.VMEM((nb * tt, D_MODEL), BF16),
                        pltpu.VMEM((nb, tt, D_MODEL), F32),
                        pltpu.VMEM((nb, SUBLANES + tt, tf), F32),
                        pltpu.VMEM((nk, nb, SUBLANES, tf), F32)],
        compiler_params=_params(("arbitrary", "arbitrary", "arbitrary")),
        name="ffn",
    )(x, gdn_o, att_o, ssm_o, mod, w_out, g2, w_gate, w_up, conv_w, conv_b.reshape(1, -1), w_down,
      conv_prev, final_g)
    tail = res[-1].transpose(0, 2, 1, 3).reshape(B, FFN_CONV_W - 1, D_FF)
    return (*res[:-1], tail)


def _rope_tables(pos):
    half = ATT_DH // 2
    inv_freq = ROPE_THETA ** (-jnp.arange(half, dtype=F32) / half)
    ang = pos.astype(F32)[:, None] * inv_freq[None, :]
    cos = jnp.cos(ang)
    sin = jnp.sin(ang)
    cos_h = jnp.concatenate([cos, cos], axis=-1)
    sin_h = jnp.concatenate([-sin, sin], axis=-1)
    ones = jnp.ones((pos.shape[0], LANES - ATT_DH), F32)
    cos_t = jnp.concatenate([jnp.tile(cos_h, (1, ATT_HEADS)), cos_h, ones], axis=-1)
    sin_t = jnp.concatenate([jnp.tile(sin_h, (1, ATT_HEADS)), sin_h, 0.0 * ones], axis=-1)
    return cos_t, sin_t


def _layer(x, mod, pos_tables, cache, states, lw, tiles, final, final_g):
    (norm1_g, w_perm, gdn_conv_w, gdn_A_log, gdn_dt_bias, gdn_norm_g, ssm_conv_w, ssm_conv_b, ssm_A_log,
     ssm_dt_bias, ssm_D, ssm_norm_g, w_out, norm2_g, w_gate, w_up, ffn_conv_w, ffn_conv_b, w_down) = lw
    gdn_conv_prev, gdn_s0, ssm_conv_prev, ssm_h0, ffn_conv_prev = states
    B, T, _ = x.shape
    nb, tt, nbb, ffn_tt, tf = tiles
    cos_t, sin_t = pos_tables
    proj = _inproj(x, mod, norm1_g.reshape(1, -1), w_perm, cos_t, sin_t, nb, tt, emit_keys=cache is None)
    gqkv, ggate, q, k, v, qi, kis, z, xbc = proj[:len(OUT_SEGS)]
    ki = kis[..., :IDX_DH]

    gdn_o, gdn_conv_new, gdn_s = _gdn(gqkv, kis, ggate, gdn_conv_w, gdn_conv_prev, gdn_s0, gdn_A_log,
                                      gdn_dt_bias, gdn_norm_g, nbb)
    ssm_o, ssm_conv_new, ssm_h = _ssd(xbc, z, kis, ssm_conv_w, ssm_conv_b, ssm_conv_prev, ssm_h0, ssm_A_log,
                                      ssm_dt_bias, ssm_D, ssm_norm_g, nbb)
    if cache is None:
        k_b, ki_b, vt_b = proj[len(OUT_SEGS):]
        att_o = _dsa_group(q, qi, kis, k_b, vt_b, ki_b, s_valid=T, q_offset=0, topk=min(TOPK_MAX, T // 4))
    else:
        ck, cv, cki = cache
        P = ck.shape[1]
        L = P + T
        pad = -(-L // LANES) * LANES - L
        k_b = jnp.concatenate([ck.reshape(B, P, ATT_W), k, jnp.zeros((B, pad, ATT_W), F32)], axis=1).astype(BF16)
        vt_b = jnp.concatenate([cv.reshape(B, P, ATT_W), v, jnp.zeros((B, pad, ATT_W), F32)],
                               axis=1).transpose(0, 2, 1).astype(BF16)
        ki_b = jnp.concatenate([cki, ki, jnp.zeros((B, pad, IDX_DH), F32)], axis=1).astype(BF16)
        att_o = _dsa_group(q, qi, kis, k_b, vt_b, ki_b, s_valid=L, q_offset=P, topk=min(TOPK_MAX, L // 4))

    res = _ffn(x, gdn_o, att_o, ssm_o, mod, w_out, norm2_g.reshape(1, -1), w_gate, w_up, ffn_conv_w, ffn_conv_b,
               w_down, ffn_conv_prev, final_g.reshape(1, -1), ffn_tt[0], ffn_tt[1], tf, final)
    if final:
        x_new, y, ffn_conv_new = res
    else:
        (x_new, ffn_conv_new), y = res, None
    st = (k.reshape(B, T, ATT_HEADS, ATT_DH), v.reshape(B, T, ATT_HEADS, ATT_DH), ki,
          gdn_conv_new, gdn_s, ssm_conv_new, ssm_h, ffn_conv_new)
    return x_new, y, st


def kernel(x_prompt, x_sample, c_prompt, c_sample, cache_k, cache_v, cache_kidx, state_gdn_conv, state_gdn,
           state_ssm_conv, state_ssm, state_ffn_conv, w_ada, b_ada, norm1_g, w_in, gdn_conv_w, gdn_A_log,
           gdn_dt_bias, gdn_norm_g, ssm_conv_w, ssm_conv_b, ssm_A_log, ssm_dt_bias, ssm_D, ssm_norm_g, w_out,
           norm2_g, w_gate, w_up, ffn_conv_w, ffn_conv_b, w_down, final_g):
    Bp, T, _ = x_prompt.shape
    Bs, Ts, _ = x_sample.shape
    P = cache_k.shape[2]

    c_all = jnp.concatenate([c_prompt, c_sample], axis=0)
    mod_all = _ada(c_all, w_ada, b_ada).reshape(DEPTH, Bp + Bs, 6, D_MODEL)

    tables_p = _rope_tables(jnp.arange(T))
    tables_s = _rope_tables(P + jnp.arange(Ts))

    w_perm = _permute_columns(w_in).astype(BF16)
    w_out_b = w_out.astype(BF16)
    w_gate_b = w_gate.astype(BF16)
    w_up_b = w_up.astype(BF16)
    w_down_b = w_down.astype(BF16)

    zeros_p = (jnp.zeros((Bp, CONV_W - 1, GDN_CONV_C), F32),
               jnp.zeros((Bp, GDN_HEADS, GDN_DK, GDN_DV), F32),
               jnp.zeros((Bp, CONV_W - 1, SSM_CONV_C), F32),
               jnp.zeros((Bp, SSM_HEADS, SSM_P, SSM_N), F32),
               jnp.zeros((Bp, FFN_CONV_W - 1, D_FF), F32))

    tf = D_FF // 2
    tiles_p = (1, min(256, T), math.gcd(Bp, 4), (1, min(512, T)), tf)
    tiles_s = (Bs, Ts, math.gcd(Bs, 4), (Bs, Ts), tf)

    xp, xs = x_prompt, x_sample
    new_p, new_s = [], []
    yp = ys = None
    for l in range(DEPTH):
        lw = (norm1_g[l], w_perm[l], gdn_conv_w[l], gdn_A_log[l], gdn_dt_bias[l], gdn_norm_g[l], ssm_conv_w[l],
              ssm_conv_b[l], ssm_A_log[l], ssm_dt_bias[l], ssm_D[l], ssm_norm_g[l], w_out_b[l], norm2_g[l],
              w_gate_b[l], w_up_b[l], ffn_conv_w[l], ffn_conv_b[l], w_down_b[l])
        final = l == DEPTH - 1
        xp, yp, st_p = _layer(xp, mod_all[l, :Bp], tables_p, None, zeros_p, lw, tiles_p, final, final_g)
        states_s = (state_gdn_conv[l], state_gdn[l], state_ssm_conv[l], state_ssm[l], state_ffn_conv[l])
        xs, ys, st_s = _layer(xs, mod_all[l, Bp:], tables_s, (cache_k[l], cache_v[l], cache_kidx[l]), states_s,
                              lw, tiles_s, final, final_g)
        new_p.append(st_p)
        new_s.append(st_s)
    outs_p = [jnp.stack([st[i] for st in new_p]) for i in range(8)]
    outs_s = [jnp.stack([st[i] for st in new_s]) for i in range(8)]
    return (yp, ys, *outs_p, *outs_s)
```

```python
import functools
import math

import jax
import jax.numpy as jnp
import numpy as np
from jax import lax
from jax.experimental import pallas as pl
from jax.experimental.pallas import tpu as pltpu

F32 = jnp.float32
BF16 = jnp.bfloat16
HI = lax.Precision.HIGHEST

D_MODEL = 1024
DEPTH = 2
CHUNK = 64
CONV_W = 4
FFN_CONV_W = 3
D_FF = 2816
ROPE_THETA = 10000.0
EPS = 1e-6
GDN_HEADS = 4
GDN_DK = 64
GDN_DV = 64
ATT_HEADS = 4
ATT_DH = 64
IDX_HEADS = 4
IDX_DH = 64
TOPK_MAX = 256
SSM_HEADS = 8
SSM_P = 64
SSM_GROUPS = 2
SSM_N = 128
GDN_W = GDN_HEADS * GDN_DV
ATT_W = ATT_HEADS * ATT_DH
SSM_W = SSM_HEADS * SSM_P
MIX_W = GDN_W + ATT_W + SSM_W
GDN_CONV_C = 2 * GDN_HEADS * GDN_DK + GDN_W
SSM_CONV_C = SSM_W + 2 * SSM_GROUPS * SSM_N
IN_SIZES = (GDN_CONV_C, GDN_HEADS, GDN_HEADS, GDN_W,
            ATT_W, ATT_W, ATT_W, IDX_HEADS * IDX_DH, IDX_DH, IDX_HEADS,
            SSM_W, SSM_CONV_C, SSM_HEADS)
IN_W = sum(IN_SIZES)

LANES = 128
SUBLANES = 8
VMEM_LIMIT = 56 * 1024 * 1024

KIS_A = IDX_DH
KIS_B = KIS_A + GDN_HEADS
KIS_WI = KIS_B + GDN_HEADS
KIS_DT = KIS_WI + IDX_HEADS
KIS_END = KIS_DT + SSM_HEADS
OUT_SEGS = (("gqkv", GDN_CONV_C), ("ggate", GDN_W), ("q", ATT_W), ("k", ATT_W), ("v", ATT_W),
            ("qi", IDX_HEADS * IDX_DH), ("kis", LANES), ("z", SSM_W), ("xbc", SSM_CONV_C))
PERM_W = sum(w for _, w in OUT_SEGS)
ROT_W = ATT_W + LANES


def _perm_columns():
    starts = np.concatenate([[0], np.cumsum(IN_SIZES)])
    (s_gqkv, s_ga, s_gb, s_gg, s_q, s_k, s_v, s_qi, s_ki, s_wi, s_z, s_xbc, s_dt) = starts[:-1]
    cols = []
    cols += list(range(s_gqkv, s_gqkv + GDN_CONV_C))
    cols += list(range(s_gg, s_gg + GDN_W))
    cols += list(range(s_q, s_q + ATT_W))
    cols += list(range(s_k, s_k + ATT_W))
    cols += list(range(s_v, s_v + ATT_W))
    cols += list(range(s_qi, s_qi + IDX_HEADS * IDX_DH))
    kis = (list(range(s_ki, s_ki + IDX_DH)) + list(range(s_ga, s_ga + GDN_HEADS))
           + list(range(s_gb, s_gb + GDN_HEADS)) + list(range(s_wi, s_wi + IDX_HEADS))
           + list(range(s_dt, s_dt + SSM_HEADS)))
    cols += kis + [-1] * (LANES - len(kis))
    cols += list(range(s_z, s_z + SSM_W))
    cols += list(range(s_xbc, s_xbc + SSM_CONV_C))
    assert len(cols) == PERM_W
    return np.asarray(cols, np.int32)


_PERM_COLS = _perm_columns()


def _permute_columns(w_in):
    pieces = []
    start = 0
    for i in range(1, PERM_W + 1):
        prev = int(_PERM_COLS[i - 1])
        if i == PERM_W or not ((prev < 0 and _PERM_COLS[i] < 0) or (prev >= 0 and _PERM_COLS[i] == prev + 1)):
            if _PERM_COLS[start] < 0:
                pieces.append(jnp.zeros(w_in.shape[:-1] + (i - start,), w_in.dtype))
            else:
                pieces.append(w_in[..., int(_PERM_COLS[start]):int(_PERM_COLS[start]) + i - start])
            start = i
    return jnp.concatenate(pieces, axis=-1)


def _silu(x):
    return x * jax.nn.sigmoid(x)


def _softplus(x):
    return jnp.maximum(x, 0.0) + jnp.log1p(jnp.exp(-jnp.abs(x)))


def _dot(a, b, precision=None):
    return jnp.dot(a, b, preferred_element_type=F32, precision=precision)


def _dot_nt(a, b, precision=None):
    return lax.dot_general(a, b, (((1,), (1,)), ((), ())), preferred_element_type=F32,
                           precision=precision)


def _dot_tn(a, b, precision=None):
    return lax.dot_general(a, b, (((0,), (0,)), ((), ())), preferred_element_type=F32,
                           precision=precision)


def _cumsum_rows(x):
    c = x.shape[0]
    row = lax.broadcasted_iota(jnp.int32, x.shape, 0)
    s = 1
    while s < c:
        x = x + jnp.where(row >= s, pltpu.roll(x, s, axis=0), 0.0)
        s *= 2
    return x


def _mm(a, b):
    return _dot(a.astype(BF16), b.astype(BF16))


def _mm_nt(a, b):
    return _dot_nt(a.astype(BF16), b.astype(BF16))


def _split(a):
    hi = a.astype(BF16)
    return hi, (a - hi.astype(F32)).astype(BF16)


def _mm3(a, b):
    (ah, al), (bh, bl) = a, b
    return _dot(ah, bh) + (_dot(ah, bl) + _dot(al, bh))


def _head_sums(x):
    r = lax.broadcasted_iota(jnp.int32, (LANES, LANES), 0) // GDN_DV
    cidx = lax.broadcasted_iota(jnp.int32, (LANES, LANES), 1) // GDN_DV
    ones = (r == cidx).astype(BF16)
    hi, lo = _split(x)
    cols = []
    for g in range(x.shape[1] // LANES):
        sl = slice(g * LANES, (g + 1) * LANES)
        cols.append(_dot(hi[:, sl], ones) + _dot(lo[:, sl], ones))
    return jnp.concatenate(cols, axis=1)


def _transpose_rows(vals):
    c = vals.shape[0]
    if c < LANES:
        vals = jnp.concatenate([vals, jnp.zeros((LANES - c, LANES), vals.dtype)], axis=0)
    return vals.T[:, :c]


def _lane_vector(vals, start):
    return jnp.zeros((1, LANES), F32).at[0, start:start + vals.shape[0]].set(vals.astype(F32))


def _params(sem):
    return pltpu.CompilerParams(dimension_semantics=sem, vmem_limit_bytes=VMEM_LIMIT)


def _ada_kernel(c_ref, w_ref, b_ref, o_ref):
    s = _silu(c_ref[...])
    o_ref[0] = _dot(s.astype(BF16), w_ref[0].astype(BF16)) + b_ref[0]


def _ada(c_all, w_ada, b_ada):
    rows = c_all.shape[0]
    n = w_ada.shape[2]
    tn = 1536
    return pl.pallas_call(
        _ada_kernel,
        grid=(DEPTH, n // tn),
        in_specs=[pl.BlockSpec((rows, D_MODEL), lambda l, j: (0, 0)),
                  pl.BlockSpec((1, D_MODEL, tn), lambda l, j: (l, 0, j)),
                  pl.BlockSpec((1, 1, tn), lambda l, j: (l, 0, j))],
        out_specs=pl.BlockSpec((1, rows, tn), lambda l, j: (l, 0, j)),
        out_shape=jax.ShapeDtypeStruct((DEPTH, rows, n), F32),
        compiler_params=_params(("arbitrary", "arbitrary")),
        name="ada",
    )(c_all, w_ada, b_ada.reshape(DEPTH, 1, n))


def _rotate(x, cos, sin_signed):
    w = x.shape[-1]
    lane = lax.broadcasted_iota(jnp.int32, x.shape, x.ndim - 1)
    first = (lane % ATT_DH) < (ATT_DH // 2)
    swapped = jnp.where(first, pltpu.roll(x, w - ATT_DH // 2, axis=x.ndim - 1),
                        pltpu.roll(x, ATT_DH // 2, axis=x.ndim - 1))
    return x * cos + swapped * sin_signed


def _inproj_kernel(x_ref, mod_ref, g_ref, w_ref, cos_ref, sin_ref, *out_refs, nb, tt, emit_keys):
    x = x_ref[...]
    ms = jnp.mean(x * x, axis=-1, keepdims=True)
    xn = x * lax.rsqrt(ms + EPS) * g_ref[...]
    mod = mod_ref[...]
    h = xn * (1.0 + mod[:, 1:2, :]) + mod[:, 0:1, :]
    u = _dot(h.reshape(nb * tt, D_MODEL).astype(BF16), w_ref[...])
    cos = cos_ref[...]
    sin = sin_ref[...]
    off = 0
    segs = {}
    for (name, width), o_ref in zip(OUT_SEGS, out_refs):
        seg = u[:, off:off + width]
        if name in ("q", "k", "qi", "kis"):
            t0 = ATT_W if name == "kis" else 0
            c3 = cos[:, t0:t0 + width][None]
            s3 = sin[:, t0:t0 + width][None]
            if nb > 1:
                c3 = jnp.broadcast_to(c3, (nb, tt, width)).reshape(nb * tt, width)
                s3 = jnp.broadcast_to(s3, (nb, tt, width)).reshape(nb * tt, width)
            else:
                c3 = c3[0]
                s3 = s3[0]
            seg = _rotate(seg, c3, s3)
        o_ref[...] = seg.reshape(nb, tt, width)
        segs[name] = seg
        off += width
    if emit_keys:
        k_b_ref, ki_b_ref, vt_b_ref = out_refs[len(OUT_SEGS):]
        k_b_ref[0] = segs["k"].astype(BF16)
        ki_b_ref[0] = segs["kis"][:, :IDX_DH].astype(BF16)
        vt_b_ref[0] = segs["v"].T.astype(BF16)


def _inproj(x, mod, g1, w_perm, cos, sin, nb, tt, emit_keys):
    B, T, _ = x.shape
    grid = (B // nb, T // tt)
    out_shape = tuple(jax.ShapeDtypeStruct((B, T, w), F32) for _, w in OUT_SEGS)
    out_specs = tuple(pl.BlockSpec((nb, tt, w), lambda b, t: (b, t, 0)) for _, w in OUT_SEGS)
    if emit_keys:
        assert nb == 1
        out_shape += (jax.ShapeDtypeStruct((B, T, ATT_W), BF16), jax.ShapeDtypeStruct((B, T, IDX_DH), BF16),
                      jax.ShapeDtypeStruct((B, ATT_W, T), BF16))
        out_specs += (pl.BlockSpec((1, tt, ATT_W), lambda b, t: (b, t, 0)),
                      pl.BlockSpec((1, tt, IDX_DH), lambda b, t: (b, t, 0)),
                      pl.BlockSpec((1, ATT_W, tt), lambda b, t: (b, 0, t)))
    return pl.pallas_call(
        functools.partial(_inproj_kernel, nb=nb, tt=tt, emit_keys=emit_keys),
        grid=grid,
        in_specs=[pl.BlockSpec((nb, tt, D_MODEL), lambda b, t: (b, t, 0)),
                  pl.BlockSpec((nb, 6, D_MODEL), lambda b, t: (b, 0, 0)),
                  pl.BlockSpec((1, D_MODEL), lambda b, t: (0, 0)),
                  pl.BlockSpec((D_MODEL, PERM_W), lambda b, t: (0, 0)),
                  pl.BlockSpec((tt, ROT_W), lambda b, t: (t, 0)),
                  pl.BlockSpec((tt, ROT_W), lambda b, t: (t, 0))],
        out_specs=out_specs,
        out_shape=out_shape,
        compiler_params=_params(("arbitrary", "arbitrary")),
        name="inproj",
    )(x, mod, g1, w_perm, cos, sin)


def _short_conv(ext_ref, bi, u, w, c):
    lo = SUBLANES - (CONV_W - 1)
    ext_ref[bi, SUBLANES:SUBLANES + c, :] = u
    y = ext_ref[bi, lo:lo + c, :] * w[0:1, :]
    for j in range(1, CONV_W):
        y = y + ext_ref[bi, lo + j:lo + j + c, :] * w[j:j + 1, :]
    tail = ext_ref[bi, c + lo:c + SUBLANES, :]
    ext_ref[bi, lo:SUBLANES, :] = tail
    return y, tail


def _neumann_inverses(mats, c):
    eye = (lax.broadcasted_iota(jnp.int32, (c, c), 0)
           == lax.broadcasted_iota(jnp.int32, (c, c), 1)).astype(F32)
    ps = [eye - a for a in mats]
    sp = [_split(a) for a in mats]
    pws = [_mm3(s, s) for s in sp]
    n = 2
    while n < c:
        sp = [_split(pw) for pw in pws]
        ps = [p + _mm3(_split(p), s) for p, s in zip(ps, sp)]
        n *= 2
        if n < c:
            pws = [_mm3(s, s) for s in sp]
    return ps


def _gdn_kernel(qkv_ref, kis_ref, gate_ref, w_ref, prev_ref, s0_ref, alog_ref, dtb_ref, ng_ref,
                o_ref, new_ref, s_ref, ext_ref, s_sc, *, c, nbb):
    j = pl.program_id(1)
    lo = SUBLANES - (CONV_W - 1)

    @pl.when(j == 0)
    def _():
        s_sc[...] = s0_ref[...]
        ext_ref[:, lo:SUBLANES, :] = prev_ref[...]

    ng = ng_ref[...]
    w_conv = w_ref[...]
    ri = lax.broadcasted_iota(jnp.int32, (c, c), 0)
    ci = lax.broadcasted_iota(jnp.int32, (c, c), 1)
    incl = ri >= ci
    strict = ri > ci
    nkd = GDN_HEADS * GDN_DK
    ch = []
    for bi in range(nbb):
        y, tail = _short_conv(ext_ref, bi, qkv_ref[bi], w_conv, c)
        new_ref[bi] = tail
        y = _silu(y)
        small = kis_ref[bi]
        g = -jnp.exp(alog_ref[...]) * _softplus(small + dtb_ref[...])
        beta = jax.nn.sigmoid(small)
        gc = _cumsum_rows(g)
        gc_t = _transpose_rows(gc)
        qk_n = y[:, :2 * nkd]
        qk_n = qk_n * lax.rsqrt(_head_sums(qk_n * qk_n) + EPS)
        for h in range(GDN_HEADS):
            qh = qk_n[:, h * GDN_DK:(h + 1) * GDN_DK] * (GDN_DK ** -0.5)
            kh = qk_n[:, nkd + h * GDN_DK:nkd + (h + 1) * GDN_DK]
            vh = y[:, 2 * nkd + h * GDN_DV:2 * nkd + (h + 1) * GDN_DV]
            ch.append((qh, kh, vh, gc[:, KIS_A + h:KIS_A + h + 1], gc_t[KIS_A + h:KIS_A + h + 1, :],
                       beta[:, KIS_B + h:KIS_B + h + 1]))
    ids = [(bi, h) for bi in range(nbb) for h in range(GDN_HEADS)]
    decay = [jnp.where(incl, jnp.exp(jnp.where(incl, gcol - grow, 0.0)), 0.0)
             for (_, _, _, gcol, grow, _) in ch]
    kb = [kh * bcol for (_, kh, _, _, _, bcol) in ch]
    eg = [jnp.exp(gcol) for (_, _, _, gcol, _, _) in ch]
    grams = [_mm_nt(jnp.concatenate([kbi, x[0]], axis=0), x[1]) for kbi, x in zip(kb, ch)]
    t_inv = _neumann_inverses([jnp.where(strict, m[:c] * d, 0.0) for m, d in zip(grams, decay)], c)
    attn = [jnp.where(incl, m[c:] * d, 0.0) for m, d in zip(grams, decay)]
    rhs = [jnp.concatenate([x[2] * x[5], kbi * e], axis=1) for x, kbi, e in zip(ch, kb, eg)]
    sol = [_mm(t, r) for t, r in zip(t_inv, rhs)]
    st = [s_sc[bi, h] for bi, h in ids]
    ws = [_mm(sl[:, GDN_DV:], s) for sl, s in zip(sol, st)]
    qs = [_mm(x[0] * e, s) for x, e, s in zip(ch, eg, st)]
    v_new = [sl[:, :GDN_DV] - w for sl, w in zip(sol, ws)]
    av = [_mm(a, v) for a, v in zip(attn, v_new)]
    g_last = [x[3][c - 1:c, :] for x in ch]
    kv = [_dot_tn(x[1] * jnp.exp(gl - x[3]), v) for x, gl, v in zip(ch, g_last, v_new)]
    for (bi, h), s, gl, kvi in zip(ids, st, g_last, kv):
        s_new = s * jnp.exp(gl) + kvi
        s_sc[bi, h] = s_new
        s_ref[bi, h] = s_new
    for bi in range(nbb):
        o = jnp.concatenate([qs[bi * GDN_HEADS + h] + av[bi * GDN_HEADS + h] for h in range(GDN_HEADS)], axis=1)
        o = o * lax.rsqrt(_head_sums(o * o) * (1.0 / GDN_DV) + EPS) * ng
        o_ref[bi] = o * _silu(gate_ref[bi])


def _gdn(gqkv, kis, ggate, conv_w, conv_prev, s0, a_log, dt_bias, norm_g, nbb):
    B, T, _ = gqkv.shape
    c = math.gcd(T, CHUNK)
    return pl.pallas_call(
        functools.partial(_gdn_kernel, c=c, nbb=nbb),
        grid=(B // nbb, T // c),
        in_specs=[pl.BlockSpec((nbb, c, GDN_CONV_C), lambda b, j: (b, j, 0)),
                  pl.BlockSpec((nbb, c, LANES), lambda b, j: (b, j, 0)),
                  pl.BlockSpec((nbb, c, GDN_W), lambda b, j: (b, j, 0)),
                  pl.BlockSpec((CONV_W, GDN_CONV_C), lambda b, j: (0, 0)),
                  pl.BlockSpec((nbb, CONV_W - 1, GDN_CONV_C), lambda b, j: (b, 0, 0)),
                  pl.BlockSpec((nbb, GDN_HEADS, GDN_DK, GDN_DV), lambda b, j: (b, 0, 0, 0)),
                  pl.BlockSpec((1, LANES), lambda b, j: (0, 0)),
                  pl.BlockSpec((1, LANES), lambda b, j: (0, 0)),
                  pl.BlockSpec((1, GDN_W), lambda b, j: (0, 0))],
        out_specs=(pl.BlockSpec((nbb, c, GDN_W), lambda b, j: (b, j, 0)),
                   pl.BlockSpec((nbb, CONV_W - 1, GDN_CONV_C), lambda b, j: (b, 0, 0)),
                   pl.BlockSpec((nbb, GDN_HEADS, GDN_DK, GDN_DV), lambda b, j: (b, 0, 0, 0))),
        out_shape=(jax.ShapeDtypeStruct((B, T, GDN_W), F32),
                   jax.ShapeDtypeStruct((B, CONV_W - 1, GDN_CONV_C), F32),
                   jax.ShapeDtypeStruct((B, GDN_HEADS, GDN_DK, GDN_DV), F32)),
        scratch_shapes=[pltpu.VMEM((nbb, SUBLANES + c, GDN_CONV_C), F32),
                        pltpu.VMEM((nbb, GDN_HEADS, GDN_DK, GDN_DV), F32)],
        compiler_params=_params(("arbitrary", "arbitrary")),
        name="gdn",
    )(gqkv, kis, ggate, conv_w, conv_prev, s0, _lane_vector(a_log, KIS_A), _lane_vector(dt_bias, KIS_A),
      jnp.tile(norm_g, GDN_HEADS).reshape(1, -1))


def _ssd_kernel(xbc_ref, z_ref, kis_ref, w_ref, cb_ref, prev_ref, h0_ref, alog_ref, dtb_ref, dsk_ref,
                ng_ref, o_ref, new_ref, h_ref, ext_ref, h_sc, *, c, nbb):
    j = pl.program_id(1)
    lo = SUBLANES - (CONV_W - 1)

    @pl.when(j == 0)
    def _():
        h_sc[...] = h0_ref[...]
        ext_ref[:, lo:SUBLANES, :] = prev_ref[...]

    dsk = dsk_ref[...]
    ng = ng_ref[...]
    w_conv = w_ref[...]
    ri = lax.broadcasted_iota(jnp.int32, (c, c), 0)
    ci = lax.broadcasted_iota(jnp.int32, (c, c), 1)
    incl = ri >= ci
    gn = SSM_GROUPS * SSM_N
    rep = SSM_HEADS // SSM_GROUPS
    gw = SSM_W // SSM_GROUPS
    grp_in = []
    ch = []
    ids = []
    for bi in range(nbb):
        y, tail = _short_conv(ext_ref, bi, xbc_ref[bi], w_conv, c)
        new_ref[bi] = tail
        y = _silu(y + cb_ref[...])
        dts = _softplus(kis_ref[bi] + dtb_ref[...])
        a = dts * (-jnp.exp(alog_ref[...]))
        acum = _cumsum_rows(a)
        acum_t = _transpose_rows(acum)
        zg = _silu(z_ref[bi])
        for grp in range(SSM_GROUPS):
            grp_in.append((y[:, SSM_W + grp * SSM_N:SSM_W + (grp + 1) * SSM_N],
                           y[:, SSM_W + gn + grp * SSM_N:SSM_W + gn + (grp + 1) * SSM_N],
                           zg[:, grp * gw:(grp + 1) * gw]))
            for hh in range(rep):
                h = grp * rep + hh
                ch.append((y[:, h * SSM_P:(h + 1) * SSM_P], acum[:, KIS_DT + h:KIS_DT + h + 1],
                           acum_t[KIS_DT + h:KIS_DT + h + 1, :], dts[:, KIS_DT + h:KIS_DT + h + 1],
                           len(grp_in) - 1))
                ids.append((bi, h))
    cb = [_mm_nt(cg, bg) for bg, cg, _ in grp_in]
    seg = [jnp.where(incl, jnp.exp(jnp.where(incl, acol - arow, 0.0)), 0.0) for (_, acol, arow, _, _) in ch]
    hs = [h_sc[bi, h] for bi, h in ids]
    y_in = [_mm(cb[x[4]] * sg, x[0] * x[3]) for x, sg in zip(ch, seg)]
    y_st = [_mm_nt(grp_in[x[4]][1], hst) for x, hst in zip(ch, hs)]
    a_last = [x[1][c - 1:c, :] for x in ch]
    upd = [_dot_tn(x[0], grp_in[x[4]][0] * (jnp.exp(al - x[1]) * x[3])) for x, al in zip(ch, a_last)]
    outs = []
    for (bi, h), x, hst, al, yi, ys, up in zip(ids, ch, hs, a_last, y_in, y_st, upd):
        h_new = hst * jnp.exp(al) + up
        h_sc[bi, h] = h_new
        h_ref[bi, h] = h_new
        outs.append(yi + ys * jnp.exp(x[1]) + dsk[:, h:h + 1] * x[0])
    for gi, (_, _, zslice) in enumerate(grp_in):
        bi, grp = divmod(gi, SSM_GROUPS)
        yg = jnp.concatenate(outs[gi * rep:(gi + 1) * rep], axis=1) * zslice
        yg = yg * lax.rsqrt(jnp.mean(yg * yg, axis=-1, keepdims=True) + EPS)
        o_ref[bi, :, grp * gw:(grp + 1) * gw] = yg * ng[:, grp * gw:(grp + 1) * gw]


def _ssd(xbc, z, kis, conv_w, conv_b, conv_prev, h0, a_log, dt_bias, d_skip, norm_g, nbb):
    B, T, _ = xbc.shape
    c = math.gcd(T, CHUNK)
    return pl.pallas_call(
        functools.partial(_ssd_kernel, c=c, nbb=nbb),
        grid=(B // nbb, T // c),
        in_specs=[pl.BlockSpec((nbb, c, SSM_CONV_C), lambda b, j: (b, j, 0)),
                  pl.BlockSpec((nbb, c, SSM_W), lambda b, j: (b, j, 0)),
                  pl.BlockSpec((nbb, c, LANES), lambda b, j: (b, j, 0)),
                  pl.BlockSpec((CONV_W, SSM_CONV_C), lambda b, j: (0, 0)),
                  pl.BlockSpec((1, SSM_CONV_C), lambda b, j: (0, 0)),
                  pl.BlockSpec((nbb, CONV_W - 1, SSM_CONV_C), lambda b, j: (b, 0, 0)),
                  pl.BlockSpec((nbb, SSM_HEADS, SSM_P, SSM_N), lambda b, j: (b, 0, 0, 0)),
                  pl.BlockSpec((1, LANES), lambda b, j: (0, 0)),
                  pl.BlockSpec((1, LANES), lambda b, j: (0, 0)),
                  pl.BlockSpec((1, SSM_HEADS), lambda b, j: (0, 0)),
                  pl.BlockSpec((1, SSM_W), lambda b, j: (0, 0))],
        out_specs=(pl.BlockSpec((nbb, c, SSM_W), lambda b, j: (b, j, 0)),
                   pl.BlockSpec((nbb, CONV_W - 1, SSM_CONV_C), lambda b, j: (b, 0, 0)),
                   pl.BlockSpec((nbb, SSM_HEADS, SSM_P, SSM_N), lambda b, j: (b, 0, 0, 0))),
        out_shape=(jax.ShapeDtypeStruct((B, T, SSM_W), F32),
                   jax.ShapeDtypeStruct((B, CONV_W - 1, SSM_CONV_C), F32),
                   jax.ShapeDtypeStruct((B, SSM_HEADS, SSM_P, SSM_N), F32)),
        scratch_shapes=[pltpu.VMEM((nbb, SUBLANES + c, SSM_CONV_C), F32),
                        pltpu.VMEM((nbb, SSM_HEADS, SSM_P, SSM_N), F32)],
        compiler_params=_params(("arbitrary", "arbitrary")),
        name="ssd",
    )(xbc, z, kis, conv_w, conv_b.reshape(1, -1), conv_prev, h0, _lane_vector(a_log, KIS_DT),
      _lane_vector(dt_bias, KIS_DT), d_skip.reshape(1, -1), norm_g.reshape(1, -1))


IDX_BITS = 12


OPENING_PROBES = 14
ROUND_PROBES = 2
SEARCH_ROUNDS = 4096
SLAB = 128
ACC_ROWS = 32


def _slab_fold(x, op):
    return op(x.reshape(SLAB // ACC_ROWS, ACC_ROWS, x.shape[1]), axis=0)


def _dsa_kernel(q_ref, qi_ref, kis_ref, k_ref, vt_ref, ki_ref, o_ref, s_sc, lg_sc, p_sc, *, tq, s_len, s_valid,
                q_offset, j0, n_valid_q, topk):
    j = pl.program_id(1)
    n_slabs = s_len // SLAB
    lane = lax.broadcasted_iota(jnp.int32, (1, tq), 1)
    qpos = q_offset + (j0 + j) * tq + lane
    lim = jnp.minimum((qpos // CHUNK + 1) * CHUNK, s_valid)
    slab_row = lax.broadcasted_iota(jnp.int32, (SLAB, tq), 0)
    neg = jnp.full((ACC_ROWS, tq), -jnp.inf, F32)
    zero = jnp.zeros((ACC_ROWS, tq), F32)

    def rows_of(i):
        return pl.ds(pl.multiple_of(i * SLAB, SLAB), SLAB)

    def sweep(body, init):
        return lax.fori_loop(0, n_slabs, lambda i, c: body(i, slab_row + i * SLAB, c), init)

    def total(acc, op):
        return op(acc, axis=0, keepdims=True)

    qi_t = qi_ref[0].T.astype(BF16)
    kis_t = kis_ref[0].T
    wi = kis_t[KIS_WI:KIS_WI + IDX_HEADS, :] * (IDX_DH ** -0.5 * IDX_HEADS ** -0.5)
    assert IDX_HEADS == ATT_HEADS
    for h in range(IDX_HEADS):
        lg_sc[h] = _dot(ki_ref[0], qi_t[h * IDX_DH:(h + 1) * IDX_DH, :])

    def score_body(i, row, c):
        sc = jnp.maximum(lg_sc[0, rows_of(i), :], 0.0) * wi[0:1, :]
        for h in range(1, IDX_HEADS):
            sc = sc + jnp.maximum(lg_sc[h, rows_of(i), :], 0.0) * wi[h:h + 1, :]
        adm = row < lim
        s_sc[rows_of(i), :] = jnp.where(adm, sc, -jnp.inf)
        return (jnp.maximum(c[0], _slab_fold(jnp.where(adm, sc, -jnp.inf), jnp.max)),
                jnp.minimum(c[1], _slab_fold(jnp.where(adm, sc, jnp.inf), jnp.min)))

    mx_acc, mn_acc = sweep(score_body, (neg, -neg))
    smax = total(mx_acc, jnp.max)
    smin = total(mn_acc, jnp.min)

    def count_ge(v):
        return total(sweep(lambda i, row, acc: acc + _slab_fold((s_sc[rows_of(i), :] >= v).astype(F32), jnp.sum),
                           zero), jnp.sum)

    def largest_below(v):
        def body(i, row, acc):
            x = s_sc[rows_of(i), :]
            return jnp.maximum(acc, _slab_fold(jnp.where(x < v, x, -jnp.inf), jnp.max))

        return total(sweep(body, neg), jnp.max)

    few = (jnp.minimum(lim, s_len) <= topk) | (lane >= n_valid_q)
    top_full = count_ge(smax) >= topk
    lo0 = jnp.where(few, 0.0, jnp.where(top_full, smax, smin))
    hi0 = jnp.where(few, 0.0, smax)
    done0 = (few | top_full).astype(F32)

    def probe(lo, hi, done):
        mid = 0.5 * lo + 0.5 * hi
        ok = (mid > lo) & (mid < hi) & (done < 0.5)
        take = count_ge(mid) >= topk
        return jnp.where(ok & take, mid, lo), jnp.where(ok & jnp.logical_not(take), mid, hi)

    def opening(lo, hi):
        return lax.fori_loop(0, OPENING_PROBES, lambda _, c: probe(c[0], c[1], done0), (lo, hi))

    lo1, hi1 = lax.cond(jnp.min(done0) < 0.5, opening, lambda lo, hi: (lo, hi), lo0, hi0)

    def search_cond(c):
        return (jnp.min(c[2]) < 0.5) & (c[3] < SEARCH_ROUNDS)

    def search_body(c):
        lo, hi, done, it = c
        last = largest_below(hi)
        reached = count_ge(last) >= topk
        active = done < 0.5
        lo = jnp.where(active & reached, last, lo)
        hi = jnp.where(active & jnp.logical_not(reached), last, hi)
        done = jnp.maximum(done, reached.astype(F32))
        for _ in range(ROUND_PROBES):
            lo, hi = probe(lo, hi, done)
        return lo, hi, done, it + 1

    lo, _, _, _ = lax.while_loop(search_cond, search_body, (lo1, hi1, done0, jnp.int32(0)))
    t = jnp.where(few, -jnp.inf, lo)

    def gt_tie_body(i, row, c):
        x = s_sc[rows_of(i), :]
        return (c[0] + _slab_fold((x > t).astype(F32), jnp.sum), c[1] + _slab_fold((x == t).astype(F32), jnp.sum))

    n_gt, n_tie = sweep(gt_tie_body, (zero, zero))
    need = topk - total(n_gt, jnp.sum)
    excess = jnp.where(few, 0.0, total(n_tie, jnp.sum) - need)

    def tie_search():
        def ibody(b, m):
            cand = m | (jnp.int32(1) << (IDX_BITS - 1 - b))

            def body(i, row, acc):
                hit = (s_sc[rows_of(i), :] == t) & (row < cand)
                return acc + _slab_fold(hit.astype(F32), jnp.sum)

            return jnp.where(total(sweep(body, zero), jnp.sum) <= need, cand, m)

        return lax.fori_loop(0, IDX_BITS, ibody, jnp.zeros((1, tq), jnp.int32))

    m = lax.cond(jnp.max(excess) > 0.0, tie_search, lambda: jnp.full((1, tq), 2 ** IDX_BITS - 1, jnp.int32))

    q_t = q_ref[0].T * (ATT_DH ** -0.5)
    pair = LANES // ATT_DH
    head_in_pair = lax.broadcasted_iota(jnp.int32, (LANES, tq), 0) // ATT_DH
    for h in range(ATT_HEADS):
        g = h // pair
        qm = jnp.where(head_in_pair == h % pair, q_t[g * LANES:(g + 1) * LANES, :], 0.0).astype(BF16)
        lg_sc[h] = _dot(k_ref[0, :, g * LANES:(g + 1) * LANES], qm)

    def logit_body(i, row, mxs):
        x = s_sc[rows_of(i), :]
        sel = (row < lim) & ((x > t) | ((x == t) & (row < m)))
        out = []
        for h in range(ATT_HEADS):
            lh = jnp.where(sel, lg_sc[h, rows_of(i), :], -jnp.inf)
            lg_sc[h, rows_of(i), :] = lh
            out.append(jnp.maximum(mxs[h], _slab_fold(lh, jnp.max)))
        return tuple(out)

    mxs = [total(a, jnp.max) for a in sweep(logit_body, (neg,) * ATT_HEADS)]

    def soft_body(i, row, dens):
        out = []
        for h in range(ATT_HEADS):
            p = jnp.exp(lg_sc[h, rows_of(i), :] - mxs[h])
            p_sc[h, rows_of(i), :] = p.astype(BF16)
            out.append(dens[h] + _slab_fold(p, jnp.sum))
        return tuple(out)

    dens = [total(a, jnp.sum) for a in sweep(soft_body, (zero,) * ATT_HEADS)]
    outs = [_dot(vt_ref[0, h * ATT_DH:(h + 1) * ATT_DH, :], p_sc[h]) * (1.0 / dens[h]) for h in range(ATT_HEADS)]
    o_ref[0] = jnp.concatenate(outs, axis=0).T


def _dsa(q, qi, kis, k_b, vt_b, ki_b, *, tq, j0, nj, s_len, s_valid, q_offset, n_valid_q, topk):
    B = q.shape[0]
    assert s_len < 2 ** IDX_BITS and s_len <= k_b.shape[1] and s_len % SLAB == 0
    return pl.pallas_call(
        functools.partial(_dsa_kernel, tq=tq, s_len=s_len, s_valid=s_valid, q_offset=q_offset, j0=j0,
                          n_valid_q=n_valid_q, topk=topk),
        grid=(B, nj),
        in_specs=[pl.BlockSpec((1, tq, ATT_W), lambda b, j: (b, j0 + j, 0)),
                  pl.BlockSpec((1, tq, IDX_HEADS * IDX_DH), lambda b, j: (b, j0 + j, 0)),
                  pl.BlockSpec((1, tq, LANES), lambda b, j: (b, j0 + j, 0)),
                  pl.BlockSpec((1, s_len, ATT_W), lambda b, j: (b, 0, 0)),
                  pl.BlockSpec((1, ATT_W, s_len), lambda b, j: (b, 0, 0)),
                  pl.BlockSpec((1, s_len, IDX_DH), lambda b, j: (b, 0, 0))],
        out_specs=pl.BlockSpec((1, tq, ATT_W), lambda b, j: (b, j, 0)),
        out_shape=jax.ShapeDtypeStruct((B, nj * tq, ATT_W), F32),
        scratch_shapes=[pltpu.VMEM((s_len, tq), F32),
                        pltpu.VMEM((ATT_HEADS, s_len, tq), F32),
                        pltpu.VMEM((ATT_HEADS, s_len, tq), BF16)],
        compiler_params=_params(("arbitrary", "arbitrary")),
        name="dsa",
    )(q, qi, kis, k_b, vt_b, ki_b)


def _dsa_group(q, qi, kis, k_b, vt_b, ki_b, *, s_valid, q_offset, topk):
    B, T, _ = q.shape
    tq = 2 * LANES if T % (2 * LANES) == 0 else LANES
    n_valid_q = min(T, tq)
    if T < tq:
        padq = lambda a: jnp.pad(a, ((0, 0), (0, tq - T), (0, 0)))
        q, qi, kis = padq(q), padq(qi), padq(kis)
    nq = q.shape[1] // tq
    s_total = k_b.shape[1]
    outs = []
    j0 = 0
    while j0 < nq:
        nj = 1
        reach = -(-(q_offset + (j0 + nj) * tq) // CHUNK) * CHUNK
        s_len = min(s_total, -(-min(reach, s_valid) // LANES) * LANES)
        outs.append(_dsa(q, qi, kis, k_b, vt_b, ki_b, tq=tq, j0=j0, nj=nj, s_len=s_len, s_valid=s_valid,
                         q_offset=q_offset, n_valid_q=n_valid_q, topk=topk))
        j0 += nj
    out = outs[0] if len(outs) == 1 else jnp.concatenate(outs, axis=1)
    return out[:, :T]


def _ffn_kernel(x_ref, gdn_ref, att_ref, ssm_ref, mod_ref, wo_ref, g2_ref, wg_ref, wu_ref, cw_ref, cb_ref,
                wd_ref, prev_ref, fg_ref, *rest, nb, tt, nk, final):
    if final:
        xo_ref, y_ref, new_ref, h2_sc, acc_sc, buf_sc, carry_sc = rest
    else:
        xo_ref, new_ref, h2_sc, acc_sc, buf_sc, carry_sc = rest
        y_ref = None
    t = pl.program_id(1)
    k = pl.program_id(2)
    rows = nb * tt
    lo = SUBLANES - (FFN_CONV_W - 1)
    mod = mod_ref[...]

    @pl.when(k == 0)
    def _():
        mix = jnp.concatenate([gdn_ref[...], att_ref[...], ssm_ref[...]], axis=-1)
        proj = _dot(mix.reshape(rows, MIX_W).astype(BF16), wo_ref[...]).reshape(nb, tt, D_MODEL)
        x1 = x_ref[...] + mod[:, 2:3, :] * proj
        acc_sc[...] = x1
        ms = jnp.mean(x1 * x1, axis=-1, keepdims=True)
        h2 = x1 * lax.rsqrt(ms + EPS) * g2_ref[...]
        h2 = h2 * (1.0 + mod[:, 4:5, :]) + mod[:, 3:4, :]
        h2_sc[...] = h2.reshape(rows, D_MODEL).astype(BF16)

    @pl.when(t == 0)
    def _():
        carry_sc[k, :, lo:SUBLANES, :] = prev_ref[...]

    h2 = h2_sc[...]
    ag = _dot(h2, wg_ref[...])
    up = _dot(h2, wu_ref[...])
    tf = ag.shape[-1]
    buf_sc[:, lo:SUBLANES, :] = carry_sc[k, :, lo:SUBLANES, :]
    buf_sc[:, SUBLANES:SUBLANES + tt, :] = ag.reshape(nb, tt, tf)
    cw = cw_ref[...]
    conv = buf_sc[:, lo:lo + tt, :] * cw[0:1, :][None]
    for jj in range(1, FFN_CONV_W):
        conv = conv + buf_sc[:, lo + jj:lo + jj + tt, :] * cw[jj:jj + 1, :][None]
    tail = buf_sc[:, tt + lo:tt + SUBLANES, :]
    carry_sc[k, :, lo:SUBLANES, :] = tail
    new_ref[:, pl.ds(k, 1), :, :] = tail[:, None]
    act = _silu(conv + cb_ref[...][None]).reshape(rows, tf) * up
    y = _dot(act.astype(BF16), wd_ref[...]).reshape(nb, tt, D_MODEL)
    acc_sc[...] += mod[:, 5:6, :] * y

    @pl.when(k == nk - 1)
    def _():
        xo = acc_sc[...]
        xo_ref[...] = xo
        if final:
            ms = jnp.mean(xo * xo, axis=-1, keepdims=True)
            y_ref[...] = xo * lax.rsqrt(ms + EPS) * fg_ref[...]


def _ffn(x, gdn_o, att_o, ssm_o, mod, w_out, g2, w_gate, w_up, conv_w, conv_b, w_down, conv_prev, final_g,
         nb, tt, tf, final):
    B, T, _ = x.shape
    nk = D_FF // tf
    grid = (B // nb, T // tt, nk)
    row_spec = lambda w: pl.BlockSpec((nb, tt, w), lambda b, t, k: (b, t, 0))
    in_specs = [row_spec(D_MODEL), row_spec(GDN_W), row_spec(ATT_W), row_spec(SSM_W),
                pl.BlockSpec((nb, 6, D_MODEL), lambda b, t, k: (b, 0, 0)),
                pl.BlockSpec((MIX_W, D_MODEL), lambda b, t, k: (0, 0)),
                pl.BlockSpec((1, D_MODEL), lambda b, t, k: (0, 0)),
                pl.BlockSpec((D_MODEL, tf), lambda b, t, k: (0, k)),
                pl.BlockSpec((D_MODEL, tf), lambda b, t, k: (0, k)),
                pl.BlockSpec((FFN_CONV_W, tf), lambda b, t, k: (0, k)),
                pl.BlockSpec((1, tf), lambda b, t, k: (0, k)),
                pl.BlockSpec((tf, D_MODEL), lambda b, t, k: (k, 0)),
                pl.BlockSpec((nb, FFN_CONV_W - 1, tf), lambda b, t, k: (b, 0, k)),
                pl.BlockSpec((1, D_MODEL), lambda b, t, k: (0, 0))]
    out_shape = [jax.ShapeDtypeStruct((B, T, D_MODEL), F32)]
    out_specs = [row_spec(D_MODEL)]
    if final:
        out_shape.append(jax.ShapeDtypeStruct((B, T, D_MODEL), F32))
        out_specs.append(row_spec(D_MODEL))
    out_shape.append(jax.ShapeDtypeStruct((B, nk, FFN_CONV_W - 1, tf), F32))
    out_specs.append(pl.BlockSpec((nb, nk, FFN_CONV_W - 1, tf), lambda b, t, k: (b, 0, 0, 0)))
    res = pl.pallas_call(
        functools.partial(_ffn_kernel, nb=nb, tt=tt, nk=nk, final=final),
        grid=grid,
        in_specs=in_specs,
        out_specs=tuple(out_specs),
        out_shape=tuple(out_shape),
        scratch_shapes=[pltpu.VMEM((nb * tt, D_MODEL), BF16),
                        pltpu.VMEM((nb, tt, D_MODEL), F32),
                        pltpu.VMEM((nb, SUBLANES + tt, tf), F32),
                        pltpu.VMEM((nk, nb, SUBLANES, tf), F32)],
        compiler_params=_params(("arbitrary", "arbitrary", "arbitrary")),
        name="ffn",
    )(x, gdn_o, att_o, ssm_o, mod, w_out, g2, w_gate, w_up, conv_w, conv_b.reshape(1, -1), w_down,
      conv_prev, final_g)
    tail = res[-1].transpose(0, 2, 1, 3).reshape(B, FFN_CONV_W - 1, D_FF)
    return (*res[:-1], tail)


def _rope_tables(pos):
    half = ATT_DH // 2
    inv_freq = ROPE_THETA ** (-jnp.arange(half, dtype=F32) / half)
    ang = pos.astype(F32)[:, None] * inv_freq[None, :]
    cos = jnp.cos(ang)
    sin = jnp.sin(ang)
    cos_h = jnp.concatenate([cos, cos], axis=-1)
    sin_h = jnp.concatenate([-sin, sin], axis=-1)
    ones = jnp.ones((pos.shape[0], LANES - ATT_DH), F32)
    cos_t = jnp.concatenate([jnp.tile(cos_h, (1, ATT_HEADS)), cos_h, ones], axis=-1)
    sin_t = jnp.concatenate([jnp.tile(sin_h, (1, ATT_HEADS)), sin_h, 0.0 * ones], axis=-1)
    return cos_t, sin_t


def _layer(x, mod, pos_tables, cache, states, lw, tiles, final, final_g):
    (norm1_g, w_perm, gdn_conv_w, gdn_A_log, gdn_dt_bias, gdn_norm_g, ssm_conv_w, ssm_conv_b, ssm_A_log,
     ssm_dt_bias, ssm_D, ssm_norm_g, w_out, norm2_g, w_gate, w_up, ffn_conv_w, ffn_conv_b, w_down) = lw
    gdn_conv_prev, gdn_s0, ssm_conv_prev, ssm_h0, ffn_conv_prev = states
    B, T, _ = x.shape
    nb, tt, nbb, ffn_tt, tf = tiles
    cos_t, sin_t = pos_tables
    proj = _inproj(x, mod, norm1_g.reshape(1, -1), w_perm, cos_t, sin_t, nb, tt, emit_keys=cache is None)
    gqkv, ggate, q, k, v, qi, kis, z, xbc = proj[:len(OUT_SEGS)]
    ki = kis[..., :IDX_DH]

    gdn_o, gdn_conv_new, gdn_s = _gdn(gqkv, kis, ggate, gdn_conv_w, gdn_conv_prev, gdn_s0, gdn_A_log,
                                      gdn_dt_bias, gdn_norm_g, nbb)
    ssm_o, ssm_conv_new, ssm_h = _ssd(xbc, z, kis, ssm_conv_w, ssm_conv_b, ssm_conv_prev, ssm_h0, ssm_A_log,
                                      ssm_dt_bias, ssm_D, ssm_norm_g, nbb)
    if cache is None:
        k_b, ki_b, vt_b = proj[len(OUT_SEGS):]
        att_o = _dsa_group(q, qi, kis, k_b, vt_b, ki_b, s_valid=T, q_offset=0, topk=min(TOPK_MAX, T // 4))
    else:
        ck, cv, cki = cache
        P = ck.shape[1]
        L = P + T
        pad = -(-L // LANES) * LANES - L
        k_b = jnp.concatenate([ck.reshape(B, P, ATT_W), k, jnp.zeros((B, pad, ATT_W), F32)], axis=1).astype(BF16)
        vt_b = jnp.concatenate([cv.reshape(B, P, ATT_W), v, jnp.zeros((B, pad, ATT_W), F32)],
                               axis=1).transpose(0, 2, 1).astype(BF16)
        ki_b = jnp.concatenate([cki, ki, jnp.zeros((B, pad, IDX_DH), F32)], axis=1).astype(BF16)
        att_o = _dsa_group(q, qi, kis, k_b, vt_b, ki_b, s_valid=L, q_offset=P, topk=min(TOPK_MAX, L // 4))

    res = _ffn(x, gdn_o, att_o, ssm_o, mod, w_out, norm2_g.reshape(1, -1), w_gate, w_up, ffn_conv_w, ffn_conv_b,
               w_down, ffn_conv_prev, final_g.reshape(1, -1), ffn_tt[0], ffn_tt[1], tf, final)
    if final:
        x_new, y, ffn_conv_new = res
    else:
        (x_new, ffn_conv_new), y = res, None
    st = (k.reshape(B, T, ATT_HEADS, ATT_DH), v.reshape(B, T, ATT_HEADS, ATT_DH), ki,
          gdn_conv_new, gdn_s, ssm_conv_new, ssm_h, ffn_conv_new)
    return x_new, y, st


def kernel(x_prompt, x_sample, c_prompt, c_sample, cache_k, cache_v, cache_kidx, state_gdn_conv, state_gdn,
           state_ssm_conv, state_ssm, state_ffn_conv, w_ada, b_ada, norm1_g, w_in, gdn_conv_w, gdn_A_log,
           gdn_dt_bias, gdn_norm_g, ssm_conv_w, ssm_conv_b, ssm_A_log, ssm_dt_bias, ssm_D, ssm_norm_g, w_out,
           norm2_g, w_gate, w_up, ffn_conv_w, ffn_conv_b, w_down, final_g):
    Bp, T, _ = x_prompt.shape
    Bs, Ts, _ = x_sample.shape
    P = cache_k.shape[2]

    c_all = jnp.concatenate([c_prompt, c_sample], axis=0)
    mod_all = _ada(c_all, w_ada, b_ada).reshape(DEPTH, Bp + Bs, 6, D_MODEL)

    tables_p = _rope_tables(jnp.arange(T))
    tables_s = _rope_tables(P + jnp.arange(Ts))

    w_perm = _permute_columns(w_in).astype(BF16)
    w_out_b = w_out.astype(BF16)
    w_gate_b = w_gate.astype(BF16)
    w_up_b = w_up.astype(BF16)
    w_down_b = w_down.astype(BF16)

    zeros_p = (jnp.zeros((Bp, CONV_W - 1, GDN_CONV_C), F32),
               jnp.zeros((Bp, GDN_HEADS, GDN_DK, GDN_DV), F32),
               jnp.zeros((Bp, CONV_W - 1, SSM_CONV_C), F32),
               jnp.zeros((Bp, SSM_HEADS, SSM_P, SSM_N), F32),
               jnp.zeros((Bp, FFN_CONV_W - 1, D_FF), F32))

    tf = D_FF // 2
    tiles_p = (1, min(256, T), math.gcd(Bp, 4), (1, min(512, T)), tf)
    tiles_s = (Bs, Ts, math.gcd(Bs, 4), (Bs, Ts), tf)

    xp, xs = x_prompt, x_sample
    new_p, new_s = [], []
    yp = ys = None
    for l in range(DEPTH):
        lw = (norm1_g[l], w_perm[l], gdn_conv_w[l], gdn_A_log[l], gdn_dt_bias[l], gdn_norm_g[l], ssm_conv_w[l],
              ssm_conv_b[l], ssm_A_log[l], ssm_dt_bias[l], ssm_D[l], ssm_norm_g[l], w_out_b[l], norm2_g[l],
              w_gate_b[l], w_up_b[l], ffn_conv_w[l], ffn_conv_b[l], w_down_b[l])
        final = l == DEPTH - 1
        xp, yp, st_p = _layer(xp, mod_all[l, :Bp], tables_p, None, zeros_p, lw, tiles_p, final, final_g)
        states_s = (state_gdn_conv[l], state_gdn[l], state_ssm_conv[l], state_ssm[l], state_ffn_conv[l])
        xs, ys, st_s = _layer(xs, mod_all[l, Bp:], tables_s, (cache_k[l], cache_v[l], cache_kidx[l]), states_s,
                              lw, tiles_s, final, final_g)
        new_p.append(st_p)
        new_s.append(st_s)
    outs_p = [jnp.stack([st[i] for st in new_p]) for i in range(8)]
    outs_s = [jnp.stack([st[i] for st in new_s]) for i in range(8)]
    return (yp, ys, *outs_p, *outs_s)
```

```python
import functools
import math

import jax
import jax.numpy as jnp
import numpy as np
from jax import lax
from jax.experimental import pallas as pl
from jax.experimental.pallas import tpu as pltpu

F32 = jnp.float32
BF16 = jnp.bfloat16
HI = lax.Precision.HIGHEST

D_MODEL = 1024
DEPTH = 2
CHUNK = 64
CONV_W = 4
FFN_CONV_W = 3
D_FF = 2816
ROPE_THETA = 10000.0
EPS = 1e-6
GDN_HEADS = 4
GDN_DK = 64
GDN_DV = 64
ATT_HEADS = 4
ATT_DH = 64
IDX_HEADS = 4
IDX_DH = 64
TOPK_MAX = 256
SSM_HEADS = 8
SSM_P = 64
SSM_GROUPS = 2
SSM_N = 128
GDN_W = GDN_HEADS * GDN_DV
ATT_W = ATT_HEADS * ATT_DH
SSM_W = SSM_HEADS * SSM_P
MIX_W = GDN_W + ATT_W + SSM_W
GDN_CONV_C = 2 * GDN_HEADS * GDN_DK + GDN_W
SSM_CONV_C = SSM_W + 2 * SSM_GROUPS * SSM_N
IN_SIZES = (GDN_CONV_C, GDN_HEADS, GDN_HEADS, GDN_W,
            ATT_W, ATT_W, ATT_W, IDX_HEADS * IDX_DH, IDX_DH, IDX_HEADS,
            SSM_W, SSM_CONV_C, SSM_HEADS)
IN_W = sum(IN_SIZES)

LANES = 128
SUBLANES = 8
VMEM_LIMIT = 56 * 1024 * 1024

KIS_A = IDX_DH
KIS_B = KIS_A + GDN_HEADS
KIS_WI = KIS_B + GDN_HEADS
KIS_DT = KIS_WI + IDX_HEADS
KIS_END = KIS_DT + SSM_HEADS
OUT_SEGS = (("gqkv", GDN_CONV_C), ("ggate", GDN_W), ("q", ATT_W), ("k", ATT_W), ("v", ATT_W),
            ("qi", IDX_HEADS * IDX_DH), ("kis", LANES), ("z", SSM_W), ("xbc", SSM_CONV_C))
PERM_W = sum(w for _, w in OUT_SEGS)
ROT_W = ATT_W + LANES


def _perm_columns():
    starts = np.concatenate([[0], np.cumsum(IN_SIZES)])
    (s_gqkv, s_ga, s_gb, s_gg, s_q, s_k, s_v, s_qi, s_ki, s_wi, s_z, s_xbc, s_dt) = starts[:-1]
    cols = []
    cols += list(range(s_gqkv, s_gqkv + GDN_CONV_C))
    cols += list(range(s_gg, s_gg + GDN_W))
    cols += list(range(s_q, s_q + ATT_W))
    cols += list(range(s_k, s_k + ATT_W))
    cols += list(range(s_v, s_v + ATT_W))
    cols += list(range(s_qi, s_qi + IDX_HEADS * IDX_DH))
    kis = (list(range(s_ki, s_ki + IDX_DH)) + list(range(s_ga, s_ga + GDN_HEADS))
           + list(range(s_gb, s_gb + GDN_HEADS)) + list(range(s_wi, s_wi + IDX_HEADS))
           + list(range(s_dt, s_dt + SSM_HEADS)))
    cols += kis + [-1] * (LANES - len(kis))
    cols += list(range(s_z, s_z + SSM_W))
    cols += list(range(s_xbc, s_xbc + SSM_CONV_C))
    assert len(cols) == PERM_W
    return np.asarray(cols, np.int32)


_PERM_COLS = _perm_columns()


def _permute_columns(w_in):
    pieces = []
    start = 0
    for i in range(1, PERM_W + 1):
        prev = int(_PERM_COLS[i - 1])
        if i == PERM_W or not ((prev < 0 and _PERM_COLS[i] < 0) or (prev >= 0 and _PERM_COLS[i] == prev + 1)):
            if _PERM_COLS[start] < 0:
                pieces.append(jnp.zeros(w_in.shape[:-1] + (i - start,), w_in.dtype))
            else:
                pieces.append(w_in[..., int(_PERM_COLS[start]):int(_PERM_COLS[start]) + i - start])
            start = i
    return jnp.concatenate(pieces, axis=-1)


def _silu(x):
    return x * jax.nn.sigmoid(x)


def _softplus(x):
    return jnp.maximum(x, 0.0) + jnp.log1p(jnp.exp(-jnp.abs(x)))


def _dot(a, b, precision=None):
    return jnp.dot(a, b, preferred_element_type=F32, precision=precision)


def _dot_nt(a, b, precision=None):
    return lax.dot_general(a, b, (((1,), (1,)), ((), ())), preferred_element_type=F32,
                           precision=precision)


def _dot_tn(a, b, precision=None):
    return lax.dot_general(a, b, (((0,), (0,)), ((), ())), preferred_element_type=F32,
                           precision=precision)


def _cumsum_rows(x):
    c = x.shape[0]
    row = lax.broadcasted_iota(jnp.int32, x.shape, 0)
    s = 1
    while s < c:
        x = x + jnp.where(row >= s, pltpu.roll(x, s, axis=0), 0.0)
        s *= 2
    return x


def _mm(a, b):
    return _dot(a.astype(BF16), b.astype(BF16))


def _mm_nt(a, b):
    return _dot_nt(a.astype(BF16), b.astype(BF16))


def _split(a):
    hi = a.astype(BF16)
    return hi, (a - hi.astype(F32)).astype(BF16)


def _mm3(a, b):
    (ah, al), (bh, bl) = a, b
    return _dot(ah, bh) + (_dot(ah, bl) + _dot(al, bh))


def _head_sums(x):
    r = lax.broadcasted_iota(jnp.int32, (LANES, LANES), 0) // GDN_DV
    cidx = lax.broadcasted_iota(jnp.int32, (LANES, LANES), 1) // GDN_DV
    ones = (r == cidx).astype(BF16)
    hi, lo = _split(x)
    cols = []
    for g in range(x.shape[1] // LANES):
        sl = slice(g * LANES, (g + 1) * LANES)
        cols.append(_dot(hi[:, sl], ones) + _dot(lo[:, sl], ones))
    return jnp.concatenate(cols, axis=1)


def _transpose_rows(vals):
    c = vals.shape[0]
    if c < LANES:
        vals = jnp.concatenate([vals, jnp.zeros((LANES - c, LANES), vals.dtype)], axis=0)
    return vals.T[:, :c]


def _lane_vector(vals, start):
    return jnp.zeros((1, LANES), F32).at[0, start:start + vals.shape[0]].set(vals.astype(F32))


def _params(sem):
    return pltpu.CompilerParams(dimension_semantics=sem, vmem_limit_bytes=VMEM_LIMIT)


def _ada_kernel(c_ref, w_ref, b_ref, o_ref):
    s = _silu(c_ref[...])
    o_ref[0] = _dot(s.astype(BF16), w_ref[0].astype(BF16)) + b_ref[0]


def _ada(c_all, w_ada, b_ada):
    rows = c_all.shape[0]
    n = w_ada.shape[2]
    tn = 1536
    return pl.pallas_call(
        _ada_kernel,
        grid=(DEPTH, n // tn),
        in_specs=[pl.BlockSpec((rows, D_MODEL), lambda l, j: (0, 0)),
                  pl.BlockSpec((1, D_MODEL, tn), lambda l, j: (l, 0, j)),
                  pl.BlockSpec((1, 1, tn), lambda l, j: (l, 0, j))],
        out_specs=pl.BlockSpec((1, rows, tn), lambda l, j: (l, 0, j)),
        out_shape=jax.ShapeDtypeStruct((DEPTH, rows, n), F32),
        compiler_params=_params(("arbitrary", "arbitrary")),
        name="ada",
    )(c_all, w_ada, b_ada.reshape(DEPTH, 1, n))


def _rotate(x, cos, sin_signed):
    w = x.shape[-1]
    lane = lax.broadcasted_iota(jnp.int32, x.shape, x.ndim - 1)
    first = (lane % ATT_DH) < (ATT_DH // 2)
    swapped = jnp.where(first, pltpu.roll(x, w - ATT_DH // 2, axis=x.ndim - 1),
                        pltpu.roll(x, ATT_DH // 2, axis=x.ndim - 1))
    return x * cos + swapped * sin_signed


def _inproj_kernel(x_ref, mod_ref, g_ref, w_ref, cos_ref, sin_ref, *out_refs, nb, tt, emit_keys):
    x = x_ref[...]
    ms = jnp.mean(x * x, axis=-1, keepdims=True)
    xn = x * lax.rsqrt(ms + EPS) * g_ref[...]
    mod = mod_ref[...]
    h = xn * (1.0 + mod[:, 1:2, :]) + mod[:, 0:1, :]
    u = _dot(h.reshape(nb * tt, D_MODEL).astype(BF16), w_ref[...])
    cos = cos_ref[...]
    sin = sin_ref[...]
    off = 0
    segs = {}
    for (name, width), o_ref in zip(OUT_SEGS, out_refs):
        seg = u[:, off:off + width]
        if name in ("q", "k", "qi", "kis"):
            t0 = ATT_W if name == "kis" else 0
            c3 = cos[:, t0:t0 + width][None]
            s3 = sin[:, t0:t0 + width][None]
            if nb > 1:
                c3 = jnp.broadcast_to(c3, (nb, tt, width)).reshape(nb * tt, width)
                s3 = jnp.broadcast_to(s3, (nb, tt, width)).reshape(nb * tt, width)
            else:
                c3 = c3[0]
                s3 = s3[0]
            seg = _rotate(seg, c3, s3)
        o_ref[...] = seg.reshape(nb, tt, width)
        segs[name] = seg
        off += width
    if emit_keys:
        k_b_ref, ki_b_ref, vt_b_ref = out_refs[len(OUT_SEGS):]
        k_b_ref[0] = segs["k"].astype(BF16)
        ki_b_ref[0] = segs["kis"][:, :IDX_DH].astype(BF16)
        vt_b_ref[0] = segs["v"].T.astype(BF16)


def _inproj(x, mod, g1, w_perm, cos, sin, nb, tt, emit_keys):
    B, T, _ = x.shape
    grid = (B // nb, T // tt)
    out_shape = tuple(jax.ShapeDtypeStruct((B, T, w), F32) for _, w in OUT_SEGS)
    out_specs = tuple(pl.BlockSpec((nb, tt, w), lambda b, t: (b, t, 0)) for _, w in OUT_SEGS)
    if emit_keys:
        assert nb == 1
        out_shape += (jax.ShapeDtypeStruct((B, T, ATT_W), BF16), jax.ShapeDtypeStruct((B, T, IDX_DH), BF16),
                      jax.ShapeDtypeStruct((B, ATT_W, T), BF16))
        out_specs += (pl.BlockSpec((1, tt, ATT_W), lambda b, t: (b, t, 0)),
                      pl.BlockSpec((1, tt, IDX_DH), lambda b, t: (b, t, 0)),
                      pl.BlockSpec((1, ATT_W, tt), lambda b, t: (b, 0, t)))
    return pl.pallas_call(
        functools.partial(_inproj_kernel, nb=nb, tt=tt, emit_keys=emit_keys),
        grid=grid,
        in_specs=[pl.BlockSpec((nb, tt, D_MODEL), lambda b, t: (b, t, 0)),
                  pl.BlockSpec((nb, 6, D_MODEL), lambda b, t: (b, 0, 0)),
                  pl.BlockSpec((1, D_MODEL), lambda b, t: (0, 0)),
                  pl.BlockSpec((D_MODEL, PERM_W), lambda b, t: (0, 0)),
                  pl.BlockSpec((tt, ROT_W), lambda b, t: (t, 0)),
                  pl.BlockSpec((tt, ROT_W), lambda b, t: (t, 0))],
        out_specs=out_specs,
        out_shape=out_shape,
        compiler_params=_params(("arbitrary", "arbitrary")),
        name="inproj",
    )(x, mod, g1, w_perm, cos, sin)


def _short_conv(ext_ref, bi, u, w, c):
    lo = SUBLANES - (CONV_W - 1)
    ext_ref[bi, SUBLANES:SUBLANES + c, :] = u
    y = ext_ref[bi, lo:lo + c, :] * w[0:1, :]
    for j in range(1, CONV_W):
        y = y + ext_ref[bi, lo + j:lo + j + c, :] * w[j:j + 1, :]
    tail = ext_ref[bi, c + lo:c + SUBLANES, :]
    ext_ref[bi, lo:SUBLANES, :] = tail
    return y, tail


def _neumann_inverses(mats, c):
    eye = (lax.broadcasted_iota(jnp.int32, (c, c), 0)
           == lax.broadcasted_iota(jnp.int32, (c, c), 1)).astype(F32)
    ps = [eye - a for a in mats]
    sp = [_split(a) for a in mats]
    pws = [_mm3(s, s) for s in sp]
    n = 2
    while n < c:
        sp = [_split(pw) for pw in pws]
        ps = [p + _mm3(_split(p), s) for p, s in zip(ps, sp)]
        n *= 2
        if n < c:
            pws = [_mm3(s, s) for s in sp]
    return ps


def _gdn_kernel(qkv_ref, kis_ref, gate_ref, w_ref, prev_ref, s0_ref, alog_ref, dtb_ref, ng_ref,
                o_ref, new_ref, s_ref, ext_ref, s_sc, *, c, nbb):
    j = pl.program_id(1)
    lo = SUBLANES - (CONV_W - 1)

    @pl.when(j == 0)
    def _():
        s_sc[...] = s0_ref[...]
        ext_ref[:, lo:SUBLANES, :] = prev_ref[...]

    ng = ng_ref[...]
    w_conv = w_ref[...]
    ri = lax.broadcasted_iota(jnp.int32, (c, c), 0)
    ci = lax.broadcasted_iota(jnp.int32, (c, c), 1)
    incl = ri >= ci
    strict = ri > ci
    nkd = GDN_HEADS * GDN_DK
    ch = []
    for bi in range(nbb):
        y, tail = _short_conv(ext_ref, bi, qkv_ref[bi], w_conv, c)
        new_ref[bi] = tail
        y = _silu(y)
        small = kis_ref[bi]
        g = -jnp.exp(alog_ref[...]) * _softplus(small + dtb_ref[...])
        beta = jax.nn.sigmoid(small)
        gc = _cumsum_rows(g)
        gc_t = _transpose_rows(gc)
        qk_n = y[:, :2 * nkd]
        qk_n = qk_n * lax.rsqrt(_head_sums(qk_n * qk_n) + EPS)
        for h in range(GDN_HEADS):
            qh = qk_n[:, h * GDN_DK:(h + 1) * GDN_DK] * (GDN_DK ** -0.5)
            kh = qk_n[:, nkd + h * GDN_DK:nkd + (h + 1) * GDN_DK]
            vh = y[:, 2 * nkd + h * GDN_DV:2 * nkd + (h + 1) * GDN_DV]
            ch.append((qh, kh, vh, gc[:, KIS_A + h:KIS_A + h + 1], gc_t[KIS_A + h:KIS_A + h + 1, :],
                       beta[:, KIS_B + h:KIS_B + h + 1]))
    ids = [(bi, h) for bi in range(nbb) for h in range(GDN_HEADS)]
    decay = [jnp.where(incl, jnp.exp(jnp.where(incl, gcol - grow, 0.0)), 0.0)
             for (_, _, _, gcol, grow, _) in ch]
    kb = [kh * bcol for (_, kh, _, _, _, bcol) in ch]
    eg = [jnp.exp(gcol) for (_, _, _, gcol, _, _) in ch]
    grams = [_mm_nt(jnp.concatenate([kbi, x[0]], axis=0), x[1]) for kbi, x in zip(kb, ch)]
    t_inv = _neumann_inverses([jnp.where(strict, m[:c] * d, 0.0) for m, d in zip(grams, decay)], c)
    attn = [jnp.where(incl, m[c:] * d, 0.0) for m, d in zip(grams, decay)]
    rhs = [jnp.concatenate([x[2] * x[5], kbi * e], axis=1) for x, kbi, e in zip(ch, kb, eg)]
    sol = [_mm(t, r) for t, r in zip(t_inv, rhs)]
    st = [s_sc[bi, h] for bi, h in ids]
    ws = [_mm(sl[:, GDN_DV:], s) for sl, s in zip(sol, st)]
    qs = [_mm(x[0] * e, s) for x, e, s in zip(ch, eg, st)]
    v_new = [sl[:, :GDN_DV] - w for sl, w in zip(sol, ws)]
    av = [_mm(a, v) for a, v in zip(attn, v_new)]
    g_last = [x[3][c - 1:c, :] for x in ch]
    kv = [_dot_tn(x[1] * jnp.exp(gl - x[3]), v) for x, gl, v in zip(ch, g_last, v_new)]
    for (bi, h), s, gl, kvi in zip(ids, st, g_last, kv):
        s_new = s * jnp.exp(gl) + kvi
        s_sc[bi, h] = s_new
        s_ref[bi, h] = s_new
    for bi in range(nbb):
        o = jnp.concatenate([qs[bi * GDN_HEADS + h] + av[bi * GDN_HEADS + h] for h in range(GDN_HEADS)], axis=1)
        o = o * lax.rsqrt(_head_sums(o * o) * (1.0 / GDN_DV) + EPS) * ng
        o_ref[bi] = o * _silu(gate_ref[bi])


def _gdn(gqkv, kis, ggate, conv_w, conv_prev, s0, a_log, dt_bias, norm_g, nbb):
    B, T, _ = gqkv.shape
    c = math.gcd(T, CHUNK)
    return pl.pallas_call(
        functools.partial(_gdn_kernel, c=c, nbb=nbb),
        grid=(B // nbb, T // c),
        in_specs=[pl.BlockSpec((nbb, c, GDN_CONV_C), lambda b, j: (b, j, 0)),
                  pl.BlockSpec((nbb, c, LANES), lambda b, j: (b, j, 0)),
                  pl.BlockSpec((nbb, c, GDN_W), lambda b, j: (b, j, 0)),
                  pl.BlockSpec((CONV_W, GDN_CONV_C), lambda b, j: (0, 0)),
                  pl.BlockSpec((nbb, CONV_W - 1, GDN_CONV_C), lambda b, j: (b, 0, 0)),
                  pl.BlockSpec((nbb, GDN_HEADS, GDN_DK, GDN_DV), lambda b, j: (b, 0, 0, 0)),
                  pl.BlockSpec((1, LANES), lambda b, j: (0, 0)),
                  pl.BlockSpec((1, LANES), lambda b, j: (0, 0)),
                  pl.BlockSpec((1, GDN_W), lambda b, j: (0, 0))],
        out_specs=(pl.BlockSpec((nbb, c, GDN_W), lambda b, j: (b, j, 0)),
                   pl.BlockSpec((nbb, CONV_W - 1, GDN_CONV_C), lambda b, j: (b, 0, 0)),
                   pl.BlockSpec((nbb, GDN_HEADS, GDN_DK, GDN_DV), lambda b, j: (b, 0, 0, 0))),
        out_shape=(jax.ShapeDtypeStruct((B, T, GDN_W), F32),
                   jax.ShapeDtypeStruct((B, CONV_W - 1, GDN_CONV_C), F32),
                   jax.ShapeDtypeStruct((B, GDN_HEADS, GDN_DK, GDN_DV), F32)),
        scratch_shapes=[pltpu.VMEM((nbb, SUBLANES + c, GDN_CONV_C), F32),
                        pltpu.VMEM((nbb, GDN_HEADS, GDN_DK, GDN_DV), F32)],
        compiler_params=_params(("arbitrary", "arbitrary")),
        name="gdn",
    )(gqkv, kis, ggate, conv_w, conv_prev, s0, _lane_vector(a_log, KIS_A), _lane_vector(dt_bias, KIS_A),
      jnp.tile(norm_g, GDN_HEADS).reshape(1, -1))


def _ssd_kernel(xbc_ref, z_ref, kis_ref, w_ref, cb_ref, prev_ref, h0_ref, alog_ref, dtb_ref, dsk_ref,
                ng_ref, o_ref, new_ref, h_ref, ext_ref, h_sc, *, c, nbb):
    j = pl.program_id(1)
    lo = SUBLANES - (CONV_W - 1)

    @pl.when(j == 0)
    def _():
        h_sc[...] = h0_ref[...]
        ext_ref[:, lo:SUBLANES, :] = prev_ref[...]

    dsk = dsk_ref[...]
    ng = ng_ref[...]
    w_conv = w_ref[...]
    ri = lax.broadcasted_iota(jnp.int32, (c, c), 0)
    ci = lax.broadcasted_iota(jnp.int32, (c, c), 1)
    incl = ri >= ci
    gn = SSM_GROUPS * SSM_N
    rep = SSM_HEADS // SSM_GROUPS
    gw = SSM_W // SSM_GROUPS
    grp_in = []
    ch = []
    ids = []
    for bi in range(nbb):
        y, tail = _short_conv(ext_ref, bi, xbc_ref[bi], w_conv, c)
        new_ref[bi] = tail
        y = _silu(y + cb_ref[...])
        dts = _softplus(kis_ref[bi] + dtb_ref[...])
        a = dts * (-jnp.exp(alog_ref[...]))
        acum = _cumsum_rows(a)
        acum_t = _transpose_rows(acum)
        zg = _silu(z_ref[bi])
        for grp in range(SSM_GROUPS):
            grp_in.append((y[:, SSM_W + grp * SSM_N:SSM_W + (grp + 1) * SSM_N],
                           y[:, SSM_W + gn + grp * SSM_N:SSM_W + gn + (grp + 1) * SSM_N],
                           zg[:, grp * gw:(grp + 1) * gw]))
            for hh in range(rep):
                h = grp * rep + hh
                ch.append((y[:, h * SSM_P:(h + 1) * SSM_P], acum[:, KIS_DT + h:KIS_DT + h + 1],
                           acum_t[KIS_DT + h:KIS_DT + h + 1, :], dts[:, KIS_DT + h:KIS_DT + h + 1],
                           len(grp_in) - 1))
                ids.append((bi, h))
    cb = [_mm_nt(cg, bg) for bg, cg, _ in grp_in]
    seg = [jnp.where(incl, jnp.exp(jnp.where(incl, acol - arow, 0.0)), 0.0) for (_, acol, arow, _, _) in ch]
    hs = [h_sc[bi, h] for bi, h in ids]
    y_in = [_mm(cb[x[4]] * sg, x[0] * x[3]) for x, sg in zip(ch, seg)]
    y_st = [_mm_nt(grp_in[x[4]][1], hst) for x, hst in zip(ch, hs)]
    a_last = [x[1][c - 1:c, :] for x in ch]
    upd = [_dot_tn(x[0], grp_in[x[4]][0] * (jnp.exp(al - x[1]) * x[3])) for x, al in zip(ch, a_last)]
    outs = []
    for (bi, h), x, hst, al, yi, ys, up in zip(ids, ch, hs, a_last, y_in, y_st, upd):
        h_new = hst * jnp.exp(al) + up
        h_sc[bi, h] = h_new
        h_ref[bi, h] = h_new
        outs.append(yi + ys * jnp.exp(x[1]) + dsk[:, h:h + 1] * x[0])
    for gi, (_, _, zslice) in enumerate(grp_in):
        bi, grp = divmod(gi, SSM_GROUPS)
        yg = jnp.concatenate(outs[gi * rep:(gi + 1) * rep], axis=1) * zslice
        yg = yg * lax.rsqrt(jnp.mean(yg * yg, axis=-1, keepdims=True) + EPS)
        o_ref[bi, :, grp * gw:(grp + 1) * gw] = yg * ng[:, grp * gw:(grp + 1) * gw]


def _ssd(xbc, z, kis, conv_w, conv_b, conv_prev, h0, a_log, dt_bias, d_skip, norm_g, nbb):
    B, T, _ = xbc.shape
    c = math.gcd(T, CHUNK)
    return pl.pallas_call(
        functools.partial(_ssd_kernel, c=c, nbb=nbb),
        grid=(B // nbb, T // c),
        in_specs=[pl.BlockSpec((nbb, c, SSM_CONV_C), lambda b, j: (b, j, 0)),
                  pl.BlockSpec((nbb, c, SSM_W), lambda b, j: (b, j, 0)),
                  pl.BlockSpec((nbb, c, LANES), lambda b, j: (b, j, 0)),
                  pl.BlockSpec((CONV_W, SSM_CONV_C), lambda b, j: (0, 0)),
                  pl.BlockSpec((1, SSM_CONV_C), lambda b, j: (0, 0)),
                  pl.BlockSpec((nbb, CONV_W - 1, SSM_CONV_C), lambda b, j: (b, 0, 0)),
                  pl.BlockSpec((nbb, SSM_HEADS, SSM_P, SSM_N), lambda b, j: (b, 0, 0, 0)),
                  pl.BlockSpec((1, LANES), lambda b, j: (0, 0)),
                  pl.BlockSpec((1, LANES), lambda b, j: (0, 0)),
                  pl.BlockSpec((1, SSM_HEADS), lambda b, j: (0, 0)),
                  pl.BlockSpec((1, SSM_W), lambda b, j: (0, 0))],
        out_specs=(pl.BlockSpec((nbb, c, SSM_W), lambda b, j: (b, j, 0)),
                   pl.BlockSpec((nbb, CONV_W - 1, SSM_CONV_C), lambda b, j: (b, 0, 0)),
                   pl.BlockSpec((nbb, SSM_HEADS, SSM_P, SSM_N), lambda b, j: (b, 0, 0, 0))),
        out_shape=(jax.ShapeDtypeStruct((B, T, SSM_W), F32),
                   jax.ShapeDtypeStruct((B, CONV_W - 1, SSM_CONV_C), F32),
                   jax.ShapeDtypeStruct((B, SSM_HEADS, SSM_P, SSM_N), F32)),
        scratch_shapes=[pltpu.VMEM((nbb, SUBLANES + c, SSM_CONV_C), F32),
                        pltpu.VMEM((nbb, SSM_HEADS, SSM_P, SSM_N), F32)],
        compiler_params=_params(("arbitrary", "arbitrary")),
        name="ssd",
    )(xbc, z, kis, conv_w, conv_b.reshape(1, -1), conv_prev, h0, _lane_vector(a_log, KIS_DT),
      _lane_vector(dt_bias, KIS_DT), d_skip.reshape(1, -1), norm_g.reshape(1, -1))


IDX_BITS = 12


OPENING_PROBES = 14
ROUND_PROBES = 2
SEARCH_ROUNDS = 4096
FOLD_ROWS = 128


def _fold_keys(x, op, axis=0):
    s = x.shape[0]
    if axis == 0 and s % FOLD_ROWS == 0 and s > FOLD_ROWS:
        x = op(x.reshape(s // FOLD_ROWS, FOLD_ROWS, x.shape[1]), axis=0)
    return op(x, axis=axis, keepdims=True)


def _fold_rows(x, op):
    return _fold_keys(x, op, 0)


def _select_topk(score, adm, pos, skip, topk, axis):
    def count(mask):
        return _fold_keys(mask.astype(F32), jnp.sum, axis)

    s = jnp.where(adm, score, -jnp.inf)

    few = (count(adm) <= topk) | skip
    smax = _fold_keys(s, jnp.max, axis)
    smin = _fold_keys(jnp.where(adm, score, jnp.inf), jnp.min, axis)
    top_full = count(s >= smax) >= topk
    lo0 = jnp.where(few, 0.0, jnp.where(top_full, smax, smin))
    hi0 = jnp.where(few, 0.0, smax)
    done0 = (few | top_full).astype(F32)

    def probe(lo, hi, done):
        mid = 0.5 * lo + 0.5 * hi
        ok = (mid > lo) & (mid < hi) & (done < 0.5)
        take = count(s >= mid) >= topk
        return jnp.where(ok & take, mid, lo), jnp.where(ok & jnp.logical_not(take), mid, hi)

    def opening(lo, hi):
        for _ in range(OPENING_PROBES):
            lo, hi = probe(lo, hi, done0)
        return lo, hi

    lo1, hi1 = lax.cond(jnp.min(done0) < 0.5, opening, lambda lo, hi: (lo, hi), lo0, hi0)

    def search_cond(c):
        return (jnp.min(c[2]) < 0.5) & (c[3] < SEARCH_ROUNDS)

    def search_body(c):
        lo, hi, done, it = c
        last = _fold_keys(jnp.where(s < hi, s, -jnp.inf), jnp.max, axis)
        reached = count(s >= last) >= topk
        active = done < 0.5
        lo = jnp.where(active & reached, last, lo)
        hi = jnp.where(active & jnp.logical_not(reached), last, hi)
        done = jnp.maximum(done, reached.astype(F32))
        for _ in range(ROUND_PROBES):
            lo, hi = probe(lo, hi, done)
        return lo, hi, done, it + 1

    lo, _, _, _ = lax.while_loop(search_cond, search_body, (lo1, hi1, done0, jnp.int32(0)))
    t = jnp.where(few, -jnp.inf, lo)
    gt = s > t
    tie = s == t
    need = topk - count(gt)
    excess = jnp.where(few, 0.0, count(tie) - need)

    def tie_search():
        def ibody(i, m):
            cand = m | (jnp.int32(1) << (IDX_BITS - 1 - i))
            return jnp.where(count(tie & (pos < cand)) <= need, cand, m)

        return lax.fori_loop(0, IDX_BITS, ibody, jnp.zeros(few.shape, jnp.int32))

    m = lax.cond(jnp.max(excess) > 0.0, tie_search, lambda: jnp.full(few.shape, 2 ** IDX_BITS - 1, jnp.int32))
    return adm & (gt | (tie & (pos < m)))


def _dsa_kernel(q_ref, qi_ref, kis_ref, k_ref, vt_ref, ki_ref, o_ref, *, tq, s_len, s_valid, q_offset, j0,
                n_valid_q, topk):
    j = pl.program_id(1)
    lane = lax.broadcasted_iota(jnp.int32, (1, tq), 1)
    qpos = q_offset + (j0 + j) * tq + lane
    lim = jnp.minimum((qpos // CHUNK + 1) * CHUNK, s_valid)
    row = lax.broadcasted_iota(jnp.int32, (s_len, tq), 0)
    adm = row < lim

    qi_t = qi_ref[0].T.astype(BF16)
    kis_t = kis_ref[0].T
    ki = ki_ref[0]
    assert IDX_DH == 64 and IDX_HEADS == 4 and ATT_DH == 64
    wi = kis_t[KIS_WI:KIS_WI + IDX_HEADS, :] * (IDX_DH ** -0.5 * IDX_HEADS ** -0.5)
    qi_all = jnp.concatenate([qi_t[h * IDX_DH:(h + 1) * IDX_DH, :] for h in range(IDX_HEADS)], axis=1)
    rel_all = _dot(ki, qi_all)
    score = jnp.zeros((s_len, tq), F32)
    for h in range(IDX_HEADS):
        score = score + jnp.maximum(rel_all[:, h * tq:(h + 1) * tq], 0.0) * wi[h:h + 1, :]
    sel = _select_topk(score, adm, row, lane >= n_valid_q, topk, 0)

    q_t = q_ref[0].T * (ATT_DH ** -0.5)
    pair = LANES // ATT_DH
    head_in_pair = lax.broadcasted_iota(jnp.int32, (LANES, tq), 0) // ATT_DH
    logits = []
    for g in range(ATT_HEADS // pair):
        qg = q_t[g * LANES:(g + 1) * LANES, :]
        qm = jnp.concatenate([jnp.where(head_in_pair == i, qg, 0.0) for i in range(pair)], axis=1)
        lg = _dot(k_ref[0, :, g * LANES:(g + 1) * LANES], qm.astype(BF16))
        logits += [lg[:, i * tq:(i + 1) * tq] for i in range(pair)]
    probs, scales = [], []
    for lg in logits:
        lg = jnp.where(sel, lg, -jnp.inf)
        p = jnp.exp(lg - _fold_rows(lg, jnp.max))
        scales.append(1.0 / _fold_rows(p, jnp.sum))
        probs.append(p.astype(BF16))
    outs = [_dot(vt_ref[0, h * ATT_DH:(h + 1) * ATT_DH, :], probs[h]) * scales[h] for h in range(ATT_HEADS)]
    o_ref[0] = jnp.concatenate(outs, axis=0).T


def _dsa(q, qi, kis, k_b, vt_b, ki_b, *, tq, j0, nj, s_len, s_valid, q_offset, n_valid_q, topk):
    B = q.shape[0]
    assert s_len < 2 ** IDX_BITS and s_len <= k_b.shape[1]
    return pl.pallas_call(
        functools.partial(_dsa_kernel, tq=tq, s_len=s_len, s_valid=s_valid, q_offset=q_offset, j0=j0,
                          n_valid_q=n_valid_q, topk=topk),
        grid=(B, nj),
        in_specs=[pl.BlockSpec((1, tq, ATT_W), lambda b, j: (b, j0 + j, 0)),
                  pl.BlockSpec((1, tq, IDX_HEADS * IDX_DH), lambda b, j: (b, j0 + j, 0)),
                  pl.BlockSpec((1, tq, LANES), lambda b, j: (b, j0 + j, 0)),
                  pl.BlockSpec((1, s_len, ATT_W), lambda b, j: (b, 0, 0)),
                  pl.BlockSpec((1, ATT_W, s_len), lambda b, j: (b, 0, 0)),
                  pl.BlockSpec((1, s_len, IDX_DH), lambda b, j: (b, 0, 0))],
        out_specs=pl.BlockSpec((1, tq, ATT_W), lambda b, j: (b, j, 0)),
        out_shape=jax.ShapeDtypeStruct((B, nj * tq, ATT_W), F32),
        compiler_params=_params(("arbitrary", "arbitrary")),
        name="dsa",
    )(q, qi, kis, k_b, vt_b, ki_b)


def _dsa_rows_kernel(q_ref, qi_ref, kis_ref, k_ref, v_ref, ki_ref, o_ref, *, nbq, tq, s_len, s_valid, q_offset,
                     topk):
    rows = nbq * tq
    qpos = q_offset + lax.broadcasted_iota(jnp.int32, (rows, 1), 0) % tq
    lim = jnp.minimum((qpos // CHUNK + 1) * CHUNK, s_valid)
    col = lax.broadcasted_iota(jnp.int32, (rows, s_len), 1)
    adm = col < lim

    scores = []
    for b in range(nbq):
        qi = qi_ref[b]
        qi_st = jnp.concatenate([qi[:, h * IDX_DH:(h + 1) * IDX_DH] for h in range(IDX_HEADS)], axis=0)
        rel = _dot_nt(qi_st.astype(BF16), ki_ref[b])
        wi = kis_ref[b][:, KIS_WI:KIS_WI + IDX_HEADS] * (IDX_DH ** -0.5 * IDX_HEADS ** -0.5)
        sc = jnp.maximum(rel[:tq, :], 0.0) * wi[:, 0:1]
        for h in range(1, IDX_HEADS):
            sc = sc + jnp.maximum(rel[h * tq:(h + 1) * tq, :], 0.0) * wi[:, h:h + 1]
        scores.append(sc)
    sel = _select_topk(jnp.concatenate(scores, axis=0), adm, col, qpos < 0, topk, 1)

    head_of_lane = lax.broadcasted_iota(jnp.int32, (tq, ATT_W), 1) // ATT_DH
    for b in range(nbq):
        q = q_ref[b] * (ATT_DH ** -0.5)
        q_st = jnp.concatenate([jnp.where(head_of_lane == h, q, 0.0) for h in range(ATT_HEADS)], axis=0)
        logits = _dot_nt(q_st.astype(BF16), k_ref[b])
        logits = jnp.where(jnp.concatenate([sel[b * tq:(b + 1) * tq, :]] * ATT_HEADS, axis=0), logits, -jnp.inf)
        p = jnp.exp(logits - jnp.max(logits, axis=1, keepdims=True))
        scale = 1.0 / jnp.sum(p, axis=1, keepdims=True)
        pv = _dot(p.astype(BF16), v_ref[b]) * scale
        out = jnp.zeros((tq, ATT_W), F32)
        for h in range(ATT_HEADS):
            out = jnp.where(head_of_lane == h, pv[h * tq:(h + 1) * tq, :], out)
        o_ref[b] = out


def _dsa_rows(q, qi, kis, k_b, v_b, ki_b, *, nbq, s_valid, q_offset, topk):
    B, T, _ = q.shape
    s_len = k_b.shape[1]
    assert s_len < 2 ** IDX_BITS and T % SUBLANES == 0 and B % nbq == 0
    return pl.pallas_call(
        functools.partial(_dsa_rows_kernel, nbq=nbq, tq=T, s_len=s_len, s_valid=s_valid, q_offset=q_offset,
                          topk=topk),
        grid=(B // nbq,),
        in_specs=[pl.BlockSpec((nbq, T, ATT_W), lambda b: (b, 0, 0)),
                  pl.BlockSpec((nbq, T, IDX_HEADS * IDX_DH), lambda b: (b, 0, 0)),
                  pl.BlockSpec((nbq, T, LANES), lambda b: (b, 0, 0)),
                  pl.BlockSpec((nbq, s_len, ATT_W), lambda b: (b, 0, 0)),
                  pl.BlockSpec((nbq, s_len, ATT_W), lambda b: (b, 0, 0)),
                  pl.BlockSpec((nbq, s_len, IDX_DH), lambda b: (b, 0, 0))],
        out_specs=pl.BlockSpec((nbq, T, ATT_W), lambda b: (b, 0, 0)),
        out_shape=jax.ShapeDtypeStruct((B, T, ATT_W), F32),
        compiler_params=_params(("arbitrary",)),
        name="dsa_rows",
    )(q, qi, kis, k_b, v_b, ki_b)


def _dsa_group(q, qi, kis, k_b, vt_b, ki_b, *, s_valid, q_offset, topk):
    B, T, _ = q.shape
    tq = 2 * LANES if T % (2 * LANES) == 0 else LANES
    n_valid_q = min(T, tq)
    if T < tq:
        padq = lambda a: jnp.pad(a, ((0, 0), (0, tq - T), (0, 0)))
        q, qi, kis = padq(q), padq(qi), padq(kis)
    nq = q.shape[1] // tq
    s_total = k_b.shape[1]
    outs = []
    j0 = 0
    while j0 < nq:
        nj = 1
        reach = -(-(q_offset + (j0 + nj) * tq) // CHUNK) * CHUNK
        s_len = min(s_total, -(-min(reach, s_valid) // LANES) * LANES)
        outs.append(_dsa(q, qi, kis, k_b, vt_b, ki_b, tq=tq, j0=j0, nj=nj, s_len=s_len, s_valid=s_valid,
                         q_offset=q_offset, n_valid_q=n_valid_q, topk=topk))
        j0 += nj
    out = outs[0] if len(outs) == 1 else jnp.concatenate(outs, axis=1)
    return out[:, :T]


def _ffn_kernel(x_ref, gdn_ref, att_ref, ssm_ref, mod_ref, wo_ref, g2_ref, wg_ref, wu_ref, cw_ref, cb_ref,
                wd_ref, prev_ref, fg_ref, *rest, nb, tt, nk, final):
    if final:
        xo_ref, y_ref, new_ref, h2_sc, acc_sc, buf_sc, carry_sc = rest
    else:
        xo_ref, new_ref, h2_sc, acc_sc, buf_sc, carry_sc = rest
        y_ref = None
    t = pl.program_id(1)
    k = pl.program_id(2)
    rows = nb * tt
    lo = SUBLANES - (FFN_CONV_W - 1)
    mod = mod_ref[...]

    @pl.when(k == 0)
    def _():
        mix = jnp.concatenate([gdn_ref[...], att_ref[...], ssm_ref[...]], axis=-1)
        proj = _dot(mix.reshape(rows, MIX_W).astype(BF16), wo_ref[...]).reshape(nb, tt, D_MODEL)
        x1 = x_ref[...] + mod[:, 2:3, :] * proj
        acc_sc[...] = x1
        ms = jnp.mean(x1 * x1, axis=-1, keepdims=True)
        h2 = x1 * lax.rsqrt(ms + EPS) * g2_ref[...]
        h2 = h2 * (1.0 + mod[:, 4:5, :]) + mod[:, 3:4, :]
        h2_sc[...] = h2.reshape(rows, D_MODEL).astype(BF16)

    @pl.when(t == 0)
    def _():
        carry_sc[k, :, lo:SUBLANES, :] = prev_ref[...]

    h2 = h2_sc[...]
    ag = _dot(h2, wg_ref[...])
    up = _dot(h2, wu_ref[...])
    tf = ag.shape[-1]
    buf_sc[:, lo:SUBLANES, :] = carry_sc[k, :, lo:SUBLANES, :]
    buf_sc[:, SUBLANES:SUBLANES + tt, :] = ag.reshape(nb, tt, tf)
    cw = cw_ref[...]
    conv = buf_sc[:, lo:lo + tt, :] * cw[0:1, :][None]
    for jj in range(1, FFN_CONV_W):
        conv = conv + buf_sc[:, lo + jj:lo + jj + tt, :] * cw[jj:jj + 1, :][None]
    tail = buf_sc[:, tt + lo:tt + SUBLANES, :]
    carry_sc[k, :, lo:SUBLANES, :] = tail
    new_ref[:, pl.ds(k, 1), :, :] = tail[:, None]
    act = _silu(conv + cb_ref[...][None]).reshape(rows, tf) * up
    y = _dot(act.astype(BF16), wd_ref[...]).reshape(nb, tt, D_MODEL)
    acc_sc[...] += mod[:, 5:6, :] * y

    @pl.when(k == nk - 1)
    def _():
        xo = acc_sc[...]
        xo_ref[...] = xo
        if final:
            ms = jnp.mean(xo * xo, axis=-1, keepdims=True)
            y_ref[...] = xo * lax.rsqrt(ms + EPS) * fg_ref[...]


def _ffn(x, gdn_o, att_o, ssm_o, mod, w_out, g2, w_gate, w_up, conv_w, conv_b, w_down, conv_prev, final_g,
         nb, tt, tf, final):
    B, T, _ = x.shape
    nk = D_FF // tf
    grid = (B // nb, T // tt, nk)
    row_spec = lambda w: pl.BlockSpec((nb, tt, w), lambda b, t, k: (b, t, 0))
    in_specs = [row_spec(D_MODEL), row_spec(GDN_W), row_spec(ATT_W), row_spec(SSM_W),
                pl.BlockSpec((nb, 6, D_MODEL), lambda b, t, k: (b, 0, 0)),
                pl.BlockSpec((MIX_W, D_MODEL), lambda b, t, k: (0, 0)),
                pl.BlockSpec((1, D_MODEL), lambda b, t, k: (0, 0)),
                pl.BlockSpec((D_MODEL, tf), lambda b, t, k: (0, k)),
                pl.BlockSpec((D_MODEL, tf), lambda b, t, k: (0, k)),
                pl.BlockSpec((FFN_CONV_W, tf), lambda b, t, k: (0, k)),
                pl.BlockSpec((1, tf), lambda b, t, k: (0, k)),
                pl.BlockSpec((tf, D_MODEL), lambda b, t, k: (k, 0)),
                pl.BlockSpec((nb, FFN_CONV_W - 1, tf), lambda b, t, k: (b, 0, k)),
                pl.BlockSpec((1, D_MODEL), lambda b, t, k: (0, 0))]
    out_shape = [jax.ShapeDtypeStruct((B, T, D_MODEL), F32)]
    out_specs = [row_spec(D_MODEL)]
    if final:
        out_shape.append(jax.ShapeDtypeStruct((B, T, D_MODEL), F32))
        out_specs.append(row_spec(D_MODEL))
    out_shape.append(jax.ShapeDtypeStruct((B, nk, FFN_CONV_W - 1, tf), F32))
    out_specs.append(pl.BlockSpec((nb, nk, FFN_CONV_W - 1, tf), lambda b, t, k: (b, 0, 0, 0)))
    res = pl.pallas_call(
        functools.partial(_ffn_kernel, nb=nb, tt=tt, nk=nk, final=final),
        grid=grid,
        in_specs=in_specs,
        out_specs=tuple(out_specs),
        out_shape=tuple(out_shape),
        scratch_shapes=[pltpu.VMEM((nb * tt, D_MODEL), BF16),
                        pltpu.VMEM((nb, tt, D_MODEL), F32),
                        pltpu.VMEM((nb, SUBLANES + tt, tf), F32),
                        pltpu.VMEM((nk, nb, SUBLANES, tf), F32)],
        compiler_params=_params(("arbitrary", "arbitrary", "arbitrary")),
        name="ffn",
    )(x, gdn_o, att_o, ssm_o, mod, w_out, g2, w_gate, w_up, conv_w, conv_b.reshape(1, -1), w_down,
      conv_prev, final_g)
    tail = res[-1].transpose(0, 2, 1, 3).reshape(B, FFN_CONV_W - 1, D_FF)
    return (*res[:-1], tail)


def _rope_tables(pos):
    half = ATT_DH // 2
    inv_freq = ROPE_THETA ** (-jnp.arange(half, dtype=F32) / half)
    ang = pos.astype(F32)[:, None] * inv_freq[None, :]
    cos = jnp.cos(ang)
    sin = jnp.sin(ang)
    cos_h = jnp.concatenate([cos, cos], axis=-1)
    sin_h = jnp.concatenate([-sin, sin], axis=-1)
    ones = jnp.ones((pos.shape[0], LANES - ATT_DH), F32)
    cos_t = jnp.concatenate([jnp.tile(cos_h, (1, ATT_HEADS)), cos_h, ones], axis=-1)
    sin_t = jnp.concatenate([jnp.tile(sin_h, (1, ATT_HEADS)), sin_h, 0.0 * ones], axis=-1)
    return cos_t, sin_t


def _layer(x, mod, pos_tables, cache, states, lw, tiles, final, final_g):
    (norm1_g, w_perm, gdn_conv_w, gdn_A_log, gdn_dt_bias, gdn_norm_g, ssm_conv_w, ssm_conv_b, ssm_A_log,
     ssm_dt_bias, ssm_D, ssm_norm_g, w_out, norm2_g, w_gate, w_up, ffn_conv_w, ffn_conv_b, w_down) = lw
    gdn_conv_prev, gdn_s0, ssm_conv_prev, ssm_h0, ffn_conv_prev = states
    B, T, _ = x.shape
    nb, tt, nbb, ffn_tt, tf = tiles
    cos_t, sin_t = pos_tables
    proj = _inproj(x, mod, norm1_g.reshape(1, -1), w_perm, cos_t, sin_t, nb, tt, emit_keys=cache is None)
    gqkv, ggate, q, k, v, qi, kis, z, xbc = proj[:len(OUT_SEGS)]
    ki = kis[..., :IDX_DH]

    gdn_o, gdn_conv_new, gdn_s = _gdn(gqkv, kis, ggate, gdn_conv_w, gdn_conv_prev, gdn_s0, gdn_A_log,
                                      gdn_dt_bias, gdn_norm_g, nbb)
    ssm_o, ssm_conv_new, ssm_h = _ssd(xbc, z, kis, ssm_conv_w, ssm_conv_b, ssm_conv_prev, ssm_h0, ssm_A_log,
                                      ssm_dt_bias, ssm_D, ssm_norm_g, nbb)
    if cache is None:
        k_b, ki_b, vt_b = proj[len(OUT_SEGS):]
        att_o = _dsa_group(q, qi, kis, k_b, vt_b, ki_b, s_valid=T, q_offset=0, topk=min(TOPK_MAX, T // 4))
    else:
        ck, cv, cki = cache
        P = ck.shape[1]
        L = P + T
        pad = -(-L // LANES) * LANES - L
        k_b = jnp.concatenate([ck.reshape(B, P, ATT_W), k, jnp.zeros((B, pad, ATT_W), F32)], axis=1).astype(BF16)
        v_b = jnp.concatenate([cv.reshape(B, P, ATT_W), v, jnp.zeros((B, pad, ATT_W), F32)], axis=1).astype(BF16)
        ki_b = jnp.concatenate([cki, ki, jnp.zeros((B, pad, IDX_DH), F32)], axis=1).astype(BF16)
        att_o = _dsa_rows(q, qi, kis, k_b, v_b, ki_b, nbq=nbb, s_valid=L, q_offset=P, topk=min(TOPK_MAX, L // 4))

    res = _ffn(x, gdn_o, att_o, ssm_o, mod, w_out, norm2_g.reshape(1, -1), w_gate, w_up, ffn_conv_w, ffn_conv_b,
               w_down, ffn_conv_prev, final_g.reshape(1, -1), ffn_tt[0], ffn_tt[1], tf, final)
    if final:
        x_new, y, ffn_conv_new = res
    else:
        (x_new, ffn_conv_new), y = res, None
    st = (k.reshape(B, T, ATT_HEADS, ATT_DH), v.reshape(B, T, ATT_HEADS, ATT_DH), ki,
          gdn_conv_new, gdn_s, ssm_conv_new, ssm_h, ffn_conv_new)
    return x_new, y, st


def kernel(x_prompt, x_sample, c_prompt, c_sample, cache_k, cache_v, cache_kidx, state_gdn_conv, state_gdn,
           state_ssm_conv, state_ssm, state_ffn_conv, w_ada, b_ada, norm1_g, w_in, gdn_conv_w, gdn_A_log,
           gdn_dt_bias, gdn_norm_g, ssm_conv_w, ssm_conv_b, ssm_A_log, ssm_dt_bias, ssm_D, ssm_norm_g, w_out,
           norm2_g, w_gate, w_up, ffn_conv_w, ffn_conv_b, w_down, final_g):
    Bp, T, _ = x_prompt.shape
    Bs, Ts, _ = x_sample.shape
    P = cache_k.shape[2]

    c_all = jnp.concatenate([c_prompt, c_sample], axis=0)
    mod_all = _ada(c_all, w_ada, b_ada).reshape(DEPTH, Bp + Bs, 6, D_MODEL)

    tables_p = _rope_tables(jnp.arange(T))
    tables_s = _rope_tables(P + jnp.arange(Ts))

    w_perm = _permute_columns(w_in.astype(BF16))
    w_out_b = w_out.astype(BF16)
    w_gate_b = w_gate.astype(BF16)
    w_up_b = w_up.astype(BF16)
    w_down_b = w_down.astype(BF16)

    zeros_p = (jnp.zeros((Bp, CONV_W - 1, GDN_CONV_C), F32),
               jnp.zeros((Bp, GDN_HEADS, GDN_DK, GDN_DV), F32),
               jnp.zeros((Bp, CONV_W - 1, SSM_CONV_C), F32),
               jnp.zeros((Bp, SSM_HEADS, SSM_P, SSM_N), F32),
               jnp.zeros((Bp, FFN_CONV_W - 1, D_FF), F32))

    tf = D_FF // 2
    tiles_p = (1, min(256, T), math.gcd(Bp, 8), (1, min(512, T)), tf)
    tiles_s = (Bs, Ts, math.gcd(Bs, 4), (Bs, Ts), tf)

    xp, xs = x_prompt, x_sample
    new_p, new_s = [], []
    yp = ys = None
    for l in range(DEPTH):
        lw = (norm1_g[l], w_perm[l], gdn_conv_w[l], gdn_A_log[l], gdn_dt_bias[l], gdn_norm_g[l], ssm_conv_w[l],
              ssm_conv_b[l], ssm_A_log[l], ssm_dt_bias[l], ssm_D[l], ssm_norm_g[l], w_out_b[l], norm2_g[l],
              w_gate_b[l], w_up_b[l], ffn_conv_w[l], ffn_conv_b[l], w_down_b[l])
        final = l == DEPTH - 1
        xp, yp, st_p = _layer(xp, mod_all[l, :Bp], tables_p, None, zeros_p, lw, tiles_p, final, final_g)
        states_s = (state_gdn_conv[l], state_gdn[l], state_ssm_conv[l], state_ssm[l], state_ffn_conv[l])
        xs, ys, st_s = _layer(xs, mod_all[l, Bp:], tables_s, (cache_k[l], cache_v[l], cache_kidx[l]), states_s,
                              lw, tiles_s, final, final_g)
        new_p.append(st_p)
        new_s.append(st_s)
    outs_p = [jnp.stack([st[i] for st in new_p]) for i in range(8)]
    outs_s = [jnp.stack([st[i] for st in new_s]) for i in range(8)]
    return (yp, ys, *outs_p, *outs_s)
```

```python
import functools
import math

import jax
import jax.numpy as jnp
import numpy as np
from jax import lax
from jax.experimental import pallas as pl
from jax.experimental.pallas import tpu as pltpu

F32 = jnp.float32
BF16 = jnp.bfloat16
HI = lax.Precision.HIGHEST

D_MODEL = 1024
DEPTH = 2
CHUNK = 64
CONV_W = 4
FFN_CONV_W = 3
D_FF = 2816
ROPE_THETA = 10000.0
EPS = 1e-6
GDN_HEADS = 4
GDN_DK = 64
GDN_DV = 64
ATT_HEADS = 4
ATT_DH = 64
IDX_HEADS = 4
IDX_DH = 64
TOPK_MAX = 256
SSM_HEADS = 8
SSM_P = 64
SSM_GROUPS = 2
SSM_N = 128
GDN_W = GDN_HEADS * GDN_DV
ATT_W = ATT_HEADS * ATT_DH
SSM_W = SSM_HEADS * SSM_P
MIX_W = GDN_W + ATT_W + SSM_W
GDN_CONV_C = 2 * GDN_HEADS * GDN_DK + GDN_W
SSM_CONV_C = SSM_W + 2 * SSM_GROUPS * SSM_N
IN_SIZES = (GDN_CONV_C, GDN_HEADS, GDN_HEADS, GDN_W,
            ATT_W, ATT_W, ATT_W, IDX_HEADS * IDX_DH, IDX_DH, IDX_HEADS,
            SSM_W, SSM_CONV_C, SSM_HEADS)
IN_W = sum(IN_SIZES)

LANES = 128
SUBLANES = 8
VMEM_LIMIT = 56 * 1024 * 1024

KIS_A = IDX_DH
KIS_B = KIS_A + GDN_HEADS
KIS_WI = KIS_B + GDN_HEADS
KIS_DT = KIS_WI + IDX_HEADS
KIS_END = KIS_DT + SSM_HEADS
OUT_SEGS = (("gqkv", GDN_CONV_C), ("ggate", GDN_W), ("q", ATT_W), ("k", ATT_W), ("v", ATT_W),
            ("qi", IDX_HEADS * IDX_DH), ("kis", LANES), ("z", SSM_W), ("xbc", SSM_CONV_C))
PERM_W = sum(w for _, w in OUT_SEGS)
ROT_W = ATT_W + LANES


def _perm_columns():
    starts = np.concatenate([[0], np.cumsum(IN_SIZES)])
    (s_gqkv, s_ga, s_gb, s_gg, s_q, s_k, s_v, s_qi, s_ki, s_wi, s_z, s_xbc, s_dt) = starts[:-1]
    cols = []
    cols += list(range(s_gqkv, s_gqkv + GDN_CONV_C))
    cols += list(range(s_gg, s_gg + GDN_W))
    cols += list(range(s_q, s_q + ATT_W))
    cols += list(range(s_k, s_k + ATT_W))
    cols += list(range(s_v, s_v + ATT_W))
    cols += list(range(s_qi, s_qi + IDX_HEADS * IDX_DH))
    kis = (list(range(s_ki, s_ki + IDX_DH)) + list(range(s_ga, s_ga + GDN_HEADS))
           + list(range(s_gb, s_gb + GDN_HEADS)) + list(range(s_wi, s_wi + IDX_HEADS))
           + list(range(s_dt, s_dt + SSM_HEADS)))
    cols += kis + [-1] * (LANES - len(kis))
    cols += list(range(s_z, s_z + SSM_W))
    cols += list(range(s_xbc, s_xbc + SSM_CONV_C))
    assert len(cols) == PERM_W
    return np.asarray(cols, np.int32)


_PERM_COLS = _perm_columns()


def _permute_columns(w_in):
    pieces = []
    start = 0
    for i in range(1, PERM_W + 1):
        prev = int(_PERM_COLS[i - 1])
        if i == PERM_W or not ((prev < 0 and _PERM_COLS[i] < 0) or (prev >= 0 and _PERM_COLS[i] == prev + 1)):
            if _PERM_COLS[start] < 0:
                pieces.append(jnp.zeros(w_in.shape[:-1] + (i - start,), w_in.dtype))
            else:
                pieces.append(w_in[..., int(_PERM_COLS[start]):int(_PERM_COLS[start]) + i - start])
            start = i
    return jnp.concatenate(pieces, axis=-1)


def _silu(x):
    return x * jax.nn.sigmoid(x)


def _softplus(x):
    return jnp.maximum(x, 0.0) + jnp.log1p(jnp.exp(-jnp.abs(x)))


def _dot(a, b, precision=None):
    return jnp.dot(a, b, preferred_element_type=F32, precision=precision)


def _dot_nt(a, b, precision=None):
    return lax.dot_general(a, b, (((1,), (1,)), ((), ())), preferred_element_type=F32,
                           precision=precision)


def _dot_tn(a, b, precision=None):
    return lax.dot_general(a, b, (((0,), (0,)), ((), ())), preferred_element_type=F32,
                           precision=precision)


def _cumsum_rows(x):
    c = x.shape[0]
    row = lax.broadcasted_iota(jnp.int32, x.shape, 0)
    s = 1
    while s < c:
        x = x + jnp.where(row >= s, pltpu.roll(x, s, axis=0), 0.0)
        s *= 2
    return x


def _mm(a, b):
    return _dot(a.astype(BF16), b.astype(BF16))


def _mm_nt(a, b):
    return _dot_nt(a.astype(BF16), b.astype(BF16))


def _split(a):
    hi = a.astype(BF16)
    return hi, (a - hi.astype(F32)).astype(BF16)


def _mm3(a, b):
    (ah, al), (bh, bl) = a, b
    return _dot(ah, bh) + (_dot(ah, bl) + _dot(al, bh))


def _lane_spread(vals, lane0, n_heads):
    width = n_heads * GDN_DV
    r = lax.broadcasted_iota(jnp.int32, (LANES, width), 0)
    cidx = lax.broadcasted_iota(jnp.int32, (LANES, width), 1)
    pick = (r == lane0 + cidx // GDN_DV).astype(BF16)
    p1 = vals.astype(BF16)
    r1 = vals - p1.astype(F32)
    p2 = r1.astype(BF16)
    p3 = (r1 - p2.astype(F32)).astype(BF16)
    return _dot(p1, pick) + (_dot(p2, pick) + _dot(p3, pick))


def _head_sums(x):
    r = lax.broadcasted_iota(jnp.int32, (LANES, LANES), 0) // GDN_DV
    cidx = lax.broadcasted_iota(jnp.int32, (LANES, LANES), 1) // GDN_DV
    ones = (r == cidx).astype(BF16)
    hi, lo = _split(x)
    cols = []
    for g in range(x.shape[1] // LANES):
        sl = slice(g * LANES, (g + 1) * LANES)
        cols.append(_dot(hi[:, sl], ones) + _dot(lo[:, sl], ones))
    return jnp.concatenate(cols, axis=1)


def _transpose_rows(vals):
    c = vals.shape[0]
    if c < LANES:
        vals = jnp.concatenate([vals, jnp.zeros((LANES - c, LANES), vals.dtype)], axis=0)
    return vals.T[:, :c]


def _lane_vector(vals, start):
    return jnp.zeros((1, LANES), F32).at[0, start:start + vals.shape[0]].set(vals.astype(F32))


def _params(sem):
    return pltpu.CompilerParams(dimension_semantics=sem, vmem_limit_bytes=VMEM_LIMIT)


def _ada_kernel(c_ref, w_ref, b_ref, o_ref):
    s = _silu(c_ref[...])
    o_ref[0] = _dot(s.astype(BF16), w_ref[0].astype(BF16)) + b_ref[0]


def _ada(c_all, w_ada, b_ada):
    rows = c_all.shape[0]
    n = w_ada.shape[2]
    tn = 1536
    return pl.pallas_call(
        _ada_kernel,
        grid=(DEPTH, n // tn),
        in_specs=[pl.BlockSpec((rows, D_MODEL), lambda l, j: (0, 0)),
                  pl.BlockSpec((1, D_MODEL, tn), lambda l, j: (l, 0, j)),
                  pl.BlockSpec((1, 1, tn), lambda l, j: (l, 0, j))],
        out_specs=pl.BlockSpec((1, rows, tn), lambda l, j: (l, 0, j)),
        out_shape=jax.ShapeDtypeStruct((DEPTH, rows, n), F32),
        compiler_params=_params(("arbitrary", "arbitrary")),
        name="ada",
    )(c_all, w_ada, b_ada.reshape(DEPTH, 1, n))


def _rotate(x, cos, sin_signed):
    w = x.shape[-1]
    lane = lax.broadcasted_iota(jnp.int32, x.shape, x.ndim - 1)
    first = (lane % ATT_DH) < (ATT_DH // 2)
    swapped = jnp.where(first, pltpu.roll(x, w - ATT_DH // 2, axis=x.ndim - 1),
                        pltpu.roll(x, ATT_DH // 2, axis=x.ndim - 1))
    return x * cos + swapped * sin_signed


def _inproj_kernel(x_ref, mod_ref, g_ref, w_ref, cos_ref, sin_ref, *out_refs, nb, tt, emit_keys):
    x = x_ref[...]
    ms = jnp.mean(x * x, axis=-1, keepdims=True)
    xn = x * lax.rsqrt(ms + EPS) * g_ref[...]
    mod = mod_ref[...]
    h = xn * (1.0 + mod[:, 1:2, :]) + mod[:, 0:1, :]
    u = _dot(h.reshape(nb * tt, D_MODEL).astype(BF16), w_ref[...])
    cos = cos_ref[...]
    sin = sin_ref[...]
    off = 0
    segs = {}
    for (name, width), o_ref in zip(OUT_SEGS, out_refs):
        seg = u[:, off:off + width]
        if name in ("q", "k", "qi", "kis"):
            t0 = ATT_W if name == "kis" else 0
            c3 = cos[:, t0:t0 + width][None]
            s3 = sin[:, t0:t0 + width][None]
            if nb > 1:
                c3 = jnp.broadcast_to(c3, (nb, tt, width)).reshape(nb * tt, width)
                s3 = jnp.broadcast_to(s3, (nb, tt, width)).reshape(nb * tt, width)
            else:
                c3 = c3[0]
                s3 = s3[0]
            seg = _rotate(seg, c3, s3)
        o_ref[...] = seg.reshape(nb, tt, width)
        segs[name] = seg
        off += width
    if emit_keys:
        k_b_ref, ki_b_ref, vt_b_ref = out_refs[len(OUT_SEGS):]
        k_b_ref[0] = segs["k"].astype(BF16)
        ki_b_ref[0] = segs["kis"][:, :IDX_DH].astype(BF16)
        vt_b_ref[0] = segs["v"].T.astype(BF16)


def _inproj(x, mod, g1, w_perm, cos, sin, nb, tt, emit_keys):
    B, T, _ = x.shape
    grid = (B // nb, T // tt)
    out_shape = tuple(jax.ShapeDtypeStruct((B, T, w), F32) for _, w in OUT_SEGS)
    out_specs = tuple(pl.BlockSpec((nb, tt, w), lambda b, t: (b, t, 0)) for _, w in OUT_SEGS)
    if emit_keys:
        assert nb == 1
        out_shape += (jax.ShapeDtypeStruct((B, T, ATT_W), BF16), jax.ShapeDtypeStruct((B, T, IDX_DH), BF16),
                      jax.ShapeDtypeStruct((B, ATT_W, T), BF16))
        out_specs += (pl.BlockSpec((1, tt, ATT_W), lambda b, t: (b, t, 0)),
                      pl.BlockSpec((1, tt, IDX_DH), lambda b, t: (b, t, 0)),
                      pl.BlockSpec((1, ATT_W, tt), lambda b, t: (b, 0, t)))
    return pl.pallas_call(
        functools.partial(_inproj_kernel, nb=nb, tt=tt, emit_keys=emit_keys),
        grid=grid,
        in_specs=[pl.BlockSpec((nb, tt, D_MODEL), lambda b, t: (b, t, 0)),
                  pl.BlockSpec((nb, 6, D_MODEL), lambda b, t: (b, 0, 0)),
                  pl.BlockSpec((1, D_MODEL), lambda b, t: (0, 0)),
                  pl.BlockSpec((D_MODEL, PERM_W), lambda b, t: (0, 0)),
                  pl.BlockSpec((tt, ROT_W), lambda b, t: (t, 0)),
                  pl.BlockSpec((tt, ROT_W), lambda b, t: (t, 0))],
        out_specs=out_specs,
        out_shape=out_shape,
        compiler_params=_params(("arbitrary", "arbitrary")),
        name="inproj",
    )(x, mod, g1, w_perm, cos, sin)


def _short_conv(ext_ref, bi, u, w, c):
    lo = SUBLANES - (CONV_W - 1)
    ext_ref[bi, SUBLANES:SUBLANES + c, :] = u
    y = ext_ref[bi, lo:lo + c, :] * w[0:1, :]
    for j in range(1, CONV_W):
        y = y + ext_ref[bi, lo + j:lo + j + c, :] * w[j:j + 1, :]
    tail = ext_ref[bi, c + lo:c + SUBLANES, :]
    ext_ref[bi, lo:SUBLANES, :] = tail
    return y, tail


def _neumann_inverses(mats, c):
    eye = (lax.broadcasted_iota(jnp.int32, (c, c), 0)
           == lax.broadcasted_iota(jnp.int32, (c, c), 1)).astype(F32)
    ps = [eye - a for a in mats]
    sp = [_split(a) for a in mats]
    pws = [_mm3(s, s) for s in sp]
    n = 2
    while n < c:
        sp = [_split(pw) for pw in pws]
        ps = [p + _mm3(_split(p), s) for p, s in zip(ps, sp)]
        n *= 2
        if n < c:
            pws = [_mm3(s, s) for s in sp]
    return ps


def _gdn_kernel(qkv_ref, kis_ref, gate_ref, w_ref, prev_ref, s0_ref, alog_ref, dtb_ref, ng_ref,
                o_ref, new_ref, s_ref, ext_ref, s_sc, *, c, nbb):
    j = pl.program_id(1)
    lo = SUBLANES - (CONV_W - 1)

    @pl.when(j == 0)
    def _():
        s_sc[...] = s0_ref[...]
        ext_ref[:, lo:SUBLANES, :] = prev_ref[...]

    ng = ng_ref[...]
    w_conv = w_ref[...]
    ri = lax.broadcasted_iota(jnp.int32, (c, c), 0)
    ci = lax.broadcasted_iota(jnp.int32, (c, c), 1)
    incl = ri >= ci
    strict = ri > ci
    nkd = GDN_HEADS * GDN_DK
    ch = []
    for bi in range(nbb):
        y, tail = _short_conv(ext_ref, bi, qkv_ref[bi], w_conv, c)
        new_ref[bi] = tail
        y = _silu(y)
        small = kis_ref[bi]
        g = -jnp.exp(alog_ref[...]) * _softplus(small + dtb_ref[...])
        beta = jax.nn.sigmoid(small)
        gc = _cumsum_rows(g)
        gc_t = _transpose_rows(gc)
        qk_n = y[:, :2 * nkd]
        qk_n = qk_n * lax.rsqrt(_head_sums(qk_n * qk_n) + EPS)
        for h in range(GDN_HEADS):
            qh = qk_n[:, h * GDN_DK:(h + 1) * GDN_DK] * (GDN_DK ** -0.5)
            kh = qk_n[:, nkd + h * GDN_DK:nkd + (h + 1) * GDN_DK]
            vh = y[:, 2 * nkd + h * GDN_DV:2 * nkd + (h + 1) * GDN_DV]
            ch.append((qh, kh, vh, gc[:, KIS_A + h:KIS_A + h + 1], gc_t[KIS_A + h:KIS_A + h + 1, :],
                       beta[:, KIS_B + h:KIS_B + h + 1]))
    ids = [(bi, h) for bi in range(nbb) for h in range(GDN_HEADS)]
    decay = [jnp.where(incl, jnp.exp(jnp.where(incl, gcol - grow, 0.0)), 0.0)
             for (_, _, _, gcol, grow, _) in ch]
    kb = [kh * bcol for (_, kh, _, _, _, bcol) in ch]
    eg = [jnp.exp(gcol) for (_, _, _, gcol, _, _) in ch]
    grams = [_mm_nt(jnp.concatenate([kbi, x[0]], axis=0), x[1]) for kbi, x in zip(kb, ch)]
    t_inv = _neumann_inverses([jnp.where(strict, m[:c] * d, 0.0) for m, d in zip(grams, decay)], c)
    attn = [jnp.where(incl, m[c:] * d, 0.0) for m, d in zip(grams, decay)]
    rhs = [jnp.concatenate([x[2] * x[5], kbi * e], axis=1) for x, kbi, e in zip(ch, kb, eg)]
    sol = [_mm(t, r) for t, r in zip(t_inv, rhs)]
    st = [s_sc[bi, h] for bi, h in ids]
    ws = [_mm(sl[:, GDN_DV:], s) for sl, s in zip(sol, st)]
    qs = [_mm(x[0] * e, s) for x, e, s in zip(ch, eg, st)]
    v_new = [sl[:, :GDN_DV] - w for sl, w in zip(sol, ws)]
    av = [_mm(a, v) for a, v in zip(attn, v_new)]
    g_last = [x[3][c - 1:c, :] for x in ch]
    kv = [_dot_tn(x[1] * jnp.exp(gl - x[3]), v) for x, gl, v in zip(ch, g_last, v_new)]
    for (bi, h), s, gl, kvi in zip(ids, st, g_last, kv):
        s_new = s * jnp.exp(gl) + kvi
        s_sc[bi, h] = s_new
        s_ref[bi, h] = s_new
    for bi in range(nbb):
        o = jnp.concatenate([qs[bi * GDN_HEADS + h] + av[bi * GDN_HEADS + h] for h in range(GDN_HEADS)], axis=1)
        o = o * lax.rsqrt(_head_sums(o * o) * (1.0 / GDN_DV) + EPS) * ng
        o_ref[bi] = o * _silu(gate_ref[bi])


def _gdn(gqkv, kis, ggate, conv_w, conv_prev, s0, a_log, dt_bias, norm_g, nbb):
    B, T, _ = gqkv.shape
    c = math.gcd(T, CHUNK)
    return pl.pallas_call(
        functools.partial(_gdn_kernel, c=c, nbb=nbb),
        grid=(B // nbb, T // c),
        in_specs=[pl.BlockSpec((nbb, c, GDN_CONV_C), lambda b, j: (b, j, 0)),
                  pl.BlockSpec((nbb, c, LANES), lambda b, j: (b, j, 0)),
                  pl.BlockSpec((nbb, c, GDN_W), lambda b, j: (b, j, 0)),
                  pl.BlockSpec((CONV_W, GDN_CONV_C), lambda b, j: (0, 0)),
                  pl.BlockSpec((nbb, CONV_W - 1, GDN_CONV_C), lambda b, j: (b, 0, 0)),
                  pl.BlockSpec((nbb, GDN_HEADS, GDN_DK, GDN_DV), lambda b, j: (b, 0, 0, 0)),
                  pl.BlockSpec((1, LANES), lambda b, j: (0, 0)),
                  pl.BlockSpec((1, LANES), lambda b, j: (0, 0)),
                  pl.BlockSpec((1, GDN_W), lambda b, j: (0, 0))],
        out_specs=(pl.BlockSpec((nbb, c, GDN_W), lambda b, j: (b, j, 0)),
                   pl.BlockSpec((nbb, CONV_W - 1, GDN_CONV_C), lambda b, j: (b, 0, 0)),
                   pl.BlockSpec((nbb, GDN_HEADS, GDN_DK, GDN_DV), lambda b, j: (b, 0, 0, 0))),
        out_shape=(jax.ShapeDtypeStruct((B, T, GDN_W), F32),
                   jax.ShapeDtypeStruct((B, CONV_W - 1, GDN_CONV_C), F32),
                   jax.ShapeDtypeStruct((B, GDN_HEADS, GDN_DK, GDN_DV), F32)),
        scratch_shapes=[pltpu.VMEM((nbb, SUBLANES + c, GDN_CONV_C), F32),
                        pltpu.VMEM((nbb, GDN_HEADS, GDN_DK, GDN_DV), F32)],
        compiler_params=_params(("arbitrary", "arbitrary")),
        name="gdn",
    )(gqkv, kis, ggate, conv_w, conv_prev, s0, _lane_vector(a_log, KIS_A), _lane_vector(dt_bias, KIS_A),
      jnp.tile(norm_g, GDN_HEADS).reshape(1, -1))


def _ssd_kernel(xbc_ref, z_ref, kis_ref, w_ref, cb_ref, prev_ref, h0_ref, alog_ref, dtb_ref, dsk_ref,
                ng_ref, o_ref, new_ref, h_ref, ext_ref, h_sc, *, c, nbb):
    j = pl.program_id(1)
    lo = SUBLANES - (CONV_W - 1)

    @pl.when(j == 0)
    def _():
        h_sc[...] = h0_ref[...]
        ext_ref[:, lo:SUBLANES, :] = prev_ref[...]

    dsk = dsk_ref[...]
    ng = ng_ref[...]
    w_conv = w_ref[...]
    ri = lax.broadcasted_iota(jnp.int32, (c, c), 0)
    ci = lax.broadcasted_iota(jnp.int32, (c, c), 1)
    incl = ri >= ci
    gn = SSM_GROUPS * SSM_N
    rep = SSM_HEADS // SSM_GROUPS
    gw = SSM_W // SSM_GROUPS
    grp_in = []
    ch = []
    slot = []
    ids = []
    for bi in range(nbb):
        y, tail = _short_conv(ext_ref, bi, xbc_ref[bi], w_conv, c)
        new_ref[bi] = tail
        y = _silu(y + cb_ref[...])
        dts = _softplus(kis_ref[bi] + dtb_ref[...])
        a = dts * (-jnp.exp(alog_ref[...]))
        acum = _cumsum_rows(a)
        acum_t = _transpose_rows(acum)
        dtm = _lane_spread(dts, KIS_DT, SSM_HEADS)
        am = _lane_spread(acum, KIS_DT, SSM_HEADS)
        a_last = am[c - 1:c, :]
        xs = y[:, :SSM_W]
        xdt = xs * dtm
        xw = xs * (jnp.exp(a_last - am) * dtm)
        h_dec = jnp.exp(a_last)
        slot.append((xs, jnp.exp(am), _silu(z_ref[bi])))
        for grp in range(SSM_GROUPS):
            grp_in.append((y[:, SSM_W + grp * SSM_N:SSM_W + (grp + 1) * SSM_N],
                           y[:, SSM_W + gn + grp * SSM_N:SSM_W + gn + (grp + 1) * SSM_N]))
            for hh in range(rep):
                h = grp * rep + hh
                hsl = slice(h * SSM_P, (h + 1) * SSM_P)
                ch.append((xdt[:, hsl], xw[:, hsl], am[:, h * SSM_P:h * SSM_P + c], acum_t[KIS_DT + h:KIS_DT + h + 1, :],
                           jnp.concatenate([h_dec[:, hsl]] * (SSM_N // SSM_P), axis=1), len(grp_in) - 1))
                ids.append((bi, h))
    cb = [_mm_nt(cg, bg) for bg, cg in grp_in]
    seg = [jnp.where(incl, jnp.exp(jnp.where(incl, x[2] - x[3], 0.0)), 0.0) for x in ch]
    hs = [h_sc[bi, h] for bi, h in ids]
    y_in = [_mm(cb[x[5]] * sg, x[0]) for x, sg in zip(ch, seg)]
    y_st = [_mm_nt(grp_in[x[5]][1], hst) for x, hst in zip(ch, hs)]
    upd = [_dot_tn(x[1], grp_in[x[5]][0]) for x in ch]
    for (bi, h), x, hst, up in zip(ids, ch, hs, upd):
        h_new = hst * x[4] + up
        h_sc[bi, h] = h_new
        h_ref[bi, h] = h_new
    for bi, (xs, ea, zg) in enumerate(slot):
        lo_c, hi_c = bi * SSM_HEADS, (bi + 1) * SSM_HEADS
        yy = (jnp.concatenate(y_in[lo_c:hi_c], axis=1) + jnp.concatenate(y_st[lo_c:hi_c], axis=1) * ea
              + dsk * xs) * zg
        for grp in range(SSM_GROUPS):
            yg = yy[:, grp * gw:(grp + 1) * gw]
            yg = yg * lax.rsqrt(jnp.mean(yg * yg, axis=-1, keepdims=True) + EPS)
            o_ref[bi, :, grp * gw:(grp + 1) * gw] = yg * ng[:, grp * gw:(grp + 1) * gw]


def _ssd(xbc, z, kis, conv_w, conv_b, conv_prev, h0, a_log, dt_bias, d_skip, norm_g, nbb):
    B, T, _ = xbc.shape
    c = math.gcd(T, CHUNK)
    return pl.pallas_call(
        functools.partial(_ssd_kernel, c=c, nbb=nbb),
        grid=(B // nbb, T // c),
        in_specs=[pl.BlockSpec((nbb, c, SSM_CONV_C), lambda b, j: (b, j, 0)),
                  pl.BlockSpec((nbb, c, SSM_W), lambda b, j: (b, j, 0)),
                  pl.BlockSpec((nbb, c, LANES), lambda b, j: (b, j, 0)),
                  pl.BlockSpec((CONV_W, SSM_CONV_C), lambda b, j: (0, 0)),
                  pl.BlockSpec((1, SSM_CONV_C), lambda b, j: (0, 0)),
                  pl.BlockSpec((nbb, CONV_W - 1, SSM_CONV_C), lambda b, j: (b, 0, 0)),
                  pl.BlockSpec((nbb, SSM_HEADS, SSM_P, SSM_N), lambda b, j: (b, 0, 0, 0)),
                  pl.BlockSpec((1, LANES), lambda b, j: (0, 0)),
                  pl.BlockSpec((1, LANES), lambda b, j: (0, 0)),
                  pl.BlockSpec((1, SSM_W), lambda b, j: (0, 0)),
                  pl.BlockSpec((1, SSM_W), lambda b, j: (0, 0))],
        out_specs=(pl.BlockSpec((nbb, c, SSM_W), lambda b, j: (b, j, 0)),
                   pl.BlockSpec((nbb, CONV_W - 1, SSM_CONV_C), lambda b, j: (b, 0, 0)),
                   pl.BlockSpec((nbb, SSM_HEADS, SSM_P, SSM_N), lambda b, j: (b, 0, 0, 0))),
        out_shape=(jax.ShapeDtypeStruct((B, T, SSM_W), F32),
                   jax.ShapeDtypeStruct((B, CONV_W - 1, SSM_CONV_C), F32),
                   jax.ShapeDtypeStruct((B, SSM_HEADS, SSM_P, SSM_N), F32)),
        scratch_shapes=[pltpu.VMEM((nbb, SUBLANES + c, SSM_CONV_C), F32),
                        pltpu.VMEM((nbb, SSM_HEADS, SSM_P, SSM_N), F32)],
        compiler_params=_params(("arbitrary", "arbitrary")),
        name="ssd",
    )(xbc, z, kis, conv_w, conv_b.reshape(1, -1), conv_prev, h0, _lane_vector(a_log, KIS_DT),
      _lane_vector(dt_bias, KIS_DT), jnp.repeat(d_skip, SSM_P).reshape(1, -1), norm_g.reshape(1, -1))


IDX_BITS = 12


OPENING_PROBES = 14
ROUND_PROBES = 2
SEARCH_ROUNDS = 4096
FOLD_ROWS = 128


def _fold_keys(x, op, axis=0):
    s = x.shape[0]
    if axis == 0 and s % FOLD_ROWS == 0 and s > FOLD_ROWS:
        x = op(x.reshape(s // FOLD_ROWS, FOLD_ROWS, x.shape[1]), axis=0)
    return op(x, axis=axis, keepdims=True)


def _fold_rows(x, op):
    return _fold_keys(x, op, 0)


def _select_topk(score, adm, pos, skip, topk, axis):
    def count(mask):
        return _fold_keys(mask.astype(F32), jnp.sum, axis)

    s = jnp.where(adm, score, -jnp.inf)

    few = (count(adm) <= topk) | skip
    smax = _fold_keys(s, jnp.max, axis)
    smin = _fold_keys(jnp.where(adm, score, jnp.inf), jnp.min, axis)
    top_full = count(s >= smax) >= topk
    lo0 = jnp.where(few, 0.0, jnp.where(top_full, smax, smin))
    hi0 = jnp.where(few, 0.0, smax)
    done0 = (few | top_full).astype(F32)

    def probe(lo, hi, done):
        mid = 0.5 * lo + 0.5 * hi
        ok = (mid > lo) & (mid < hi) & (done < 0.5)
        take = count(s >= mid) >= topk
        return jnp.where(ok & take, mid, lo), jnp.where(ok & jnp.logical_not(take), mid, hi)

    def opening(lo, hi):
        for _ in range(OPENING_PROBES):
            lo, hi = probe(lo, hi, done0)
        return lo, hi

    lo1, hi1 = lax.cond(jnp.min(done0) < 0.5, opening, lambda lo, hi: (lo, hi), lo0, hi0)

    def search_cond(c):
        return (jnp.min(c[2]) < 0.5) & (c[3] < SEARCH_ROUNDS)

    def search_body(c):
        lo, hi, done, it = c
        last = _fold_keys(jnp.where(s < hi, s, -jnp.inf), jnp.max, axis)
        reached = count(s >= last) >= topk
        active = done < 0.5
        lo = jnp.where(active & reached, last, lo)
        hi = jnp.where(active & jnp.logical_not(reached), last, hi)
        done = jnp.maximum(done, reached.astype(F32))
        for _ in range(ROUND_PROBES):
            lo, hi = probe(lo, hi, done)
        return lo, hi, done, it + 1

    lo, _, _, _ = lax.while_loop(search_cond, search_body, (lo1, hi1, done0, jnp.int32(0)))
    t = jnp.where(few, -jnp.inf, lo)
    gt = s > t
    tie = s == t
    need = topk - count(gt)
    excess = jnp.where(few, 0.0, count(tie) - need)

    def tie_search():
        def ibody(i, m):
            cand = m | (jnp.int32(1) << (IDX_BITS - 1 - i))
            return jnp.where(count(tie & (pos < cand)) <= need, cand, m)

        return lax.fori_loop(0, IDX_BITS, ibody, jnp.zeros(few.shape, jnp.int32))

    m = lax.cond(jnp.max(excess) > 0.0, tie_search, lambda: jnp.full(few.shape, 2 ** IDX_BITS - 1, jnp.int32))
    return adm & (gt | (tie & (pos < m)))


def _dsa_kernel(q_ref, qi_ref, kis_ref, k_ref, vt_ref, ki_ref, o_ref, *, tq, s_len, s_valid, q_offset, j0,
                n_valid_q, topk):
    j = pl.program_id(1)
    lane = lax.broadcasted_iota(jnp.int32, (1, tq), 1)
    qpos = q_offset + (j0 + j) * tq + lane
    lim = jnp.minimum((qpos // CHUNK + 1) * CHUNK, s_valid)
    row = lax.broadcasted_iota(jnp.int32, (s_len, tq), 0)
    adm = row < lim

    qi_t = qi_ref[0].T.astype(BF16)
    kis_t = kis_ref[0].T
    ki = ki_ref[0]
    assert IDX_DH == 64 and IDX_HEADS == 4 and ATT_DH == 64
    wi = kis_t[KIS_WI:KIS_WI + IDX_HEADS, :] * (IDX_DH ** -0.5 * IDX_HEADS ** -0.5)
    qi_all = jnp.concatenate([qi_t[h * IDX_DH:(h + 1) * IDX_DH, :] for h in range(IDX_HEADS)], axis=1)
    rel_all = _dot(ki, qi_all)
    score = jnp.zeros((s_len, tq), F32)
    for h in range(IDX_HEADS):
        score = score + jnp.maximum(rel_all[:, h * tq:(h + 1) * tq], 0.0) * wi[h:h + 1, :]
    sel = _select_topk(score, adm, row, lane >= n_valid_q, topk, 0)

    q_t = q_ref[0].T * (ATT_DH ** -0.5)
    pair = LANES // ATT_DH
    head_in_pair = lax.broadcasted_iota(jnp.int32, (LANES, tq), 0) // ATT_DH
    logits = []
    for g in range(ATT_HEADS // pair):
        qg = q_t[g * LANES:(g + 1) * LANES, :]
        qm = jnp.concatenate([jnp.where(head_in_pair == i, qg, 0.0) for i in range(pair)], axis=1)
        lg = _dot(k_ref[0, :, g * LANES:(g + 1) * LANES], qm.astype(BF16))
        logits += [lg[:, i * tq:(i + 1) * tq] for i in range(pair)]
    probs, scales = [], []
    for lg in logits:
        lg = jnp.where(sel, lg, -jnp.inf)
        p = jnp.exp(lg - _fold_rows(lg, jnp.max))
        scales.append(1.0 / _fold_rows(p, jnp.sum))
        probs.append(p.astype(BF16))
    outs = [_dot(vt_ref[0, h * ATT_DH:(h + 1) * ATT_DH, :], probs[h]) * scales[h] for h in range(ATT_HEADS)]
    o_ref[0] = jnp.concatenate(outs, axis=0).T


def _dsa(q, qi, kis, k_b, vt_b, ki_b, *, tq, j0, nj, s_len, s_valid, q_offset, n_valid_q, topk):
    B = q.shape[0]
    assert s_len < 2 ** IDX_BITS and s_len <= k_b.shape[1]
    return pl.pallas_call(
        functools.partial(_dsa_kernel, tq=tq, s_len=s_len, s_valid=s_valid, q_offset=q_offset, j0=j0,
                          n_valid_q=n_valid_q, topk=topk),
        grid=(B, nj),
        in_specs=[pl.BlockSpec((1, tq, ATT_W), lambda b, j: (b, j0 + j, 0)),
                  pl.BlockSpec((1, tq, IDX_HEADS * IDX_DH), lambda b, j: (b, j0 + j, 0)),
                  pl.BlockSpec((1, tq, LANES), lambda b, j: (b, j0 + j, 0)),
                  pl.BlockSpec((1, s_len, ATT_W), lambda b, j: (b, 0, 0)),
                  pl.BlockSpec((1, ATT_W, s_len), lambda b, j: (b, 0, 0)),
                  pl.BlockSpec((1, s_len, IDX_DH), lambda b, j: (b, 0, 0))],
        out_specs=pl.BlockSpec((1, tq, ATT_W), lambda b, j: (b, j, 0)),
        out_shape=jax.ShapeDtypeStruct((B, nj * tq, ATT_W), F32),
        compiler_params=_params(("arbitrary", "arbitrary")),
        name="dsa",
    )(q, qi, kis, k_b, vt_b, ki_b)


def _dsa_rows_kernel(q_ref, qi_ref, kis_ref, k_ref, v_ref, ki_ref, o_ref, *, nbq, tq, s_len, s_valid, q_offset,
                     topk):
    rows = nbq * tq
    qpos = q_offset + lax.broadcasted_iota(jnp.int32, (rows, 1), 0) % tq
    lim = jnp.minimum((qpos // CHUNK + 1) * CHUNK, s_valid)
    col = lax.broadcasted_iota(jnp.int32, (rows, s_len), 1)
    adm = col < lim

    scores = []
    for b in range(nbq):
        qi = qi_ref[b]
        qi_st = jnp.concatenate([qi[:, h * IDX_DH:(h + 1) * IDX_DH] for h in range(IDX_HEADS)], axis=0)
        rel = _dot_nt(qi_st.astype(BF16), ki_ref[b])
        wi = kis_ref[b][:, KIS_WI:KIS_WI + IDX_HEADS] * (IDX_DH ** -0.5 * IDX_HEADS ** -0.5)
        sc = jnp.maximum(rel[:tq, :], 0.0) * wi[:, 0:1]
        for h in range(1, IDX_HEADS):
            sc = sc + jnp.maximum(rel[h * tq:(h + 1) * tq, :], 0.0) * wi[:, h:h + 1]
        scores.append(sc)
    sel = _select_topk(jnp.concatenate(scores, axis=0), adm, col, qpos < 0, topk, 1)

    head_of_lane = lax.broadcasted_iota(jnp.int32, (tq, ATT_W), 1) // ATT_DH
    for b in range(nbq):
        q = q_ref[b] * (ATT_DH ** -0.5)
        q_st = jnp.concatenate([jnp.where(head_of_lane == h, q, 0.0) for h in range(ATT_HEADS)], axis=0)
        logits = _dot_nt(q_st.astype(BF16), k_ref[b])
        logits = jnp.where(jnp.concatenate([sel[b * tq:(b + 1) * tq, :]] * ATT_HEADS, axis=0), logits, -jnp.inf)
        p = jnp.exp(logits - jnp.max(logits, axis=1, keepdims=True))
        scale = 1.0 / jnp.sum(p, axis=1, keepdims=True)
        pv = _dot(p.astype(BF16), v_ref[b]) * scale
        out = jnp.zeros((tq, ATT_W), F32)
        for h in range(ATT_HEADS):
            out = jnp.where(head_of_lane == h, pv[h * tq:(h + 1) * tq, :], out)
        o_ref[b] = out


def _dsa_rows(q, qi, kis, k_b, v_b, ki_b, *, nbq, s_valid, q_offset, topk):
    B, T, _ = q.shape
    s_len = k_b.shape[1]
    assert s_len < 2 ** IDX_BITS and T % SUBLANES == 0 and B % nbq == 0
    return pl.pallas_call(
        functools.partial(_dsa_rows_kernel, nbq=nbq, tq=T, s_len=s_len, s_valid=s_valid, q_offset=q_offset,
                          topk=topk),
        grid=(B // nbq,),
        in_specs=[pl.BlockSpec((nbq, T, ATT_W), lambda b: (b, 0, 0)),
                  pl.BlockSpec((nbq, T, IDX_HEADS * IDX_DH), lambda b: (b, 0, 0)),
                  pl.BlockSpec((nbq, T, LANES), lambda b: (b, 0, 0)),
                  pl.BlockSpec((nbq, s_len, ATT_W), lambda b: (b, 0, 0)),
                  pl.BlockSpec((nbq, s_len, ATT_W), lambda b: (b, 0, 0)),
                  pl.BlockSpec((nbq, s_len, IDX_DH), lambda b: (b, 0, 0))],
        out_specs=pl.BlockSpec((nbq, T, ATT_W), lambda b: (b, 0, 0)),
        out_shape=jax.ShapeDtypeStruct((B, T, ATT_W), F32),
        compiler_params=_params(("arbitrary",)),
        name="dsa_rows",
    )(q, qi, kis, k_b, v_b, ki_b)


def _dsa_group(q, qi, kis, k_b, vt_b, ki_b, *, s_valid, q_offset, topk):
    B, T, _ = q.shape
    tq = 2 * LANES if T % (2 * LANES) == 0 else LANES
    n_valid_q = min(T, tq)
    if T < tq:
        padq = lambda a: jnp.pad(a, ((0, 0), (0, tq - T), (0, 0)))
        q, qi, kis = padq(q), padq(qi), padq(kis)
    nq = q.shape[1] // tq
    s_total = k_b.shape[1]
    outs = []
    j0 = 0
    while j0 < nq:
        nj = 1
        reach = -(-(q_offset + (j0 + nj) * tq) // CHUNK) * CHUNK
        s_len = min(s_total, -(-min(reach, s_valid) // LANES) * LANES)
        outs.append(_dsa(q, qi, kis, k_b, vt_b, ki_b, tq=tq, j0=j0, nj=nj, s_len=s_len, s_valid=s_valid,
                         q_offset=q_offset, n_valid_q=n_valid_q, topk=topk))
        j0 += nj
    out = outs[0] if len(outs) == 1 else jnp.concatenate(outs, axis=1)
    return out[:, :T]


def _ffn_kernel(x_ref, gdn_ref, att_ref, ssm_ref, mod_ref, wo_ref, g2_ref, wg_ref, wu_ref, cw_ref, cb_ref,
                wd_ref, prev_ref, fg_ref, *rest, nb, tt, nk, final):
    if final:
        xo_ref, y_ref, new_ref, h2_sc, acc_sc, buf_sc, carry_sc = rest
    else:
        xo_ref, new_ref, h2_sc, acc_sc, buf_sc, carry_sc = rest
        y_ref = None
    t = pl.program_id(1)
    k = pl.program_id(2)
    rows = nb * tt
    lo = SUBLANES - (FFN_CONV_W - 1)
    mod = mod_ref[...]

    @pl.when(k == 0)
    def _():
        mix = jnp.concatenate([gdn_ref[...], att_ref[...], ssm_ref[...]], axis=-1)
        proj = _dot(mix.reshape(rows, MIX_W).astype(BF16), wo_ref[...]).reshape(nb, tt, D_MODEL)
        x1 = x_ref[...] + mod[:, 2:3, :] * proj
        acc_sc[...] = x1
        ms = jnp.mean(x1 * x1, axis=-1, keepdims=True)
        h2 = x1 * lax.rsqrt(ms + EPS) * g2_ref[...]
        h2 = h2 * (1.0 + mod[:, 4:5, :]) + mod[:, 3:4, :]
        h2_sc[...] = h2.reshape(rows, D_MODEL).astype(BF16)

    @pl.when(t == 0)
    def _():
        carry_sc[k, :, lo:SUBLANES, :] = prev_ref[...]

    h2 = h2_sc[...]
    ag = _dot(h2, wg_ref[...])
    up = _dot(h2, wu_ref[...])
    tf = ag.shape[-1]
    buf_sc[:, lo:SUBLANES, :] = carry_sc[k, :, lo:SUBLANES, :]
    buf_sc[:, SUBLANES:SUBLANES + tt, :] = ag.reshape(nb, tt, tf)
    cw = cw_ref[...]
    conv = buf_sc[:, lo:lo + tt, :] * cw[0:1, :][None]
    for jj in range(1, FFN_CONV_W):
        conv = conv + buf_sc[:, lo + jj:lo + jj + tt, :] * cw[jj:jj + 1, :][None]
    tail = buf_sc[:, tt + lo:tt + SUBLANES, :]
    carry_sc[k, :, lo:SUBLANES, :] = tail
    new_ref[:, pl.ds(k, 1), :, :] = tail[:, None]
    act = _silu(conv + cb_ref[...][None]).reshape(rows, tf) * up
    y = _dot(act.astype(BF16), wd_ref[...]).reshape(nb, tt, D_MODEL)
    acc_sc[...] += mod[:, 5:6, :] * y

    @pl.when(k == nk - 1)
    def _():
        xo = acc_sc[...]
        xo_ref[...] = xo
        if final:
            ms = jnp.mean(xo * xo, axis=-1, keepdims=True)
            y_ref[...] = xo * lax.rsqrt(ms + EPS) * fg_ref[...]


def _ffn(x, gdn_o, att_o, ssm_o, mod, w_out, g2, w_gate, w_up, conv_w, conv_b, w_down, conv_prev, final_g,
         nb, tt, tf, final):
    B, T, _ = x.shape
    nk = D_FF // tf
    grid = (B // nb, T // tt, nk)
    row_spec = lambda w: pl.BlockSpec((nb, tt, w), lambda b, t, k: (b, t, 0))
    in_specs = [row_spec(D_MODEL), row_spec(GDN_W), row_spec(ATT_W), row_spec(SSM_W),
                pl.BlockSpec((nb, 6, D_MODEL), lambda b, t, k: (b, 0, 0)),
                pl.BlockSpec((MIX_W, D_MODEL), lambda b, t, k: (0, 0)),
                pl.BlockSpec((1, D_MODEL), lambda b, t, k: (0, 0)),
                pl.BlockSpec((D_MODEL, tf), lambda b, t, k: (0, k)),
                pl.BlockSpec((D_MODEL, tf), lambda b, t, k: (0, k)),
                pl.BlockSpec((FFN_CONV_W, tf), lambda b, t, k: (0, k)),
                pl.BlockSpec((1, tf), lambda b, t, k: (0, k)),
                pl.BlockSpec((tf, D_MODEL), lambda b, t, k: (k, 0)),
                pl.BlockSpec((nb, FFN_CONV_W - 1, tf), lambda b, t, k: (b, 0, k)),
                pl.BlockSpec((1, D_MODEL), lambda b, t, k: (0, 0))]
    out_shape = [jax.ShapeDtypeStruct((B, T, D_MODEL), F32)]
    out_specs = [row_spec(D_MODEL)]
    if final:
        out_shape.append(jax.ShapeDtypeStruct((B, T, D_MODEL), F32))
        out_specs.append(row_spec(D_MODEL))
    out_shape.append(jax.ShapeDtypeStruct((B, nk, FFN_CONV_W - 1, tf), F32))
    out_specs.append(pl.BlockSpec((nb, nk, FFN_CONV_W - 1, tf), lambda b, t, k: (b, 0, 0, 0)))
    res = pl.pallas_call(
        functools.partial(_ffn_kernel, nb=nb, tt=tt, nk=nk, final=final),
        grid=grid,
        in_specs=in_specs,
        out_specs=tuple(out_specs),
        out_shape=tuple(out_shape),
        scratch_shapes=[pltpu.VMEM((nb * tt, D_MODEL), BF16),
                        pltpu.VMEM((nb, tt, D_MODEL), F32),
                        pltpu.VMEM((nb, SUBLANES + tt, tf), F32),
                        pltpu.VMEM((nk, nb, SUBLANES, tf), F32)],
        compiler_params=_params(("arbitrary", "arbitrary", "arbitrary")),
        name="ffn",
    )(x, gdn_o, att_o, ssm_o, mod, w_out, g2, w_gate, w_up, conv_w, conv_b.reshape(1, -1), w_down,
      conv_prev, final_g)
    tail = res[-1].transpose(0, 2, 1, 3).reshape(B, FFN_CONV_W - 1, D_FF)
    return (*res[:-1], tail)


def _rope_tables(pos):
    half = ATT_DH // 2
    inv_freq = ROPE_THETA ** (-jnp.arange(half, dtype=F32) / half)
    ang = pos.astype(F32)[:, None] * inv_freq[None, :]
    cos = jnp.cos(ang)
    sin = jnp.sin(ang)
    cos_h = jnp.concatenate([cos, cos], axis=-1)
    sin_h = jnp.concatenate([-sin, sin], axis=-1)
    ones = jnp.ones((pos.shape[0], LANES - ATT_DH), F32)
    cos_t = jnp.concatenate([jnp.tile(cos_h, (1, ATT_HEADS)), cos_h, ones], axis=-1)
    sin_t = jnp.concatenate([jnp.tile(sin_h, (1, ATT_HEADS)), sin_h, 0.0 * ones], axis=-1)
    return cos_t, sin_t


def _layer(x, mod, pos_tables, cache, states, lw, tiles, final, final_g):
    (norm1_g, w_perm, gdn_conv_w, gdn_A_log, gdn_dt_bias, gdn_norm_g, ssm_conv_w, ssm_conv_b, ssm_A_log,
     ssm_dt_bias, ssm_D, ssm_norm_g, w_out, norm2_g, w_gate, w_up, ffn_conv_w, ffn_conv_b, w_down) = lw
    gdn_conv_prev, gdn_s0, ssm_conv_prev, ssm_h0, ffn_conv_prev = states
    B, T, _ = x.shape
    nb, tt, nbb, ffn_tt, tf = tiles
    cos_t, sin_t = pos_tables
    proj = _inproj(x, mod, norm1_g.reshape(1, -1), w_perm, cos_t, sin_t, nb, tt, emit_keys=cache is None)
    gqkv, ggate, q, k, v, qi, kis, z, xbc = proj[:len(OUT_SEGS)]
    ki = kis[..., :IDX_DH]

    gdn_o, gdn_conv_new, gdn_s = _gdn(gqkv, kis, ggate, gdn_conv_w, gdn_conv_prev, gdn_s0, gdn_A_log,
                                      gdn_dt_bias, gdn_norm_g, nbb)
    ssm_o, ssm_conv_new, ssm_h = _ssd(xbc, z, kis, ssm_conv_w, ssm_conv_b, ssm_conv_prev, ssm_h0, ssm_A_log,
                                      ssm_dt_bias, ssm_D, ssm_norm_g, nbb)
    if cache is None:
        k_b, ki_b, vt_b = proj[len(OUT_SEGS):]
        att_o = _dsa_group(q, qi, kis, k_b, vt_b, ki_b, s_valid=T, q_offset=0, topk=min(TOPK_MAX, T // 4))
    else:
        ck, cv, cki = cache
        P = ck.shape[1]
        L = P + T
        pad = -(-L // LANES) * LANES - L
        k_b = jnp.concatenate([ck.reshape(B, P, ATT_W), k, jnp.zeros((B, pad, ATT_W), F32)], axis=1).astype(BF16)
        v_b = jnp.concatenate([cv.reshape(B, P, ATT_W), v, jnp.zeros((B, pad, ATT_W), F32)], axis=1).astype(BF16)
        ki_b = jnp.concatenate([cki, ki, jnp.zeros((B, pad, IDX_DH), F32)], axis=1).astype(BF16)
        att_o = _dsa_rows(q, qi, kis, k_b, v_b, ki_b, nbq=nbb, s_valid=L, q_offset=P, topk=min(TOPK_MAX, L // 4))

    res = _ffn(x, gdn_o, att_o, ssm_o, mod, w_out, norm2_g.reshape(1, -1), w_gate, w_up, ffn_conv_w, ffn_conv_b,
               w_down, ffn_conv_prev, final_g.reshape(1, -1), ffn_tt[0], ffn_tt[1], tf, final)
    if final:
        x_new, y, ffn_conv_new = res
    else:
        (x_new, ffn_conv_new), y = res, None
    st = (k.reshape(B, T, ATT_HEADS, ATT_DH), v.reshape(B, T, ATT_HEADS, ATT_DH), ki,
          gdn_conv_new, gdn_s, ssm_conv_new, ssm_h, ffn_conv_new)
    return x_new, y, st


def kernel(x_prompt, x_sample, c_prompt, c_sample, cache_k, cache_v, cache_kidx, state_gdn_conv, state_gdn,
           state_ssm_conv, state_ssm, state_ffn_conv, w_ada, b_ada, norm1_g, w_in, gdn_conv_w, gdn_A_log,
           gdn_dt_bias, gdn_norm_g, ssm_conv_w, ssm_conv_b, ssm_A_log, ssm_dt_bias, ssm_D, ssm_norm_g, w_out,
           norm2_g, w_gate, w_up, ffn_conv_w, ffn_conv_b, w_down, final_g):
    Bp, T, _ = x_prompt.shape
    Bs, Ts, _ = x_sample.shape
    P = cache_k.shape[2]

    c_all = jnp.concatenate([c_prompt, c_sample], axis=0)
    mod_all = _ada(c_all, w_ada, b_ada).reshape(DEPTH, Bp + Bs, 6, D_MODEL)

    tables_p = _rope_tables(jnp.arange(T))
    tables_s = _rope_tables(P + jnp.arange(Ts))

    w_perm = _permute_columns(w_in.astype(BF16))
    w_out_b = w_out.astype(BF16)
    w_gate_b = w_gate.astype(BF16)
    w_up_b = w_up.astype(BF16)
    w_down_b = w_down.astype(BF16)

    zeros_p = (jnp.zeros((Bp, CONV_W - 1, GDN_CONV_C), F32),
               jnp.zeros((Bp, GDN_HEADS, GDN_DK, GDN_DV), F32),
               jnp.zeros((Bp, CONV_W - 1, SSM_CONV_C), F32),
               jnp.zeros((Bp, SSM_HEADS, SSM_P, SSM_N), F32),
               jnp.zeros((Bp, FFN_CONV_W - 1, D_FF), F32))

    tf = D_FF // 2
    tiles_p = (1, min(256, T), math.gcd(Bp, 8), (1, min(512, T)), tf)
    tiles_s = (Bs, Ts, math.gcd(Bs, 4), (Bs, Ts), tf)

    xp, xs = x_prompt, x_sample
    new_p, new_s = [], []
    yp = ys = None
    for l in range(DEPTH):
        lw = (norm1_g[l], w_perm[l], gdn_conv_w[l], gdn_A_log[l], gdn_dt_bias[l], gdn_norm_g[l], ssm_conv_w[l],
              ssm_conv_b[l], ssm_A_log[l], ssm_dt_bias[l], ssm_D[l], ssm_norm_g[l], w_out_b[l], norm2_g[l],
              w_gate_b[l], w_up_b[l], ffn_conv_w[l], ffn_conv_b[l], w_down_b[l])
        final = l == DEPTH - 1
        xp, yp, st_p = _layer(xp, mod_all[l, :Bp], tables_p, None, zeros_p, lw, tiles_p, final, final_g)
        states_s = (state_gdn_conv[l], state_gdn[l], state_ssm_conv[l], state_ssm[l], state_ffn_conv[l])
        xs, ys, st_s = _layer(xs, mod_all[l, Bp:], tables_s, (cache_k[l], cache_v[l], cache_kidx[l]), states_s,
                              lw, tiles_s, final, final_g)
        new_p.append(st_p)
        new_s.append(st_s)
    outs_p = [jnp.stack([st[i] for st in new_p]) for i in range(8)]
    outs_s = [jnp.stack([st[i] for st in new_s]) for i in range(8)]
    return (yp, ys, *outs_p, *outs_s)
```

```python
import functools
import math

import jax
import jax.numpy as jnp
import numpy as np
from jax import lax
from jax.experimental import pallas as pl
from jax.experimental.pallas import tpu as pltpu

F32 = jnp.float32
BF16 = jnp.bfloat16
HI = lax.Precision.HIGHEST

D_MODEL = 1024
DEPTH = 2
CHUNK = 64
CONV_W = 4
FFN_CONV_W = 3
D_FF = 2816
ROPE_THETA = 10000.0
EPS = 1e-6
GDN_HEADS = 4
GDN_DK = 64
GDN_DV = 64
ATT_HEADS = 4
ATT_DH = 64
IDX_HEADS = 4
IDX_DH = 64
TOPK_MAX = 256
SSM_HEADS = 8
SSM_P = 64
SSM_GROUPS = 2
SSM_N = 128
GDN_W = GDN_HEADS * GDN_DV
ATT_W = ATT_HEADS * ATT_DH
SSM_W = SSM_HEADS * SSM_P
MIX_W = GDN_W + ATT_W + SSM_W
GDN_CONV_C = 2 * GDN_HEADS * GDN_DK + GDN_W
SSM_CONV_C = SSM_W + 2 * SSM_GROUPS * SSM_N
IN_SIZES = (GDN_CONV_C, GDN_HEADS, GDN_HEADS, GDN_W,
            ATT_W, ATT_W, ATT_W, IDX_HEADS * IDX_DH, IDX_DH, IDX_HEADS,
            SSM_W, SSM_CONV_C, SSM_HEADS)
IN_W = sum(IN_SIZES)

LANES = 128
SUBLANES = 8
VMEM_LIMIT = 56 * 1024 * 1024

KIS_A = IDX_DH
KIS_B = KIS_A + GDN_HEADS
KIS_WI = KIS_B + GDN_HEADS
KIS_DT = KIS_WI + IDX_HEADS
KIS_END = KIS_DT + SSM_HEADS
OUT_SEGS = (("gqkv", GDN_CONV_C), ("ggate", GDN_W), ("q", ATT_W), ("k", ATT_W), ("v", ATT_W),
            ("qi", IDX_HEADS * IDX_DH), ("kis", LANES), ("z", SSM_W), ("xbc", SSM_CONV_C))
PERM_W = sum(w for _, w in OUT_SEGS)
ROT_W = ATT_W + LANES


def _perm_columns():
    starts = np.concatenate([[0], np.cumsum(IN_SIZES)])
    (s_gqkv, s_ga, s_gb, s_gg, s_q, s_k, s_v, s_qi, s_ki, s_wi, s_z, s_xbc, s_dt) = starts[:-1]
    cols = []
    cols += list(range(s_gqkv, s_gqkv + GDN_CONV_C))
    cols += list(range(s_gg, s_gg + GDN_W))
    cols += list(range(s_q, s_q + ATT_W))
    cols += list(range(s_k, s_k + ATT_W))
    cols += list(range(s_v, s_v + ATT_W))
    cols += list(range(s_qi, s_qi + IDX_HEADS * IDX_DH))
    kis = (list(range(s_ki, s_ki + IDX_DH)) + list(range(s_ga, s_ga + GDN_HEADS))
           + list(range(s_gb, s_gb + GDN_HEADS)) + list(range(s_wi, s_wi + IDX_HEADS))
           + list(range(s_dt, s_dt + SSM_HEADS)))
    cols += kis + [-1] * (LANES - len(kis))
    cols += list(range(s_z, s_z + SSM_W))
    cols += list(range(s_xbc, s_xbc + SSM_CONV_C))
    assert len(cols) == PERM_W
    return np.asarray(cols, np.int32)


_PERM_COLS = _perm_columns()


def _permute_columns(w_in):
    pieces = []
    start = 0
    for i in range(1, PERM_W + 1):
        prev = int(_PERM_COLS[i - 1])
        if i == PERM_W or not ((prev < 0 and _PERM_COLS[i] < 0) or (prev >= 0 and _PERM_COLS[i] == prev + 1)):
            if _PERM_COLS[start] < 0:
                pieces.append(jnp.zeros(w_in.shape[:-1] + (i - start,), w_in.dtype))
            else:
                pieces.append(w_in[..., int(_PERM_COLS[start]):int(_PERM_COLS[start]) + i - start])
            start = i
    return jnp.concatenate(pieces, axis=-1)


def _silu(x):
    return x * jax.nn.sigmoid(x)


def _softplus(x):
    return jnp.maximum(x, 0.0) + jnp.log1p(jnp.exp(-jnp.abs(x)))


def _dot(a, b, precision=None):
    return jnp.dot(a, b, preferred_element_type=F32, precision=precision)


def _dot_nt(a, b, precision=None):
    return lax.dot_general(a, b, (((1,), (1,)), ((), ())), preferred_element_type=F32,
                           precision=precision)


def _dot_tn(a, b, precision=None):
    return lax.dot_general(a, b, (((0,), (0,)), ((), ())), preferred_element_type=F32,
                           precision=precision)


def _cumsum_rows(x):
    c = x.shape[0]
    row = lax.broadcasted_iota(jnp.int32, x.shape, 0)
    s = 1
    while s < c:
        x = x + jnp.where(row >= s, pltpu.roll(x, s, axis=0), 0.0)
        s *= 2
    return x


def _mm(a, b):
    return _dot(a.astype(BF16), b.astype(BF16))


def _mm_nt(a, b):
    return _dot_nt(a.astype(BF16), b.astype(BF16))


def _split(a):
    hi = a.astype(BF16)
    return hi, (a - hi.astype(F32)).astype(BF16)


def _mm3(a, b):
    (ah, al), (bh, bl) = a, b
    return _dot(ah, bh) + (_dot(ah, bl) + _dot(al, bh))


def _lane_spread(vals, lane0, n_heads):
    width = n_heads * GDN_DV
    r = lax.broadcasted_iota(jnp.int32, (LANES, width), 0)
    cidx = lax.broadcasted_iota(jnp.int32, (LANES, width), 1)
    pick = (r == lane0 + cidx // GDN_DV).astype(BF16)
    p1 = vals.astype(BF16)
    r1 = vals - p1.astype(F32)
    p2 = r1.astype(BF16)
    p3 = (r1 - p2.astype(F32)).astype(BF16)
    return _dot(p1, pick) + (_dot(p2, pick) + _dot(p3, pick))


def _head_sums(x):
    r = lax.broadcasted_iota(jnp.int32, (LANES, LANES), 0) // GDN_DV
    cidx = lax.broadcasted_iota(jnp.int32, (LANES, LANES), 1) // GDN_DV
    ones = (r == cidx).astype(BF16)
    hi, lo = _split(x)
    cols = []
    for g in range(x.shape[1] // LANES):
        sl = slice(g * LANES, (g + 1) * LANES)
        cols.append(_dot(hi[:, sl], ones) + _dot(lo[:, sl], ones))
    return jnp.concatenate(cols, axis=1)


def _transpose_rows(vals):
    c = vals.shape[0]
    if c < LANES:
        vals = jnp.concatenate([vals, jnp.zeros((LANES - c, LANES), vals.dtype)], axis=0)
    return vals.T[:, :c]


def _lane_vector(vals, start):
    return jnp.zeros((1, LANES), F32).at[0, start:start + vals.shape[0]].set(vals.astype(F32))


def _params(sem):
    return pltpu.CompilerParams(dimension_semantics=sem, vmem_limit_bytes=VMEM_LIMIT)


def _ada_kernel(c_ref, w_ref, b_ref, o_ref):
    s = _silu(c_ref[...])
    o_ref[0] = _dot(s.astype(BF16), w_ref[0].astype(BF16)) + b_ref[0]


def _ada(c_all, w_ada, b_ada):
    rows = c_all.shape[0]
    n = w_ada.shape[2]
    tn = 1536
    return pl.pallas_call(
        _ada_kernel,
        grid=(DEPTH, n // tn),
        in_specs=[pl.BlockSpec((rows, D_MODEL), lambda l, j: (0, 0)),
                  pl.BlockSpec((1, D_MODEL, tn), lambda l, j: (l, 0, j)),
                  pl.BlockSpec((1, 1, tn), lambda l, j: (l, 0, j))],
        out_specs=pl.BlockSpec((1, rows, tn), lambda l, j: (l, 0, j)),
        out_shape=jax.ShapeDtypeStruct((DEPTH, rows, n), F32),
        compiler_params=_params(("arbitrary", "arbitrary")),
        name="ada",
    )(c_all, w_ada, b_ada.reshape(DEPTH, 1, n))


def _rotate(x, cos, sin_signed):
    w = x.shape[-1]
    lane = lax.broadcasted_iota(jnp.int32, x.shape, x.ndim - 1)
    first = (lane % ATT_DH) < (ATT_DH // 2)
    swapped = jnp.where(first, pltpu.roll(x, w - ATT_DH // 2, axis=x.ndim - 1),
                        pltpu.roll(x, ATT_DH // 2, axis=x.ndim - 1))
    return x * cos + swapped * sin_signed


def _inproj_kernel(x_ref, mod_ref, g_ref, w_ref, cos_ref, sin_ref, *out_refs, nb, tt, emit_keys):
    x = x_ref[...]
    ms = jnp.mean(x * x, axis=-1, keepdims=True)
    xn = x * lax.rsqrt(ms + EPS) * g_ref[...]
    mod = mod_ref[...]
    h = xn * (1.0 + mod[:, 1:2, :]) + mod[:, 0:1, :]
    u = _dot(h.reshape(nb * tt, D_MODEL).astype(BF16), w_ref[...])
    cos = cos_ref[...]
    sin = sin_ref[...]
    off = 0
    segs = {}
    for (name, width), o_ref in zip(OUT_SEGS, out_refs):
        seg = u[:, off:off + width]
        if name in ("q", "k", "qi", "kis"):
            t0 = ATT_W if name == "kis" else 0
            c3 = cos[:, t0:t0 + width][None]
            s3 = sin[:, t0:t0 + width][None]
            if nb > 1:
                c3 = jnp.broadcast_to(c3, (nb, tt, width)).reshape(nb * tt, width)
                s3 = jnp.broadcast_to(s3, (nb, tt, width)).reshape(nb * tt, width)
            else:
                c3 = c3[0]
                s3 = s3[0]
            seg = _rotate(seg, c3, s3)
        o_ref[...] = seg.reshape(nb, tt, width)
        segs[name] = seg
        off += width
    out_refs[len(OUT_SEGS)][...] = segs["kis"][:, :IDX_DH].reshape(nb, tt, IDX_DH)
    if emit_keys:
        k_b_ref, ki_b_ref, vt_b_ref = out_refs[len(OUT_SEGS) + 1:]
        k_b_ref[0] = segs["k"].astype(BF16)
        ki_b_ref[0] = segs["kis"][:, :IDX_DH].astype(BF16)
        vt_b_ref[0] = segs["v"].T.astype(BF16)


def _inproj(x, mod, g1, w_perm, cos, sin, nb, tt, emit_keys):
    B, T, _ = x.shape
    grid = (B // nb, T // tt)
    out_shape = tuple(jax.ShapeDtypeStruct((B, T, w), F32) for _, w in OUT_SEGS)
    out_specs = tuple(pl.BlockSpec((nb, tt, w), lambda b, t: (b, t, 0)) for _, w in OUT_SEGS)
    out_shape += (jax.ShapeDtypeStruct((B, T, IDX_DH), F32),)
    out_specs += (pl.BlockSpec((nb, tt, IDX_DH), lambda b, t: (b, t, 0)),)
    if emit_keys:
        assert nb == 1
        out_shape += (jax.ShapeDtypeStruct((B, T, ATT_W), BF16), jax.ShapeDtypeStruct((B, T, IDX_DH), BF16),
                      jax.ShapeDtypeStruct((B, ATT_W, T), BF16))
        out_specs += (pl.BlockSpec((1, tt, ATT_W), lambda b, t: (b, t, 0)),
                      pl.BlockSpec((1, tt, IDX_DH), lambda b, t: (b, t, 0)),
                      pl.BlockSpec((1, ATT_W, tt), lambda b, t: (b, 0, t)))
    return pl.pallas_call(
        functools.partial(_inproj_kernel, nb=nb, tt=tt, emit_keys=emit_keys),
        grid=grid,
        in_specs=[pl.BlockSpec((nb, tt, D_MODEL), lambda b, t: (b, t, 0)),
                  pl.BlockSpec((nb, 6, D_MODEL), lambda b, t: (b, 0, 0)),
                  pl.BlockSpec((1, D_MODEL), lambda b, t: (0, 0)),
                  pl.BlockSpec((D_MODEL, PERM_W), lambda b, t: (0, 0)),
                  pl.BlockSpec((tt, ROT_W), lambda b, t: (t, 0)),
                  pl.BlockSpec((tt, ROT_W), lambda b, t: (t, 0))],
        out_specs=out_specs,
        out_shape=out_shape,
        compiler_params=_params(("arbitrary", "arbitrary")),
        name="inproj",
    )(x, mod, g1, w_perm, cos, sin)


def _short_conv(ext_ref, bi, u, w, c):
    lo = SUBLANES - (CONV_W - 1)
    ext_ref[bi, SUBLANES:SUBLANES + c, :] = u
    y = ext_ref[bi, lo:lo + c, :] * w[0:1, :]
    for j in range(1, CONV_W):
        y = y + ext_ref[bi, lo + j:lo + j + c, :] * w[j:j + 1, :]
    tail = ext_ref[bi, c + lo:c + SUBLANES, :]
    ext_ref[bi, lo:SUBLANES, :] = tail
    return y, tail


def _neumann_inverses(mats, c):
    eye = (lax.broadcasted_iota(jnp.int32, (c, c), 0)
           == lax.broadcasted_iota(jnp.int32, (c, c), 1)).astype(F32)
    ps = [eye - a for a in mats]
    sp = [_split(a) for a in mats]
    pws = [_mm3(s, s) for s in sp]
    n = 2
    while n < c:
        sp = [_split(pw) for pw in pws]
        ps = [p + _mm3(_split(p), s) for p, s in zip(ps, sp)]
        n *= 2
        if n < c:
            pws = [_mm3(s, s) for s in sp]
    return ps


def _gdn_kernel(qkv_ref, kis_ref, gate_ref, w_ref, prev_ref, s0_ref, alog_ref, dtb_ref, ng_ref,
                o_ref, new_ref, s_ref, ext_ref, s_sc, *, c, nbb):
    j = pl.program_id(1)
    lo = SUBLANES - (CONV_W - 1)

    @pl.when(j == 0)
    def _():
        s_sc[...] = s0_ref[...]
        ext_ref[:, lo:SUBLANES, :] = prev_ref[...]

    ng = ng_ref[...]
    w_conv = w_ref[...]
    ri = lax.broadcasted_iota(jnp.int32, (c, c), 0)
    ci = lax.broadcasted_iota(jnp.int32, (c, c), 1)
    incl = ri >= ci
    strict = ri > ci
    nkd = GDN_HEADS * GDN_DK
    ch = []
    for bi in range(nbb):
        y, tail = _short_conv(ext_ref, bi, qkv_ref[bi], w_conv, c)
        new_ref[bi] = tail
        y = _silu(y)
        small = kis_ref[bi]
        g = -jnp.exp(alog_ref[...]) * _softplus(small + dtb_ref[...])
        beta = jax.nn.sigmoid(small)
        gc = _cumsum_rows(g)
        gc_t = _transpose_rows(gc)
        qk_n = y[:, :2 * nkd]
        qk_n = qk_n * lax.rsqrt(_head_sums(qk_n * qk_n) + EPS)
        for h in range(GDN_HEADS):
            qh = qk_n[:, h * GDN_DK:(h + 1) * GDN_DK] * (GDN_DK ** -0.5)
            kh = qk_n[:, nkd + h * GDN_DK:nkd + (h + 1) * GDN_DK]
            vh = y[:, 2 * nkd + h * GDN_DV:2 * nkd + (h + 1) * GDN_DV]
            ch.append((qh, kh, vh, gc[:, KIS_A + h:KIS_A + h + 1], gc_t[KIS_A + h:KIS_A + h + 1, :],
                       beta[:, KIS_B + h:KIS_B + h + 1]))
    ids = [(bi, h) for bi in range(nbb) for h in range(GDN_HEADS)]
    decay = [jnp.where(incl, jnp.exp(jnp.where(incl, gcol - grow, 0.0)), 0.0)
             for (_, _, _, gcol, grow, _) in ch]
    kb = [kh * bcol for (_, kh, _, _, _, bcol) in ch]
    eg = [jnp.exp(gcol) for (_, _, _, gcol, _, _) in ch]
    grams = [_mm_nt(jnp.concatenate([kbi, x[0]], axis=0), x[1]) for kbi, x in zip(kb, ch)]
    t_inv = _neumann_inverses([jnp.where(strict, m[:c] * d, 0.0) for m, d in zip(grams, decay)], c)
    attn = [jnp.where(incl, m[c:] * d, 0.0) for m, d in zip(grams, decay)]
    rhs = [jnp.concatenate([x[2] * x[5], kbi * e], axis=1) for x, kbi, e in zip(ch, kb, eg)]
    sol = [_mm(t, r) for t, r in zip(t_inv, rhs)]
    st = [s_sc[bi, h] for bi, h in ids]
    ws = [_mm(sl[:, GDN_DV:], s) for sl, s in zip(sol, st)]
    qs = [_mm(x[0] * e, s) for x, e, s in zip(ch, eg, st)]
    v_new = [sl[:, :GDN_DV] - w for sl, w in zip(sol, ws)]
    av = [_mm(a, v) for a, v in zip(attn, v_new)]
    g_last = [x[3][c - 1:c, :] for x in ch]
    kv = [_dot_tn(x[1] * jnp.exp(gl - x[3]), v) for x, gl, v in zip(ch, g_last, v_new)]
    for (bi, h), s, gl, kvi in zip(ids, st, g_last, kv):
        s_new = s * jnp.exp(gl) + kvi
        s_sc[bi, h] = s_new
        s_ref[bi, h] = s_new
    for bi in range(nbb):
        o = jnp.concatenate([qs[bi * GDN_HEADS + h] + av[bi * GDN_HEADS + h] for h in range(GDN_HEADS)], axis=1)
        o = o * lax.rsqrt(_head_sums(o * o) * (1.0 / GDN_DV) + EPS) * ng
        o_ref[bi] = o * _silu(gate_ref[bi])


def _gdn(gqkv, kis, ggate, conv_w, conv_prev, s0, a_log, dt_bias, norm_g, nbb):
    B, T, _ = gqkv.shape
    c = math.gcd(T, CHUNK)
    return pl.pallas_call(
        functools.partial(_gdn_kernel, c=c, nbb=nbb),
        grid=(B // nbb, T // c),
        in_specs=[pl.BlockSpec((nbb, c, GDN_CONV_C), lambda b, j: (b, j, 0)),
                  pl.BlockSpec((nbb, c, LANES), lambda b, j: (b, j, 0)),
                  pl.BlockSpec((nbb, c, GDN_W), lambda b, j: (b, j, 0)),
                  pl.BlockSpec((CONV_W, GDN_CONV_C), lambda b, j: (0, 0)),
                  pl.BlockSpec((nbb, CONV_W - 1, GDN_CONV_C), lambda b, j: (b, 0, 0)),
                  pl.BlockSpec((nbb, GDN_HEADS, GDN_DK, GDN_DV), lambda b, j: (b, 0, 0, 0)),
                  pl.BlockSpec((1, LANES), lambda b, j: (0, 0)),
                  pl.BlockSpec((1, LANES), lambda b, j: (0, 0)),
                  pl.BlockSpec((1, GDN_W), lambda b, j: (0, 0))],
        out_specs=(pl.BlockSpec((nbb, c, GDN_W), lambda b, j: (b, j, 0)),
                   pl.BlockSpec((nbb, CONV_W - 1, GDN_CONV_C), lambda b, j: (b, 0, 0)),
                   pl.BlockSpec((nbb, GDN_HEADS, GDN_DK, GDN_DV), lambda b, j: (b, 0, 0, 0))),
        out_shape=(jax.ShapeDtypeStruct((B, T, GDN_W), F32),
                   jax.ShapeDtypeStruct((B, CONV_W - 1, GDN_CONV_C), F32),
                   jax.ShapeDtypeStruct((B, GDN_HEADS, GDN_DK, GDN_DV), F32)),
        scratch_shapes=[pltpu.VMEM((nbb, SUBLANES + c, GDN_CONV_C), F32),
                        pltpu.VMEM((nbb, GDN_HEADS, GDN_DK, GDN_DV), F32)],
        compiler_params=_params(("arbitrary", "arbitrary")),
        name="gdn",
    )(gqkv, kis, ggate, conv_w, conv_prev, s0, _lane_vector(a_log, KIS_A), _lane_vector(dt_bias, KIS_A),
      jnp.tile(norm_g, GDN_HEADS).reshape(1, -1))


def _ssd_kernel(xbc_ref, z_ref, kis_ref, w_ref, cb_ref, prev_ref, h0_ref, alog_ref, dtb_ref, dsk_ref,
                ng_ref, o_ref, new_ref, h_ref, ext_ref, h_sc, *, c, nbb):
    j = pl.program_id(1)
    lo = SUBLANES - (CONV_W - 1)

    @pl.when(j == 0)
    def _():
        h_sc[...] = h0_ref[...]
        ext_ref[:, lo:SUBLANES, :] = prev_ref[...]

    dsk = dsk_ref[...]
    ng = ng_ref[...]
    w_conv = w_ref[...]
    ri = lax.broadcasted_iota(jnp.int32, (c, c), 0)
    ci = lax.broadcasted_iota(jnp.int32, (c, c), 1)
    incl = ri >= ci
    gn = SSM_GROUPS * SSM_N
    rep = SSM_HEADS // SSM_GROUPS
    gw = SSM_W // SSM_GROUPS
    grp_in = []
    ch = []
    slot = []
    ids = []
    for bi in range(nbb):
        y, tail = _short_conv(ext_ref, bi, xbc_ref[bi], w_conv, c)
        new_ref[bi] = tail
        y = _silu(y + cb_ref[...])
        dts = _softplus(kis_ref[bi] + dtb_ref[...])
        a = dts * (-jnp.exp(alog_ref[...]))
        acum = _cumsum_rows(a)
        acum_t = _transpose_rows(acum)
        dtm = _lane_spread(dts, KIS_DT, SSM_HEADS)
        am = _lane_spread(acum, KIS_DT, SSM_HEADS)
        a_last = am[c - 1:c, :]
        xs = y[:, :SSM_W]
        xdt = xs * dtm
        xw = xs * (jnp.exp(a_last - am) * dtm)
        h_dec = jnp.exp(a_last)
        slot.append((xs, jnp.exp(am), _silu(z_ref[bi])))
        for grp in range(SSM_GROUPS):
            grp_in.append((y[:, SSM_W + grp * SSM_N:SSM_W + (grp + 1) * SSM_N],
                           y[:, SSM_W + gn + grp * SSM_N:SSM_W + gn + (grp + 1) * SSM_N]))
            for hh in range(rep):
                h = grp * rep + hh
                hsl = slice(h * SSM_P, (h + 1) * SSM_P)
                ch.append((xdt[:, hsl], xw[:, hsl], am[:, h * SSM_P:h * SSM_P + c], acum_t[KIS_DT + h:KIS_DT + h + 1, :],
                           jnp.concatenate([h_dec[:, hsl]] * (SSM_N // SSM_P), axis=1), len(grp_in) - 1))
                ids.append((bi, h))
    cb = [_mm_nt(cg, bg) for bg, cg in grp_in]
    seg = [jnp.where(incl, jnp.exp(jnp.where(incl, x[2] - x[3], 0.0)), 0.0) for x in ch]
    hs = [h_sc[bi, h] for bi, h in ids]
    y_in = [_mm(cb[x[5]] * sg, x[0]) for x, sg in zip(ch, seg)]
    y_st = [_mm_nt(grp_in[x[5]][1], hst) for x, hst in zip(ch, hs)]
    upd = [_dot_tn(x[1], grp_in[x[5]][0]) for x in ch]
    for (bi, h), x, hst, up in zip(ids, ch, hs, upd):
        h_new = hst * x[4] + up
        h_sc[bi, h] = h_new
        h_ref[bi, h] = h_new
    for bi, (xs, ea, zg) in enumerate(slot):
        lo_c, hi_c = bi * SSM_HEADS, (bi + 1) * SSM_HEADS
        yy = (jnp.concatenate(y_in[lo_c:hi_c], axis=1) + jnp.concatenate(y_st[lo_c:hi_c], axis=1) * ea
              + dsk * xs) * zg
        for grp in range(SSM_GROUPS):
            yg = yy[:, grp * gw:(grp + 1) * gw]
            yg = yg * lax.rsqrt(jnp.mean(yg * yg, axis=-1, keepdims=True) + EPS)
            o_ref[bi, :, grp * gw:(grp + 1) * gw] = yg * ng[:, grp * gw:(grp + 1) * gw]


def _ssd(xbc, z, kis, conv_w, conv_b, conv_prev, h0, a_log, dt_bias, d_skip, norm_g, nbb):
    B, T, _ = xbc.shape
    c = math.gcd(T, CHUNK)
    return pl.pallas_call(
        functools.partial(_ssd_kernel, c=c, nbb=nbb),
        grid=(B // nbb, T // c),
        in_specs=[pl.BlockSpec((nbb, c, SSM_CONV_C), lambda b, j: (b, j, 0)),
                  pl.BlockSpec((nbb, c, SSM_W), lambda b, j: (b, j, 0)),
                  pl.BlockSpec((nbb, c, LANES), lambda b, j: (b, j, 0)),
                  pl.BlockSpec((CONV_W, SSM_CONV_C), lambda b, j: (0, 0)),
                  pl.BlockSpec((1, SSM_CONV_C), lambda b, j: (0, 0)),
                  pl.BlockSpec((nbb, CONV_W - 1, SSM_CONV_C), lambda b, j: (b, 0, 0)),
                  pl.BlockSpec((nbb, SSM_HEADS, SSM_P, SSM_N), lambda b, j: (b, 0, 0, 0)),
                  pl.BlockSpec((1, LANES), lambda b, j: (0, 0)),
                  pl.BlockSpec((1, LANES), lambda b, j: (0, 0)),
                  pl.BlockSpec((1, SSM_W), lambda b, j: (0, 0)),
                  pl.BlockSpec((1, SSM_W), lambda b, j: (0, 0))],
        out_specs=(pl.BlockSpec((nbb, c, SSM_W), lambda b, j: (b, j, 0)),
                   pl.BlockSpec((nbb, CONV_W - 1, SSM_CONV_C), lambda b, j: (b, 0, 0)),
                   pl.BlockSpec((nbb, SSM_HEADS, SSM_P, SSM_N), lambda b, j: (b, 0, 0, 0))),
        out_shape=(jax.ShapeDtypeStruct((B, T, SSM_W), F32),
                   jax.ShapeDtypeStruct((B, CONV_W - 1, SSM_CONV_C), F32),
                   jax.ShapeDtypeStruct((B, SSM_HEADS, SSM_P, SSM_N), F32)),
        scratch_shapes=[pltpu.VMEM((nbb, SUBLANES + c, SSM_CONV_C), F32),
                        pltpu.VMEM((nbb, SSM_HEADS, SSM_P, SSM_N), F32)],
        compiler_params=_params(("arbitrary", "arbitrary")),
        name="ssd",
    )(xbc, z, kis, conv_w, conv_b.reshape(1, -1), conv_prev, h0, _lane_vector(a_log, KIS_DT),
      _lane_vector(dt_bias, KIS_DT), jnp.repeat(d_skip, SSM_P).reshape(1, -1), norm_g.reshape(1, -1))


IDX_BITS = 12


OPENING_PROBES = 14
ROUND_PROBES = 2
SEARCH_ROUNDS = 4096
FOLD_ROWS = 128


def _fold_keys(x, op, axis=0):
    s = x.shape[0]
    if axis == 0 and s % FOLD_ROWS == 0 and s > FOLD_ROWS:
        x = op(x.reshape(s // FOLD_ROWS, FOLD_ROWS, x.shape[1]), axis=0)
    return op(x, axis=axis, keepdims=True)


def _fold_rows(x, op):
    return _fold_keys(x, op, 0)


def _select_topk(score, adm, pos, skip, topk, axis):
    def count(mask):
        return _fold_keys(mask.astype(F32), jnp.sum, axis)

    s = jnp.where(adm, score, -jnp.inf)

    few = (count(adm) <= topk) | skip
    smax = _fold_keys(s, jnp.max, axis)
    smin = _fold_keys(jnp.where(adm, score, jnp.inf), jnp.min, axis)
    top_full = count(s >= smax) >= topk
    lo0 = jnp.where(few, 0.0, jnp.where(top_full, smax, smin))
    hi0 = jnp.where(few, 0.0, smax)
    done0 = (few | top_full).astype(F32)

    def probe(lo, hi, done):
        mid = 0.5 * lo + 0.5 * hi
        ok = (mid > lo) & (mid < hi) & (done < 0.5)
        take = count(s >= mid) >= topk
        return jnp.where(ok & take, mid, lo), jnp.where(ok & jnp.logical_not(take), mid, hi)

    def opening(lo, hi):
        for _ in range(OPENING_PROBES):
            lo, hi = probe(lo, hi, done0)
        return lo, hi

    lo1, hi1 = lax.cond(jnp.min(done0) < 0.5, opening, lambda lo, hi: (lo, hi), lo0, hi0)

    def search_cond(c):
        return (jnp.min(c[2]) < 0.5) & (c[3] < SEARCH_ROUNDS)

    def search_body(c):
        lo, hi, done, it = c
        last = _fold_keys(jnp.where(s < hi, s, -jnp.inf), jnp.max, axis)
        reached = count(s >= last) >= topk
        active = done < 0.5
        lo = jnp.where(active & reached, last, lo)
        hi = jnp.where(active & jnp.logical_not(reached), last, hi)
        done = jnp.maximum(done, reached.astype(F32))
        for _ in range(ROUND_PROBES):
            lo, hi = probe(lo, hi, done)
        return lo, hi, done, it + 1

    lo, _, _, _ = lax.while_loop(search_cond, search_body, (lo1, hi1, done0, jnp.int32(0)))
    t = jnp.where(few, -jnp.inf, lo)
    gt = s > t
    tie = s == t
    need = topk - count(gt)
    excess = jnp.where(few, 0.0, count(tie) - need)

    def tie_search():
        def ibody(i, m):
            cand = m | (jnp.int32(1) << (IDX_BITS - 1 - i))
            return jnp.where(count(tie & (pos < cand)) <= need, cand, m)

        return lax.fori_loop(0, IDX_BITS, ibody, jnp.zeros(few.shape, jnp.int32))

    m = lax.cond(jnp.max(excess) > 0.0, tie_search, lambda: jnp.full(few.shape, 2 ** IDX_BITS - 1, jnp.int32))
    return adm & (gt | (tie & (pos < m)))


def _dsa_kernel(q_ref, qi_ref, kis_ref, k_ref, vt_ref, ki_ref, o_ref, *, tq, s_len, s_valid, q_offset, j0,
                n_valid_q, topk):
    j = pl.program_id(1)
    lane = lax.broadcasted_iota(jnp.int32, (1, tq), 1)
    qpos = q_offset + (j0 + j) * tq + lane
    lim = jnp.minimum((qpos // CHUNK + 1) * CHUNK, s_valid)
    row = lax.broadcasted_iota(jnp.int32, (s_len, tq), 0)
    adm = row < lim

    qi_t = qi_ref[0].T.astype(BF16)
    kis_t = kis_ref[0].T
    ki = ki_ref[0]
    assert IDX_DH == 64 and IDX_HEADS == 4 and ATT_DH == 64
    wi = kis_t[KIS_WI:KIS_WI + IDX_HEADS, :] * (IDX_DH ** -0.5 * IDX_HEADS ** -0.5)
    qi_all = jnp.concatenate([qi_t[h * IDX_DH:(h + 1) * IDX_DH, :] for h in range(IDX_HEADS)], axis=1)
    rel_all = _dot(ki, qi_all)
    score = jnp.zeros((s_len, tq), F32)
    for h in range(IDX_HEADS):
        score = score + jnp.maximum(rel_all[:, h * tq:(h + 1) * tq], 0.0) * wi[h:h + 1, :]
    sel = _select_topk(score, adm, row, lane >= n_valid_q, topk, 0)

    q_t = q_ref[0].T * (ATT_DH ** -0.5)
    pair = LANES // ATT_DH
    head_in_pair = lax.broadcasted_iota(jnp.int32, (LANES, tq), 0) // ATT_DH
    logits = []
    for g in range(ATT_HEADS // pair):
        qg = q_t[g * LANES:(g + 1) * LANES, :]
        qm = jnp.concatenate([jnp.where(head_in_pair == i, qg, 0.0) for i in range(pair)], axis=1)
        lg = _dot(k_ref[0, :, g * LANES:(g + 1) * LANES], qm.astype(BF16))
        logits += [lg[:, i * tq:(i + 1) * tq] for i in range(pair)]
    probs, scales = [], []
    for lg in logits:
        lg = jnp.where(sel, lg, -jnp.inf)
        p = jnp.exp(lg - _fold_rows(lg, jnp.max))
        scales.append(1.0 / _fold_rows(p, jnp.sum))
        probs.append(p.astype(BF16))
    outs = [_dot(vt_ref[0, h * ATT_DH:(h + 1) * ATT_DH, :], probs[h]) * scales[h] for h in range(ATT_HEADS)]
    o_ref[0] = jnp.concatenate(outs, axis=0).T


def _dsa(q, qi, kis, k_b, vt_b, ki_b, *, tq, j0, nj, s_len, s_valid, q_offset, n_valid_q, topk):
    B = q.shape[0]
    assert s_len < 2 ** IDX_BITS and s_len <= k_b.shape[1]
    return pl.pallas_call(
        functools.partial(_dsa_kernel, tq=tq, s_len=s_len, s_valid=s_valid, q_offset=q_offset, j0=j0,
                          n_valid_q=n_valid_q, topk=topk),
        grid=(B, nj),
        in_specs=[pl.BlockSpec((1, tq, ATT_W), lambda b, j: (b, j0 + j, 0)),
                  pl.BlockSpec((1, tq, IDX_HEADS * IDX_DH), lambda b, j: (b, j0 + j, 0)),
                  pl.BlockSpec((1, tq, LANES), lambda b, j: (b, j0 + j, 0)),
                  pl.BlockSpec((1, s_len, ATT_W), lambda b, j: (b, 0, 0)),
                  pl.BlockSpec((1, ATT_W, s_len), lambda b, j: (b, 0, 0)),
                  pl.BlockSpec((1, s_len, IDX_DH), lambda b, j: (b, 0, 0))],
        out_specs=pl.BlockSpec((1, tq, ATT_W), lambda b, j: (b, j, 0)),
        out_shape=jax.ShapeDtypeStruct((B, nj * tq, ATT_W), F32),
        compiler_params=_params(("arbitrary", "arbitrary")),
        name="dsa",
    )(q, qi, kis, k_b, vt_b, ki_b)


def _dsa_rows_kernel(q_ref, qi_ref, kis_ref, k_ref, v_ref, ck_ref, cv_ref, cki_ref, o_ref, *, nbq, tq, s_valid,
                     q_offset, topk):
    rows = nbq * tq
    s_len = q_offset + LANES
    qpos = q_offset + lax.broadcasted_iota(jnp.int32, (rows, 1), 0) % tq
    lim = jnp.minimum((qpos // CHUNK + 1) * CHUNK, s_valid)
    col = lax.broadcasted_iota(jnp.int32, (rows, s_len), 1)
    adm = col < lim

    def with_new(cache, new):
        return jnp.concatenate([cache.astype(BF16), new.astype(BF16),
                                jnp.zeros((LANES - tq, new.shape[1]), BF16)], axis=0)

    scores = []
    for b in range(nbq):
        qi = qi_ref[b]
        qi_st = jnp.concatenate([qi[:, h * IDX_DH:(h + 1) * IDX_DH] for h in range(IDX_HEADS)], axis=0)
        rel = _dot_nt(qi_st.astype(BF16), with_new(cki_ref[b], kis_ref[b][:, :IDX_DH]))
        wi = kis_ref[b][:, KIS_WI:KIS_WI + IDX_HEADS] * (IDX_DH ** -0.5 * IDX_HEADS ** -0.5)
        sc = jnp.maximum(rel[:tq, :], 0.0) * wi[:, 0:1]
        for h in range(1, IDX_HEADS):
            sc = sc + jnp.maximum(rel[h * tq:(h + 1) * tq, :], 0.0) * wi[:, h:h + 1]
        scores.append(sc)
    sel = _select_topk(jnp.concatenate(scores, axis=0), adm, col, qpos < 0, topk, 1)

    head_of_lane = lax.broadcasted_iota(jnp.int32, (tq, ATT_W), 1) // ATT_DH
    for b in range(nbq):
        q = q_ref[b] * (ATT_DH ** -0.5)
        q_st = jnp.concatenate([jnp.where(head_of_lane == h, q, 0.0) for h in range(ATT_HEADS)], axis=0)
        logits = _dot_nt(q_st.astype(BF16), with_new(ck_ref[b], k_ref[b]))
        logits = jnp.where(jnp.concatenate([sel[b * tq:(b + 1) * tq, :]] * ATT_HEADS, axis=0), logits, -jnp.inf)
        p = jnp.exp(logits - jnp.max(logits, axis=1, keepdims=True))
        scale = 1.0 / jnp.sum(p, axis=1, keepdims=True)
        pv = _dot(p.astype(BF16), with_new(cv_ref[b], v_ref[b])) * scale
        out = jnp.zeros((tq, ATT_W), F32)
        for h in range(ATT_HEADS):
            out = jnp.where(head_of_lane == h, pv[h * tq:(h + 1) * tq, :], out)
        o_ref[b] = out


def _dsa_rows(q, qi, kis, k, v, ck, cv, cki, *, nbq, topk):
    B, T, _ = q.shape
    P = ck.shape[1]
    assert P + LANES < 2 ** IDX_BITS and T % (2 * SUBLANES) == 0 and T <= LANES and P % (2 * SUBLANES) == 0
    assert B % nbq == 0
    new_spec = lambda w: pl.BlockSpec((nbq, T, w), lambda b: (b, 0, 0))
    cache_spec = lambda w: pl.BlockSpec((nbq, P, w), lambda b: (b, 0, 0))
    return pl.pallas_call(
        functools.partial(_dsa_rows_kernel, nbq=nbq, tq=T, s_valid=P + T, q_offset=P, topk=topk),
        grid=(B // nbq,),
        in_specs=[new_spec(ATT_W), new_spec(IDX_HEADS * IDX_DH), new_spec(LANES), new_spec(ATT_W), new_spec(ATT_W),
                  cache_spec(ATT_W), cache_spec(ATT_W), cache_spec(IDX_DH)],
        out_specs=new_spec(ATT_W),
        out_shape=jax.ShapeDtypeStruct((B, T, ATT_W), F32),
        compiler_params=_params(("arbitrary",)),
        name="dsa_rows",
    )(q, qi, kis, k, v, ck, cv, cki)


def _dsa_group(q, qi, kis, k_b, vt_b, ki_b, *, s_valid, q_offset, topk):
    B, T, _ = q.shape
    tq = 2 * LANES if T % (2 * LANES) == 0 else LANES
    n_valid_q = min(T, tq)
    if T < tq:
        padq = lambda a: jnp.pad(a, ((0, 0), (0, tq - T), (0, 0)))
        q, qi, kis = padq(q), padq(qi), padq(kis)
    nq = q.shape[1] // tq
    s_total = k_b.shape[1]
    outs = []
    j0 = 0
    while j0 < nq:
        nj = 1
        reach = -(-(q_offset + (j0 + nj) * tq) // CHUNK) * CHUNK
        s_len = min(s_total, -(-min(reach, s_valid) // LANES) * LANES)
        outs.append(_dsa(q, qi, kis, k_b, vt_b, ki_b, tq=tq, j0=j0, nj=nj, s_len=s_len, s_valid=s_valid,
                         q_offset=q_offset, n_valid_q=n_valid_q, topk=topk))
        j0 += nj
    out = outs[0] if len(outs) == 1 else jnp.concatenate(outs, axis=1)
    return out[:, :T]


def _ffn_kernel(x_ref, gdn_ref, att_ref, ssm_ref, mod_ref, wo_ref, g2_ref, wg_ref, wu_ref, cw_ref, cb_ref,
                wd_ref, prev_ref, fg_ref, *rest, nb, tt, nk, final):
    if final:
        xo_ref, y_ref, new_ref, h2_sc, acc_sc, buf_sc, carry_sc = rest
    else:
        xo_ref, new_ref, h2_sc, acc_sc, buf_sc, carry_sc = rest
        y_ref = None
    t = pl.program_id(1)
    k = pl.program_id(2)
    rows = nb * tt
    lo = SUBLANES - (FFN_CONV_W - 1)
    mod = mod_ref[...]

    @pl.when(k == 0)
    def _():
        mix = jnp.concatenate([gdn_ref[...], att_ref[...], ssm_ref[...]], axis=-1)
        proj = _dot(mix.reshape(rows, MIX_W).astype(BF16), wo_ref[...]).reshape(nb, tt, D_MODEL)
        x1 = x_ref[...] + mod[:, 2:3, :] * proj
        acc_sc[...] = x1
        ms = jnp.mean(x1 * x1, axis=-1, keepdims=True)
        h2 = x1 * lax.rsqrt(ms + EPS) * g2_ref[...]
        h2 = h2 * (1.0 + mod[:, 4:5, :]) + mod[:, 3:4, :]
        h2_sc[...] = h2.reshape(rows, D_MODEL).astype(BF16)

    @pl.when(t == 0)
    def _():
        carry_sc[k, :, lo:SUBLANES, :] = prev_ref[...]

    h2 = h2_sc[...]
    ag = _dot(h2, wg_ref[...])
    up = _dot(h2, wu_ref[...])
    tf = ag.shape[-1]
    buf_sc[:, lo:SUBLANES, :] = carry_sc[k, :, lo:SUBLANES, :]
    buf_sc[:, SUBLANES:SUBLANES + tt, :] = ag.reshape(nb, tt, tf)
    cw = cw_ref[...]
    conv = buf_sc[:, lo:lo + tt, :] * cw[0:1, :][None]
    for jj in range(1, FFN_CONV_W):
        conv = conv + buf_sc[:, lo + jj:lo + jj + tt, :] * cw[jj:jj + 1, :][None]
    tail = buf_sc[:, tt + lo:tt + SUBLANES, :]
    carry_sc[k, :, lo:SUBLANES, :] = tail
    new_ref[:, pl.ds(k, 1), :, :] = tail[:, None]
    act = _silu(conv + cb_ref[...][None]).reshape(rows, tf) * up
    y = _dot(act.astype(BF16), wd_ref[...]).reshape(nb, tt, D_MODEL)
    acc_sc[...] += mod[:, 5:6, :] * y

    @pl.when(k == nk - 1)
    def _():
        xo = acc_sc[...]
        xo_ref[...] = xo
        if final:
            ms = jnp.mean(xo * xo, axis=-1, keepdims=True)
            y_ref[...] = xo * lax.rsqrt(ms + EPS) * fg_ref[...]


def _ffn(x, gdn_o, att_o, ssm_o, mod, w_out, g2, w_gate, w_up, conv_w, conv_b, w_down, conv_prev, final_g,
         nb, tt, tf, final):
    B, T, _ = x.shape
    nk = D_FF // tf
    grid = (B // nb, T // tt, nk)
    row_spec = lambda w: pl.BlockSpec((nb, tt, w), lambda b, t, k: (b, t, 0))
    in_specs = [row_spec(D_MODEL), row_spec(GDN_W), row_spec(ATT_W), row_spec(SSM_W),
                pl.BlockSpec((nb, 6, D_MODEL), lambda b, t, k: (b, 0, 0)),
                pl.BlockSpec((MIX_W, D_MODEL), lambda b, t, k: (0, 0)),
                pl.BlockSpec((1, D_MODEL), lambda b, t, k: (0, 0)),
                pl.BlockSpec((D_MODEL, tf), lambda b, t, k: (0, k)),
                pl.BlockSpec((D_MODEL, tf), lambda b, t, k: (0, k)),
                pl.BlockSpec((FFN_CONV_W, tf), lambda b, t, k: (0, k)),
                pl.BlockSpec((1, tf), lambda b, t, k: (0, k)),
                pl.BlockSpec((tf, D_MODEL), lambda b, t, k: (k, 0)),
                pl.BlockSpec((nb, FFN_CONV_W - 1, tf), lambda b, t, k: (b, 0, k)),
                pl.BlockSpec((1, D_MODEL), lambda b, t, k: (0, 0))]
    out_shape = [jax.ShapeDtypeStruct((B, T, D_MODEL), F32)]
    out_specs = [row_spec(D_MODEL)]
    if final:
        out_shape.append(jax.ShapeDtypeStruct((B, T, D_MODEL), F32))
        out_specs.append(row_spec(D_MODEL))
    out_shape.append(jax.ShapeDtypeStruct((B, nk, FFN_CONV_W - 1, tf), F32))
    out_specs.append(pl.BlockSpec((nb, nk, FFN_CONV_W - 1, tf), lambda b, t, k: (b, 0, 0, 0)))
    res = pl.pallas_call(
        functools.partial(_ffn_kernel, nb=nb, tt=tt, nk=nk, final=final),
        grid=grid,
        in_specs=in_specs,
        out_specs=tuple(out_specs),
        out_shape=tuple(out_shape),
        scratch_shapes=[pltpu.VMEM((nb * tt, D_MODEL), BF16),
                        pltpu.VMEM((nb, tt, D_MODEL), F32),
                        pltpu.VMEM((nb, SUBLANES + tt, tf), F32),
                        pltpu.VMEM((nk, nb, SUBLANES, tf), F32)],
        compiler_params=_params(("arbitrary", "arbitrary", "arbitrary")),
        name="ffn",
    )(x, gdn_o, att_o, ssm_o, mod, w_out, g2, w_gate, w_up, conv_w, conv_b.reshape(1, -1), w_down,
      conv_prev, final_g)
    tail = res[-1].transpose(0, 2, 1, 3).reshape(B, FFN_CONV_W - 1, D_FF)
    return (*res[:-1], tail)


def _rope_tables(pos):
    half = ATT_DH // 2
    inv_freq = ROPE_THETA ** (-jnp.arange(half, dtype=F32) / half)
    ang = pos.astype(F32)[:, None] * inv_freq[None, :]
    cos = jnp.cos(ang)
    sin = jnp.sin(ang)
    cos_h = jnp.concatenate([cos, cos], axis=-1)
    sin_h = jnp.concatenate([-sin, sin], axis=-1)
    ones = jnp.ones((pos.shape[0], LANES - ATT_DH), F32)
    cos_t = jnp.concatenate([jnp.tile(cos_h, (1, ATT_HEADS)), cos_h, ones], axis=-1)
    sin_t = jnp.concatenate([jnp.tile(sin_h, (1, ATT_HEADS)), sin_h, 0.0 * ones], axis=-1)
    return cos_t, sin_t


def _layer(x, mod, pos_tables, cache, states, lw, tiles, final, final_g):
    (norm1_g, w_perm, gdn_conv_w, gdn_A_log, gdn_dt_bias, gdn_norm_g, ssm_conv_w, ssm_conv_b, ssm_A_log,
     ssm_dt_bias, ssm_D, ssm_norm_g, w_out, norm2_g, w_gate, w_up, ffn_conv_w, ffn_conv_b, w_down) = lw
    gdn_conv_prev, gdn_s0, ssm_conv_prev, ssm_h0, ffn_conv_prev = states
    B, T, _ = x.shape
    nb, tt, nbb, ffn_tt, tf = tiles
    cos_t, sin_t = pos_tables
    proj = _inproj(x, mod, norm1_g.reshape(1, -1), w_perm, cos_t, sin_t, nb, tt, emit_keys=cache is None)
    gqkv, ggate, q, k, v, qi, kis, z, xbc = proj[:len(OUT_SEGS)]
    ki = proj[len(OUT_SEGS)]

    gdn_o, gdn_conv_new, gdn_s = _gdn(gqkv, kis, ggate, gdn_conv_w, gdn_conv_prev, gdn_s0, gdn_A_log,
                                      gdn_dt_bias, gdn_norm_g, nbb)
    ssm_o, ssm_conv_new, ssm_h = _ssd(xbc, z, kis, ssm_conv_w, ssm_conv_b, ssm_conv_prev, ssm_h0, ssm_A_log,
                                      ssm_dt_bias, ssm_D, ssm_norm_g, nbb)
    if cache is None:
        k_b, ki_b, vt_b = proj[len(OUT_SEGS) + 1:]
        att_o = _dsa_group(q, qi, kis, k_b, vt_b, ki_b, s_valid=T, q_offset=0, topk=min(TOPK_MAX, T // 4))
    else:
        ck, cv, cki = cache
        P = ck.shape[1]
        att_o = _dsa_rows(q, qi, kis, k, v, ck.reshape(B, P, ATT_W), cv.reshape(B, P, ATT_W), cki,
                          nbq=math.gcd(B, 2), topk=min(TOPK_MAX, (P + T) // 4))

    res = _ffn(x, gdn_o, att_o, ssm_o, mod, w_out, norm2_g.reshape(1, -1), w_gate, w_up, ffn_conv_w, ffn_conv_b,
               w_down, ffn_conv_prev, final_g.reshape(1, -1), ffn_tt[0], ffn_tt[1], tf, final)
    if final:
        x_new, y, ffn_conv_new = res
    else:
        (x_new, ffn_conv_new), y = res, None
    st = (k.reshape(B, T, ATT_HEADS, ATT_DH), v.reshape(B, T, ATT_HEADS, ATT_DH), ki,
          gdn_conv_new, gdn_s, ssm_conv_new, ssm_h, ffn_conv_new)
    return x_new, y, st


def kernel(x_prompt, x_sample, c_prompt, c_sample, cache_k, cache_v, cache_kidx, state_gdn_conv, state_gdn,
           state_ssm_conv, state_ssm, state_ffn_conv, w_ada, b_ada, norm1_g, w_in, gdn_conv_w, gdn_A_log,
           gdn_dt_bias, gdn_norm_g, ssm_conv_w, ssm_conv_b, ssm_A_log, ssm_dt_bias, ssm_D, ssm_norm_g, w_out,
           norm2_g, w_gate, w_up, ffn_conv_w, ffn_conv_b, w_down, final_g):
    Bp, T, _ = x_prompt.shape
    Bs, Ts, _ = x_sample.shape
    P = cache_k.shape[2]

    c_all = jnp.concatenate([c_prompt, c_sample], axis=0)
    mod_all = _ada(c_all, w_ada, b_ada).reshape(DEPTH, Bp + Bs, 6, D_MODEL)

    tables_p = _rope_tables(jnp.arange(T))
    tables_s = _rope_tables(P + jnp.arange(Ts))

    w_perm = _permute_columns(w_in.astype(BF16))
    w_out_b = w_out.astype(BF16)
    w_gate_b = w_gate.astype(BF16)
    w_up_b = w_up.astype(BF16)
    w_down_b = w_down.astype(BF16)

    zeros_p = (jnp.zeros((Bp, CONV_W - 1, GDN_CONV_C), F32),
               jnp.zeros((Bp, GDN_HEADS, GDN_DK, GDN_DV), F32),
               jnp.zeros((Bp, CONV_W - 1, SSM_CONV_C), F32),
               jnp.zeros((Bp, SSM_HEADS, SSM_P, SSM_N), F32),
               jnp.zeros((Bp, FFN_CONV_W - 1, D_FF), F32))

    tf = D_FF // 2
    tiles_p = (1, min(256, T), math.gcd(Bp, 8), (1, min(512, T)), tf)
    tiles_s = (Bs, Ts, math.gcd(Bs, 4), (Bs, Ts), tf)

    xp, xs = x_prompt, x_sample
    new_p, new_s = [], []
    yp = ys = None
    for l in range(DEPTH):
        lw = (norm1_g[l], w_perm[l], gdn_conv_w[l], gdn_A_log[l], gdn_dt_bias[l], gdn_norm_g[l], ssm_conv_w[l],
              ssm_conv_b[l], ssm_A_log[l], ssm_dt_bias[l], ssm_D[l], ssm_norm_g[l], w_out_b[l], norm2_g[l],
              w_gate_b[l], w_up_b[l], ffn_conv_w[l], ffn_conv_b[l], w_down_b[l])
        final = l == DEPTH - 1
        xp, yp, st_p = _layer(xp, mod_all[l, :Bp], tables_p, None, zeros_p, lw, tiles_p, final, final_g)
        states_s = (state_gdn_conv[l], state_gdn[l], state_ssm_conv[l], state_ssm[l], state_ffn_conv[l])
        xs, ys, st_s = _layer(xs, mod_all[l, Bp:], tables_s, (cache_k[l], cache_v[l], cache_kidx[l]), states_s,
                              lw, tiles_s, final, final_g)
        new_p.append(st_p)
        new_s.append(st_s)
    outs_p = [jnp.stack([st[i] for st in new_p]) for i in range(8)]
    outs_s = [jnp.stack([st[i] for st in new_s]) for i in range(8)]
    return (yp, ys, *outs_p, *outs_s)
```

```python
import functools
import math

import jax
import jax.numpy as jnp
import numpy as np
from jax import lax
from jax.experimental import pallas as pl
from jax.experimental.pallas import tpu as pltpu

F32 = jnp.float32
BF16 = jnp.bfloat16
HI = lax.Precision.HIGHEST

D_MODEL = 1024
DEPTH = 2
CHUNK = 64
CONV_W = 4
FFN_CONV_W = 3
D_FF = 2816
ROPE_THETA = 10000.0
EPS = 1e-6
GDN_HEADS = 4
GDN_DK = 64
GDN_DV = 64
ATT_HEADS = 4
ATT_DH = 64
IDX_HEADS = 4
IDX_DH = 64
TOPK_MAX = 256
SSM_HEADS = 8
SSM_P = 64
SSM_GROUPS = 2
SSM_N = 128
GDN_W = GDN_HEADS * GDN_DV
ATT_W = ATT_HEADS * ATT_DH
SSM_W = SSM_HEADS * SSM_P
MIX_W = GDN_W + ATT_W + SSM_W
GDN_CONV_C = 2 * GDN_HEADS * GDN_DK + GDN_W
SSM_CONV_C = SSM_W + 2 * SSM_GROUPS * SSM_N
IN_SIZES = (GDN_CONV_C, GDN_HEADS, GDN_HEADS, GDN_W,
            ATT_W, ATT_W, ATT_W, IDX_HEADS * IDX_DH, IDX_DH, IDX_HEADS,
            SSM_W, SSM_CONV_C, SSM_HEADS)
IN_W = sum(IN_SIZES)

LANES = 128
SUBLANES = 8
VMEM_LIMIT = 56 * 1024 * 1024

KIS_A = IDX_DH
KIS_B = KIS_A + GDN_HEADS
KIS_WI = KIS_B + GDN_HEADS
KIS_DT = KIS_WI + IDX_HEADS
KIS_END = KIS_DT + SSM_HEADS
OUT_SEGS = (("gqkv", GDN_CONV_C), ("ggate", GDN_W), ("q", ATT_W), ("k", ATT_W), ("v", ATT_W),
            ("qi", IDX_HEADS * IDX_DH), ("kis", LANES), ("z", SSM_W), ("xbc", SSM_CONV_C))
PERM_W = sum(w for _, w in OUT_SEGS)
ROT_W = ATT_W + LANES


def _perm_columns():
    starts = np.concatenate([[0], np.cumsum(IN_SIZES)])
    (s_gqkv, s_ga, s_gb, s_gg, s_q, s_k, s_v, s_qi, s_ki, s_wi, s_z, s_xbc, s_dt) = starts[:-1]
    cols = []
    cols += list(range(s_gqkv, s_gqkv + GDN_CONV_C))
    cols += list(range(s_gg, s_gg + GDN_W))
    cols += list(range(s_q, s_q + ATT_W))
    cols += list(range(s_k, s_k + ATT_W))
    cols += list(range(s_v, s_v + ATT_W))
    cols += list(range(s_qi, s_qi + IDX_HEADS * IDX_DH))
    kis = (list(range(s_ki, s_ki + IDX_DH)) + list(range(s_ga, s_ga + GDN_HEADS))
           + list(range(s_gb, s_gb + GDN_HEADS)) + list(range(s_wi, s_wi + IDX_HEADS))
           + list(range(s_dt, s_dt + SSM_HEADS)))
    cols += kis + [-1] * (LANES - len(kis))
    cols += list(range(s_z, s_z + SSM_W))
    cols += list(range(s_xbc, s_xbc + SSM_CONV_C))
    assert len(cols) == PERM_W
    return np.asarray(cols, np.int32)


_PERM_COLS = _perm_columns()


def _permute_columns(w_in):
    pieces = []
    start = 0
    for i in range(1, PERM_W + 1):
        prev = int(_PERM_COLS[i - 1])
        if i == PERM_W or not ((prev < 0 and _PERM_COLS[i] < 0) or (prev >= 0 and _PERM_COLS[i] == prev + 1)):
            if _PERM_COLS[start] < 0:
                pieces.append(jnp.zeros(w_in.shape[:-1] + (i - start,), w_in.dtype))
            else:
                pieces.append(w_in[..., int(_PERM_COLS[start]):int(_PERM_COLS[start]) + i - start])
            start = i
    return jnp.concatenate(pieces, axis=-1)


def _silu(x):
    return x * jax.nn.sigmoid(x)


def _softplus(x):
    return jnp.maximum(x, 0.0) + jnp.log1p(jnp.exp(-jnp.abs(x)))


def _dot(a, b, precision=None):
    return jnp.dot(a, b, preferred_element_type=F32, precision=precision)


def _dot_nt(a, b, precision=None):
    return lax.dot_general(a, b, (((1,), (1,)), ((), ())), preferred_element_type=F32,
                           precision=precision)


def _dot_tn(a, b, precision=None):
    return lax.dot_general(a, b, (((0,), (0,)), ((), ())), preferred_element_type=F32,
                           precision=precision)


def _cumsum_rows(x):
    c = x.shape[0]
    row = lax.broadcasted_iota(jnp.int32, x.shape, 0)
    s = 1
    while s < c:
        x = x + jnp.where(row >= s, pltpu.roll(x, s, axis=0), 0.0)
        s *= 2
    return x


def _mm(a, b):
    return _dot(a.astype(BF16), b.astype(BF16))


def _mm_nt(a, b):
    return _dot_nt(a.astype(BF16), b.astype(BF16))


def _split(a):
    hi = a.astype(BF16)
    return hi, (a - hi.astype(F32)).astype(BF16)


def _mm3(a, b):
    (ah, al), (bh, bl) = a, b
    return _dot(ah, bh) + (_dot(ah, bl) + _dot(al, bh))


def _lane_spread(vals, lane0, n_heads):
    width = n_heads * GDN_DV
    r = lax.broadcasted_iota(jnp.int32, (LANES, width), 0)
    cidx = lax.broadcasted_iota(jnp.int32, (LANES, width), 1)
    pick = (r == lane0 + cidx // GDN_DV).astype(BF16)
    p1 = vals.astype(BF16)
    r1 = vals - p1.astype(F32)
    p2 = r1.astype(BF16)
    p3 = (r1 - p2.astype(F32)).astype(BF16)
    return _dot(p1, pick) + (_dot(p2, pick) + _dot(p3, pick))


def _head_sums(x):
    r = lax.broadcasted_iota(jnp.int32, (LANES, LANES), 0) // GDN_DV
    cidx = lax.broadcasted_iota(jnp.int32, (LANES, LANES), 1) // GDN_DV
    ones = (r == cidx).astype(BF16)
    hi, lo = _split(x)
    cols = []
    for g in range(x.shape[1] // LANES):
        sl = slice(g * LANES, (g + 1) * LANES)
        cols.append(_dot(hi[:, sl], ones) + _dot(lo[:, sl], ones))
    return jnp.concatenate(cols, axis=1)


def _transpose_rows(vals):
    c = vals.shape[0]
    if c < LANES:
        vals = jnp.concatenate([vals, jnp.zeros((LANES - c, LANES), vals.dtype)], axis=0)
    return vals.T[:, :c]


def _lane_vector(vals, start):
    return jnp.zeros((1, LANES), F32).at[0, start:start + vals.shape[0]].set(vals.astype(F32))


def _params(sem):
    return pltpu.CompilerParams(dimension_semantics=sem, vmem_limit_bytes=VMEM_LIMIT)


def _ada_kernel(c_ref, w_ref, b_ref, o_ref):
    s = _silu(c_ref[...])
    o_ref[0] = _dot(s.astype(BF16), w_ref[0].astype(BF16)) + b_ref[0]


def _ada(c_all, w_ada, b_ada):
    rows = c_all.shape[0]
    n = w_ada.shape[2]
    tn = 1536
    return pl.pallas_call(
        _ada_kernel,
        grid=(DEPTH, n // tn),
        in_specs=[pl.BlockSpec((rows, D_MODEL), lambda l, j: (0, 0)),
                  pl.BlockSpec((1, D_MODEL, tn), lambda l, j: (l, 0, j)),
                  pl.BlockSpec((1, 1, tn), lambda l, j: (l, 0, j))],
        out_specs=pl.BlockSpec((1, rows, tn), lambda l, j: (l, 0, j)),
        out_shape=jax.ShapeDtypeStruct((DEPTH, rows, n), F32),
        compiler_params=_params(("arbitrary", "arbitrary")),
        name="ada",
    )(c_all, w_ada, b_ada.reshape(DEPTH, 1, n))


def _rotate(x, cos, sin_signed):
    w = x.shape[-1]
    lane = lax.broadcasted_iota(jnp.int32, x.shape, x.ndim - 1)
    first = (lane % ATT_DH) < (ATT_DH // 2)
    swapped = jnp.where(first, pltpu.roll(x, w - ATT_DH // 2, axis=x.ndim - 1),
                        pltpu.roll(x, ATT_DH // 2, axis=x.ndim - 1))
    return x * cos + swapped * sin_signed


def _inproj_kernel(x_ref, mod_ref, g_ref, w_ref, cos_ref, sin_ref, *out_refs, nb, tt, emit_keys):
    x = x_ref[...]
    ms = jnp.mean(x * x, axis=-1, keepdims=True)
    xn = x * lax.rsqrt(ms + EPS) * g_ref[...]
    mod = mod_ref[...]
    h = xn * (1.0 + mod[:, 1:2, :]) + mod[:, 0:1, :]
    u = _dot(h.reshape(nb * tt, D_MODEL).astype(BF16), w_ref[...])
    cos = cos_ref[...]
    sin = sin_ref[...]
    off = 0
    segs = {}
    for (name, width), o_ref in zip(OUT_SEGS, out_refs):
        seg = u[:, off:off + width]
        if name in ("q", "k", "qi", "kis"):
            t0 = ATT_W if name == "kis" else 0
            c3 = cos[:, t0:t0 + width][None]
            s3 = sin[:, t0:t0 + width][None]
            if nb > 1:
                c3 = jnp.broadcast_to(c3, (nb, tt, width)).reshape(nb * tt, width)
                s3 = jnp.broadcast_to(s3, (nb, tt, width)).reshape(nb * tt, width)
            else:
                c3 = c3[0]
                s3 = s3[0]
            seg = _rotate(seg, c3, s3)
        o_ref[...] = seg.reshape(nb, tt, width)
        segs[name] = seg
        off += width
    out_refs[len(OUT_SEGS)][...] = segs["kis"][:, :IDX_DH].reshape(nb, tt, IDX_DH)
    if emit_keys:
        k_b_ref, ki_b_ref, vt_b_ref = out_refs[len(OUT_SEGS) + 1:]
        k_b_ref[0] = segs["k"].astype(BF16)
        ki_b_ref[0] = segs["kis"][:, :IDX_DH].astype(BF16)
        vt_b_ref[0] = segs["v"].T.astype(BF16)


def _inproj(x, mod, g1, w_perm, cos, sin, nb, tt, emit_keys):
    B, T, _ = x.shape
    grid = (B // nb, T // tt)
    out_shape = tuple(jax.ShapeDtypeStruct((B, T, w), F32) for _, w in OUT_SEGS)
    out_specs = tuple(pl.BlockSpec((nb, tt, w), lambda b, t: (b, t, 0)) for _, w in OUT_SEGS)
    out_shape += (jax.ShapeDtypeStruct((B, T, IDX_DH), F32),)
    out_specs += (pl.BlockSpec((nb, tt, IDX_DH), lambda b, t: (b, t, 0)),)
    if emit_keys:
        assert nb == 1
        out_shape += (jax.ShapeDtypeStruct((B, T, ATT_W), BF16), jax.ShapeDtypeStruct((B, T, IDX_DH), BF16),
                      jax.ShapeDtypeStruct((B, ATT_W, T), BF16))
        out_specs += (pl.BlockSpec((1, tt, ATT_W), lambda b, t: (b, t, 0)),
                      pl.BlockSpec((1, tt, IDX_DH), lambda b, t: (b, t, 0)),
                      pl.BlockSpec((1, ATT_W, tt), lambda b, t: (b, 0, t)))
    return pl.pallas_call(
        functools.partial(_inproj_kernel, nb=nb, tt=tt, emit_keys=emit_keys),
        grid=grid,
        in_specs=[pl.BlockSpec((nb, tt, D_MODEL), lambda b, t: (b, t, 0)),
                  pl.BlockSpec((nb, 6, D_MODEL), lambda b, t: (b, 0, 0)),
                  pl.BlockSpec((1, D_MODEL), lambda b, t: (0, 0)),
                  pl.BlockSpec((D_MODEL, PERM_W), lambda b, t: (0, 0)),
                  pl.BlockSpec((tt, ROT_W), lambda b, t: (t, 0)),
                  pl.BlockSpec((tt, ROT_W), lambda b, t: (t, 0))],
        out_specs=out_specs,
        out_shape=out_shape,
        compiler_params=_params(("arbitrary", "arbitrary")),
        name="inproj",
    )(x, mod, g1, w_perm, cos, sin)


def _short_conv(ext_ref, bi, u, w, c):
    lo = SUBLANES - (CONV_W - 1)
    ext_ref[bi, SUBLANES:SUBLANES + c, :] = u
    y = ext_ref[bi, lo:lo + c, :] * w[0:1, :]
    for j in range(1, CONV_W):
        y = y + ext_ref[bi, lo + j:lo + j + c, :] * w[j:j + 1, :]
    tail = ext_ref[bi, c + lo:c + SUBLANES, :]
    ext_ref[bi, lo:SUBLANES, :] = tail
    return y, tail


def _neumann_inverses(mats, c):
    eye = (lax.broadcasted_iota(jnp.int32, (c, c), 0)
           == lax.broadcasted_iota(jnp.int32, (c, c), 1)).astype(F32)
    ps = [eye - a for a in mats]
    sp = [_split(a) for a in mats]
    pws = [_mm3(s, s) for s in sp]
    n = 2
    while n < c:
        sp = [_split(pw) for pw in pws]
        ps = [p + _mm3(_split(p), s) for p, s in zip(ps, sp)]
        n *= 2
        if n < c:
            pws = [_mm3(s, s) for s in sp]
    return ps


def _gdn_kernel(qkv_ref, kis_ref, gate_ref, w_ref, prev_ref, s0_ref, alog_ref, dtb_ref, ng_ref,
                o_ref, new_ref, s_ref, ext_ref, s_sc, *, c, nbb):
    j = pl.program_id(1)
    lo = SUBLANES - (CONV_W - 1)

    @pl.when(j == 0)
    def _():
        s_sc[...] = s0_ref[...]
        ext_ref[:, lo:SUBLANES, :] = prev_ref[...]

    ng = ng_ref[...]
    w_conv = w_ref[...]
    ri = lax.broadcasted_iota(jnp.int32, (c, c), 0)
    ci = lax.broadcasted_iota(jnp.int32, (c, c), 1)
    incl = ri >= ci
    strict = ri > ci
    nkd = GDN_HEADS * GDN_DK
    ch = []
    for bi in range(nbb):
        y, tail = _short_conv(ext_ref, bi, qkv_ref[bi], w_conv, c)
        new_ref[bi] = tail
        y = _silu(y)
        small = kis_ref[bi]
        g = -jnp.exp(alog_ref[...]) * _softplus(small + dtb_ref[...])
        beta = jax.nn.sigmoid(small)
        gc = _cumsum_rows(g)
        gc_t = _transpose_rows(gc)
        qk_n = y[:, :2 * nkd]
        qk_n = qk_n * lax.rsqrt(_head_sums(qk_n * qk_n) + EPS)
        for h in range(GDN_HEADS):
            qh = qk_n[:, h * GDN_DK:(h + 1) * GDN_DK] * (GDN_DK ** -0.5)
            kh = qk_n[:, nkd + h * GDN_DK:nkd + (h + 1) * GDN_DK]
            vh = y[:, 2 * nkd + h * GDN_DV:2 * nkd + (h + 1) * GDN_DV]
            ch.append((qh, kh, vh, gc[:, KIS_A + h:KIS_A + h + 1], gc_t[KIS_A + h:KIS_A + h + 1, :],
                       beta[:, KIS_B + h:KIS_B + h + 1]))
    ids = [(bi, h) for bi in range(nbb) for h in range(GDN_HEADS)]
    decay = [jnp.where(incl, jnp.exp(jnp.where(incl, gcol - grow, 0.0)), 0.0)
             for (_, _, _, gcol, grow, _) in ch]
    kb = [kh * bcol for (_, kh, _, _, _, bcol) in ch]
    eg = [jnp.exp(gcol) for (_, _, _, gcol, _, _) in ch]
    grams = [_mm_nt(jnp.concatenate([kbi, x[0]], axis=0), x[1]) for kbi, x in zip(kb, ch)]
    t_inv = _neumann_inverses([jnp.where(strict, m[:c] * d, 0.0) for m, d in zip(grams, decay)], c)
    attn = [jnp.where(incl, m[c:] * d, 0.0) for m, d in zip(grams, decay)]
    rhs = [jnp.concatenate([x[2] * x[5], kbi * e], axis=1) for x, kbi, e in zip(ch, kb, eg)]
    sol = [_mm(t, r) for t, r in zip(t_inv, rhs)]
    st = [s_sc[bi, h] for bi, h in ids]
    ws = [_mm(sl[:, GDN_DV:], s) for sl, s in zip(sol, st)]
    qs = [_mm(x[0] * e, s) for x, e, s in zip(ch, eg, st)]
    v_new = [sl[:, :GDN_DV] - w for sl, w in zip(sol, ws)]
    av = [_mm(a, v) for a, v in zip(attn, v_new)]
    g_last = [x[3][c - 1:c, :] for x in ch]
    kv = [_dot_tn(x[1] * jnp.exp(gl - x[3]), v) for x, gl, v in zip(ch, g_last, v_new)]
    for (bi, h), s, gl, kvi in zip(ids, st, g_last, kv):
        s_new = s * jnp.exp(gl) + kvi
        s_sc[bi, h] = s_new
        s_ref[bi, h] = s_new
    for bi in range(nbb):
        o = jnp.concatenate([qs[bi * GDN_HEADS + h] + av[bi * GDN_HEADS + h] for h in range(GDN_HEADS)], axis=1)
        o = o * lax.rsqrt(_head_sums(o * o) * (1.0 / GDN_DV) + EPS) * ng
        o_ref[bi] = o * _silu(gate_ref[bi])


def _gdn(gqkv, kis, ggate, conv_w, conv_prev, s0, a_log, dt_bias, norm_g, nbb):
    B, T, _ = gqkv.shape
    c = math.gcd(T, CHUNK)
    return pl.pallas_call(
        functools.partial(_gdn_kernel, c=c, nbb=nbb),
        grid=(B // nbb, T // c),
        in_specs=[pl.BlockSpec((nbb, c, GDN_CONV_C), lambda b, j: (b, j, 0)),
                  pl.BlockSpec((nbb, c, LANES), lambda b, j: (b, j, 0)),
                  pl.BlockSpec((nbb, c, GDN_W), lambda b, j: (b, j, 0)),
                  pl.BlockSpec((CONV_W, GDN_CONV_C), lambda b, j: (0, 0)),
                  pl.BlockSpec((nbb, CONV_W - 1, GDN_CONV_C), lambda b, j: (b, 0, 0)),
                  pl.BlockSpec((nbb, GDN_HEADS, GDN_DK, GDN_DV), lambda b, j: (b, 0, 0, 0)),
                  pl.BlockSpec((1, LANES), lambda b, j: (0, 0)),
                  pl.BlockSpec((1, LANES), lambda b, j: (0, 0)),
                  pl.BlockSpec((1, GDN_W), lambda b, j: (0, 0))],
        out_specs=(pl.BlockSpec((nbb, c, GDN_W), lambda b, j: (b, j, 0)),
                   pl.BlockSpec((nbb, CONV_W - 1, GDN_CONV_C), lambda b, j: (b, 0, 0)),
                   pl.BlockSpec((nbb, GDN_HEADS, GDN_DK, GDN_DV), lambda b, j: (b, 0, 0, 0))),
        out_shape=(jax.ShapeDtypeStruct((B, T, GDN_W), F32),
                   jax.ShapeDtypeStruct((B, CONV_W - 1, GDN_CONV_C), F32),
                   jax.ShapeDtypeStruct((B, GDN_HEADS, GDN_DK, GDN_DV), F32)),
        scratch_shapes=[pltpu.VMEM((nbb, SUBLANES + c, GDN_CONV_C), F32),
                        pltpu.VMEM((nbb, GDN_HEADS, GDN_DK, GDN_DV), F32)],
        compiler_params=_params(("arbitrary", "arbitrary")),
        name="gdn",
    )(gqkv, kis, ggate, conv_w, conv_prev, s0, _lane_vector(a_log, KIS_A), _lane_vector(dt_bias, KIS_A),
      jnp.tile(norm_g, GDN_HEADS).reshape(1, -1))


def _ssd_kernel(xbc_ref, z_ref, kis_ref, w_ref, cb_ref, prev_ref, h0_ref, alog_ref, dtb_ref, dsk_ref,
                ng_ref, o_ref, new_ref, h_ref, ext_ref, h_sc, *, c, nbb):
    j = pl.program_id(1)
    lo = SUBLANES - (CONV_W - 1)

    @pl.when(j == 0)
    def _():
        h_sc[...] = h0_ref[...]
        ext_ref[:, lo:SUBLANES, :] = prev_ref[...]

    dsk = dsk_ref[...]
    ng = ng_ref[...]
    w_conv = w_ref[...]
    ri = lax.broadcasted_iota(jnp.int32, (c, c), 0)
    ci = lax.broadcasted_iota(jnp.int32, (c, c), 1)
    incl = ri >= ci
    gn = SSM_GROUPS * SSM_N
    rep = SSM_HEADS // SSM_GROUPS
    gw = SSM_W // SSM_GROUPS
    grp_in = []
    ch = []
    slot = []
    ids = []
    for bi in range(nbb):
        y, tail = _short_conv(ext_ref, bi, xbc_ref[bi], w_conv, c)
        new_ref[bi] = tail
        y = _silu(y + cb_ref[...])
        dts = _softplus(kis_ref[bi] + dtb_ref[...])
        a = dts * (-jnp.exp(alog_ref[...]))
        acum = _cumsum_rows(a)
        acum_t = _transpose_rows(acum)
        dtm = _lane_spread(dts, KIS_DT, SSM_HEADS)
        am = _lane_spread(acum, KIS_DT, SSM_HEADS)
        a_last = am[c - 1:c, :]
        xs = y[:, :SSM_W]
        xdt = xs * dtm
        xw = xs * (jnp.exp(a_last - am) * dtm)
        h_dec = jnp.exp(a_last)
        slot.append((xs, jnp.exp(am), _silu(z_ref[bi])))
        for grp in range(SSM_GROUPS):
            grp_in.append((y[:, SSM_W + grp * SSM_N:SSM_W + (grp + 1) * SSM_N],
                           y[:, SSM_W + gn + grp * SSM_N:SSM_W + gn + (grp + 1) * SSM_N]))
            for hh in range(rep):
                h = grp * rep + hh
                hsl = slice(h * SSM_P, (h + 1) * SSM_P)
                ch.append((xdt[:, hsl], xw[:, hsl], am[:, h * SSM_P:h * SSM_P + c], acum_t[KIS_DT + h:KIS_DT + h + 1, :],
                           jnp.concatenate([h_dec[:, hsl]] * (SSM_N // SSM_P), axis=1), len(grp_in) - 1))
                ids.append((bi, h))
    cb = [_mm_nt(cg, bg) for bg, cg in grp_in]
    seg = [jnp.where(incl, jnp.exp(jnp.where(incl, x[2] - x[3], 0.0)), 0.0) for x in ch]
    hs = [h_sc[bi, h] for bi, h in ids]
    y_in = [_mm(cb[x[5]] * sg, x[0]) for x, sg in zip(ch, seg)]
    y_st = [_mm_nt(grp_in[x[5]][1], hst) for x, hst in zip(ch, hs)]
    upd = [_dot_tn(x[1], grp_in[x[5]][0]) for x in ch]
    for (bi, h), x, hst, up in zip(ids, ch, hs, upd):
        h_new = hst * x[4] + up
        h_sc[bi, h] = h_new
        h_ref[bi, h] = h_new
    for bi, (xs, ea, zg) in enumerate(slot):
        lo_c, hi_c = bi * SSM_HEADS, (bi + 1) * SSM_HEADS
        yy = (jnp.concatenate(y_in[lo_c:hi_c], axis=1) + jnp.concatenate(y_st[lo_c:hi_c], axis=1) * ea
              + dsk * xs) * zg
        for grp in range(SSM_GROUPS):
            yg = yy[:, grp * gw:(grp + 1) * gw]
            yg = yg * lax.rsqrt(jnp.mean(yg * yg, axis=-1, keepdims=True) + EPS)
            o_ref[bi, :, grp * gw:(grp + 1) * gw] = yg * ng[:, grp * gw:(grp + 1) * gw]


def _ssd(xbc, z, kis, conv_w, conv_b, conv_prev, h0, a_log, dt_bias, d_skip, norm_g, nbb):
    B, T, _ = xbc.shape
    c = math.gcd(T, CHUNK)
    return pl.pallas_call(
        functools.partial(_ssd_kernel, c=c, nbb=nbb),
        grid=(B // nbb, T // c),
        in_specs=[pl.BlockSpec((nbb, c, SSM_CONV_C), lambda b, j: (b, j, 0)),
                  pl.BlockSpec((nbb, c, SSM_W), lambda b, j: (b, j, 0)),
                  pl.BlockSpec((nbb, c, LANES), lambda b, j: (b, j, 0)),
                  pl.BlockSpec((CONV_W, SSM_CONV_C), lambda b, j: (0, 0)),
                  pl.BlockSpec((1, SSM_CONV_C), lambda b, j: (0, 0)),
                  pl.BlockSpec((nbb, CONV_W - 1, SSM_CONV_C), lambda b, j: (b, 0, 0)),
                  pl.BlockSpec((nbb, SSM_HEADS, SSM_P, SSM_N), lambda b, j: (b, 0, 0, 0)),
                  pl.BlockSpec((1, LANES), lambda b, j: (0, 0)),
                  pl.BlockSpec((1, LANES), lambda b, j: (0, 0)),
                  pl.BlockSpec((1, SSM_W), lambda b, j: (0, 0)),
                  pl.BlockSpec((1, SSM_W), lambda b, j: (0, 0))],
        out_specs=(pl.BlockSpec((nbb, c, SSM_W), lambda b, j: (b, j, 0)),
                   pl.BlockSpec((nbb, CONV_W - 1, SSM_CONV_C), lambda b, j: (b, 0, 0)),
                   pl.BlockSpec((nbb, SSM_HEADS, SSM_P, SSM_N), lambda b, j: (b, 0, 0, 0))),
        out_shape=(jax.ShapeDtypeStruct((B, T, SSM_W), F32),
                   jax.ShapeDtypeStruct((B, CONV_W - 1, SSM_CONV_C), F32),
                   jax.ShapeDtypeStruct((B, SSM_HEADS, SSM_P, SSM_N), F32)),
        scratch_shapes=[pltpu.VMEM((nbb, SUBLANES + c, SSM_CONV_C), F32),
                        pltpu.VMEM((nbb, SSM_HEADS, SSM_P, SSM_N), F32)],
        compiler_params=_params(("arbitrary", "arbitrary")),
        name="ssd",
    )(xbc, z, kis, conv_w, conv_b.reshape(1, -1), conv_prev, h0, _lane_vector(a_log, KIS_DT),
      _lane_vector(dt_bias, KIS_DT), jnp.repeat(d_skip, SSM_P).reshape(1, -1), norm_g.reshape(1, -1))


IDX_BITS = 12


OPENING_PROBES = 14
ROUND_PROBES = 2
SEARCH_ROUNDS = 4096
FOLD_ROWS = 128


def _fold_keys(x, op, axis=0):
    s = x.shape[0]
    if axis == 0 and s % FOLD_ROWS == 0 and s > FOLD_ROWS:
        x = op(x.reshape(s // FOLD_ROWS, FOLD_ROWS, x.shape[1]), axis=0)
    return op(x, axis=axis, keepdims=True)


def _fold_rows(x, op):
    return _fold_keys(x, op, 0)


def _select_topk(score, adm, pos, skip, topk, axis):
    def count(mask):
        return _fold_keys(mask.astype(F32), jnp.sum, axis)

    s = jnp.where(adm, score, -jnp.inf)

    few = (count(adm) <= topk) | skip
    smax = _fold_keys(s, jnp.max, axis)
    smin = _fold_keys(jnp.where(adm, score, jnp.inf), jnp.min, axis)
    top_full = count(s >= smax) >= topk
    lo0 = jnp.where(few, 0.0, jnp.where(top_full, smax, smin))
    hi0 = jnp.where(few, 0.0, smax)
    done0 = (few | top_full).astype(F32)

    def probe(lo, hi, done):
        mid = 0.5 * lo + 0.5 * hi
        ok = (mid > lo) & (mid < hi) & (done < 0.5)
        take = count(s >= mid) >= topk
        return jnp.where(ok & take, mid, lo), jnp.where(ok & jnp.logical_not(take), mid, hi)

    def opening(lo, hi):
        for _ in range(OPENING_PROBES):
            lo, hi = probe(lo, hi, done0)
        return lo, hi

    lo1, hi1 = lax.cond(jnp.min(done0) < 0.5, opening, lambda lo, hi: (lo, hi), lo0, hi0)

    def search_cond(c):
        return (jnp.min(c[2]) < 0.5) & (c[3] < SEARCH_ROUNDS)

    def search_body(c):
        lo, hi, done, it = c
        last = _fold_keys(jnp.where(s < hi, s, -jnp.inf), jnp.max, axis)
        reached = count(s >= last) >= topk
        active = done < 0.5
        lo = jnp.where(active & reached, last, lo)
        hi = jnp.where(active & jnp.logical_not(reached), last, hi)
        done = jnp.maximum(done, reached.astype(F32))
        for _ in range(ROUND_PROBES):
            lo, hi = probe(lo, hi, done)
        return lo, hi, done, it + 1

    lo, _, _, _ = lax.while_loop(search_cond, search_body, (lo1, hi1, done0, jnp.int32(0)))
    t = jnp.where(few, -jnp.inf, lo)
    gt = s > t
    tie = s == t
    need = topk - count(gt)
    excess = jnp.where(few, 0.0, count(tie) - need)

    def tie_search():
        def ibody(i, m):
            cand = m | (jnp.int32(1) << (IDX_BITS - 1 - i))
            return jnp.where(count(tie & (pos < cand)) <= need, cand, m)

        return lax.fori_loop(0, IDX_BITS, ibody, jnp.zeros(few.shape, jnp.int32))

    m = lax.cond(jnp.max(excess) > 0.0, tie_search, lambda: jnp.full(few.shape, 2 ** IDX_BITS - 1, jnp.int32))
    return adm & (gt | (tie & (pos < m)))


def _dsa_kernel(q_ref, qi_ref, kis_ref, k_ref, vt_ref, ki_ref, o_ref, *, tq, s_len, s_valid, q_offset, j0,
                n_valid_q, topk):
    j = pl.program_id(1)
    lane = lax.broadcasted_iota(jnp.int32, (1, tq), 1)
    qpos = q_offset + (j0 + j) * tq + lane
    lim = jnp.minimum((qpos // CHUNK + 1) * CHUNK, s_valid)
    row = lax.broadcasted_iota(jnp.int32, (s_len, tq), 0)
    adm = row < lim

    qi_t = qi_ref[0].T.astype(BF16)
    kis_t = kis_ref[0].T
    ki = ki_ref[0]
    assert IDX_DH == 64 and IDX_HEADS == 4 and ATT_DH == 64
    wi = kis_t[KIS_WI:KIS_WI + IDX_HEADS, :] * (IDX_DH ** -0.5 * IDX_HEADS ** -0.5)
    qi_all = jnp.concatenate([qi_t[h * IDX_DH:(h + 1) * IDX_DH, :] for h in range(IDX_HEADS)], axis=1)
    rel_all = _dot(ki, qi_all)
    score = jnp.zeros((s_len, tq), F32)
    for h in range(IDX_HEADS):
        score = score + jnp.maximum(rel_all[:, h * tq:(h + 1) * tq], 0.0) * wi[h:h + 1, :]
    sel = _select_topk(score, adm, row, lane >= n_valid_q, topk, 0)

    q_t = q_ref[0].T * (ATT_DH ** -0.5)
    pair = LANES // ATT_DH
    head_in_pair = lax.broadcasted_iota(jnp.int32, (LANES, tq), 0) // ATT_DH
    logits = []
    for g in range(ATT_HEADS // pair):
        qg = q_t[g * LANES:(g + 1) * LANES, :]
        qm = jnp.concatenate([jnp.where(head_in_pair == i, qg, 0.0) for i in range(pair)], axis=1)
        lg = _dot(k_ref[0, :, g * LANES:(g + 1) * LANES], qm.astype(BF16))
        logits += [lg[:, i * tq:(i + 1) * tq] for i in range(pair)]
    probs, scales = [], []
    for lg in logits:
        lg = jnp.where(sel, lg, -jnp.inf)
        p = jnp.exp(lg - _fold_rows(lg, jnp.max))
        scales.append(1.0 / _fold_rows(p, jnp.sum))
        probs.append(p.astype(BF16))
    outs = [_dot(vt_ref[0, h * ATT_DH:(h + 1) * ATT_DH, :], probs[h]) * scales[h] for h in range(ATT_HEADS)]
    o_ref[0] = jnp.concatenate(outs, axis=0).T


def _dsa(q, qi, kis, k_b, vt_b, ki_b, *, tq, j0, nj, s_len, s_valid, q_offset, n_valid_q, topk):
    B = q.shape[0]
    assert s_len < 2 ** IDX_BITS and s_len <= k_b.shape[1]
    return pl.pallas_call(
        functools.partial(_dsa_kernel, tq=tq, s_len=s_len, s_valid=s_valid, q_offset=q_offset, j0=j0,
                          n_valid_q=n_valid_q, topk=topk),
        grid=(B, nj),
        in_specs=[pl.BlockSpec((1, tq, ATT_W), lambda b, j: (b, j0 + j, 0)),
                  pl.BlockSpec((1, tq, IDX_HEADS * IDX_DH), lambda b, j: (b, j0 + j, 0)),
                  pl.BlockSpec((1, tq, LANES), lambda b, j: (b, j0 + j, 0)),
                  pl.BlockSpec((1, s_len, ATT_W), lambda b, j: (b, 0, 0)),
                  pl.BlockSpec((1, ATT_W, s_len), lambda b, j: (b, 0, 0)),
                  pl.BlockSpec((1, s_len, IDX_DH), lambda b, j: (b, 0, 0))],
        out_specs=pl.BlockSpec((1, tq, ATT_W), lambda b, j: (b, j, 0)),
        out_shape=jax.ShapeDtypeStruct((B, nj * tq, ATT_W), F32),
        compiler_params=_params(("arbitrary", "arbitrary")),
        name="dsa",
    )(q, qi, kis, k_b, vt_b, ki_b)


def _dsa_rows_kernel(q_ref, qi_ref, kis_ref, k_ref, v_ref, ck_ref, cv_ref, cki_ref, o_ref, *, nbq, tq, s_valid,
                     q_offset, topk):
    rows = nbq * tq
    s_len = q_offset + LANES
    qpos = q_offset + lax.broadcasted_iota(jnp.int32, (rows, 1), 0) % tq
    lim = jnp.minimum((qpos // CHUNK + 1) * CHUNK, s_valid)
    col = lax.broadcasted_iota(jnp.int32, (rows, s_len), 1)
    adm = col < lim

    def with_new(cache, new):
        return jnp.concatenate([cache, new.astype(BF16),
                                jnp.zeros((LANES - tq, new.shape[1]), BF16)], axis=0)

    scores = []
    for b in range(nbq):
        qi = qi_ref[b]
        qi_st = jnp.concatenate([qi[:, h * IDX_DH:(h + 1) * IDX_DH] for h in range(IDX_HEADS)], axis=0)
        rel = _dot_nt(qi_st.astype(BF16), with_new(cki_ref[b], kis_ref[b][:, :IDX_DH]))
        wi = kis_ref[b][:, KIS_WI:KIS_WI + IDX_HEADS] * (IDX_DH ** -0.5 * IDX_HEADS ** -0.5)
        sc = jnp.maximum(rel[:tq, :], 0.0) * wi[:, 0:1]
        for h in range(1, IDX_HEADS):
            sc = sc + jnp.maximum(rel[h * tq:(h + 1) * tq, :], 0.0) * wi[:, h:h + 1]
        scores.append(sc)
    sel = _select_topk(jnp.concatenate(scores, axis=0), adm, col, qpos < 0, topk, 1)

    head_of_lane = lax.broadcasted_iota(jnp.int32, (tq, ATT_W), 1) // ATT_DH
    for b in range(nbq):
        q = q_ref[b] * (ATT_DH ** -0.5)
        q_st = jnp.concatenate([jnp.where(head_of_lane == h, q, 0.0) for h in range(ATT_HEADS)], axis=0)
        logits = _dot_nt(q_st.astype(BF16), with_new(ck_ref[b], k_ref[b]))
        logits = jnp.where(jnp.concatenate([sel[b * tq:(b + 1) * tq, :]] * ATT_HEADS, axis=0), logits, -jnp.inf)
        p = jnp.exp(logits - jnp.max(logits, axis=1, keepdims=True))
        scale = 1.0 / jnp.sum(p, axis=1, keepdims=True)
        pv = _dot(p.astype(BF16), with_new(cv_ref[b], v_ref[b])) * scale
        out = jnp.zeros((tq, ATT_W), F32)
        for h in range(ATT_HEADS):
            out = jnp.where(head_of_lane == h, pv[h * tq:(h + 1) * tq, :], out)
        o_ref[b] = out


def _dsa_rows(q, qi, kis, k, v, ck, cv, cki, *, nbq, topk):
    B, T, _ = q.shape
    P = ck.shape[1]
    assert P + LANES < 2 ** IDX_BITS and T % (2 * SUBLANES) == 0 and T <= LANES and P % (2 * SUBLANES) == 0
    assert B % nbq == 0
    new_spec = lambda w: pl.BlockSpec((nbq, T, w), lambda b: (b, 0, 0))
    cache_spec = lambda w: pl.BlockSpec((nbq, P, w), lambda b: (b, 0, 0))
    return pl.pallas_call(
        functools.partial(_dsa_rows_kernel, nbq=nbq, tq=T, s_valid=P + T, q_offset=P, topk=topk),
        grid=(B // nbq,),
        in_specs=[new_spec(ATT_W), new_spec(IDX_HEADS * IDX_DH), new_spec(LANES), new_spec(ATT_W), new_spec(ATT_W),
                  cache_spec(ATT_W), cache_spec(ATT_W), cache_spec(IDX_DH)],
        out_specs=new_spec(ATT_W),
        out_shape=jax.ShapeDtypeStruct((B, T, ATT_W), F32),
        compiler_params=_params(("arbitrary",)),
        name="dsa_rows",
    )(q, qi, kis, k, v, ck, cv, cki)


def _dsa_group(q, qi, kis, k_b, vt_b, ki_b, *, s_valid, q_offset, topk):
    B, T, _ = q.shape
    tq = 2 * LANES if T % (2 * LANES) == 0 else LANES
    n_valid_q = min(T, tq)
    if T < tq:
        padq = lambda a: jnp.pad(a, ((0, 0), (0, tq - T), (0, 0)))
        q, qi, kis = padq(q), padq(qi), padq(kis)
    nq = q.shape[1] // tq
    s_total = k_b.shape[1]
    outs = []
    j0 = 0
    while j0 < nq:
        nj = 1
        reach = -(-(q_offset + (j0 + nj) * tq) // CHUNK) * CHUNK
        s_len = min(s_total, -(-min(reach, s_valid) // LANES) * LANES)
        outs.append(_dsa(q, qi, kis, k_b, vt_b, ki_b, tq=tq, j0=j0, nj=nj, s_len=s_len, s_valid=s_valid,
                         q_offset=q_offset, n_valid_q=n_valid_q, topk=topk))
        j0 += nj
    out = outs[0] if len(outs) == 1 else jnp.concatenate(outs, axis=1)
    return out[:, :T]


def _ffn_kernel(x_ref, gdn_ref, att_ref, ssm_ref, mod_ref, wo_ref, g2_ref, wg_ref, wu_ref, cw_ref, cb_ref,
                wd_ref, prev_ref, fg_ref, *rest, nb, tt, nk, final):
    if final:
        xo_ref, y_ref, new_ref, h2_sc, acc_sc, buf_sc, carry_sc = rest
    else:
        xo_ref, new_ref, h2_sc, acc_sc, buf_sc, carry_sc = rest
        y_ref = None
    t = pl.program_id(1)
    k = pl.program_id(2)
    rows = nb * tt
    lo = SUBLANES - (FFN_CONV_W - 1)
    mod = mod_ref[...]

    @pl.when(k == 0)
    def _():
        mix = jnp.concatenate([gdn_ref[...], att_ref[...], ssm_ref[...]], axis=-1)
        proj = _dot(mix.reshape(rows, MIX_W).astype(BF16), wo_ref[...]).reshape(nb, tt, D_MODEL)
        x1 = x_ref[...] + mod[:, 2:3, :] * proj
        acc_sc[...] = x1
        ms = jnp.mean(x1 * x1, axis=-1, keepdims=True)
        h2 = x1 * lax.rsqrt(ms + EPS) * g2_ref[...]
        h2 = h2 * (1.0 + mod[:, 4:5, :]) + mod[:, 3:4, :]
        h2_sc[...] = h2.reshape(rows, D_MODEL).astype(BF16)

    @pl.when(t == 0)
    def _():
        carry_sc[k, :, lo:SUBLANES, :] = prev_ref[...]

    h2 = h2_sc[...]
    ag = _dot(h2, wg_ref[...])
    up = _dot(h2, wu_ref[...])
    tf = ag.shape[-1]
    buf_sc[:, lo:SUBLANES, :] = carry_sc[k, :, lo:SUBLANES, :]
    buf_sc[:, SUBLANES:SUBLANES + tt, :] = ag.reshape(nb, tt, tf)
    cw = cw_ref[...]
    conv = buf_sc[:, lo:lo + tt, :] * cw[0:1, :][None]
    for jj in range(1, FFN_CONV_W):
        conv = conv + buf_sc[:, lo + jj:lo + jj + tt, :] * cw[jj:jj + 1, :][None]
    tail = buf_sc[:, tt + lo:tt + SUBLANES, :]
    carry_sc[k, :, lo:SUBLANES, :] = tail
    new_ref[:, pl.ds(k, 1), :, :] = tail[:, None]
    act = _silu(conv + cb_ref[...][None]).reshape(rows, tf) * up
    y = _dot(act.astype(BF16), wd_ref[...]).reshape(nb, tt, D_MODEL)
    acc_sc[...] += mod[:, 5:6, :] * y

    @pl.when(k == nk - 1)
    def _():
        xo = acc_sc[...]
        xo_ref[...] = xo
        if final:
            ms = jnp.mean(xo * xo, axis=-1, keepdims=True)
            y_ref[...] = xo * lax.rsqrt(ms + EPS) * fg_ref[...]


def _ffn(x, gdn_o, att_o, ssm_o, mod, w_out, g2, w_gate, w_up, conv_w, conv_b, w_down, conv_prev, final_g,
         nb, tt, tf, final):
    B, T, _ = x.shape
    nk = D_FF // tf
    grid = (B // nb, T // tt, nk)
    row_spec = lambda w: pl.BlockSpec((nb, tt, w), lambda b, t, k: (b, t, 0))
    in_specs = [row_spec(D_MODEL), row_spec(GDN_W), row_spec(ATT_W), row_spec(SSM_W),
                pl.BlockSpec((nb, 6, D_MODEL), lambda b, t, k: (b, 0, 0)),
                pl.BlockSpec((MIX_W, D_MODEL), lambda b, t, k: (0, 0)),
                pl.BlockSpec((1, D_MODEL), lambda b, t, k: (0, 0)),
                pl.BlockSpec((D_MODEL, tf), lambda b, t, k: (0, k)),
                pl.BlockSpec((D_MODEL, tf), lambda b, t, k: (0, k)),
                pl.BlockSpec((FFN_CONV_W, tf), lambda b, t, k: (0, k)),
                pl.BlockSpec((1, tf), lambda b, t, k: (0, k)),
                pl.BlockSpec((tf, D_MODEL), lambda b, t, k: (k, 0)),
                pl.BlockSpec((nb, FFN_CONV_W - 1, tf), lambda b, t, k: (b, 0, k)),
                pl.BlockSpec((1, D_MODEL), lambda b, t, k: (0, 0))]
    out_shape = [jax.ShapeDtypeStruct((B, T, D_MODEL), F32)]
    out_specs = [row_spec(D_MODEL)]
    if final:
        out_shape.append(jax.ShapeDtypeStruct((B, T, D_MODEL), F32))
        out_specs.append(row_spec(D_MODEL))
    out_shape.append(jax.ShapeDtypeStruct((B, nk, FFN_CONV_W - 1, tf), F32))
    out_specs.append(pl.BlockSpec((nb, nk, FFN_CONV_W - 1, tf), lambda b, t, k: (b, 0, 0, 0)))
    res = pl.pallas_call(
        functools.partial(_ffn_kernel, nb=nb, tt=tt, nk=nk, final=final),
        grid=grid,
        in_specs=in_specs,
        out_specs=tuple(out_specs),
        out_shape=tuple(out_shape),
        scratch_shapes=[pltpu.VMEM((nb * tt, D_MODEL), BF16),
                        pltpu.VMEM((nb, tt, D_MODEL), F32),
                        pltpu.VMEM((nb, SUBLANES + tt, tf), F32),
                        pltpu.VMEM((nk, nb, SUBLANES, tf), F32)],
        compiler_params=_params(("arbitrary", "arbitrary", "arbitrary")),
        name="ffn",
    )(x, gdn_o, att_o, ssm_o, mod, w_out, g2, w_gate, w_up, conv_w, conv_b.reshape(1, -1), w_down,
      conv_prev, final_g)
    tail = res[-1].transpose(0, 2, 1, 3).reshape(B, FFN_CONV_W - 1, D_FF)
    return (*res[:-1], tail)


def _rope_tables(pos):
    half = ATT_DH // 2
    inv_freq = ROPE_THETA ** (-jnp.arange(half, dtype=F32) / half)
    ang = pos.astype(F32)[:, None] * inv_freq[None, :]
    cos = jnp.cos(ang)
    sin = jnp.sin(ang)
    cos_h = jnp.concatenate([cos, cos], axis=-1)
    sin_h = jnp.concatenate([-sin, sin], axis=-1)
    ones = jnp.ones((pos.shape[0], LANES - ATT_DH), F32)
    cos_t = jnp.concatenate([jnp.tile(cos_h, (1, ATT_HEADS)), cos_h, ones], axis=-1)
    sin_t = jnp.concatenate([jnp.tile(sin_h, (1, ATT_HEADS)), sin_h, 0.0 * ones], axis=-1)
    return cos_t, sin_t


def _layer(x, mod, pos_tables, cache, states, lw, tiles, final, final_g):
    (norm1_g, w_perm, gdn_conv_w, gdn_A_log, gdn_dt_bias, gdn_norm_g, ssm_conv_w, ssm_conv_b, ssm_A_log,
     ssm_dt_bias, ssm_D, ssm_norm_g, w_out, norm2_g, w_gate, w_up, ffn_conv_w, ffn_conv_b, w_down) = lw
    gdn_conv_prev, gdn_s0, ssm_conv_prev, ssm_h0, ffn_conv_prev = states
    B, T, _ = x.shape
    nb, tt, nbb, ffn_tt, tf = tiles
    cos_t, sin_t = pos_tables
    proj = _inproj(x, mod, norm1_g.reshape(1, -1), w_perm, cos_t, sin_t, nb, tt, emit_keys=cache is None)
    gqkv, ggate, q, k, v, qi, kis, z, xbc = proj[:len(OUT_SEGS)]
    ki = proj[len(OUT_SEGS)]

    gdn_o, gdn_conv_new, gdn_s = _gdn(gqkv, kis, ggate, gdn_conv_w, gdn_conv_prev, gdn_s0, gdn_A_log,
                                      gdn_dt_bias, gdn_norm_g, nbb)
    ssm_o, ssm_conv_new, ssm_h = _ssd(xbc, z, kis, ssm_conv_w, ssm_conv_b, ssm_conv_prev, ssm_h0, ssm_A_log,
                                      ssm_dt_bias, ssm_D, ssm_norm_g, nbb)
    if cache is None:
        k_b, ki_b, vt_b = proj[len(OUT_SEGS) + 1:]
        att_o = _dsa_group(q, qi, kis, k_b, vt_b, ki_b, s_valid=T, q_offset=0, topk=min(TOPK_MAX, T // 4))
    else:
        ck, cv, cki = cache
        P = ck.shape[1]
        att_o = _dsa_rows(q, qi, kis, k, v, ck.reshape(B, P, ATT_W).astype(BF16), cv.reshape(B, P, ATT_W).astype(BF16),
                          cki.astype(BF16), nbq=nbb, topk=min(TOPK_MAX, (P + T) // 4))

    res = _ffn(x, gdn_o, att_o, ssm_o, mod, w_out, norm2_g.reshape(1, -1), w_gate, w_up, ffn_conv_w, ffn_conv_b,
               w_down, ffn_conv_prev, final_g.reshape(1, -1), ffn_tt[0], ffn_tt[1], tf, final)
    if final:
        x_new, y, ffn_conv_new = res
    else:
        (x_new, ffn_conv_new), y = res, None
    st = (k.reshape(B, T, ATT_HEADS, ATT_DH), v.reshape(B, T, ATT_HEADS, ATT_DH), ki,
          gdn_conv_new, gdn_s, ssm_conv_new, ssm_h, ffn_conv_new)
    return x_new, y, st


def kernel(x_prompt, x_sample, c_prompt, c_sample, cache_k, cache_v, cache_kidx, state_gdn_conv, state_gdn,
           state_ssm_conv, state_ssm, state_ffn_conv, w_ada, b_ada, norm1_g, w_in, gdn_conv_w, gdn_A_log,
           gdn_dt_bias, gdn_norm_g, ssm_conv_w, ssm_conv_b, ssm_A_log, ssm_dt_bias, ssm_D, ssm_norm_g, w_out,
           norm2_g, w_gate, w_up, ffn_conv_w, ffn_conv_b, w_down, final_g):
    Bp, T, _ = x_prompt.shape
    Bs, Ts, _ = x_sample.shape
    P = cache_k.shape[2]

    c_all = jnp.concatenate([c_prompt, c_sample], axis=0)
    mod_all = _ada(c_all, w_ada, b_ada).reshape(DEPTH, Bp + Bs, 6, D_MODEL)

    tables_p = _rope_tables(jnp.arange(T))
    tables_s = _rope_tables(P + jnp.arange(Ts))

    w_perm = _permute_columns(w_in.astype(BF16))
    w_out_b = w_out.astype(BF16)
    w_gate_b = w_gate.astype(BF16)
    w_up_b = w_up.astype(BF16)
    w_down_b = w_down.astype(BF16)

    zeros_p = (jnp.zeros((Bp, CONV_W - 1, GDN_CONV_C), F32),
               jnp.zeros((Bp, GDN_HEADS, GDN_DK, GDN_DV), F32),
               jnp.zeros((Bp, CONV_W - 1, SSM_CONV_C), F32),
               jnp.zeros((Bp, SSM_HEADS, SSM_P, SSM_N), F32),
               jnp.zeros((Bp, FFN_CONV_W - 1, D_FF), F32))

    tf = D_FF // 2
    tiles_p = (1, min(256, T), math.gcd(Bp, 8), (1, min(512, T)), tf)
    tiles_s = (Bs, Ts, math.gcd(Bs, 4), (Bs, Ts), tf)

    xp, xs = x_prompt, x_sample
    new_p, new_s = [], []
    yp = ys = None
    for l in range(DEPTH):
        lw = (norm1_g[l], w_perm[l], gdn_conv_w[l], gdn_A_log[l], gdn_dt_bias[l], gdn_norm_g[l], ssm_conv_w[l],
              ssm_conv_b[l], ssm_A_log[l], ssm_dt_bias[l], ssm_D[l], ssm_norm_g[l], w_out_b[l], norm2_g[l],
              w_gate_b[l], w_up_b[l], ffn_conv_w[l], ffn_conv_b[l], w_down_b[l])
        final = l == DEPTH - 1
        xp, yp, st_p = _layer(xp, mod_all[l, :Bp], tables_p, None, zeros_p, lw, tiles_p, final, final_g)
        states_s = (state_gdn_conv[l], state_gdn[l], state_ssm_conv[l], state_ssm[l], state_ffn_conv[l])
        xs, ys, st_s = _layer(xs, mod_all[l, Bp:], tables_s, (cache_k[l], cache_v[l], cache_kidx[l]), states_s,
                              lw, tiles_s, final, final_g)
        new_p.append(st_p)
        new_s.append(st_s)
    outs_p = [jnp.stack([st[i] for st in new_p]) for i in range(8)]
    outs_s = [jnp.stack([st[i] for st in new_s]) for i in range(8)]
    return (yp, ys, *outs_p, *outs_s)
```

```python
import functools
import math

import jax
import jax.numpy as jnp
import numpy as np
from jax import lax
from jax.experimental import pallas as pl
from jax.experimental.pallas import tpu as pltpu

F32 = jnp.float32
BF16 = jnp.bfloat16
HI = lax.Precision.HIGHEST

D_MODEL = 1024
DEPTH = 2
CHUNK = 64
CONV_W = 4
FFN_CONV_W = 3
D_FF = 2816
ROPE_THETA = 10000.0
EPS = 1e-6
GDN_HEADS = 4
GDN_DK = 64
GDN_DV = 64
ATT_HEADS = 4
ATT_DH = 64
IDX_HEADS = 4
IDX_DH = 64
TOPK_MAX = 256
SSM_HEADS = 8
SSM_P = 64
SSM_GROUPS = 2
SSM_N = 128
GDN_W = GDN_HEADS * GDN_DV
ATT_W = ATT_HEADS * ATT_DH
SSM_W = SSM_HEADS * SSM_P
MIX_W = GDN_W + ATT_W + SSM_W
GDN_CONV_C = 2 * GDN_HEADS * GDN_DK + GDN_W
SSM_CONV_C = SSM_W + 2 * SSM_GROUPS * SSM_N
IN_SIZES = (GDN_CONV_C, GDN_HEADS, GDN_HEADS, GDN_W,
            ATT_W, ATT_W, ATT_W, IDX_HEADS * IDX_DH, IDX_DH, IDX_HEADS,
            SSM_W, SSM_CONV_C, SSM_HEADS)
IN_W = sum(IN_SIZES)

LANES = 128
SUBLANES = 8
VMEM_LIMIT = 56 * 1024 * 1024

KIS_A = IDX_DH
KIS_B = KIS_A + GDN_HEADS
KIS_WI = KIS_B + GDN_HEADS
KIS_DT = KIS_WI + IDX_HEADS
KIS_END = KIS_DT + SSM_HEADS
OUT_SEGS = (("gqkv", GDN_CONV_C), ("ggate", GDN_W), ("q", ATT_W), ("k", ATT_W), ("v", ATT_W),
            ("qi", IDX_HEADS * IDX_DH), ("kis", LANES), ("z", SSM_W), ("xbc", SSM_CONV_C))
PERM_W = sum(w for _, w in OUT_SEGS)
ROT_W = ATT_W + LANES


def _perm_columns():
    starts = np.concatenate([[0], np.cumsum(IN_SIZES)])
    (s_gqkv, s_ga, s_gb, s_gg, s_q, s_k, s_v, s_qi, s_ki, s_wi, s_z, s_xbc, s_dt) = starts[:-1]
    cols = []
    cols += list(range(s_gqkv, s_gqkv + GDN_CONV_C))
    cols += list(range(s_gg, s_gg + GDN_W))
    cols += list(range(s_q, s_q + ATT_W))
    cols += list(range(s_k, s_k + ATT_W))
    cols += list(range(s_v, s_v + ATT_W))
    cols += list(range(s_qi, s_qi + IDX_HEADS * IDX_DH))
    kis = (list(range(s_ki, s_ki + IDX_DH)) + list(range(s_ga, s_ga + GDN_HEADS))
           + list(range(s_gb, s_gb + GDN_HEADS)) + list(range(s_wi, s_wi + IDX_HEADS))
           + list(range(s_dt, s_dt + SSM_HEADS)))
    cols += kis + [-1] * (LANES - len(kis))
    cols += list(range(s_z, s_z + SSM_W))
    cols += list(range(s_xbc, s_xbc + SSM_CONV_C))
    assert len(cols) == PERM_W
    return np.asarray(cols, np.int32)


_PERM_COLS = _perm_columns()


def _permute_columns(w_in):
    pieces = []
    start = 0
    for i in range(1, PERM_W + 1):
        prev = int(_PERM_COLS[i - 1])
        if i == PERM_W or not ((prev < 0 and _PERM_COLS[i] < 0) or (prev >= 0 and _PERM_COLS[i] == prev + 1)):
            if _PERM_COLS[start] < 0:
                pieces.append(jnp.zeros(w_in.shape[:-1] + (i - start,), w_in.dtype))
            else:
                pieces.append(w_in[..., int(_PERM_COLS[start]):int(_PERM_COLS[start]) + i - start])
            start = i
    return jnp.concatenate(pieces, axis=-1)


def _silu(x):
    return x * jax.nn.sigmoid(x)


def _softplus(x):
    return jnp.maximum(x, 0.0) + jnp.log1p(jnp.exp(-jnp.abs(x)))


def _dot(a, b, precision=None):
    return jnp.dot(a, b, preferred_element_type=F32, precision=precision)


def _dot_nt(a, b, precision=None):
    return lax.dot_general(a, b, (((1,), (1,)), ((), ())), preferred_element_type=F32,
                           precision=precision)


def _dot_tn(a, b, precision=None):
    return lax.dot_general(a, b, (((0,), (0,)), ((), ())), preferred_element_type=F32,
                           precision=precision)


def _cumsum_rows(x):
    c = x.shape[0]
    row = lax.broadcasted_iota(jnp.int32, x.shape, 0)
    s = 1
    while s < c:
        x = x + jnp.where(row >= s, pltpu.roll(x, s, axis=0), 0.0)
        s *= 2
    return x


def _mm(a, b):
    return _dot(a.astype(BF16), b.astype(BF16))


def _mm_nt(a, b):
    return _dot_nt(a.astype(BF16), b.astype(BF16))


def _split(a):
    hi = a.astype(BF16)
    return hi, (a - hi.astype(F32)).astype(BF16)


def _mm3(a, b):
    (ah, al), (bh, bl) = a, b
    return _dot(ah, bh) + (_dot(ah, bl) + _dot(al, bh))


def _lane_spread(vals, lane0, n_heads):
    width = n_heads * GDN_DV
    r = lax.broadcasted_iota(jnp.int32, (LANES, width), 0)
    cidx = lax.broadcasted_iota(jnp.int32, (LANES, width), 1)
    pick = (r == lane0 + cidx // GDN_DV).astype(BF16)
    p1 = vals.astype(BF16)
    r1 = vals - p1.astype(F32)
    p2 = r1.astype(BF16)
    p3 = (r1 - p2.astype(F32)).astype(BF16)
    return _dot(p1, pick) + (_dot(p2, pick) + _dot(p3, pick))


def _head_sums(x):
    r = lax.broadcasted_iota(jnp.int32, (LANES, LANES), 0) // GDN_DV
    cidx = lax.broadcasted_iota(jnp.int32, (LANES, LANES), 1) // GDN_DV
    ones = (r == cidx).astype(BF16)
    hi, lo = _split(x)
    cols = []
    for g in range(x.shape[1] // LANES):
        sl = slice(g * LANES, (g + 1) * LANES)
        cols.append(_dot(hi[:, sl], ones) + _dot(lo[:, sl], ones))
    return jnp.concatenate(cols, axis=1)


def _transpose_rows(vals):
    c = vals.shape[0]
    if c < LANES:
        vals = jnp.concatenate([vals, jnp.zeros((LANES - c, LANES), vals.dtype)], axis=0)
    return vals.T[:, :c]


def _lane_vector(vals, start):
    return jnp.zeros((1, LANES), F32).at[0, start:start + vals.shape[0]].set(vals.astype(F32))


def _params(sem):
    return pltpu.CompilerParams(dimension_semantics=sem, vmem_limit_bytes=VMEM_LIMIT)


def _ada_kernel(c_ref, w_ref, b_ref, o_ref):
    s = _silu(c_ref[...])
    o_ref[0] = _dot(s.astype(BF16), w_ref[0].astype(BF16)) + b_ref[0]


def _ada(c_all, w_ada, b_ada):
    rows = c_all.shape[0]
    n = w_ada.shape[2]
    tn = 1536
    return pl.pallas_call(
        _ada_kernel,
        grid=(DEPTH, n // tn),
        in_specs=[pl.BlockSpec((rows, D_MODEL), lambda l, j: (0, 0)),
                  pl.BlockSpec((1, D_MODEL, tn), lambda l, j: (l, 0, j)),
                  pl.BlockSpec((1, 1, tn), lambda l, j: (l, 0, j))],
        out_specs=pl.BlockSpec((1, rows, tn), lambda l, j: (l, 0, j)),
        out_shape=jax.ShapeDtypeStruct((DEPTH, rows, n), F32),
        compiler_params=_params(("arbitrary", "arbitrary")),
        name="ada",
    )(c_all, w_ada, b_ada.reshape(DEPTH, 1, n))


def _rotate(x, cos, sin_signed):
    w = x.shape[-1]
    lane = lax.broadcasted_iota(jnp.int32, x.shape, x.ndim - 1)
    first = (lane % ATT_DH) < (ATT_DH // 2)
    swapped = jnp.where(first, pltpu.roll(x, w - ATT_DH // 2, axis=x.ndim - 1),
                        pltpu.roll(x, ATT_DH // 2, axis=x.ndim - 1))
    return x * cos + swapped * sin_signed


def _inproj_kernel(x_ref, mod_ref, g_ref, w_ref, cos_ref, sin_ref, *refs, nb, tt, emit_keys, n_alias):
    out_refs = refs[n_alias:]
    x = x_ref[...]
    ms = jnp.mean(x * x, axis=-1, keepdims=True)
    xn = x * lax.rsqrt(ms + EPS) * g_ref[...]
    mod = mod_ref[...]
    h = xn * (1.0 + mod[:, 1:2, :]) + mod[:, 0:1, :]
    u = _dot(h.reshape(nb * tt, D_MODEL).astype(BF16), w_ref[...])
    cos = cos_ref[...]
    sin = sin_ref[...]
    off = 0
    segs = {}
    for (name, width), o_ref in zip(OUT_SEGS, out_refs):
        seg = u[:, off:off + width]
        if name in ("q", "k", "qi", "kis"):
            t0 = ATT_W if name == "kis" else 0
            c3 = cos[:, t0:t0 + width][None]
            s3 = sin[:, t0:t0 + width][None]
            if nb > 1:
                c3 = jnp.broadcast_to(c3, (nb, tt, width)).reshape(nb * tt, width)
                s3 = jnp.broadcast_to(s3, (nb, tt, width)).reshape(nb * tt, width)
            else:
                c3 = c3[0]
                s3 = s3[0]
            seg = _rotate(seg, c3, s3)
        o_ref[...] = seg.reshape(o_ref.shape)
        segs[name] = seg
        off += width
    ki_ref = out_refs[len(OUT_SEGS)]
    ki_ref[...] = segs["kis"][:, :IDX_DH].reshape(ki_ref.shape)
    if emit_keys:
        k_b_ref, ki_b_ref, vt_b_ref = out_refs[len(OUT_SEGS) + 1:]
        k_b_ref[0] = segs["k"].astype(BF16)
        ki_b_ref[0] = segs["kis"][:, :IDX_DH].astype(BF16)
        vt_b_ref[0] = segs["v"].T.astype(BF16)


STACKED = ("k", "v", "ki")


def _inproj(x, mod, g1, w_perm, cos, sin, nb, tt, emit_keys, layer, stacks):
    B, T, _ = x.shape
    grid = (B // nb, T // tt)
    names = [n for n, _ in OUT_SEGS] + ["ki"]
    widths = [w for _, w in OUT_SEGS] + [IDX_DH]
    out_shape, out_specs = (), ()
    for n, w in zip(names, widths):
        if n in STACKED:
            out_shape += (jax.ShapeDtypeStruct((DEPTH, B, T, w), F32),)
            out_specs += (pl.BlockSpec((1, nb, tt, w), lambda b, t: (layer, b, t, 0)),)
        else:
            out_shape += (jax.ShapeDtypeStruct((B, T, w), F32),)
            out_specs += (pl.BlockSpec((nb, tt, w), lambda b, t: (b, t, 0)),)
    alias_in = () if stacks is None else tuple(stacks)
    n_in = 6
    aliases = {n_in + i: names.index(n) for i, n in enumerate(STACKED)} if alias_in else {}
    if emit_keys:
        assert nb == 1
        out_shape += (jax.ShapeDtypeStruct((B, T, ATT_W), BF16), jax.ShapeDtypeStruct((B, T, IDX_DH), BF16),
                      jax.ShapeDtypeStruct((B, ATT_W, T), BF16))
        out_specs += (pl.BlockSpec((1, tt, ATT_W), lambda b, t: (b, t, 0)),
                      pl.BlockSpec((1, tt, IDX_DH), lambda b, t: (b, t, 0)),
                      pl.BlockSpec((1, ATT_W, tt), lambda b, t: (b, 0, t)))
    return pl.pallas_call(
        functools.partial(_inproj_kernel, nb=nb, tt=tt, emit_keys=emit_keys, n_alias=len(alias_in)),
        grid=grid,
        in_specs=[pl.BlockSpec((nb, tt, D_MODEL), lambda b, t: (b, t, 0)),
                  pl.BlockSpec((nb, 6, D_MODEL), lambda b, t: (b, 0, 0)),
                  pl.BlockSpec((1, D_MODEL), lambda b, t: (0, 0)),
                  pl.BlockSpec((D_MODEL, PERM_W), lambda b, t: (0, 0)),
                  pl.BlockSpec((tt, ROT_W), lambda b, t: (t, 0)),
                  pl.BlockSpec((tt, ROT_W), lambda b, t: (t, 0))]
        + [pl.BlockSpec(memory_space=pl.ANY)] * len(alias_in),
        out_specs=out_specs,
        out_shape=out_shape,
        input_output_aliases=aliases,
        compiler_params=_params(("arbitrary", "arbitrary")),
        name="inproj",
    )(x, mod, g1, w_perm, cos, sin, *alias_in)


def _short_conv(ext_ref, bi, u, w, c):
    lo = SUBLANES - (CONV_W - 1)
    ext_ref[bi, SUBLANES:SUBLANES + c, :] = u
    y = ext_ref[bi, lo:lo + c, :] * w[0:1, :]
    for j in range(1, CONV_W):
        y = y + ext_ref[bi, lo + j:lo + j + c, :] * w[j:j + 1, :]
    tail = ext_ref[bi, c + lo:c + SUBLANES, :]
    ext_ref[bi, lo:SUBLANES, :] = tail
    return y, tail


def _neumann_inverses(mats, c):
    eye = (lax.broadcasted_iota(jnp.int32, (c, c), 0)
           == lax.broadcasted_iota(jnp.int32, (c, c), 1)).astype(F32)
    ps = [eye - a for a in mats]
    sp = [_split(a) for a in mats]
    pws = [_mm3(s, s) for s in sp]
    n = 2
    while n < c:
        sp = [_split(pw) for pw in pws]
        ps = [p + _mm3(_split(p), s) for p, s in zip(ps, sp)]
        n *= 2
        if n < c:
            pws = [_mm3(s, s) for s in sp]
    return ps


def _gdn_kernel(qkv_ref, kis_ref, gate_ref, w_ref, prev_ref, s0_ref, alog_ref, dtb_ref, ng_ref,
                o_ref, new_ref, s_ref, ext_ref, s_sc, *, c, nbb):
    j = pl.program_id(1)
    lo = SUBLANES - (CONV_W - 1)

    @pl.when(j == 0)
    def _():
        s_sc[...] = s0_ref[...]
        ext_ref[:, lo:SUBLANES, :] = prev_ref[...]

    ng = ng_ref[...]
    w_conv = w_ref[...]
    ri = lax.broadcasted_iota(jnp.int32, (c, c), 0)
    ci = lax.broadcasted_iota(jnp.int32, (c, c), 1)
    incl = ri >= ci
    strict = ri > ci
    nkd = GDN_HEADS * GDN_DK
    ch = []
    for bi in range(nbb):
        y, tail = _short_conv(ext_ref, bi, qkv_ref[bi], w_conv, c)
        new_ref[bi] = tail
        y = _silu(y)
        small = kis_ref[bi]
        g = -jnp.exp(alog_ref[...]) * _softplus(small + dtb_ref[...])
        beta = jax.nn.sigmoid(small)
        gc = _cumsum_rows(g)
        gc_t = _transpose_rows(gc)
        qk_n = y[:, :2 * nkd]
        qk_n = qk_n * lax.rsqrt(_head_sums(qk_n * qk_n) + EPS)
        for h in range(GDN_HEADS):
            qh = qk_n[:, h * GDN_DK:(h + 1) * GDN_DK] * (GDN_DK ** -0.5)
            kh = qk_n[:, nkd + h * GDN_DK:nkd + (h + 1) * GDN_DK]
            vh = y[:, 2 * nkd + h * GDN_DV:2 * nkd + (h + 1) * GDN_DV]
            ch.append((qh, kh, vh, gc[:, KIS_A + h:KIS_A + h + 1], gc_t[KIS_A + h:KIS_A + h + 1, :],
                       beta[:, KIS_B + h:KIS_B + h + 1]))
    ids = [(bi, h) for bi in range(nbb) for h in range(GDN_HEADS)]
    decay = [jnp.where(incl, jnp.exp(jnp.where(incl, gcol - grow, 0.0)), 0.0)
             for (_, _, _, gcol, grow, _) in ch]
    kb = [kh * bcol for (_, kh, _, _, _, bcol) in ch]
    eg = [jnp.exp(gcol) for (_, _, _, gcol, _, _) in ch]
    grams = [_mm_nt(jnp.concatenate([kbi, x[0]], axis=0), x[1]) for kbi, x in zip(kb, ch)]
    t_inv = _neumann_inverses([jnp.where(strict, m[:c] * d, 0.0) for m, d in zip(grams, decay)], c)
    attn = [jnp.where(incl, m[c:] * d, 0.0) for m, d in zip(grams, decay)]
    rhs = [jnp.concatenate([x[2] * x[5], kbi * e], axis=1) for x, kbi, e in zip(ch, kb, eg)]
    sol = [_mm(t, r) for t, r in zip(t_inv, rhs)]
    st = [s_sc[bi, h] for bi, h in ids]
    ws = [_mm(sl[:, GDN_DV:], s) for sl, s in zip(sol, st)]
    qs = [_mm(x[0] * e, s) for x, e, s in zip(ch, eg, st)]
    v_new = [sl[:, :GDN_DV] - w for sl, w in zip(sol, ws)]
    av = [_mm(a, v) for a, v in zip(attn, v_new)]
    g_last = [x[3][c - 1:c, :] for x in ch]
    kv = [_dot_tn(x[1] * jnp.exp(gl - x[3]), v) for x, gl, v in zip(ch, g_last, v_new)]
    for (bi, h), s, gl, kvi in zip(ids, st, g_last, kv):
        s_new = s * jnp.exp(gl) + kvi
        s_sc[bi, h] = s_new
        s_ref[bi, h] = s_new
    for bi in range(nbb):
        o = jnp.concatenate([qs[bi * GDN_HEADS + h] + av[bi * GDN_HEADS + h] for h in range(GDN_HEADS)], axis=1)
        o = o * lax.rsqrt(_head_sums(o * o) * (1.0 / GDN_DV) + EPS) * ng
        o_ref[bi] = o * _silu(gate_ref[bi])


def _gdn(gqkv, kis, ggate, conv_w, conv_prev, s0, a_log, dt_bias, norm_g, nbb):
    B, T, _ = gqkv.shape
    c = math.gcd(T, CHUNK)
    return pl.pallas_call(
        functools.partial(_gdn_kernel, c=c, nbb=nbb),
        grid=(B // nbb, T // c),
        in_specs=[pl.BlockSpec((nbb, c, GDN_CONV_C), lambda b, j: (b, j, 0)),
                  pl.BlockSpec((nbb, c, LANES), lambda b, j: (b, j, 0)),
                  pl.BlockSpec((nbb, c, GDN_W), lambda b, j: (b, j, 0)),
                  pl.BlockSpec((CONV_W, GDN_CONV_C), lambda b, j: (0, 0)),
                  pl.BlockSpec((nbb, CONV_W - 1, GDN_CONV_C), lambda b, j: (b, 0, 0)),
                  pl.BlockSpec((nbb, GDN_HEADS, GDN_DK, GDN_DV), lambda b, j: (b, 0, 0, 0)),
                  pl.BlockSpec((1, LANES), lambda b, j: (0, 0)),
                  pl.BlockSpec((1, LANES), lambda b, j: (0, 0)),
                  pl.BlockSpec((1, GDN_W), lambda b, j: (0, 0))],
        out_specs=(pl.BlockSpec((nbb, c, GDN_W), lambda b, j: (b, j, 0)),
                   pl.BlockSpec((nbb, CONV_W - 1, GDN_CONV_C), lambda b, j: (b, 0, 0)),
                   pl.BlockSpec((nbb, GDN_HEADS, GDN_DK, GDN_DV), lambda b, j: (b, 0, 0, 0))),
        out_shape=(jax.ShapeDtypeStruct((B, T, GDN_W), F32),
                   jax.ShapeDtypeStruct((B, CONV_W - 1, GDN_CONV_C), F32),
                   jax.ShapeDtypeStruct((B, GDN_HEADS, GDN_DK, GDN_DV), F32)),
        scratch_shapes=[pltpu.VMEM((nbb, SUBLANES + c, GDN_CONV_C), F32),
                        pltpu.VMEM((nbb, GDN_HEADS, GDN_DK, GDN_DV), F32)],
        compiler_params=_params(("arbitrary", "arbitrary")),
        name="gdn",
    )(gqkv, kis, ggate, conv_w, conv_prev, s0, _lane_vector(a_log, KIS_A), _lane_vector(dt_bias, KIS_A),
      jnp.tile(norm_g, GDN_HEADS).reshape(1, -1))


def _ssd_kernel(xbc_ref, z_ref, kis_ref, w_ref, cb_ref, prev_ref, h0_ref, alog_ref, dtb_ref, dsk_ref,
                ng_ref, o_ref, new_ref, h_ref, ext_ref, h_sc, *, c, nbb):
    j = pl.program_id(1)
    lo = SUBLANES - (CONV_W - 1)

    @pl.when(j == 0)
    def _():
        h_sc[...] = h0_ref[...]
        ext_ref[:, lo:SUBLANES, :] = prev_ref[...]

    dsk = dsk_ref[...]
    ng = ng_ref[...]
    w_conv = w_ref[...]
    ri = lax.broadcasted_iota(jnp.int32, (c, c), 0)
    ci = lax.broadcasted_iota(jnp.int32, (c, c), 1)
    incl = ri >= ci
    gn = SSM_GROUPS * SSM_N
    rep = SSM_HEADS // SSM_GROUPS
    gw = SSM_W // SSM_GROUPS
    grp_in = []
    ch = []
    slot = []
    ids = []
    for bi in range(nbb):
        y, tail = _short_conv(ext_ref, bi, xbc_ref[bi], w_conv, c)
        new_ref[bi] = tail
        y = _silu(y + cb_ref[...])
        dts = _softplus(kis_ref[bi] + dtb_ref[...])
        a = dts * (-jnp.exp(alog_ref[...]))
        acum = _cumsum_rows(a)
        acum_t = _transpose_rows(acum)
        dtm = _lane_spread(dts, KIS_DT, SSM_HEADS)
        am = _lane_spread(acum, KIS_DT, SSM_HEADS)
        a_last = am[c - 1:c, :]
        xs = y[:, :SSM_W]
        xdt = xs * dtm
        xw = xs * (jnp.exp(a_last - am) * dtm)
        h_dec = jnp.exp(a_last)
        slot.append((xs, jnp.exp(am), _silu(z_ref[bi])))
        for grp in range(SSM_GROUPS):
            grp_in.append((y[:, SSM_W + grp * SSM_N:SSM_W + (grp + 1) * SSM_N],
                           y[:, SSM_W + gn + grp * SSM_N:SSM_W + gn + (grp + 1) * SSM_N]))
            for hh in range(rep):
                h = grp * rep + hh
                hsl = slice(h * SSM_P, (h + 1) * SSM_P)
                ch.append((xdt[:, hsl], xw[:, hsl], am[:, h * SSM_P:h * SSM_P + c], acum_t[KIS_DT + h:KIS_DT + h + 1, :],
                           jnp.concatenate([h_dec[:, hsl]] * (SSM_N // SSM_P), axis=1), len(grp_in) - 1))
                ids.append((bi, h))
    cb = [_mm_nt(cg, bg) for bg, cg in grp_in]
    seg = [jnp.where(incl, jnp.exp(jnp.where(incl, x[2] - x[3], 0.0)), 0.0) for x in ch]
    hs = [h_sc[bi, h] for bi, h in ids]
    y_in = [_mm(cb[x[5]] * sg, x[0]) for x, sg in zip(ch, seg)]
    y_st = [_mm_nt(grp_in[x[5]][1], hst) for x, hst in zip(ch, hs)]
    upd = [_dot_tn(x[1], grp_in[x[5]][0]) for x in ch]
    for (bi, h), x, hst, up in zip(ids, ch, hs, upd):
        h_new = hst * x[4] + up
        h_sc[bi, h] = h_new
        h_ref[bi, h] = h_new
    for bi, (xs, ea, zg) in enumerate(slot):
        lo_c, hi_c = bi * SSM_HEADS, (bi + 1) * SSM_HEADS
        yy = (jnp.concatenate(y_in[lo_c:hi_c], axis=1) + jnp.concatenate(y_st[lo_c:hi_c], axis=1) * ea
              + dsk * xs) * zg
        for grp in range(SSM_GROUPS):
            yg = yy[:, grp * gw:(grp + 1) * gw]
            yg = yg * lax.rsqrt(jnp.mean(yg * yg, axis=-1, keepdims=True) + EPS)
            o_ref[bi, :, grp * gw:(grp + 1) * gw] = yg * ng[:, grp * gw:(grp + 1) * gw]


def _ssd(xbc, z, kis, conv_w, conv_b, conv_prev, h0, a_log, dt_bias, d_skip, norm_g, nbb):
    B, T, _ = xbc.shape
    c = math.gcd(T, CHUNK)
    return pl.pallas_call(
        functools.partial(_ssd_kernel, c=c, nbb=nbb),
        grid=(B // nbb, T // c),
        in_specs=[pl.BlockSpec((nbb, c, SSM_CONV_C), lambda b, j: (b, j, 0)),
                  pl.BlockSpec((nbb, c, SSM_W), lambda b, j: (b, j, 0)),
                  pl.BlockSpec((nbb, c, LANES), lambda b, j: (b, j, 0)),
                  pl.BlockSpec((CONV_W, SSM_CONV_C), lambda b, j: (0, 0)),
                  pl.BlockSpec((1, SSM_CONV_C), lambda b, j: (0, 0)),
                  pl.BlockSpec((nbb, CONV_W - 1, SSM_CONV_C), lambda b, j: (b, 0, 0)),
                  pl.BlockSpec((nbb, SSM_HEADS, SSM_P, SSM_N), lambda b, j: (b, 0, 0, 0)),
                  pl.BlockSpec((1, LANES), lambda b, j: (0, 0)),
                  pl.BlockSpec((1, LANES), lambda b, j: (0, 0)),
                  pl.BlockSpec((1, SSM_W), lambda b, j: (0, 0)),
                  pl.BlockSpec((1, SSM_W), lambda b, j: (0, 0))],
        out_specs=(pl.BlockSpec((nbb, c, SSM_W), lambda b, j: (b, j, 0)),
                   pl.BlockSpec((nbb, CONV_W - 1, SSM_CONV_C), lambda b, j: (b, 0, 0)),
                   pl.BlockSpec((nbb, SSM_HEADS, SSM_P, SSM_N), lambda b, j: (b, 0, 0, 0))),
        out_shape=(jax.ShapeDtypeStruct((B, T, SSM_W), F32),
                   jax.ShapeDtypeStruct((B, CONV_W - 1, SSM_CONV_C), F32),
                   jax.ShapeDtypeStruct((B, SSM_HEADS, SSM_P, SSM_N), F32)),
        scratch_shapes=[pltpu.VMEM((nbb, SUBLANES + c, SSM_CONV_C), F32),
                        pltpu.VMEM((nbb, SSM_HEADS, SSM_P, SSM_N), F32)],
        compiler_params=_params(("arbitrary", "arbitrary")),
        name="ssd",
    )(xbc, z, kis, conv_w, conv_b.reshape(1, -1), conv_prev, h0, _lane_vector(a_log, KIS_DT),
      _lane_vector(dt_bias, KIS_DT), jnp.repeat(d_skip, SSM_P).reshape(1, -1), norm_g.reshape(1, -1))


IDX_BITS = 12


OPENING_PROBES = 14
ROUND_PROBES = 2
SEARCH_ROUNDS = 4096
FOLD_ROWS = 128


def _fold_keys(x, op, axis=0):
    s = x.shape[0]
    if axis == 0 and s % FOLD_ROWS == 0 and s > FOLD_ROWS:
        x = op(x.reshape(s // FOLD_ROWS, FOLD_ROWS, x.shape[1]), axis=0)
    return op(x, axis=axis, keepdims=True)


def _fold_rows(x, op):
    return _fold_keys(x, op, 0)


def _select_topk(score, adm, pos, skip, topk, axis):
    def count(mask):
        return _fold_keys(mask.astype(F32), jnp.sum, axis)

    s = jnp.where(adm, score, -jnp.inf)

    few = (count(adm) <= topk) | skip
    smax = _fold_keys(s, jnp.max, axis)
    smin = _fold_keys(jnp.where(adm, score, jnp.inf), jnp.min, axis)
    top_full = count(s >= smax) >= topk
    lo0 = jnp.where(few, 0.0, jnp.where(top_full, smax, smin))
    hi0 = jnp.where(few, 0.0, smax)
    done0 = (few | top_full).astype(F32)

    def probe(lo, hi, done):
        mid = 0.5 * lo + 0.5 * hi
        ok = (mid > lo) & (mid < hi) & (done < 0.5)
        take = count(s >= mid) >= topk
        return jnp.where(ok & take, mid, lo), jnp.where(ok & jnp.logical_not(take), mid, hi)

    def opening(lo, hi):
        for _ in range(OPENING_PROBES):
            lo, hi = probe(lo, hi, done0)
        return lo, hi

    lo1, hi1 = lax.cond(jnp.min(done0) < 0.5, opening, lambda lo, hi: (lo, hi), lo0, hi0)

    def search_cond(c):
        return (jnp.min(c[2]) < 0.5) & (c[3] < SEARCH_ROUNDS)

    def search_body(c):
        lo, hi, done, it = c
        last = _fold_keys(jnp.where(s < hi, s, -jnp.inf), jnp.max, axis)
        reached = count(s >= last) >= topk
        active = done < 0.5
        lo = jnp.where(active & reached, last, lo)
        hi = jnp.where(active & jnp.logical_not(reached), last, hi)
        done = jnp.maximum(done, reached.astype(F32))
        for _ in range(ROUND_PROBES):
            lo, hi = probe(lo, hi, done)
        return lo, hi, done, it + 1

    lo, _, _, _ = lax.while_loop(search_cond, search_body, (lo1, hi1, done0, jnp.int32(0)))
    t = jnp.where(few, -jnp.inf, lo)
    gt = s > t
    tie = s == t
    need = topk - count(gt)
    excess = jnp.where(few, 0.0, count(tie) - need)

    def tie_search():
        def ibody(i, m):
            cand = m | (jnp.int32(1) << (IDX_BITS - 1 - i))
            return jnp.where(count(tie & (pos < cand)) <= need, cand, m)

        return lax.fori_loop(0, IDX_BITS, ibody, jnp.zeros(few.shape, jnp.int32))

    m = lax.cond(jnp.max(excess) > 0.0, tie_search, lambda: jnp.full(few.shape, 2 ** IDX_BITS - 1, jnp.int32))
    return adm & (gt | (tie & (pos < m)))


def _dsa_kernel(q_ref, qi_ref, kis_ref, k_ref, vt_ref, ki_ref, *refs, tq, s_len, s_valid, q_offset, j0,
                n_valid_q, topk):
    o_ref = refs[-1]
    j = pl.program_id(1)
    lane = lax.broadcasted_iota(jnp.int32, (1, tq), 1)
    qpos = q_offset + (j0 + j) * tq + lane
    lim = jnp.minimum((qpos // CHUNK + 1) * CHUNK, s_valid)
    row = lax.broadcasted_iota(jnp.int32, (s_len, tq), 0)
    adm = row < lim

    qi_t = qi_ref[0].T.astype(BF16)
    kis_t = kis_ref[0].T
    ki = ki_ref[0]
    assert IDX_DH == 64 and IDX_HEADS == 4 and ATT_DH == 64
    wi = kis_t[KIS_WI:KIS_WI + IDX_HEADS, :] * (IDX_DH ** -0.5 * IDX_HEADS ** -0.5)
    qi_all = jnp.concatenate([qi_t[h * IDX_DH:(h + 1) * IDX_DH, :] for h in range(IDX_HEADS)], axis=1)
    rel_all = _dot(ki, qi_all)
    score = jnp.zeros((s_len, tq), F32)
    for h in range(IDX_HEADS):
        score = score + jnp.maximum(rel_all[:, h * tq:(h + 1) * tq], 0.0) * wi[h:h + 1, :]
    sel = _select_topk(score, adm, row, lane >= n_valid_q, topk, 0)

    q_t = q_ref[0].T * (ATT_DH ** -0.5)
    pair = LANES // ATT_DH
    head_in_pair = lax.broadcasted_iota(jnp.int32, (LANES, tq), 0) // ATT_DH
    logits = []
    for g in range(ATT_HEADS // pair):
        qg = q_t[g * LANES:(g + 1) * LANES, :]
        qm = jnp.concatenate([jnp.where(head_in_pair == i, qg, 0.0) for i in range(pair)], axis=1)
        lg = _dot(k_ref[0, :, g * LANES:(g + 1) * LANES], qm.astype(BF16))
        logits += [lg[:, i * tq:(i + 1) * tq] for i in range(pair)]
    probs, scales = [], []
    for lg in logits:
        lg = jnp.where(sel, lg, -jnp.inf)
        p = jnp.exp(lg - _fold_rows(lg, jnp.max))
        scales.append(1.0 / _fold_rows(p, jnp.sum))
        probs.append(p.astype(BF16))
    outs = [_dot(vt_ref[0, h * ATT_DH:(h + 1) * ATT_DH, :], probs[h]) * scales[h] for h in range(ATT_HEADS)]
    o_ref[0] = jnp.concatenate(outs, axis=0).T


def _dsa(q, qi, kis, k_b, vt_b, ki_b, out_prev, *, tq, j0, nj, s_len, s_valid, q_offset, n_valid_q, topk):
    B, t_all, _ = q.shape
    assert s_len < 2 ** IDX_BITS and s_len <= k_b.shape[1]
    alias_in = () if out_prev is None else (out_prev,)
    return pl.pallas_call(
        functools.partial(_dsa_kernel, tq=tq, s_len=s_len, s_valid=s_valid, q_offset=q_offset, j0=j0,
                          n_valid_q=n_valid_q, topk=topk),
        grid=(B, nj),
        in_specs=[pl.BlockSpec((1, tq, ATT_W), lambda b, j: (b, j0 + j, 0)),
                  pl.BlockSpec((1, tq, IDX_HEADS * IDX_DH), lambda b, j: (b, j0 + j, 0)),
                  pl.BlockSpec((1, tq, LANES), lambda b, j: (b, j0 + j, 0)),
                  pl.BlockSpec((1, s_len, ATT_W), lambda b, j: (b, 0, 0)),
                  pl.BlockSpec((1, ATT_W, s_len), lambda b, j: (b, 0, 0)),
                  pl.BlockSpec((1, s_len, IDX_DH), lambda b, j: (b, 0, 0))]
        + [pl.BlockSpec(memory_space=pl.ANY)] * len(alias_in),
        out_specs=pl.BlockSpec((1, tq, ATT_W), lambda b, j: (b, j0 + j, 0)),
        out_shape=jax.ShapeDtypeStruct((B, t_all, ATT_W), F32),
        input_output_aliases={6: 0} if alias_in else {},
        compiler_params=_params(("arbitrary", "arbitrary")),
        name="dsa",
    )(q, qi, kis, k_b, vt_b, ki_b, *alias_in)


def _dsa_rows_kernel(q_ref, qi_ref, kis_ref, k_ref, v_ref, ck_ref, cv_ref, cki_ref, o_ref, *, nbq, tq, s_valid,
                     q_offset, topk):
    rows = nbq * tq
    s_len = q_offset + LANES
    qpos = q_offset + lax.broadcasted_iota(jnp.int32, (rows, 1), 0) % tq
    lim = jnp.minimum((qpos // CHUNK + 1) * CHUNK, s_valid)
    col = lax.broadcasted_iota(jnp.int32, (rows, s_len), 1)
    adm = col < lim

    def with_new(cache, new):
        return jnp.concatenate([cache, new.astype(BF16),
                                jnp.zeros((LANES - tq, new.shape[1]), BF16)], axis=0)

    scores = []
    for b in range(nbq):
        qi = qi_ref[b]
        qi_st = jnp.concatenate([qi[:, h * IDX_DH:(h + 1) * IDX_DH] for h in range(IDX_HEADS)], axis=0)
        rel = _dot_nt(qi_st.astype(BF16), with_new(cki_ref[b], kis_ref[b][:, :IDX_DH]))
        wi = kis_ref[b][:, KIS_WI:KIS_WI + IDX_HEADS] * (IDX_DH ** -0.5 * IDX_HEADS ** -0.5)
        sc = jnp.maximum(rel[:tq, :], 0.0) * wi[:, 0:1]
        for h in range(1, IDX_HEADS):
            sc = sc + jnp.maximum(rel[h * tq:(h + 1) * tq, :], 0.0) * wi[:, h:h + 1]
        scores.append(sc)
    sel = _select_topk(jnp.concatenate(scores, axis=0), adm, col, qpos < 0, topk, 1)

    head_of_lane = lax.broadcasted_iota(jnp.int32, (tq, ATT_W), 1) // ATT_DH
    for b in range(nbq):
        q = q_ref[b] * (ATT_DH ** -0.5)
        q_st = jnp.concatenate([jnp.where(head_of_lane == h, q, 0.0) for h in range(ATT_HEADS)], axis=0)
        logits = _dot_nt(q_st.astype(BF16), with_new(ck_ref[b], k_ref[b]))
        logits = jnp.where(jnp.concatenate([sel[b * tq:(b + 1) * tq, :]] * ATT_HEADS, axis=0), logits, -jnp.inf)
        p = jnp.exp(logits - jnp.max(logits, axis=1, keepdims=True))
        scale = 1.0 / jnp.sum(p, axis=1, keepdims=True)
        pv = _dot(p.astype(BF16), with_new(cv_ref[b], v_ref[b])) * scale
        out = jnp.zeros((tq, ATT_W), F32)
        for h in range(ATT_HEADS):
            out = jnp.where(head_of_lane == h, pv[h * tq:(h + 1) * tq, :], out)
        o_ref[b] = out


def _dsa_rows(q, qi, kis, k, v, ck, cv, cki, *, nbq, topk):
    B, T, _ = q.shape
    P = ck.shape[1]
    assert P + LANES < 2 ** IDX_BITS and T % (2 * SUBLANES) == 0 and T <= LANES and P % (2 * SUBLANES) == 0
    assert B % nbq == 0
    new_spec = lambda w: pl.BlockSpec((nbq, T, w), lambda b: (b, 0, 0))
    cache_spec = lambda w: pl.BlockSpec((nbq, P, w), lambda b: (b, 0, 0))
    return pl.pallas_call(
        functools.partial(_dsa_rows_kernel, nbq=nbq, tq=T, s_valid=P + T, q_offset=P, topk=topk),
        grid=(B // nbq,),
        in_specs=[new_spec(ATT_W), new_spec(IDX_HEADS * IDX_DH), new_spec(LANES), new_spec(ATT_W), new_spec(ATT_W),
                  cache_spec(ATT_W), cache_spec(ATT_W), cache_spec(IDX_DH)],
        out_specs=new_spec(ATT_W),
        out_shape=jax.ShapeDtypeStruct((B, T, ATT_W), F32),
        compiler_params=_params(("arbitrary",)),
        name="dsa_rows",
    )(q, qi, kis, k, v, ck, cv, cki)


def _dsa_group(q, qi, kis, k_b, vt_b, ki_b, *, s_valid, q_offset, topk):
    B, T, _ = q.shape
    tq = 2 * LANES if T % (2 * LANES) == 0 else LANES
    n_valid_q = min(T, tq)
    if T < tq:
        padq = lambda a: jnp.pad(a, ((0, 0), (0, tq - T), (0, 0)))
        q, qi, kis = padq(q), padq(qi), padq(kis)
    nq = q.shape[1] // tq
    s_total = k_b.shape[1]
    out = None
    for j0 in range(nq):
        reach = -(-(q_offset + (j0 + 1) * tq) // CHUNK) * CHUNK
        s_len = min(s_total, -(-min(reach, s_valid) // LANES) * LANES)
        out = _dsa(q, qi, kis, k_b, vt_b, ki_b, out, tq=tq, j0=j0, nj=1, s_len=s_len, s_valid=s_valid,
                   q_offset=q_offset, n_valid_q=n_valid_q, topk=topk)
    return out[:, :T]


def _ffn_kernel(x_ref, gdn_ref, att_ref, ssm_ref, mod_ref, wo_ref, g2_ref, wg_ref, wu_ref, cw_ref, cb_ref,
                wd_ref, prev_ref, fg_ref, *rest, nb, tt, nk, final):
    if final:
        xo_ref, y_ref, new_ref, h2_sc, acc_sc, buf_sc, carry_sc = rest
    else:
        xo_ref, new_ref, h2_sc, acc_sc, buf_sc, carry_sc = rest
        y_ref = None
    t = pl.program_id(1)
    k = pl.program_id(2)
    rows = nb * tt
    lo = SUBLANES - (FFN_CONV_W - 1)
    mod = mod_ref[...]

    @pl.when(k == 0)
    def _():
        mix = jnp.concatenate([gdn_ref[...], att_ref[...], ssm_ref[...]], axis=-1)
        proj = _dot(mix.reshape(rows, MIX_W).astype(BF16), wo_ref[...]).reshape(nb, tt, D_MODEL)
        x1 = x_ref[...] + mod[:, 2:3, :] * proj
        acc_sc[...] = x1
        ms = jnp.mean(x1 * x1, axis=-1, keepdims=True)
        h2 = x1 * lax.rsqrt(ms + EPS) * g2_ref[...]
        h2 = h2 * (1.0 + mod[:, 4:5, :]) + mod[:, 3:4, :]
        h2_sc[...] = h2.reshape(rows, D_MODEL).astype(BF16)

    @pl.when(t == 0)
    def _():
        carry_sc[k, :, lo:SUBLANES, :] = prev_ref[...]

    h2 = h2_sc[...]
    ag = _dot(h2, wg_ref[...])
    up = _dot(h2, wu_ref[...])
    tf = ag.shape[-1]
    buf_sc[:, lo:SUBLANES, :] = carry_sc[k, :, lo:SUBLANES, :]
    buf_sc[:, SUBLANES:SUBLANES + tt, :] = ag.reshape(nb, tt, tf)
    cw = cw_ref[...]
    conv = buf_sc[:, lo:lo + tt, :] * cw[0:1, :][None]
    for jj in range(1, FFN_CONV_W):
        conv = conv + buf_sc[:, lo + jj:lo + jj + tt, :] * cw[jj:jj + 1, :][None]
    tail = buf_sc[:, tt + lo:tt + SUBLANES, :]
    carry_sc[k, :, lo:SUBLANES, :] = tail
    new_ref[:, pl.ds(k, 1), :, :] = tail[:, None]
    act = _silu(conv + cb_ref[...][None]).reshape(rows, tf) * up
    y = _dot(act.astype(BF16), wd_ref[...]).reshape(nb, tt, D_MODEL)
    acc_sc[...] += mod[:, 5:6, :] * y

    @pl.when(k == nk - 1)
    def _():
        xo = acc_sc[...]
        xo_ref[...] = xo
        if final:
            ms = jnp.mean(xo * xo, axis=-1, keepdims=True)
            y_ref[...] = xo * lax.rsqrt(ms + EPS) * fg_ref[...]


def _ffn(x, gdn_o, att_o, ssm_o, mod, w_out, g2, w_gate, w_up, conv_w, conv_b, w_down, conv_prev, final_g,
         nb, tt, tf, final):
    B, T, _ = x.shape
    nk = D_FF // tf
    grid = (B // nb, T // tt, nk)
    row_spec = lambda w: pl.BlockSpec((nb, tt, w), lambda b, t, k: (b, t, 0))
    in_specs = [row_spec(D_MODEL), row_spec(GDN_W), row_spec(ATT_W), row_spec(SSM_W),
                pl.BlockSpec((nb, 6, D_MODEL), lambda b, t, k: (b, 0, 0)),
                pl.BlockSpec((MIX_W, D_MODEL), lambda b, t, k: (0, 0)),
                pl.BlockSpec((1, D_MODEL), lambda b, t, k: (0, 0)),
                pl.BlockSpec((D_MODEL, tf), lambda b, t, k: (0, k)),
                pl.BlockSpec((D_MODEL, tf), lambda b, t, k: (0, k)),
                pl.BlockSpec((FFN_CONV_W, tf), lambda b, t, k: (0, k)),
                pl.BlockSpec((1, tf), lambda b, t, k: (0, k)),
                pl.BlockSpec((tf, D_MODEL), lambda b, t, k: (k, 0)),
                pl.BlockSpec((nb, FFN_CONV_W - 1, tf), lambda b, t, k: (b, 0, k)),
                pl.BlockSpec((1, D_MODEL), lambda b, t, k: (0, 0))]
    out_shape = [jax.ShapeDtypeStruct((B, T, D_MODEL), F32)]
    out_specs = [row_spec(D_MODEL)]
    if final:
        out_shape.append(jax.ShapeDtypeStruct((B, T, D_MODEL), F32))
        out_specs.append(row_spec(D_MODEL))
    out_shape.append(jax.ShapeDtypeStruct((B, nk, FFN_CONV_W - 1, tf), F32))
    out_specs.append(pl.BlockSpec((nb, nk, FFN_CONV_W - 1, tf), lambda b, t, k: (b, 0, 0, 0)))
    res = pl.pallas_call(
        functools.partial(_ffn_kernel, nb=nb, tt=tt, nk=nk, final=final),
        grid=grid,
        in_specs=in_specs,
        out_specs=tuple(out_specs),
        out_shape=tuple(out_shape),
        scratch_shapes=[pltpu.VMEM((nb * tt, D_MODEL), BF16),
                        pltpu.VMEM((nb, tt, D_MODEL), F32),
                        pltpu.VMEM((nb, SUBLANES + tt, tf), F32),
                        pltpu.VMEM((nk, nb, SUBLANES, tf), F32)],
        compiler_params=_params(("arbitrary", "arbitrary", "arbitrary")),
        name="ffn",
    )(x, gdn_o, att_o, ssm_o, mod, w_out, g2, w_gate, w_up, conv_w, conv_b.reshape(1, -1), w_down,
      conv_prev, final_g)
    tail = res[-1].transpose(0, 2, 1, 3).reshape(B, FFN_CONV_W - 1, D_FF)
    return (*res[:-1], tail)


def _rope_tables(pos):
    half = ATT_DH // 2
    inv_freq = ROPE_THETA ** (-jnp.arange(half, dtype=F32) / half)
    ang = pos.astype(F32)[:, None] * inv_freq[None, :]
    cos = jnp.cos(ang)
    sin = jnp.sin(ang)
    cos_h = jnp.concatenate([cos, cos], axis=-1)
    sin_h = jnp.concatenate([-sin, sin], axis=-1)
    ones = jnp.ones((pos.shape[0], LANES - ATT_DH), F32)
    cos_t = jnp.concatenate([jnp.tile(cos_h, (1, ATT_HEADS)), cos_h, ones], axis=-1)
    sin_t = jnp.concatenate([jnp.tile(sin_h, (1, ATT_HEADS)), sin_h, 0.0 * ones], axis=-1)
    return cos_t, sin_t


def _layer(x, mod, pos_tables, cache, states, lw, tiles, layer, stacks, final_g):
    final = layer == DEPTH - 1
    (norm1_g, w_perm, gdn_conv_w, gdn_A_log, gdn_dt_bias, gdn_norm_g, ssm_conv_w, ssm_conv_b, ssm_A_log,
     ssm_dt_bias, ssm_D, ssm_norm_g, w_out, norm2_g, w_gate, w_up, ffn_conv_w, ffn_conv_b, w_down) = lw
    gdn_conv_prev, gdn_s0, ssm_conv_prev, ssm_h0, ffn_conv_prev = states
    B, T, _ = x.shape
    nb, tt, nbb, ffn_tt, tf = tiles
    cos_t, sin_t = pos_tables
    proj = _inproj(x, mod, norm1_g.reshape(1, -1), w_perm, cos_t, sin_t, nb, tt, cache is None, layer, stacks)
    gqkv, ggate, q, k_stack, v_stack, qi, kis, z, xbc = proj[:len(OUT_SEGS)]
    stacks = (k_stack, v_stack, proj[len(OUT_SEGS)])

    gdn_o, gdn_conv_new, gdn_s = _gdn(gqkv, kis, ggate, gdn_conv_w, gdn_conv_prev, gdn_s0, gdn_A_log,
                                      gdn_dt_bias, gdn_norm_g, nbb)
    ssm_o, ssm_conv_new, ssm_h = _ssd(xbc, z, kis, ssm_conv_w, ssm_conv_b, ssm_conv_prev, ssm_h0, ssm_A_log,
                                      ssm_dt_bias, ssm_D, ssm_norm_g, nbb)
    if cache is None:
        k_b, ki_b, vt_b = proj[len(OUT_SEGS) + 1:]
        att_o = _dsa_group(q, qi, kis, k_b, vt_b, ki_b, s_valid=T, q_offset=0, topk=min(TOPK_MAX, T // 4))
    else:
        ck, cv, cki = cache
        P = ck.shape[1]
        att_o = _dsa_rows(q, qi, kis, k_stack[layer], v_stack[layer], ck.reshape(B, P, ATT_W).astype(BF16),
                          cv.reshape(B, P, ATT_W).astype(BF16), cki.astype(BF16), nbq=nbb,
                          topk=min(TOPK_MAX, (P + T) // 4))

    res = _ffn(x, gdn_o, att_o, ssm_o, mod, w_out, norm2_g.reshape(1, -1), w_gate, w_up, ffn_conv_w, ffn_conv_b,
               w_down, ffn_conv_prev, final_g.reshape(1, -1), ffn_tt[0], ffn_tt[1], tf, final)
    if final:
        x_new, y, ffn_conv_new = res
    else:
        (x_new, ffn_conv_new), y = res, None
    return x_new, y, (gdn_conv_new, gdn_s, ssm_conv_new, ssm_h, ffn_conv_new), stacks


def kernel(x_prompt, x_sample, c_prompt, c_sample, cache_k, cache_v, cache_kidx, state_gdn_conv, state_gdn,
           state_ssm_conv, state_ssm, state_ffn_conv, w_ada, b_ada, norm1_g, w_in, gdn_conv_w, gdn_A_log,
           gdn_dt_bias, gdn_norm_g, ssm_conv_w, ssm_conv_b, ssm_A_log, ssm_dt_bias, ssm_D, ssm_norm_g, w_out,
           norm2_g, w_gate, w_up, ffn_conv_w, ffn_conv_b, w_down, final_g):
    Bp, T, _ = x_prompt.shape
    Bs, Ts, _ = x_sample.shape
    P = cache_k.shape[2]

    c_all = jnp.concatenate([c_prompt, c_sample], axis=0)
    mod_all = _ada(c_all, w_ada, b_ada).reshape(DEPTH, Bp + Bs, 6, D_MODEL)

    tables_p = _rope_tables(jnp.arange(T))
    tables_s = _rope_tables(P + jnp.arange(Ts))

    w_perm = _permute_columns(w_in.astype(BF16))
    w_out_b = w_out.astype(BF16)
    w_gate_b = w_gate.astype(BF16)
    w_up_b = w_up.astype(BF16)
    w_down_b = w_down.astype(BF16)

    zeros_p = (jnp.zeros((Bp, CONV_W - 1, GDN_CONV_C), F32),
               jnp.zeros((Bp, GDN_HEADS, GDN_DK, GDN_DV), F32),
               jnp.zeros((Bp, CONV_W - 1, SSM_CONV_C), F32),
               jnp.zeros((Bp, SSM_HEADS, SSM_P, SSM_N), F32),
               jnp.zeros((Bp, FFN_CONV_W - 1, D_FF), F32))

    tf = D_FF // 2
    tiles_p = (1, min(256, T), math.gcd(Bp, 8), (1, min(512, T)), tf)
    tiles_s = (Bs, Ts, math.gcd(Bs, 4), (Bs, Ts), tf)

    xp, xs = x_prompt, x_sample
    new_p, new_s = [], []
    yp = ys = stacks_p = stacks_s = None
    for l in range(DEPTH):
        lw = (norm1_g[l], w_perm[l], gdn_conv_w[l], gdn_A_log[l], gdn_dt_bias[l], gdn_norm_g[l], ssm_conv_w[l],
              ssm_conv_b[l], ssm_A_log[l], ssm_dt_bias[l], ssm_D[l], ssm_norm_g[l], w_out_b[l], norm2_g[l],
              w_gate_b[l], w_up_b[l], ffn_conv_w[l], ffn_conv_b[l], w_down_b[l])
        xp, yp, st_p, stacks_p = _layer(xp, mod_all[l, :Bp], tables_p, None, zeros_p, lw, tiles_p, l, stacks_p,
                                        final_g)
        states_s = (state_gdn_conv[l], state_gdn[l], state_ssm_conv[l], state_ssm[l], state_ffn_conv[l])
        xs, ys, st_s, stacks_s = _layer(xs, mod_all[l, Bp:], tables_s, (cache_k[l], cache_v[l], cache_kidx[l]),
                                        states_s, lw, tiles_s, l, stacks_s, final_g)
        new_p.append(st_p)
        new_s.append(st_s)

    def outputs(stacks, per_layer):
        k_stack, v_stack, ki_stack = stacks
        heads = k_stack.shape[:3] + (ATT_HEADS, ATT_DH)
        return [k_stack.reshape(heads), v_stack.reshape(heads), ki_stack] + [
            jnp.stack([st[i] for st in per_layer]) for i in range(len(per_layer[0]))]

    return (yp, ys, *outputs(stacks_p, new_p), *outputs(stacks_s, new_s))
```

```python
import functools
import math

import jax
import jax.numpy as jnp
import numpy as np
from jax import lax
from jax.experimental import pallas as pl
from jax.experimental.pallas import tpu as pltpu

F32 = jnp.float32
BF16 = jnp.bfloat16
HI = lax.Precision.HIGHEST

D_MODEL = 1024
DEPTH = 2
CHUNK = 64
CONV_W = 4
FFN_CONV_W = 3
D_FF = 2816
ROPE_THETA = 10000.0
EPS = 1e-6
GDN_HEADS = 4
GDN_DK = 64
GDN_DV = 64
ATT_HEADS = 4
ATT_DH = 64
IDX_HEADS = 4
IDX_DH = 64
TOPK_MAX = 256
SSM_HEADS = 8
SSM_P = 64
SSM_GROUPS = 2
SSM_N = 128
GDN_W = GDN_HEADS * GDN_DV
ATT_W = ATT_HEADS * ATT_DH
SSM_W = SSM_HEADS * SSM_P
MIX_W = GDN_W + ATT_W + SSM_W
GDN_CONV_C = 2 * GDN_HEADS * GDN_DK + GDN_W
SSM_CONV_C = SSM_W + 2 * SSM_GROUPS * SSM_N
IN_SIZES = (GDN_CONV_C, GDN_HEADS, GDN_HEADS, GDN_W,
            ATT_W, ATT_W, ATT_W, IDX_HEADS * IDX_DH, IDX_DH, IDX_HEADS,
            SSM_W, SSM_CONV_C, SSM_HEADS)
IN_W = sum(IN_SIZES)

LANES = 128
SUBLANES = 8
VMEM_LIMIT = 56 * 1024 * 1024

KIS_A = IDX_DH
KIS_B = KIS_A + GDN_HEADS
KIS_WI = KIS_B + GDN_HEADS
KIS_DT = KIS_WI + IDX_HEADS
KIS_END = KIS_DT + SSM_HEADS
OUT_SEGS = (("gqkv", GDN_CONV_C), ("ggate", GDN_W), ("q", ATT_W), ("k", ATT_W), ("v", ATT_W),
            ("qi", IDX_HEADS * IDX_DH), ("kis", LANES), ("z", SSM_W), ("xbc", SSM_CONV_C))
PERM_W = sum(w for _, w in OUT_SEGS)
ROT_W = ATT_W + LANES


def _perm_columns():
    starts = np.concatenate([[0], np.cumsum(IN_SIZES)])
    (s_gqkv, s_ga, s_gb, s_gg, s_q, s_k, s_v, s_qi, s_ki, s_wi, s_z, s_xbc, s_dt) = starts[:-1]
    cols = []
    cols += list(range(s_gqkv, s_gqkv + GDN_CONV_C))
    cols += list(range(s_gg, s_gg + GDN_W))
    cols += list(range(s_q, s_q + ATT_W))
    cols += list(range(s_k, s_k + ATT_W))
    cols += list(range(s_v, s_v + ATT_W))
    cols += list(range(s_qi, s_qi + IDX_HEADS * IDX_DH))
    kis = (list(range(s_ki, s_ki + IDX_DH)) + list(range(s_ga, s_ga + GDN_HEADS))
           + list(range(s_gb, s_gb + GDN_HEADS)) + list(range(s_wi, s_wi + IDX_HEADS))
           + list(range(s_dt, s_dt + SSM_HEADS)))
    cols += kis + [-1] * (LANES - len(kis))
    cols += list(range(s_z, s_z + SSM_W))
    cols += list(range(s_xbc, s_xbc + SSM_CONV_C))
    assert len(cols) == PERM_W
    return np.asarray(cols, np.int32)


_PERM_COLS = _perm_columns()


def _perm_runs():
    runs = []
    start = 0
    for i in range(1, PERM_W + 1):
        prev = int(_PERM_COLS[i - 1])
        if i == PERM_W or not ((prev < 0 and _PERM_COLS[i] < 0) or (prev >= 0 and _PERM_COLS[i] == prev + 1)):
            runs.append((start, int(_PERM_COLS[start]), i - start))
            start = i
    return runs


_PERM_RUNS = _perm_runs()
WPREP_ROWS = 256


def _wprep_kernel(w_ref, o_ref):
    w = w_ref[0]
    for dst, src, n in _PERM_RUNS:
        if src < 0:
            o_ref[0, :, dst:dst + n] = jnp.zeros((w.shape[0], n), BF16)
        else:
            o_ref[0, :, dst:dst + n] = w[:, src:src + n].astype(BF16)


def _permute_columns(w_in):
    depth, rows, _ = w_in.shape
    return pl.pallas_call(
        _wprep_kernel,
        grid=(depth, rows // WPREP_ROWS),
        in_specs=[pl.BlockSpec((1, WPREP_ROWS, IN_W), lambda l, r: (l, r, 0))],
        out_specs=pl.BlockSpec((1, WPREP_ROWS, PERM_W), lambda l, r: (l, r, 0)),
        out_shape=jax.ShapeDtypeStruct((depth, rows, PERM_W), BF16),
        compiler_params=pltpu.CompilerParams(dimension_semantics=("arbitrary", "arbitrary"),
                                             vmem_limit_bytes=VMEM_LIMIT),
        name="wprep",
    )(w_in)


def _silu(x):
    return x * jax.nn.sigmoid(x)


def _softplus(x):
    return jnp.maximum(x, 0.0) + jnp.log1p(jnp.exp(-jnp.abs(x)))


def _dot(a, b, precision=None):
    return jnp.dot(a, b, preferred_element_type=F32, precision=precision)


def _dot_nt(a, b, precision=None):
    return lax.dot_general(a, b, (((1,), (1,)), ((), ())), preferred_element_type=F32,
                           precision=precision)


def _dot_tn(a, b, precision=None):
    return lax.dot_general(a, b, (((0,), (0,)), ((), ())), preferred_element_type=F32,
                           precision=precision)


def _cumsum_rows(x):
    c = x.shape[0]
    row = lax.broadcasted_iota(jnp.int32, x.shape, 0)
    s = 1
    while s < c:
        x = x + jnp.where(row >= s, pltpu.roll(x, s, axis=0), 0.0)
        s *= 2
    return x


def _mm(a, b):
    return _dot(a.astype(BF16), b.astype(BF16))


def _mm_nt(a, b):
    return _dot_nt(a.astype(BF16), b.astype(BF16))


def _split(a):
    hi = a.astype(BF16)
    return hi, (a - hi.astype(F32)).astype(BF16)


def _mm3(a, b):
    (ah, al), (bh, bl) = a, b
    return _dot(ah, bh) + (_dot(ah, bl) + _dot(al, bh))


def _lane_spread(vals, lane0, n_heads):
    width = n_heads * GDN_DV
    r = lax.broadcasted_iota(jnp.int32, (LANES, width), 0)
    cidx = lax.broadcasted_iota(jnp.int32, (LANES, width), 1)
    pick = (r == lane0 + cidx // GDN_DV).astype(BF16)
    p1 = vals.astype(BF16)
    r1 = vals - p1.astype(F32)
    p2 = r1.astype(BF16)
    p3 = (r1 - p2.astype(F32)).astype(BF16)
    return _dot(p1, pick) + (_dot(p2, pick) + _dot(p3, pick))


def _head_sums(x):
    r = lax.broadcasted_iota(jnp.int32, (LANES, LANES), 0) // GDN_DV
    cidx = lax.broadcasted_iota(jnp.int32, (LANES, LANES), 1) // GDN_DV
    ones = (r == cidx).astype(BF16)
    hi, lo = _split(x)
    cols = []
    for g in range(x.shape[1] // LANES):
        sl = slice(g * LANES, (g + 1) * LANES)
        cols.append(_dot(hi[:, sl], ones) + _dot(lo[:, sl], ones))
    return jnp.concatenate(cols, axis=1)


def _transpose_rows(vals):
    c = vals.shape[0]
    if c < LANES:
        vals = jnp.concatenate([vals, jnp.zeros((LANES - c, LANES), vals.dtype)], axis=0)
    return vals.T[:, :c]


def _lane_vector(vals, start):
    return jnp.zeros((1, LANES), F32).at[0, start:start + vals.shape[0]].set(vals.astype(F32))


def _params(sem):
    return pltpu.CompilerParams(dimension_semantics=sem, vmem_limit_bytes=VMEM_LIMIT)


def _ada_kernel(c_ref, w_ref, b_ref, o_ref):
    s = _silu(c_ref[...])
    o_ref[0] = _dot(s.astype(BF16), w_ref[0].astype(BF16)) + b_ref[0]


def _ada(c_all, w_ada, b_ada):
    rows = c_all.shape[0]
    n = w_ada.shape[2]
    tn = 1536
    return pl.pallas_call(
        _ada_kernel,
        grid=(DEPTH, n // tn),
        in_specs=[pl.BlockSpec((rows, D_MODEL), lambda l, j: (0, 0)),
                  pl.BlockSpec((1, D_MODEL, tn), lambda l, j: (l, 0, j)),
                  pl.BlockSpec((1, 1, tn), lambda l, j: (l, 0, j))],
        out_specs=pl.BlockSpec((1, rows, tn), lambda l, j: (l, 0, j)),
        out_shape=jax.ShapeDtypeStruct((DEPTH, rows, n), F32),
        compiler_params=_params(("arbitrary", "arbitrary")),
        name="ada",
    )(c_all, w_ada, b_ada.reshape(DEPTH, 1, n))


def _rotate(x, cos, sin_signed):
    w = x.shape[-1]
    lane = lax.broadcasted_iota(jnp.int32, x.shape, x.ndim - 1)
    first = (lane % ATT_DH) < (ATT_DH // 2)
    swapped = jnp.where(first, pltpu.roll(x, w - ATT_DH // 2, axis=x.ndim - 1),
                        pltpu.roll(x, ATT_DH // 2, axis=x.ndim - 1))
    return x * cos + swapped * sin_signed


def _inproj_kernel(x_ref, mod_ref, g_ref, w_ref, cos_ref, sin_ref, *refs, nb, tt, emit_keys, n_alias):
    out_refs = refs[n_alias:]
    x = x_ref[...]
    ms = jnp.mean(x * x, axis=-1, keepdims=True)
    xn = x * lax.rsqrt(ms + EPS) * g_ref[...]
    mod = mod_ref[...]
    h = xn * (1.0 + mod[:, 1:2, :]) + mod[:, 0:1, :]
    u = _dot(h.reshape(nb * tt, D_MODEL).astype(BF16), w_ref[...])
    cos = cos_ref[...]
    sin = sin_ref[...]
    off = 0
    segs = {}
    for (name, width), o_ref in zip(OUT_SEGS, out_refs):
        seg = u[:, off:off + width]
        if name in ("q", "k", "qi", "kis"):
            t0 = ATT_W if name == "kis" else 0
            c3 = cos[:, t0:t0 + width][None]
            s3 = sin[:, t0:t0 + width][None]
            if nb > 1:
                c3 = jnp.broadcast_to(c3, (nb, tt, width)).reshape(nb * tt, width)
                s3 = jnp.broadcast_to(s3, (nb, tt, width)).reshape(nb * tt, width)
            else:
                c3 = c3[0]
                s3 = s3[0]
            seg = _rotate(seg, c3, s3)
        o_ref[...] = seg.reshape(o_ref.shape)
        segs[name] = seg
        off += width
    ki_ref = out_refs[len(OUT_SEGS)]
    ki_ref[...] = segs["kis"][:, :IDX_DH].reshape(ki_ref.shape)
    if emit_keys:
        k_b_ref, ki_b_ref, vt_b_ref = out_refs[len(OUT_SEGS) + 1:]
        k_b_ref[0] = segs["k"].astype(BF16)
        ki_b_ref[0] = segs["kis"][:, :IDX_DH].astype(BF16)
        vt_b_ref[0] = segs["v"].T.astype(BF16)


STACKED = ("k", "v", "ki")


def _inproj(x, mod, g1, w_perm, cos, sin, nb, tt, emit_keys, layer, stacks):
    B, T, _ = x.shape
    grid = (B // nb, T // tt)
    names = [n for n, _ in OUT_SEGS] + ["ki"]
    widths = [w for _, w in OUT_SEGS] + [IDX_DH]
    out_shape, out_specs = (), ()
    for n, w in zip(names, widths):
        if n in STACKED:
            out_shape += (jax.ShapeDtypeStruct((DEPTH, B, T, w), F32),)
            out_specs += (pl.BlockSpec((1, nb, tt, w), lambda b, t: (layer, b, t, 0)),)
        else:
            out_shape += (jax.ShapeDtypeStruct((B, T, w), F32),)
            out_specs += (pl.BlockSpec((nb, tt, w), lambda b, t: (b, t, 0)),)
    alias_in = () if stacks is None else tuple(stacks)
    n_in = 6
    aliases = {n_in + i: names.index(n) for i, n in enumerate(STACKED)} if alias_in else {}
    if emit_keys:
        assert nb == 1
        out_shape += (jax.ShapeDtypeStruct((B, T, ATT_W), BF16), jax.ShapeDtypeStruct((B, T, IDX_DH), BF16),
                      jax.ShapeDtypeStruct((B, ATT_W, T), BF16))
        out_specs += (pl.BlockSpec((1, tt, ATT_W), lambda b, t: (b, t, 0)),
                      pl.BlockSpec((1, tt, IDX_DH), lambda b, t: (b, t, 0)),
                      pl.BlockSpec((1, ATT_W, tt), lambda b, t: (b, 0, t)))
    return pl.pallas_call(
        functools.partial(_inproj_kernel, nb=nb, tt=tt, emit_keys=emit_keys, n_alias=len(alias_in)),
        grid=grid,
        in_specs=[pl.BlockSpec((nb, tt, D_MODEL), lambda b, t: (b, t, 0)),
                  pl.BlockSpec((nb, 6, D_MODEL), lambda b, t: (b, 0, 0)),
                  pl.BlockSpec((1, D_MODEL), lambda b, t: (0, 0)),
                  pl.BlockSpec((D_MODEL, PERM_W), lambda b, t: (0, 0)),
                  pl.BlockSpec((tt, ROT_W), lambda b, t: (t, 0)),
                  pl.BlockSpec((tt, ROT_W), lambda b, t: (t, 0))]
        + [pl.BlockSpec(memory_space=pl.ANY)] * len(alias_in),
        out_specs=out_specs,
        out_shape=out_shape,
        input_output_aliases=aliases,
        compiler_params=_params(("arbitrary", "arbitrary")),
        name="inproj",
    )(x, mod, g1, w_perm, cos, sin, *alias_in)


def _short_conv(ext_ref, bi, u, w, c):
    lo = SUBLANES - (CONV_W - 1)
    ext_ref[bi, SUBLANES:SUBLANES + c, :] = u
    y = ext_ref[bi, lo:lo + c, :] * w[0:1, :]
    for j in range(1, CONV_W):
        y = y + ext_ref[bi, lo + j:lo + j + c, :] * w[j:j + 1, :]
    tail = ext_ref[bi, c + lo:c + SUBLANES, :]
    ext_ref[bi, lo:SUBLANES, :] = tail
    return y, tail


def _neumann_inverses(mats, c):
    eye = (lax.broadcasted_iota(jnp.int32, (c, c), 0)
           == lax.broadcasted_iota(jnp.int32, (c, c), 1)).astype(F32)
    ps = [eye - a for a in mats]
    sp = [_split(a) for a in mats]
    pws = [_mm3(s, s) for s in sp]
    n = 2
    while n < c:
        sp = [_split(pw) for pw in pws]
        ps = [p + _mm3(_split(p), s) for p, s in zip(ps, sp)]
        n *= 2
        if n < c:
            pws = [_mm3(s, s) for s in sp]
    return ps


def _gdn_kernel(qkv_ref, kis_ref, gate_ref, w_ref, prev_ref, s0_ref, alog_ref, dtb_ref, ng_ref,
                o_ref, new_ref, s_ref, ext_ref, s_sc, *, c, nbb):
    j = pl.program_id(1)
    lo = SUBLANES - (CONV_W - 1)

    @pl.when(j == 0)
    def _():
        s_sc[...] = s0_ref[...]
        ext_ref[:, lo:SUBLANES, :] = prev_ref[...]

    ng = ng_ref[...]
    w_conv = w_ref[...]
    ri = lax.broadcasted_iota(jnp.int32, (c, c), 0)
    ci = lax.broadcasted_iota(jnp.int32, (c, c), 1)
    incl = ri >= ci
    strict = ri > ci
    nkd = GDN_HEADS * GDN_DK
    ch = []
    for bi in range(nbb):
        y, tail = _short_conv(ext_ref, bi, qkv_ref[bi], w_conv, c)
        new_ref[bi] = tail
        y = _silu(y)
        small = kis_ref[bi]
        g = -jnp.exp(alog_ref[...]) * _softplus(small + dtb_ref[...])
        beta = jax.nn.sigmoid(small)
        gc = _cumsum_rows(g)
        gc_t = _transpose_rows(gc)
        qk_n = y[:, :2 * nkd]
        qk_n = qk_n * lax.rsqrt(_head_sums(qk_n * qk_n) + EPS)
        for h in range(GDN_HEADS):
            qh = qk_n[:, h * GDN_DK:(h + 1) * GDN_DK] * (GDN_DK ** -0.5)
            kh = qk_n[:, nkd + h * GDN_DK:nkd + (h + 1) * GDN_DK]
            vh = y[:, 2 * nkd + h * GDN_DV:2 * nkd + (h + 1) * GDN_DV]
            ch.append((qh, kh, vh, gc[:, KIS_A + h:KIS_A + h + 1], gc_t[KIS_A + h:KIS_A + h + 1, :],
                       beta[:, KIS_B + h:KIS_B + h + 1]))
    ids = [(bi, h) for bi in range(nbb) for h in range(GDN_HEADS)]
    decay = [jnp.where(incl, jnp.exp(jnp.where(incl, gcol - grow, 0.0)), 0.0)
             for (_, _, _, gcol, grow, _) in ch]
    kb = [kh * bcol for (_, kh, _, _, _, bcol) in ch]
    eg = [jnp.exp(gcol) for (_, _, _, gcol, _, _) in ch]
    grams = [_mm_nt(jnp.concatenate([kbi, x[0]], axis=0), x[1]) for kbi, x in zip(kb, ch)]
    t_inv = _neumann_inverses([jnp.where(strict, m[:c] * d, 0.0) for m, d in zip(grams, decay)], c)
    attn = [jnp.where(incl, m[c:] * d, 0.0) for m, d in zip(grams, decay)]
    rhs = [jnp.concatenate([x[2] * x[5], kbi * e], axis=1) for x, kbi, e in zip(ch, kb, eg)]
    sol = [_mm(t, r) for t, r in zip(t_inv, rhs)]
    st = [s_sc[bi, h] for bi, h in ids]
    ws = [_mm(sl[:, GDN_DV:], s) for sl, s in zip(sol, st)]
    qs = [_mm(x[0] * e, s) for x, e, s in zip(ch, eg, st)]
    v_new = [sl[:, :GDN_DV] - w for sl, w in zip(sol, ws)]
    av = [_mm(a, v) for a, v in zip(attn, v_new)]
    g_last = [x[3][c - 1:c, :] for x in ch]
    kv = [_dot_tn(x[1] * jnp.exp(gl - x[3]), v) for x, gl, v in zip(ch, g_last, v_new)]
    for (bi, h), s, gl, kvi in zip(ids, st, g_last, kv):
        s_new = s * jnp.exp(gl) + kvi
        s_sc[bi, h] = s_new
        s_ref[bi, h] = s_new
    for bi in range(nbb):
        o = jnp.concatenate([qs[bi * GDN_HEADS + h] + av[bi * GDN_HEADS + h] for h in range(GDN_HEADS)], axis=1)
        o = o * lax.rsqrt(_head_sums(o * o) * (1.0 / GDN_DV) + EPS) * ng
        o_ref[bi] = o * _silu(gate_ref[bi])


def _gdn(gqkv, kis, ggate, conv_w, conv_prev, s0, a_log, dt_bias, norm_g, nbb):
    B, T, _ = gqkv.shape
    c = math.gcd(T, CHUNK)
    return pl.pallas_call(
        functools.partial(_gdn_kernel, c=c, nbb=nbb),
        grid=(B // nbb, T // c),
        in_specs=[pl.BlockSpec((nbb, c, GDN_CONV_C), lambda b, j: (b, j, 0)),
                  pl.BlockSpec((nbb, c, LANES), lambda b, j: (b, j, 0)),
                  pl.BlockSpec((nbb, c, GDN_W), lambda b, j: (b, j, 0)),
                  pl.BlockSpec((CONV_W, GDN_CONV_C), lambda b, j: (0, 0)),
                  pl.BlockSpec((nbb, CONV_W - 1, GDN_CONV_C), lambda b, j: (b, 0, 0)),
                  pl.BlockSpec((nbb, GDN_HEADS, GDN_DK, GDN_DV), lambda b, j: (b, 0, 0, 0)),
                  pl.BlockSpec((1, LANES), lambda b, j: (0, 0)),
                  pl.BlockSpec((1, LANES), lambda b, j: (0, 0)),
                  pl.BlockSpec((1, GDN_W), lambda b, j: (0, 0))],
        out_specs=(pl.BlockSpec((nbb, c, GDN_W), lambda b, j: (b, j, 0)),
                   pl.BlockSpec((nbb, CONV_W - 1, GDN_CONV_C), lambda b, j: (b, 0, 0)),
                   pl.BlockSpec((nbb, GDN_HEADS, GDN_DK, GDN_DV), lambda b, j: (b, 0, 0, 0))),
        out_shape=(jax.ShapeDtypeStruct((B, T, GDN_W), F32),
                   jax.ShapeDtypeStruct((B, CONV_W - 1, GDN_CONV_C), F32),
                   jax.ShapeDtypeStruct((B, GDN_HEADS, GDN_DK, GDN_DV), F32)),
        scratch_shapes=[pltpu.VMEM((nbb, SUBLANES + c, GDN_CONV_C), F32),
                        pltpu.VMEM((nbb, GDN_HEADS, GDN_DK, GDN_DV), F32)],
        compiler_params=_params(("arbitrary", "arbitrary")),
        name="gdn",
    )(gqkv, kis, ggate, conv_w, conv_prev, s0, _lane_vector(a_log, KIS_A), _lane_vector(dt_bias, KIS_A),
      jnp.tile(norm_g, GDN_HEADS).reshape(1, -1))


def _ssd_kernel(xbc_ref, z_ref, kis_ref, w_ref, cb_ref, prev_ref, h0_ref, alog_ref, dtb_ref, dsk_ref,
                ng_ref, o_ref, new_ref, h_ref, ext_ref, h_sc, *, c, nbb):
    j = pl.program_id(1)
    lo = SUBLANES - (CONV_W - 1)

    @pl.when(j == 0)
    def _():
        h_sc[...] = h0_ref[...]
        ext_ref[:, lo:SUBLANES, :] = prev_ref[...]

    dsk = dsk_ref[...]
    ng = ng_ref[...]
    w_conv = w_ref[...]
    ri = lax.broadcasted_iota(jnp.int32, (c, c), 0)
    ci = lax.broadcasted_iota(jnp.int32, (c, c), 1)
    incl = ri >= ci
    gn = SSM_GROUPS * SSM_N
    rep = SSM_HEADS // SSM_GROUPS
    gw = SSM_W // SSM_GROUPS
    grp_in = []
    ch = []
    slot = []
    ids = []
    for bi in range(nbb):
        y, tail = _short_conv(ext_ref, bi, xbc_ref[bi], w_conv, c)
        new_ref[bi] = tail
        y = _silu(y + cb_ref[...])
        dts = _softplus(kis_ref[bi] + dtb_ref[...])
        a = dts * (-jnp.exp(alog_ref[...]))
        acum = _cumsum_rows(a)
        acum_t = _transpose_rows(acum)
        dtm = _lane_spread(dts, KIS_DT, SSM_HEADS)
        am = _lane_spread(acum, KIS_DT, SSM_HEADS)
        a_last = am[c - 1:c, :]
        xs = y[:, :SSM_W]
        xdt = xs * dtm
        xw = xs * (jnp.exp(a_last - am) * dtm)
        h_dec = jnp.exp(a_last)
        slot.append((xs, jnp.exp(am), _silu(z_ref[bi])))
        for grp in range(SSM_GROUPS):
            grp_in.append((y[:, SSM_W + grp * SSM_N:SSM_W + (grp + 1) * SSM_N],
                           y[:, SSM_W + gn + grp * SSM_N:SSM_W + gn + (grp + 1) * SSM_N]))
            for hh in range(rep):
                h = grp * rep + hh
                hsl = slice(h * SSM_P, (h + 1) * SSM_P)
                ch.append((xdt[:, hsl], xw[:, hsl], am[:, h * SSM_P:h * SSM_P + c], acum_t[KIS_DT + h:KIS_DT + h + 1, :],
                           jnp.concatenate([h_dec[:, hsl]] * (SSM_N // SSM_P), axis=1), len(grp_in) - 1))
                ids.append((bi, h))
    cb = [_mm_nt(cg, bg) for bg, cg in grp_in]
    seg = [jnp.where(incl, jnp.exp(jnp.where(incl, x[2] - x[3], 0.0)), 0.0) for x in ch]
    hs = [h_sc[bi, h] for bi, h in ids]
    y_in = [_mm(cb[x[5]] * sg, x[0]) for x, sg in zip(ch, seg)]
    y_st = [_mm_nt(grp_in[x[5]][1], hst) for x, hst in zip(ch, hs)]
    upd = [_dot_tn(x[1], grp_in[x[5]][0]) for x in ch]
    for (bi, h), x, hst, up in zip(ids, ch, hs, upd):
        h_new = hst * x[4] + up
        h_sc[bi, h] = h_new
        h_ref[bi, h] = h_new
    for bi, (xs, ea, zg) in enumerate(slot):
        lo_c, hi_c = bi * SSM_HEADS, (bi + 1) * SSM_HEADS
        yy = (jnp.concatenate(y_in[lo_c:hi_c], axis=1) + jnp.concatenate(y_st[lo_c:hi_c], axis=1) * ea
              + dsk * xs) * zg
        for grp in range(SSM_GROUPS):
            yg = yy[:, grp * gw:(grp + 1) * gw]
            yg = yg * lax.rsqrt(jnp.mean(yg * yg, axis=-1, keepdims=True) + EPS)
            o_ref[bi, :, grp * gw:(grp + 1) * gw] = yg * ng[:, grp * gw:(grp + 1) * gw]


def _ssd(xbc, z, kis, conv_w, conv_b, conv_prev, h0, a_log, dt_bias, d_skip, norm_g, nbb):
    B, T, _ = xbc.shape
    c = math.gcd(T, CHUNK)
    return pl.pallas_call(
        functools.partial(_ssd_kernel, c=c, nbb=nbb),
        grid=(B // nbb, T // c),
        in_specs=[pl.BlockSpec((nbb, c, SSM_CONV_C), lambda b, j: (b, j, 0)),
                  pl.BlockSpec((nbb, c, SSM_W), lambda b, j: (b, j, 0)),
                  pl.BlockSpec((nbb, c, LANES), lambda b, j: (b, j, 0)),
                  pl.BlockSpec((CONV_W, SSM_CONV_C), lambda b, j: (0, 0)),
                  pl.BlockSpec((1, SSM_CONV_C), lambda b, j: (0, 0)),
                  pl.BlockSpec((nbb, CONV_W - 1, SSM_CONV_C), lambda b, j: (b, 0, 0)),
                  pl.BlockSpec((nbb, SSM_HEADS, SSM_P, SSM_N), lambda b, j: (b, 0, 0, 0)),
                  pl.BlockSpec((1, LANES), lambda b, j: (0, 0)),
                  pl.BlockSpec((1, LANES), lambda b, j: (0, 0)),
                  pl.BlockSpec((1, SSM_W), lambda b, j: (0, 0)),
                  pl.BlockSpec((1, SSM_W), lambda b, j: (0, 0))],
        out_specs=(pl.BlockSpec((nbb, c, SSM_W), lambda b, j: (b, j, 0)),
                   pl.BlockSpec((nbb, CONV_W - 1, SSM_CONV_C), lambda b, j: (b, 0, 0)),
                   pl.BlockSpec((nbb, SSM_HEADS, SSM_P, SSM_N), lambda b, j: (b, 0, 0, 0))),
        out_shape=(jax.ShapeDtypeStruct((B, T, SSM_W), F32),
                   jax.ShapeDtypeStruct((B, CONV_W - 1, SSM_CONV_C), F32),
                   jax.ShapeDtypeStruct((B, SSM_HEADS, SSM_P, SSM_N), F32)),
        scratch_shapes=[pltpu.VMEM((nbb, SUBLANES + c, SSM_CONV_C), F32),
                        pltpu.VMEM((nbb, SSM_HEADS, SSM_P, SSM_N), F32)],
        compiler_params=_params(("arbitrary", "arbitrary")),
        name="ssd",
    )(xbc, z, kis, conv_w, conv_b.reshape(1, -1), conv_prev, h0, _lane_vector(a_log, KIS_DT),
      _lane_vector(dt_bias, KIS_DT), jnp.repeat(d_skip, SSM_P).reshape(1, -1), norm_g.reshape(1, -1))


IDX_BITS = 12


OPENING_PROBES = 14
ROUND_PROBES = 2
SEARCH_ROUNDS = 4096
FOLD_ROWS = 128


def _fold_keys(x, op, axis=0):
    s = x.shape[0]
    if axis == 0 and s % FOLD_ROWS == 0 and s > FOLD_ROWS:
        x = op(x.reshape(s // FOLD_ROWS, FOLD_ROWS, x.shape[1]), axis=0)
    return op(x, axis=axis, keepdims=True)


def _fold_rows(x, op):
    return _fold_keys(x, op, 0)


def _select_topk(score, adm, pos, skip, topk, axis):
    def count(mask):
        return _fold_keys(mask.astype(F32), jnp.sum, axis)

    s = jnp.where(adm, score, -jnp.inf)

    few = (count(adm) <= topk) | skip
    smax = _fold_keys(s, jnp.max, axis)
    smin = _fold_keys(jnp.where(adm, score, jnp.inf), jnp.min, axis)
    top_full = count(s >= smax) >= topk
    lo0 = jnp.where(few, 0.0, jnp.where(top_full, smax, smin))
    hi0 = jnp.where(few, 0.0, smax)
    done0 = (few | top_full).astype(F32)

    def probe(lo, hi, done):
        mid = 0.5 * lo + 0.5 * hi
        ok = (mid > lo) & (mid < hi) & (done < 0.5)
        take = count(s >= mid) >= topk
        return jnp.where(ok & take, mid, lo), jnp.where(ok & jnp.logical_not(take), mid, hi)

    def opening(lo, hi):
        for _ in range(OPENING_PROBES):
            lo, hi = probe(lo, hi, done0)
        return lo, hi

    lo1, hi1 = lax.cond(jnp.min(done0) < 0.5, opening, lambda lo, hi: (lo, hi), lo0, hi0)

    def search_cond(c):
        return (jnp.min(c[2]) < 0.5) & (c[3] < SEARCH_ROUNDS)

    def search_body(c):
        lo, hi, done, it = c
        last = _fold_keys(jnp.where(s < hi, s, -jnp.inf), jnp.max, axis)
        reached = count(s >= last) >= topk
        active = done < 0.5
        lo = jnp.where(active & reached, last, lo)
        hi = jnp.where(active & jnp.logical_not(reached), last, hi)
        done = jnp.maximum(done, reached.astype(F32))
        for _ in range(ROUND_PROBES):
            lo, hi = probe(lo, hi, done)
        return lo, hi, done, it + 1

    lo, _, _, _ = lax.while_loop(search_cond, search_body, (lo1, hi1, done0, jnp.int32(0)))
    t = jnp.where(few, -jnp.inf, lo)
    gt = s > t
    tie = s == t
    need = topk - count(gt)
    excess = jnp.where(few, 0.0, count(tie) - need)

    def tie_search():
        def ibody(i, m):
            cand = m | (jnp.int32(1) << (IDX_BITS - 1 - i))
            return jnp.where(count(tie & (pos < cand)) <= need, cand, m)

        return lax.fori_loop(0, IDX_BITS, ibody, jnp.zeros(few.shape, jnp.int32))

    m = lax.cond(jnp.max(excess) > 0.0, tie_search, lambda: jnp.full(few.shape, 2 ** IDX_BITS - 1, jnp.int32))
    return adm & (gt | (tie & (pos < m)))


def _dsa_kernel(q_ref, qi_ref, kis_ref, k_ref, vt_ref, ki_ref, *refs, tq, s_len, s_valid, q_offset, j0,
                n_valid_q, topk):
    o_ref = refs[-1]
    j = pl.program_id(1)
    lane = lax.broadcasted_iota(jnp.int32, (1, tq), 1)
    qpos = q_offset + (j0 + j) * tq + lane
    lim = jnp.minimum((qpos // CHUNK + 1) * CHUNK, s_valid)
    row = lax.broadcasted_iota(jnp.int32, (s_len, tq), 0)
    adm = row < lim

    qi_t = qi_ref[0].T.astype(BF16)
    kis_t = kis_ref[0].T
    ki = ki_ref[0]
    assert IDX_DH == 64 and IDX_HEADS == 4 and ATT_DH == 64
    wi = kis_t[KIS_WI:KIS_WI + IDX_HEADS, :] * (IDX_DH ** -0.5 * IDX_HEADS ** -0.5)
    qi_all = jnp.concatenate([qi_t[h * IDX_DH:(h + 1) * IDX_DH, :] for h in range(IDX_HEADS)], axis=1)
    rel_all = _dot(ki, qi_all)
    score = jnp.zeros((s_len, tq), F32)
    for h in range(IDX_HEADS):
        score = score + jnp.maximum(rel_all[:, h * tq:(h + 1) * tq], 0.0) * wi[h:h + 1, :]
    sel = _select_topk(score, adm, row, lane >= n_valid_q, topk, 0)

    q_t = q_ref[0].T * (ATT_DH ** -0.5)
    pair = LANES // ATT_DH
    head_in_pair = lax.broadcasted_iota(jnp.int32, (LANES, tq), 0) // ATT_DH
    logits = []
    for g in range(ATT_HEADS // pair):
        qg = q_t[g * LANES:(g + 1) * LANES, :]
        qm = jnp.concatenate([jnp.where(head_in_pair == i, qg, 0.0) for i in range(pair)], axis=1)
        lg = _dot(k_ref[0, :, g * LANES:(g + 1) * LANES], qm.astype(BF16))
        logits += [lg[:, i * tq:(i + 1) * tq] for i in range(pair)]
    probs, scales = [], []
    for lg in logits:
        lg = jnp.where(sel, lg, -jnp.inf)
        p = jnp.exp(lg - _fold_rows(lg, jnp.max))
        scales.append(1.0 / _fold_rows(p, jnp.sum))
        probs.append(p.astype(BF16))
    outs = [_dot(vt_ref[0, h * ATT_DH:(h + 1) * ATT_DH, :], probs[h]) * scales[h] for h in range(ATT_HEADS)]
    o_ref[0] = jnp.concatenate(outs, axis=0).T


def _dsa(q, qi, kis, k_b, vt_b, ki_b, out_prev, *, tq, j0, nj, s_len, s_valid, q_offset, n_valid_q, topk):
    B, t_all, _ = q.shape
    assert s_len < 2 ** IDX_BITS and s_len <= k_b.shape[1]
    alias_in = () if out_prev is None else (out_prev,)
    return pl.pallas_call(
        functools.partial(_dsa_kernel, tq=tq, s_len=s_len, s_valid=s_valid, q_offset=q_offset, j0=j0,
                          n_valid_q=n_valid_q, topk=topk),
        grid=(B, nj),
        in_specs=[pl.BlockSpec((1, tq, ATT_W), lambda b, j: (b, j0 + j, 0)),
                  pl.BlockSpec((1, tq, IDX_HEADS * IDX_DH), lambda b, j: (b, j0 + j, 0)),
                  pl.BlockSpec((1, tq, LANES), lambda b, j: (b, j0 + j, 0)),
                  pl.BlockSpec((1, s_len, ATT_W), lambda b, j: (b, 0, 0)),
                  pl.BlockSpec((1, ATT_W, s_len), lambda b, j: (b, 0, 0)),
                  pl.BlockSpec((1, s_len, IDX_DH), lambda b, j: (b, 0, 0))]
        + [pl.BlockSpec(memory_space=pl.ANY)] * len(alias_in),
        out_specs=pl.BlockSpec((1, tq, ATT_W), lambda b, j: (b, j0 + j, 0)),
        out_shape=jax.ShapeDtypeStruct((B, t_all, ATT_W), F32),
        input_output_aliases={6: 0} if alias_in else {},
        compiler_params=_params(("arbitrary", "arbitrary")),
        name="dsa",
    )(q, qi, kis, k_b, vt_b, ki_b, *alias_in)


def _dsa_rows_kernel(q_ref, qi_ref, kis_ref, k_ref, v_ref, ck_ref, cv_ref, cki_ref, o_ref, *, nbq, tq, s_valid,
                     q_offset, topk):
    rows = nbq * tq
    s_len = q_offset + LANES
    qpos = q_offset + lax.broadcasted_iota(jnp.int32, (rows, 1), 0) % tq
    lim = jnp.minimum((qpos // CHUNK + 1) * CHUNK, s_valid)
    col = lax.broadcasted_iota(jnp.int32, (rows, s_len), 1)
    adm = col < lim

    def with_new(cache, new):
        return jnp.concatenate([cache, new.astype(BF16),
                                jnp.zeros((LANES - tq, new.shape[1]), BF16)], axis=0)

    scores = []
    for b in range(nbq):
        qi = qi_ref[b]
        qi_st = jnp.concatenate([qi[:, h * IDX_DH:(h + 1) * IDX_DH] for h in range(IDX_HEADS)], axis=0)
        rel = _dot_nt(qi_st.astype(BF16), with_new(cki_ref[b], kis_ref[b][:, :IDX_DH]))
        wi = kis_ref[b][:, KIS_WI:KIS_WI + IDX_HEADS] * (IDX_DH ** -0.5 * IDX_HEADS ** -0.5)
        sc = jnp.maximum(rel[:tq, :], 0.0) * wi[:, 0:1]
        for h in range(1, IDX_HEADS):
            sc = sc + jnp.maximum(rel[h * tq:(h + 1) * tq, :], 0.0) * wi[:, h:h + 1]
        scores.append(sc)
    sel = _select_topk(jnp.concatenate(scores, axis=0), adm, col, qpos < 0, topk, 1)

    head_of_lane = lax.broadcasted_iota(jnp.int32, (tq, ATT_W), 1) // ATT_DH
    for b in range(nbq):
        q = q_ref[b] * (ATT_DH ** -0.5)
        q_st = jnp.concatenate([jnp.where(head_of_lane == h, q, 0.0) for h in range(ATT_HEADS)], axis=0)
        logits = _dot_nt(q_st.astype(BF16), with_new(ck_ref[b], k_ref[b]))
        logits = jnp.where(jnp.concatenate([sel[b * tq:(b + 1) * tq, :]] * ATT_HEADS, axis=0), logits, -jnp.inf)
        p = jnp.exp(logits - jnp.max(logits, axis=1, keepdims=True))
        scale = 1.0 / jnp.sum(p, axis=1, keepdims=True)
        pv = _dot(p.astype(BF16), with_new(cv_ref[b], v_ref[b])) * scale
        out = jnp.zeros((tq, ATT_W), F32)
        for h in range(ATT_HEADS):
            out = jnp.where(head_of_lane == h, pv[h * tq:(h + 1) * tq, :], out)
        o_ref[b] = out


def _dsa_rows(q, qi, kis, k, v, ck, cv, cki, *, nbq, topk):
    B, T, _ = q.shape
    P = ck.shape[1]
    assert P + LANES < 2 ** IDX_BITS and T % (2 * SUBLANES) == 0 and T <= LANES and P % (2 * SUBLANES) == 0
    assert B % nbq == 0
    new_spec = lambda w: pl.BlockSpec((nbq, T, w), lambda b: (b, 0, 0))
    cache_spec = lambda w: pl.BlockSpec((nbq, P, w), lambda b: (b, 0, 0))
    return pl.pallas_call(
        functools.partial(_dsa_rows_kernel, nbq=nbq, tq=T, s_valid=P + T, q_offset=P, topk=topk),
        grid=(B // nbq,),
        in_specs=[new_spec(ATT_W), new_spec(IDX_HEADS * IDX_DH), new_spec(LANES), new_spec(ATT_W), new_spec(ATT_W),
                  cache_spec(ATT_W), cache_spec(ATT_W), cache_spec(IDX_DH)],
        out_specs=new_spec(ATT_W),
        out_shape=jax.ShapeDtypeStruct((B, T, ATT_W), F32),
        compiler_params=_params(("arbitrary",)),
        name="dsa_rows",
    )(q, qi, kis, k, v, ck, cv, cki)


def _dsa_group(q, qi, kis, k_b, vt_b, ki_b, *, s_valid, q_offset, topk):
    B, T, _ = q.shape
    tq = 2 * LANES if T % (2 * LANES) == 0 else LANES
    n_valid_q = min(T, tq)
    if T < tq:
        padq = lambda a: jnp.pad(a, ((0, 0), (0, tq - T), (0, 0)))
        q, qi, kis = padq(q), padq(qi), padq(kis)
    nq = q.shape[1] // tq
    s_total = k_b.shape[1]
    out = None
    for j0 in range(nq):
        reach = -(-(q_offset + (j0 + 1) * tq) // CHUNK) * CHUNK
        s_len = min(s_total, -(-min(reach, s_valid) // LANES) * LANES)
        out = _dsa(q, qi, kis, k_b, vt_b, ki_b, out, tq=tq, j0=j0, nj=1, s_len=s_len, s_valid=s_valid,
                   q_offset=q_offset, n_valid_q=n_valid_q, topk=topk)
    return out[:, :T]


def _ffn_kernel(x_ref, gdn_ref, att_ref, ssm_ref, mod_ref, wo_ref, g2_ref, wg_ref, wu_ref, cw_ref, cb_ref,
                wd_ref, prev_ref, fg_ref, *rest, nb, tt, nk, final):
    if final:
        xo_ref, y_ref, new_ref, h2_sc, acc_sc, buf_sc, carry_sc = rest
    else:
        xo_ref, new_ref, h2_sc, acc_sc, buf_sc, carry_sc = rest
        y_ref = None
    t = pl.program_id(1)
    k = pl.program_id(2)
    rows = nb * tt
    lo = SUBLANES - (FFN_CONV_W - 1)
    mod = mod_ref[...]

    @pl.when(k == 0)
    def _():
        mix = jnp.concatenate([gdn_ref[...], att_ref[...], ssm_ref[...]], axis=-1)
        proj = _dot(mix.reshape(rows, MIX_W).astype(BF16), wo_ref[...]).reshape(nb, tt, D_MODEL)
        x1 = x_ref[...] + mod[:, 2:3, :] * proj
        acc_sc[...] = x1
        ms = jnp.mean(x1 * x1, axis=-1, keepdims=True)
        h2 = x1 * lax.rsqrt(ms + EPS) * g2_ref[...]
        h2 = h2 * (1.0 + mod[:, 4:5, :]) + mod[:, 3:4, :]
        h2_sc[...] = h2.reshape(rows, D_MODEL).astype(BF16)

    @pl.when(t == 0)
    def _():
        carry_sc[k, :, lo:SUBLANES, :] = prev_ref[...]

    h2 = h2_sc[...]
    ag = _dot(h2, wg_ref[...])
    up = _dot(h2, wu_ref[...])
    tf = ag.shape[-1]
    buf_sc[:, lo:SUBLANES, :] = carry_sc[k, :, lo:SUBLANES, :]
    buf_sc[:, SUBLANES:SUBLANES + tt, :] = ag.reshape(nb, tt, tf)
    cw = cw_ref[...]
    conv = buf_sc[:, lo:lo + tt, :] * cw[0:1, :][None]
    for jj in range(1, FFN_CONV_W):
        conv = conv + buf_sc[:, lo + jj:lo + jj + tt, :] * cw[jj:jj + 1, :][None]
    tail = buf_sc[:, tt + lo:tt + SUBLANES, :]
    carry_sc[k, :, lo:SUBLANES, :] = tail
    new_ref[:, pl.ds(k, 1), :, :] = tail[:, None]
    act = _silu(conv + cb_ref[...][None]).reshape(rows, tf) * up
    y = _dot(act.astype(BF16), wd_ref[...]).reshape(nb, tt, D_MODEL)
    acc_sc[...] += mod[:, 5:6, :] * y

    @pl.when(k == nk - 1)
    def _():
        xo = acc_sc[...]
        xo_ref[...] = xo
        if final:
            ms = jnp.mean(xo * xo, axis=-1, keepdims=True)
            y_ref[...] = xo * lax.rsqrt(ms + EPS) * fg_ref[...]


def _ffn(x, gdn_o, att_o, ssm_o, mod, w_out, g2, w_gate, w_up, conv_w, conv_b, w_down, conv_prev, final_g,
         nb, tt, tf, final):
    B, T, _ = x.shape
    nk = D_FF // tf
    grid = (B // nb, T // tt, nk)
    row_spec = lambda w: pl.BlockSpec((nb, tt, w), lambda b, t, k: (b, t, 0))
    in_specs = [row_spec(D_MODEL), row_spec(GDN_W), row_spec(ATT_W), row_spec(SSM_W),
                pl.BlockSpec((nb, 6, D_MODEL), lambda b, t, k: (b, 0, 0)),
                pl.BlockSpec((MIX_W, D_MODEL), lambda b, t, k: (0, 0)),
                pl.BlockSpec((1, D_MODEL), lambda b, t, k: (0, 0)),
                pl.BlockSpec((D_MODEL, tf), lambda b, t, k: (0, k)),
                pl.BlockSpec((D_MODEL, tf), lambda b, t, k: (0, k)),
                pl.BlockSpec((FFN_CONV_W, tf), lambda b, t, k: (0, k)),
                pl.BlockSpec((1, tf), lambda b, t, k: (0, k)),
                pl.BlockSpec((tf, D_MODEL), lambda b, t, k: (k, 0)),
                pl.BlockSpec((nb, FFN_CONV_W - 1, tf), lambda b, t, k: (b, 0, k)),
                pl.BlockSpec((1, D_MODEL), lambda b, t, k: (0, 0))]
    out_shape = [jax.ShapeDtypeStruct((B, T, D_MODEL), F32)]
    out_specs = [row_spec(D_MODEL)]
    if final:
        out_shape.append(jax.ShapeDtypeStruct((B, T, D_MODEL), F32))
        out_specs.append(row_spec(D_MODEL))
    out_shape.append(jax.ShapeDtypeStruct((B, nk, FFN_CONV_W - 1, tf), F32))
    out_specs.append(pl.BlockSpec((nb, nk, FFN_CONV_W - 1, tf), lambda b, t, k: (b, 0, 0, 0)))
    res = pl.pallas_call(
        functools.partial(_ffn_kernel, nb=nb, tt=tt, nk=nk, final=final),
        grid=grid,
        in_specs=in_specs,
        out_specs=tuple(out_specs),
        out_shape=tuple(out_shape),
        scratch_shapes=[pltpu.VMEM((nb * tt, D_MODEL), BF16),
                        pltpu.VMEM((nb, tt, D_MODEL), F32),
                        pltpu.VMEM((nb, SUBLANES + tt, tf), F32),
                        pltpu.VMEM((nk, nb, SUBLANES, tf), F32)],
        compiler_params=_params(("arbitrary", "arbitrary", "arbitrary")),
        name="ffn",
    )(x, gdn_o, att_o, ssm_o, mod, w_out, g2, w_gate, w_up, conv_w, conv_b.reshape(1, -1), w_down,
      conv_prev, final_g)
    tail = res[-1].transpose(0, 2, 1, 3).reshape(B, FFN_CONV_W - 1, D_FF)
    return (*res[:-1], tail)


def _rope_tables(pos):
    half = ATT_DH // 2
    inv_freq = ROPE_THETA ** (-jnp.arange(half, dtype=F32) / half)
    ang = pos.astype(F32)[:, None] * inv_freq[None, :]
    cos = jnp.cos(ang)
    sin = jnp.sin(ang)
    cos_h = jnp.concatenate([cos, cos], axis=-1)
    sin_h = jnp.concatenate([-sin, sin], axis=-1)
    ones = jnp.ones((pos.shape[0], LANES - ATT_DH), F32)
    cos_t = jnp.concatenate([jnp.tile(cos_h, (1, ATT_HEADS)), cos_h, ones], axis=-1)
    sin_t = jnp.concatenate([jnp.tile(sin_h, (1, ATT_HEADS)), sin_h, 0.0 * ones], axis=-1)
    return cos_t, sin_t


def _layer(x, mod, pos_tables, cache, states, lw, tiles, layer, stacks, final_g):
    final = layer == DEPTH - 1
    (norm1_g, w_perm, gdn_conv_w, gdn_A_log, gdn_dt_bias, gdn_norm_g, ssm_conv_w, ssm_conv_b, ssm_A_log,
     ssm_dt_bias, ssm_D, ssm_norm_g, w_out, norm2_g, w_gate, w_up, ffn_conv_w, ffn_conv_b, w_down) = lw
    gdn_conv_prev, gdn_s0, ssm_conv_prev, ssm_h0, ffn_conv_prev = states
    B, T, _ = x.shape
    nb, tt, nbb, ffn_tt, tf = tiles
    cos_t, sin_t = pos_tables
    proj = _inproj(x, mod, norm1_g.reshape(1, -1), w_perm, cos_t, sin_t, nb, tt, cache is None, layer, stacks)
    gqkv, ggate, q, k_stack, v_stack, qi, kis, z, xbc = proj[:len(OUT_SEGS)]
    stacks = (k_stack, v_stack, proj[len(OUT_SEGS)])

    gdn_o, gdn_conv_new, gdn_s = _gdn(gqkv, kis, ggate, gdn_conv_w, gdn_conv_prev, gdn_s0, gdn_A_log,
                                      gdn_dt_bias, gdn_norm_g, nbb)
    ssm_o, ssm_conv_new, ssm_h = _ssd(xbc, z, kis, ssm_conv_w, ssm_conv_b, ssm_conv_prev, ssm_h0, ssm_A_log,
                                      ssm_dt_bias, ssm_D, ssm_norm_g, nbb)
    if cache is None:
        k_b, ki_b, vt_b = proj[len(OUT_SEGS) + 1:]
        att_o = _dsa_group(q, qi, kis, k_b, vt_b, ki_b, s_valid=T, q_offset=0, topk=min(TOPK_MAX, T // 4))
    else:
        ck, cv, cki = cache
        P = ck.shape[1]
        att_o = _dsa_rows(q, qi, kis, k_stack[layer], v_stack[layer], ck.reshape(B, P, ATT_W).astype(BF16),
                          cv.reshape(B, P, ATT_W).astype(BF16), cki.astype(BF16), nbq=nbb,
                          topk=min(TOPK_MAX, (P + T) // 4))

    res = _ffn(x, gdn_o, att_o, ssm_o, mod, w_out, norm2_g.reshape(1, -1), w_gate, w_up, ffn_conv_w, ffn_conv_b,
               w_down, ffn_conv_prev, final_g.reshape(1, -1), ffn_tt[0], ffn_tt[1], tf, final)
    if final:
        x_new, y, ffn_conv_new = res
    else:
        (x_new, ffn_conv_new), y = res, None
    return x_new, y, (gdn_conv_new, gdn_s, ssm_conv_new, ssm_h, ffn_conv_new), stacks


def kernel(x_prompt, x_sample, c_prompt, c_sample, cache_k, cache_v, cache_kidx, state_gdn_conv, state_gdn,
           state_ssm_conv, state_ssm, state_ffn_conv, w_ada, b_ada, norm1_g, w_in, gdn_conv_w, gdn_A_log,
           gdn_dt_bias, gdn_norm_g, ssm_conv_w, ssm_conv_b, ssm_A_log, ssm_dt_bias, ssm_D, ssm_norm_g, w_out,
           norm2_g, w_gate, w_up, ffn_conv_w, ffn_conv_b, w_down, final_g):
    Bp, T, _ = x_prompt.shape
    Bs, Ts, _ = x_sample.shape
    P = cache_k.shape[2]

    c_all = jnp.concatenate([c_prompt, c_sample], axis=0)
    mod_all = _ada(c_all, w_ada, b_ada).reshape(DEPTH, Bp + Bs, 6, D_MODEL)

    tables_p = _rope_tables(jnp.arange(T))
    tables_s = _rope_tables(P + jnp.arange(Ts))

    w_perm = _permute_columns(w_in)
    w_out_b = w_out.astype(BF16)
    w_gate_b = w_gate.astype(BF16)
    w_up_b = w_up.astype(BF16)
    w_down_b = w_down.astype(BF16)

    zeros_p = (jnp.zeros((Bp, CONV_W - 1, GDN_CONV_C), F32),
               jnp.zeros((Bp, GDN_HEADS, GDN_DK, GDN_DV), F32),
               jnp.zeros((Bp, CONV_W - 1, SSM_CONV_C), F32),
               jnp.zeros((Bp, SSM_HEADS, SSM_P, SSM_N), F32),
               jnp.zeros((Bp, FFN_CONV_W - 1, D_FF), F32))

    tf = D_FF // 2
    tiles_p = (1, min(256, T), math.gcd(Bp, 8), (1, min(512, T)), tf)
    tiles_s = (Bs, Ts, math.gcd(Bs, 4), (Bs, Ts), tf)

    xp, xs = x_prompt, x_sample
    new_p, new_s = [], []
    yp = ys = stacks_p = stacks_s = None
    for l in range(DEPTH):
        lw = (norm1_g[l], w_perm[l], gdn_conv_w[l], gdn_A_log[l], gdn_dt_bias[l], gdn_norm_g[l], ssm_conv_w[l],
              ssm_conv_b[l], ssm_A_log[l], ssm_dt_bias[l], ssm_D[l], ssm_norm_g[l], w_out_b[l], norm2_g[l],
              w_gate_b[l], w_up_b[l], ffn_conv_w[l], ffn_conv_b[l], w_down_b[l])
        xp, yp, st_p, stacks_p = _layer(xp, mod_all[l, :Bp], tables_p, None, zeros_p, lw, tiles_p, l, stacks_p,
                                        final_g)
        states_s = (state_gdn_conv[l], state_gdn[l], state_ssm_conv[l], state_ssm[l], state_ffn_conv[l])
        xs, ys, st_s, stacks_s = _layer(xs, mod_all[l, Bp:], tables_s, (cache_k[l], cache_v[l], cache_kidx[l]),
                                        states_s, lw, tiles_s, l, stacks_s, final_g)
        new_p.append(st_p)
        new_s.append(st_s)

    def outputs(stacks, per_layer):
        k_stack, v_stack, ki_stack = stacks
        heads = k_stack.shape[:3] + (ATT_HEADS, ATT_DH)
        return [k_stack.reshape(heads), v_stack.reshape(heads), ki_stack] + [
            jnp.stack([st[i] for st in per_layer]) for i in range(len(per_layer[0]))]

    return (yp, ys, *outputs(stacks_p, new_p), *outputs(stacks_s, new_s))
```

```python
import functools
import math

import jax
import jax.numpy as jnp
import numpy as np
from jax import lax
from jax.experimental import pallas as pl
from jax.experimental.pallas import tpu as pltpu

F32 = jnp.float32
BF16 = jnp.bfloat16
HI = lax.Precision.HIGHEST

D_MODEL = 1024
DEPTH = 2
CHUNK = 64
CONV_W = 4
FFN_CONV_W = 3
D_FF = 2816
ROPE_THETA = 10000.0
EPS = 1e-6
GDN_HEADS = 4
GDN_DK = 64
GDN_DV = 64
ATT_HEADS = 4
ATT_DH = 64
IDX_HEADS = 4
IDX_DH = 64
TOPK_MAX = 256
SSM_HEADS = 8
SSM_P = 64
SSM_GROUPS = 2
SSM_N = 128
GDN_W = GDN_HEADS * GDN_DV
ATT_W = ATT_HEADS * ATT_DH
SSM_W = SSM_HEADS * SSM_P
MIX_W = GDN_W + ATT_W + SSM_W
GDN_CONV_C = 2 * GDN_HEADS * GDN_DK + GDN_W
SSM_CONV_C = SSM_W + 2 * SSM_GROUPS * SSM_N
IN_SIZES = (GDN_CONV_C, GDN_HEADS, GDN_HEADS, GDN_W,
            ATT_W, ATT_W, ATT_W, IDX_HEADS * IDX_DH, IDX_DH, IDX_HEADS,
            SSM_W, SSM_CONV_C, SSM_HEADS)
IN_W = sum(IN_SIZES)

LANES = 128
SUBLANES = 8
VMEM_LIMIT = 56 * 1024 * 1024

KIS_A = IDX_DH
KIS_B = KIS_A + GDN_HEADS
KIS_WI = KIS_B + GDN_HEADS
KIS_DT = KIS_WI + IDX_HEADS
KIS_END = KIS_DT + SSM_HEADS
OUT_SEGS = (("gqkv", GDN_CONV_C), ("ggate", GDN_W), ("q", ATT_W), ("k", ATT_W), ("v", ATT_W),
            ("qi", IDX_HEADS * IDX_DH), ("kis", LANES), ("z", SSM_W), ("xbc", SSM_CONV_C))
PERM_W = sum(w for _, w in OUT_SEGS)
ROT_W = ATT_W + LANES


def _perm_columns():
    starts = np.concatenate([[0], np.cumsum(IN_SIZES)])
    (s_gqkv, s_ga, s_gb, s_gg, s_q, s_k, s_v, s_qi, s_ki, s_wi, s_z, s_xbc, s_dt) = starts[:-1]
    cols = []
    cols += list(range(s_gqkv, s_gqkv + GDN_CONV_C))
    cols += list(range(s_gg, s_gg + GDN_W))
    cols += list(range(s_q, s_q + ATT_W))
    cols += list(range(s_k, s_k + ATT_W))
    cols += list(range(s_v, s_v + ATT_W))
    cols += list(range(s_qi, s_qi + IDX_HEADS * IDX_DH))
    kis = (list(range(s_ki, s_ki + IDX_DH)) + list(range(s_ga, s_ga + GDN_HEADS))
           + list(range(s_gb, s_gb + GDN_HEADS)) + list(range(s_wi, s_wi + IDX_HEADS))
           + list(range(s_dt, s_dt + SSM_HEADS)))
    cols += kis + [-1] * (LANES - len(kis))
    cols += list(range(s_z, s_z + SSM_W))
    cols += list(range(s_xbc, s_xbc + SSM_CONV_C))
    assert len(cols) == PERM_W
    return np.asarray(cols, np.int32)


_PERM_COLS = _perm_columns()


def _perm_runs():
    runs = []
    start = 0
    for i in range(1, PERM_W + 1):
        prev = int(_PERM_COLS[i - 1])
        if i == PERM_W or not ((prev < 0 and _PERM_COLS[i] < 0) or (prev >= 0 and _PERM_COLS[i] == prev + 1)):
            runs.append((start, int(_PERM_COLS[start]), i - start))
            start = i
    return runs


_PERM_RUNS = _perm_runs()
WPREP_ROWS = 256


def _wprep_kernel(w_ref, o_ref):
    w = w_ref[0]
    for dst, src, n in _PERM_RUNS:
        if src < 0:
            o_ref[0, :, dst:dst + n] = jnp.zeros((w.shape[0], n), BF16)
        else:
            o_ref[0, :, dst:dst + n] = w[:, src:src + n].astype(BF16)


def _permute_columns(w_in):
    depth, rows, _ = w_in.shape
    return pl.pallas_call(
        _wprep_kernel,
        grid=(depth, rows // WPREP_ROWS),
        in_specs=[pl.BlockSpec((1, WPREP_ROWS, IN_W), lambda l, r: (l, r, 0))],
        out_specs=pl.BlockSpec((1, WPREP_ROWS, PERM_W), lambda l, r: (l, r, 0)),
        out_shape=jax.ShapeDtypeStruct((depth, rows, PERM_W), BF16),
        compiler_params=pltpu.CompilerParams(dimension_semantics=("arbitrary", "arbitrary"),
                                             vmem_limit_bytes=VMEM_LIMIT),
        name="wprep",
    )(w_in)


def _silu(x):
    return x * jax.nn.sigmoid(x)


def _softplus(x):
    return jnp.maximum(x, 0.0) + jnp.log1p(jnp.exp(-jnp.abs(x)))


def _dot(a, b, precision=None):
    return jnp.dot(a, b, preferred_element_type=F32, precision=precision)


def _dot_nt(a, b, precision=None):
    return lax.dot_general(a, b, (((1,), (1,)), ((), ())), preferred_element_type=F32,
                           precision=precision)


def _dot_tn(a, b, precision=None):
    return lax.dot_general(a, b, (((0,), (0,)), ((), ())), preferred_element_type=F32,
                           precision=precision)


def _cumsum_rows(x):
    c = x.shape[0]
    row = lax.broadcasted_iota(jnp.int32, x.shape, 0)
    s = 1
    while s < c:
        x = x + jnp.where(row >= s, pltpu.roll(x, s, axis=0), 0.0)
        s *= 2
    return x


def _mm(a, b):
    return _dot(a.astype(BF16), b.astype(BF16))


def _mm_nt(a, b):
    return _dot_nt(a.astype(BF16), b.astype(BF16))


def _split(a):
    hi = a.astype(BF16)
    return hi, (a - hi.astype(F32)).astype(BF16)


def _mm3(a, b):
    (ah, al), (bh, bl) = a, b
    return _dot(ah, bh) + (_dot(ah, bl) + _dot(al, bh))


def _lane_spread(vals, lane0, n_heads):
    width = n_heads * GDN_DV
    r = lax.broadcasted_iota(jnp.int32, (LANES, width), 0)
    cidx = lax.broadcasted_iota(jnp.int32, (LANES, width), 1)
    pick = (r == lane0 + cidx // GDN_DV).astype(BF16)
    p1 = vals.astype(BF16)
    r1 = vals - p1.astype(F32)
    p2 = r1.astype(BF16)
    p3 = (r1 - p2.astype(F32)).astype(BF16)
    return _dot(p1, pick) + (_dot(p2, pick) + _dot(p3, pick))


def _head_sums(x):
    r = lax.broadcasted_iota(jnp.int32, (LANES, LANES), 0) // GDN_DV
    cidx = lax.broadcasted_iota(jnp.int32, (LANES, LANES), 1) // GDN_DV
    ones = (r == cidx).astype(BF16)
    hi, lo = _split(x)
    cols = []
    for g in range(x.shape[1] // LANES):
        sl = slice(g * LANES, (g + 1) * LANES)
        cols.append(_dot(hi[:, sl], ones) + _dot(lo[:, sl], ones))
    return jnp.concatenate(cols, axis=1)


def _transpose_rows(vals):
    c = vals.shape[0]
    if c < LANES:
        vals = jnp.concatenate([vals, jnp.zeros((LANES - c, LANES), vals.dtype)], axis=0)
    return vals.T[:, :c]


def _lane_vector(vals, start):
    return jnp.zeros((1, LANES), F32).at[0, start:start + vals.shape[0]].set(vals.astype(F32))


def _params(sem):
    return pltpu.CompilerParams(dimension_semantics=sem, vmem_limit_bytes=VMEM_LIMIT)


def _ada_kernel(c_ref, w_ref, b_ref, o_ref):
    s = _silu(c_ref[...])
    o_ref[0] = _dot(s.astype(BF16), w_ref[0].astype(BF16)) + b_ref[0]


def _ada(c_all, w_ada, b_ada):
    rows = c_all.shape[0]
    n = w_ada.shape[2]
    tn = 1536
    return pl.pallas_call(
        _ada_kernel,
        grid=(DEPTH, n // tn),
        in_specs=[pl.BlockSpec((rows, D_MODEL), lambda l, j: (0, 0)),
                  pl.BlockSpec((1, D_MODEL, tn), lambda l, j: (l, 0, j)),
                  pl.BlockSpec((1, 1, tn), lambda l, j: (l, 0, j))],
        out_specs=pl.BlockSpec((1, rows, tn), lambda l, j: (l, 0, j)),
        out_shape=jax.ShapeDtypeStruct((DEPTH, rows, n), F32),
        compiler_params=_params(("arbitrary", "arbitrary")),
        name="ada",
    )(c_all, w_ada, b_ada.reshape(DEPTH, 1, n))


def _rotate(x, cos, sin_signed):
    w = x.shape[-1]
    lane = lax.broadcasted_iota(jnp.int32, x.shape, x.ndim - 1)
    first = (lane % ATT_DH) < (ATT_DH // 2)
    swapped = jnp.where(first, pltpu.roll(x, w - ATT_DH // 2, axis=x.ndim - 1),
                        pltpu.roll(x, ATT_DH // 2, axis=x.ndim - 1))
    return x * cos + swapped * sin_signed


def _inproj_kernel(x_ref, mod_ref, g_ref, w_ref, cos_ref, sin_ref, *refs, nb, tt, emit_keys, n_alias):
    out_refs = refs[n_alias:]
    x = x_ref[...]
    ms = jnp.mean(x * x, axis=-1, keepdims=True)
    xn = x * lax.rsqrt(ms + EPS) * g_ref[...]
    mod = mod_ref[...]
    h = xn * (1.0 + mod[:, 1:2, :]) + mod[:, 0:1, :]
    u = _dot(h.reshape(nb * tt, D_MODEL).astype(BF16), w_ref[...])
    cos = cos_ref[...]
    sin = sin_ref[...]
    off = 0
    segs = {}
    for (name, width), o_ref in zip(OUT_SEGS, out_refs):
        seg = u[:, off:off + width]
        if name in ("q", "k", "qi", "kis"):
            t0 = ATT_W if name == "kis" else 0
            c3 = cos[:, t0:t0 + width][None]
            s3 = sin[:, t0:t0 + width][None]
            if nb > 1:
                c3 = jnp.broadcast_to(c3, (nb, tt, width)).reshape(nb * tt, width)
                s3 = jnp.broadcast_to(s3, (nb, tt, width)).reshape(nb * tt, width)
            else:
                c3 = c3[0]
                s3 = s3[0]
            seg = _rotate(seg, c3, s3)
        o_ref[...] = seg.reshape(o_ref.shape)
        segs[name] = seg
        off += width
    ki_ref = out_refs[len(OUT_SEGS)]
    ki_ref[...] = segs["kis"][:, :IDX_DH].reshape(ki_ref.shape)
    if emit_keys:
        k_b_ref, ki_b_ref, vt_b_ref = out_refs[len(OUT_SEGS) + 1:]
        k_b_ref[0] = segs["k"].astype(BF16)
        ki_b_ref[0] = segs["kis"][:, :IDX_DH].astype(BF16)
        vt_b_ref[0] = segs["v"].T.astype(BF16)


STACKED = ("k", "v", "ki")


def _inproj(x, mod, g1, w_perm, cos, sin, nb, tt, emit_keys, layer, stacks):
    B, T, _ = x.shape
    grid = (B // nb, T // tt)
    names = [n for n, _ in OUT_SEGS] + ["ki"]
    widths = [w for _, w in OUT_SEGS] + [IDX_DH]
    out_shape, out_specs = (), ()
    for n, w in zip(names, widths):
        if n in STACKED:
            out_shape += (jax.ShapeDtypeStruct((DEPTH, B, T, w), F32),)
            out_specs += (pl.BlockSpec((1, nb, tt, w), lambda b, t: (layer, b, t, 0)),)
        else:
            out_shape += (jax.ShapeDtypeStruct((B, T, w), F32),)
            out_specs += (pl.BlockSpec((nb, tt, w), lambda b, t: (b, t, 0)),)
    alias_in = () if stacks is None else tuple(stacks)
    n_in = 6
    aliases = {n_in + i: names.index(n) for i, n in enumerate(STACKED)} if alias_in else {}
    if emit_keys:
        assert nb == 1
        out_shape += (jax.ShapeDtypeStruct((B, T, ATT_W), BF16), jax.ShapeDtypeStruct((B, T, IDX_DH), BF16),
                      jax.ShapeDtypeStruct((B, ATT_W, T), BF16))
        out_specs += (pl.BlockSpec((1, tt, ATT_W), lambda b, t: (b, t, 0)),
                      pl.BlockSpec((1, tt, IDX_DH), lambda b, t: (b, t, 0)),
                      pl.BlockSpec((1, ATT_W, tt), lambda b, t: (b, 0, t)))
    return pl.pallas_call(
        functools.partial(_inproj_kernel, nb=nb, tt=tt, emit_keys=emit_keys, n_alias=len(alias_in)),
        grid=grid,
        in_specs=[pl.BlockSpec((nb, tt, D_MODEL), lambda b, t: (b, t, 0)),
                  pl.BlockSpec((nb, 6, D_MODEL), lambda b, t: (b, 0, 0)),
                  pl.BlockSpec((1, D_MODEL), lambda b, t: (0, 0)),
                  pl.BlockSpec((D_MODEL, PERM_W), lambda b, t: (0, 0)),
                  pl.BlockSpec((tt, ROT_W), lambda b, t: (t, 0)),
                  pl.BlockSpec((tt, ROT_W), lambda b, t: (t, 0))]
        + [pl.BlockSpec(memory_space=pl.ANY)] * len(alias_in),
        out_specs=out_specs,
        out_shape=out_shape,
        input_output_aliases=aliases,
        compiler_params=_params(("arbitrary", "arbitrary")),
        name="inproj",
    )(x, mod, g1, w_perm, cos, sin, *alias_in)


def _short_conv(ext_ref, bi, u, w, c):
    lo = SUBLANES - (CONV_W - 1)
    ext_ref[bi, SUBLANES:SUBLANES + c, :] = u
    y = ext_ref[bi, lo:lo + c, :] * w[0:1, :]
    for j in range(1, CONV_W):
        y = y + ext_ref[bi, lo + j:lo + j + c, :] * w[j:j + 1, :]
    tail = ext_ref[bi, c + lo:c + SUBLANES, :]
    ext_ref[bi, lo:SUBLANES, :] = tail
    return y, tail


def _neumann_inverses(mats, c):
    eye = (lax.broadcasted_iota(jnp.int32, (c, c), 0)
           == lax.broadcasted_iota(jnp.int32, (c, c), 1)).astype(F32)
    ps = [eye - a for a in mats]
    sp = [_split(a) for a in mats]
    pws = [_mm3(s, s) for s in sp]
    n = 2
    while n < c:
        sp = [_split(pw) for pw in pws]
        ps = [p + _mm3(_split(p), s) for p, s in zip(ps, sp)]
        n *= 2
        if n < c:
            pws = [_mm3(s, s) for s in sp]
    return ps


def _gdn_kernel(qkv_ref, kis_ref, gate_ref, w_ref, prev_ref, s0_ref, alog_ref, dtb_ref, ng_ref,
                o_ref, new_ref, s_ref, ext_ref, s_sc, *, c, nbb):
    j = pl.program_id(1)
    lo = SUBLANES - (CONV_W - 1)

    @pl.when(j == 0)
    def _():
        s_sc[...] = s0_ref[...]
        ext_ref[:, lo:SUBLANES, :] = prev_ref[...]

    ng = ng_ref[...]
    w_conv = w_ref[...]
    ri = lax.broadcasted_iota(jnp.int32, (c, c), 0)
    ci = lax.broadcasted_iota(jnp.int32, (c, c), 1)
    incl = ri >= ci
    strict = ri > ci
    nkd = GDN_HEADS * GDN_DK
    ch = []
    for bi in range(nbb):
        y, tail = _short_conv(ext_ref, bi, qkv_ref[bi], w_conv, c)
        new_ref[bi] = tail
        y = _silu(y)
        small = kis_ref[bi]
        g = -jnp.exp(alog_ref[...]) * _softplus(small + dtb_ref[...])
        beta = jax.nn.sigmoid(small)
        gc = _cumsum_rows(g)
        gc_t = _transpose_rows(gc)
        qk_n = y[:, :2 * nkd]
        qk_n = qk_n * lax.rsqrt(_head_sums(qk_n * qk_n) + EPS)
        for h in range(GDN_HEADS):
            qh = qk_n[:, h * GDN_DK:(h + 1) * GDN_DK] * (GDN_DK ** -0.5)
            kh = qk_n[:, nkd + h * GDN_DK:nkd + (h + 1) * GDN_DK]
            vh = y[:, 2 * nkd + h * GDN_DV:2 * nkd + (h + 1) * GDN_DV]
            ch.append((qh, kh, vh, gc[:, KIS_A + h:KIS_A + h + 1], gc_t[KIS_A + h:KIS_A + h + 1, :],
                       beta[:, KIS_B + h:KIS_B + h + 1]))
    ids = [(bi, h) for bi in range(nbb) for h in range(GDN_HEADS)]
    decay = [jnp.where(incl, jnp.exp(jnp.where(incl, gcol - grow, 0.0)), 0.0)
             for (_, _, _, gcol, grow, _) in ch]
    kb = [kh * bcol for (_, kh, _, _, _, bcol) in ch]
    eg = [jnp.exp(gcol) for (_, _, _, gcol, _, _) in ch]
    grams = [_mm_nt(jnp.concatenate([kbi, x[0]], axis=0), x[1]) for kbi, x in zip(kb, ch)]
    t_inv = _neumann_inverses([jnp.where(strict, m[:c] * d, 0.0) for m, d in zip(grams, decay)], c)
    attn = [jnp.where(incl, m[c:] * d, 0.0) for m, d in zip(grams, decay)]
    rhs = [jnp.concatenate([x[2] * x[5], kbi * e], axis=1) for x, kbi, e in zip(ch, kb, eg)]
    sol = [_mm(t, r) for t, r in zip(t_inv, rhs)]
    st = [s_sc[bi, h] for bi, h in ids]
    ws = [_mm(sl[:, GDN_DV:], s) for sl, s in zip(sol, st)]
    qs = [_mm(x[0] * e, s) for x, e, s in zip(ch, eg, st)]
    v_new = [sl[:, :GDN_DV] - w for sl, w in zip(sol, ws)]
    av = [_mm(a, v) for a, v in zip(attn, v_new)]
    g_last = [x[3][c - 1:c, :] for x in ch]
    kv = [_dot_tn(x[1] * jnp.exp(gl - x[3]), v) for x, gl, v in zip(ch, g_last, v_new)]
    for (bi, h), s, gl, kvi in zip(ids, st, g_last, kv):
        s_new = s * jnp.exp(gl) + kvi
        s_sc[bi, h] = s_new
        s_ref[bi, h] = s_new
    for bi in range(nbb):
        o = jnp.concatenate([qs[bi * GDN_HEADS + h] + av[bi * GDN_HEADS + h] for h in range(GDN_HEADS)], axis=1)
        o = o * lax.rsqrt(_head_sums(o * o) * (1.0 / GDN_DV) + EPS) * ng
        o_ref[bi] = o * _silu(gate_ref[bi])


def _gdn(gqkv, kis, ggate, conv_w, conv_prev, s0, a_log, dt_bias, norm_g, nbb):
    B, T, _ = gqkv.shape
    c = math.gcd(T, CHUNK)
    return pl.pallas_call(
        functools.partial(_gdn_kernel, c=c, nbb=nbb),
        grid=(B // nbb, T // c),
        in_specs=[pl.BlockSpec((nbb, c, GDN_CONV_C), lambda b, j: (b, j, 0)),
                  pl.BlockSpec((nbb, c, LANES), lambda b, j: (b, j, 0)),
                  pl.BlockSpec((nbb, c, GDN_W), lambda b, j: (b, j, 0)),
                  pl.BlockSpec((CONV_W, GDN_CONV_C), lambda b, j: (0, 0)),
                  pl.BlockSpec((nbb, CONV_W - 1, GDN_CONV_C), lambda b, j: (b, 0, 0)),
                  pl.BlockSpec((nbb, GDN_HEADS, GDN_DK, GDN_DV), lambda b, j: (b, 0, 0, 0)),
                  pl.BlockSpec((1, LANES), lambda b, j: (0, 0)),
                  pl.BlockSpec((1, LANES), lambda b, j: (0, 0)),
                  pl.BlockSpec((1, GDN_W), lambda b, j: (0, 0))],
        out_specs=(pl.BlockSpec((nbb, c, GDN_W), lambda b, j: (b, j, 0)),
                   pl.BlockSpec((nbb, CONV_W - 1, GDN_CONV_C), lambda b, j: (b, 0, 0)),
                   pl.BlockSpec((nbb, GDN_HEADS, GDN_DK, GDN_DV), lambda b, j: (b, 0, 0, 0))),
        out_shape=(jax.ShapeDtypeStruct((B, T, GDN_W), F32),
                   jax.ShapeDtypeStruct((B, CONV_W - 1, GDN_CONV_C), F32),
                   jax.ShapeDtypeStruct((B, GDN_HEADS, GDN_DK, GDN_DV), F32)),
        scratch_shapes=[pltpu.VMEM((nbb, SUBLANES + c, GDN_CONV_C), F32),
                        pltpu.VMEM((nbb, GDN_HEADS, GDN_DK, GDN_DV), F32)],
        compiler_params=_params(("arbitrary", "arbitrary")),
        name="gdn",
    )(gqkv, kis, ggate, conv_w, conv_prev, s0, _lane_vector(a_log, KIS_A), _lane_vector(dt_bias, KIS_A),
      jnp.tile(norm_g, GDN_HEADS).reshape(1, -1))


def _ssd_kernel(xbc_ref, z_ref, kis_ref, w_ref, cb_ref, prev_ref, h0_ref, alog_ref, dtb_ref, dsk_ref,
                ng_ref, o_ref, new_ref, h_ref, ext_ref, h_sc, *, c, nbb):
    j = pl.program_id(1)
    lo = SUBLANES - (CONV_W - 1)

    @pl.when(j == 0)
    def _():
        h_sc[...] = h0_ref[...]
        ext_ref[:, lo:SUBLANES, :] = prev_ref[...]

    dsk = dsk_ref[...]
    ng = ng_ref[...]
    w_conv = w_ref[...]
    ri = lax.broadcasted_iota(jnp.int32, (c, c), 0)
    ci = lax.broadcasted_iota(jnp.int32, (c, c), 1)
    incl = ri >= ci
    gn = SSM_GROUPS * SSM_N
    rep = SSM_HEADS // SSM_GROUPS
    gw = SSM_W // SSM_GROUPS
    grp_in = []
    ch = []
    slot = []
    ids = []
    for bi in range(nbb):
        y, tail = _short_conv(ext_ref, bi, xbc_ref[bi], w_conv, c)
        new_ref[bi] = tail
        y = _silu(y + cb_ref[...])
        dts = _softplus(kis_ref[bi] + dtb_ref[...])
        a = dts * (-jnp.exp(alog_ref[...]))
        acum = _cumsum_rows(a)
        acum_t = _transpose_rows(acum)
        dtm = _lane_spread(dts, KIS_DT, SSM_HEADS)
        am = _lane_spread(acum, KIS_DT, SSM_HEADS)
        a_last = am[c - 1:c, :]
        xs = y[:, :SSM_W]
        xdt = xs * dtm
        xw = xs * (jnp.exp(a_last - am) * dtm)
        h_dec = jnp.exp(a_last)
        slot.append((xs, jnp.exp(am), _silu(z_ref[bi])))
        for grp in range(SSM_GROUPS):
            grp_in.append((y[:, SSM_W + grp * SSM_N:SSM_W + (grp + 1) * SSM_N],
                           y[:, SSM_W + gn + grp * SSM_N:SSM_W + gn + (grp + 1) * SSM_N]))
            for hh in range(rep):
                h = grp * rep + hh
                hsl = slice(h * SSM_P, (h + 1) * SSM_P)
                ch.append((xdt[:, hsl], xw[:, hsl], am[:, h * SSM_P:h * SSM_P + c], acum_t[KIS_DT + h:KIS_DT + h + 1, :],
                           jnp.concatenate([h_dec[:, hsl]] * (SSM_N // SSM_P), axis=1), len(grp_in) - 1))
                ids.append((bi, h))
    cb = [_mm_nt(cg, bg) for bg, cg in grp_in]
    seg = [jnp.where(incl, jnp.exp(jnp.where(incl, x[2] - x[3], 0.0)), 0.0) for x in ch]
    hs = [h_sc[bi, h] for bi, h in ids]
    y_in = [_mm(cb[x[5]] * sg, x[0]) for x, sg in zip(ch, seg)]
    y_st = [_mm_nt(grp_in[x[5]][1], hst) for x, hst in zip(ch, hs)]
    upd = [_dot_tn(x[1], grp_in[x[5]][0]) for x in ch]
    for (bi, h), x, hst, up in zip(ids, ch, hs, upd):
        h_new = hst * x[4] + up
        h_sc[bi, h] = h_new
        h_ref[bi, h] = h_new
    for bi, (xs, ea, zg) in enumerate(slot):
        lo_c, hi_c = bi * SSM_HEADS, (bi + 1) * SSM_HEADS
        yy = (jnp.concatenate(y_in[lo_c:hi_c], axis=1) + jnp.concatenate(y_st[lo_c:hi_c], axis=1) * ea
              + dsk * xs) * zg
        for grp in range(SSM_GROUPS):
            yg = yy[:, grp * gw:(grp + 1) * gw]
            yg = yg * lax.rsqrt(jnp.mean(yg * yg, axis=-1, keepdims=True) + EPS)
            o_ref[bi, :, grp * gw:(grp + 1) * gw] = yg * ng[:, grp * gw:(grp + 1) * gw]


def _ssd(xbc, z, kis, conv_w, conv_b, conv_prev, h0, a_log, dt_bias, d_skip, norm_g, nbb):
    B, T, _ = xbc.shape
    c = math.gcd(T, CHUNK)
    return pl.pallas_call(
        functools.partial(_ssd_kernel, c=c, nbb=nbb),
        grid=(B // nbb, T // c),
        in_specs=[pl.BlockSpec((nbb, c, SSM_CONV_C), lambda b, j: (b, j, 0)),
                  pl.BlockSpec((nbb, c, SSM_W), lambda b, j: (b, j, 0)),
                  pl.BlockSpec((nbb, c, LANES), lambda b, j: (b, j, 0)),
                  pl.BlockSpec((CONV_W, SSM_CONV_C), lambda b, j: (0, 0)),
                  pl.BlockSpec((1, SSM_CONV_C), lambda b, j: (0, 0)),
                  pl.BlockSpec((nbb, CONV_W - 1, SSM_CONV_C), lambda b, j: (b, 0, 0)),
                  pl.BlockSpec((nbb, SSM_HEADS, SSM_P, SSM_N), lambda b, j: (b, 0, 0, 0)),
                  pl.BlockSpec((1, LANES), lambda b, j: (0, 0)),
                  pl.BlockSpec((1, LANES), lambda b, j: (0, 0)),
                  pl.BlockSpec((1, SSM_W), lambda b, j: (0, 0)),
                  pl.BlockSpec((1, SSM_W), lambda b, j: (0, 0))],
        out_specs=(pl.BlockSpec((nbb, c, SSM_W), lambda b, j: (b, j, 0)),
                   pl.BlockSpec((nbb, CONV_W - 1, SSM_CONV_C), lambda b, j: (b, 0, 0)),
                   pl.BlockSpec((nbb, SSM_HEADS, SSM_P, SSM_N), lambda b, j: (b, 0, 0, 0))),
        out_shape=(jax.ShapeDtypeStruct((B, T, SSM_W), F32),
                   jax.ShapeDtypeStruct((B, CONV_W - 1, SSM_CONV_C), F32),
                   jax.ShapeDtypeStruct((B, SSM_HEADS, SSM_P, SSM_N), F32)),
        scratch_shapes=[pltpu.VMEM((nbb, SUBLANES + c, SSM_CONV_C), F32),
                        pltpu.VMEM((nbb, SSM_HEADS, SSM_P, SSM_N), F32)],
        compiler_params=_params(("arbitrary", "arbitrary")),
        name="ssd",
    )(xbc, z, kis, conv_w, conv_b.reshape(1, -1), conv_prev, h0, _lane_vector(a_log, KIS_DT),
      _lane_vector(dt_bias, KIS_DT), jnp.repeat(d_skip, SSM_P).reshape(1, -1), norm_g.reshape(1, -1))


IDX_BITS = 12


OPENING_PROBES = 14
ROUND_PROBES = 2
SEARCH_ROUNDS = 4096
FOLD_ROWS = 128


def _fold_keys(x, op, axis=0):
    s = x.shape[0]
    if axis == 0 and s % FOLD_ROWS == 0 and s > FOLD_ROWS:
        x = op(x.reshape(s // FOLD_ROWS, FOLD_ROWS, x.shape[1]), axis=0)
    return op(x, axis=axis, keepdims=True)


def _fold_rows(x, op):
    return _fold_keys(x, op, 0)


def _select_topk(score, adm, pos, skip, topk, axis):
    def count(mask):
        return _fold_keys(mask.astype(F32), jnp.sum, axis)

    s = jnp.where(adm, score, -jnp.inf)

    few = (count(adm) <= topk) | skip
    smax = _fold_keys(s, jnp.max, axis)
    smin = _fold_keys(jnp.where(adm, score, jnp.inf), jnp.min, axis)
    top_full = count(s >= smax) >= topk
    lo0 = jnp.where(few, 0.0, jnp.where(top_full, smax, smin))
    hi0 = jnp.where(few, 0.0, smax)
    done0 = (few | top_full).astype(F32)

    def probe(lo, hi, done):
        mid = 0.5 * lo + 0.5 * hi
        ok = (mid > lo) & (mid < hi) & (done < 0.5)
        take = count(s >= mid) >= topk
        return jnp.where(ok & take, mid, lo), jnp.where(ok & jnp.logical_not(take), mid, hi)

    def opening(lo, hi):
        for _ in range(OPENING_PROBES):
            lo, hi = probe(lo, hi, done0)
        return lo, hi

    lo1, hi1 = lax.cond(jnp.min(done0) < 0.5, opening, lambda lo, hi: (lo, hi), lo0, hi0)

    def search_cond(c):
        return (jnp.min(c[2]) < 0.5) & (c[3] < SEARCH_ROUNDS)

    def search_body(c):
        lo, hi, done, it = c
        last = _fold_keys(jnp.where(s < hi, s, -jnp.inf), jnp.max, axis)
        reached = count(s >= last) >= topk
        active = done < 0.5
        lo = jnp.where(active & reached, last, lo)
        hi = jnp.where(active & jnp.logical_not(reached), last, hi)
        done = jnp.maximum(done, reached.astype(F32))
        for _ in range(ROUND_PROBES):
            lo, hi = probe(lo, hi, done)
        return lo, hi, done, it + 1

    lo, _, _, _ = lax.while_loop(search_cond, search_body, (lo1, hi1, done0, jnp.int32(0)))
    t = jnp.where(few, -jnp.inf, lo)
    gt = s > t
    tie = s == t
    need = topk - count(gt)
    excess = jnp.where(few, 0.0, count(tie) - need)

    def tie_search():
        def ibody(i, m):
            cand = m | (jnp.int32(1) << (IDX_BITS - 1 - i))
            return jnp.where(count(tie & (pos < cand)) <= need, cand, m)

        return lax.fori_loop(0, IDX_BITS, ibody, jnp.zeros(few.shape, jnp.int32))

    m = lax.cond(jnp.max(excess) > 0.0, tie_search, lambda: jnp.full(few.shape, 2 ** IDX_BITS - 1, jnp.int32))
    return adm & (gt | (tie & (pos < m)))


def _dsa_kernel(q_ref, qi_ref, kis_ref, k_ref, vt_ref, ki_ref, *refs, tq, s_len, s_valid, q_offset, j0,
                n_valid_q, topk):
    o_ref = refs[-1]
    j = pl.program_id(1)
    lane = lax.broadcasted_iota(jnp.int32, (1, tq), 1)
    qpos = q_offset + (j0 + j) * tq + lane
    lim = jnp.minimum((qpos // CHUNK + 1) * CHUNK, s_valid)
    row = lax.broadcasted_iota(jnp.int32, (s_len, tq), 0)
    adm = row < lim

    qi_t = qi_ref[0].T.astype(BF16)
    kis_t = kis_ref[0].T
    ki = ki_ref[0]
    assert IDX_DH == 64 and IDX_HEADS == 4 and ATT_DH == 64
    wi = kis_t[KIS_WI:KIS_WI + IDX_HEADS, :] * (IDX_DH ** -0.5 * IDX_HEADS ** -0.5)
    qi_all = jnp.concatenate([qi_t[h * IDX_DH:(h + 1) * IDX_DH, :] for h in range(IDX_HEADS)], axis=1)
    rel_all = _dot(ki, qi_all)
    score = jnp.zeros((s_len, tq), F32)
    for h in range(IDX_HEADS):
        score = score + jnp.maximum(rel_all[:, h * tq:(h + 1) * tq], 0.0) * wi[h:h + 1, :]
    sel = _select_topk(score, adm, row, lane >= n_valid_q, topk, 0)

    q_t = q_ref[0].T * (ATT_DH ** -0.5)
    pair = LANES // ATT_DH
    head_in_pair = lax.broadcasted_iota(jnp.int32, (LANES, tq), 0) // ATT_DH
    logits = []
    for g in range(ATT_HEADS // pair):
        qg = q_t[g * LANES:(g + 1) * LANES, :]
        qm = jnp.concatenate([jnp.where(head_in_pair == i, qg, 0.0) for i in range(pair)], axis=1)
        lg = _dot(k_ref[0, :, g * LANES:(g + 1) * LANES], qm.astype(BF16))
        logits += [lg[:, i * tq:(i + 1) * tq] for i in range(pair)]
    probs, scales = [], []
    for lg in logits:
        lg = jnp.where(sel, lg, -jnp.inf)
        p = jnp.exp(lg - _fold_rows(lg, jnp.max))
        scales.append(1.0 / _fold_rows(p, jnp.sum))
        probs.append(p.astype(BF16))
    outs = [_dot(vt_ref[0, h * ATT_DH:(h + 1) * ATT_DH, :], probs[h]) * scales[h] for h in range(ATT_HEADS)]
    o_ref[0] = jnp.concatenate(outs, axis=0).T


def _dsa(q, qi, kis, k_b, vt_b, ki_b, out_prev, *, tq, j0, nj, s_len, s_valid, q_offset, n_valid_q, topk):
    B, t_all, _ = q.shape
    assert s_len < 2 ** IDX_BITS and s_len <= k_b.shape[1]
    alias_in = () if out_prev is None else (out_prev,)
    return pl.pallas_call(
        functools.partial(_dsa_kernel, tq=tq, s_len=s_len, s_valid=s_valid, q_offset=q_offset, j0=j0,
                          n_valid_q=n_valid_q, topk=topk),
        grid=(B, nj),
        in_specs=[pl.BlockSpec((1, tq, ATT_W), lambda b, j: (b, j0 + j, 0)),
                  pl.BlockSpec((1, tq, IDX_HEADS * IDX_DH), lambda b, j: (b, j0 + j, 0)),
                  pl.BlockSpec((1, tq, LANES), lambda b, j: (b, j0 + j, 0)),
                  pl.BlockSpec((1, s_len, ATT_W), lambda b, j: (b, 0, 0)),
                  pl.BlockSpec((1, ATT_W, s_len), lambda b, j: (b, 0, 0)),
                  pl.BlockSpec((1, s_len, IDX_DH), lambda b, j: (b, 0, 0))]
        + [pl.BlockSpec(memory_space=pl.ANY)] * len(alias_in),
        out_specs=pl.BlockSpec((1, tq, ATT_W), lambda b, j: (b, j0 + j, 0)),
        out_shape=jax.ShapeDtypeStruct((B, t_all, ATT_W), F32),
        input_output_aliases={6: 0} if alias_in else {},
        compiler_params=_params(("arbitrary", "arbitrary")),
        name="dsa",
    )(q, qi, kis, k_b, vt_b, ki_b, *alias_in)


def _dsa_rows_kernel(q_ref, qi_ref, kis_ref, k_ref, v_ref, ck_ref, cv_ref, cki_ref, o_ref, *, nbq, tq, s_valid,
                     q_offset, topk):
    rows = nbq * tq
    s_len = q_offset + LANES
    qpos = q_offset + lax.broadcasted_iota(jnp.int32, (rows, 1), 0) % tq
    lim = jnp.minimum((qpos // CHUNK + 1) * CHUNK, s_valid)
    col = lax.broadcasted_iota(jnp.int32, (rows, s_len), 1)
    adm = col < lim

    def with_new(cache, new):
        return jnp.concatenate([cache, new.astype(BF16),
                                jnp.zeros((LANES - tq, new.shape[1]), BF16)], axis=0)

    scores = []
    for b in range(nbq):
        qi = qi_ref[b]
        qi_st = jnp.concatenate([qi[:, h * IDX_DH:(h + 1) * IDX_DH] for h in range(IDX_HEADS)], axis=0)
        rel = _dot_nt(qi_st.astype(BF16), with_new(cki_ref[b], kis_ref[b][:, :IDX_DH]))
        wi = kis_ref[b][:, KIS_WI:KIS_WI + IDX_HEADS] * (IDX_DH ** -0.5 * IDX_HEADS ** -0.5)
        sc = jnp.maximum(rel[:tq, :], 0.0) * wi[:, 0:1]
        for h in range(1, IDX_HEADS):
            sc = sc + jnp.maximum(rel[h * tq:(h + 1) * tq, :], 0.0) * wi[:, h:h + 1]
        scores.append(sc)
    sel = _select_topk(jnp.concatenate(scores, axis=0), adm, col, qpos < 0, topk, 1)

    head_of_lane = lax.broadcasted_iota(jnp.int32, (tq, ATT_W), 1) // ATT_DH
    for b in range(nbq):
        q = q_ref[b] * (ATT_DH ** -0.5)
        q_st = jnp.concatenate([jnp.where(head_of_lane == h, q, 0.0) for h in range(ATT_HEADS)], axis=0)
        logits = _dot_nt(q_st.astype(BF16), with_new(ck_ref[b], k_ref[b]))
        logits = jnp.where(jnp.concatenate([sel[b * tq:(b + 1) * tq, :]] * ATT_HEADS, axis=0), logits, -jnp.inf)
        p = jnp.exp(logits - jnp.max(logits, axis=1, keepdims=True))
        scale = 1.0 / jnp.sum(p, axis=1, keepdims=True)
        pv = _dot(p.astype(BF16), with_new(cv_ref[b], v_ref[b])) * scale
        out = jnp.zeros((tq, ATT_W), F32)
        for h in range(ATT_HEADS):
            out = jnp.where(head_of_lane == h, pv[h * tq:(h + 1) * tq, :], out)
        o_ref[b] = out


def _dsa_rows(q, qi, kis, k, v, ck, cv, cki, *, nbq, topk):
    B, T, _ = q.shape
    P = ck.shape[1]
    assert P + LANES < 2 ** IDX_BITS and T % (2 * SUBLANES) == 0 and T <= LANES and P % (2 * SUBLANES) == 0
    assert B % nbq == 0
    new_spec = lambda w: pl.BlockSpec((nbq, T, w), lambda b: (b, 0, 0))
    cache_spec = lambda w: pl.BlockSpec((nbq, P, w), lambda b: (b, 0, 0))
    return pl.pallas_call(
        functools.partial(_dsa_rows_kernel, nbq=nbq, tq=T, s_valid=P + T, q_offset=P, topk=topk),
        grid=(B // nbq,),
        in_specs=[new_spec(ATT_W), new_spec(IDX_HEADS * IDX_DH), new_spec(LANES), new_spec(ATT_W), new_spec(ATT_W),
                  cache_spec(ATT_W), cache_spec(ATT_W), cache_spec(IDX_DH)],
        out_specs=new_spec(ATT_W),
        out_shape=jax.ShapeDtypeStruct((B, T, ATT_W), F32),
        compiler_params=_params(("arbitrary",)),
        name="dsa_rows",
    )(q, qi, kis, k, v, ck, cv, cki)


def _dsa_group(q, qi, kis, k_b, vt_b, ki_b, *, s_valid, q_offset, topk):
    B, T, _ = q.shape
    tq = 2 * LANES if T % (2 * LANES) == 0 else LANES
    n_valid_q = min(T, tq)
    if T < tq:
        padq = lambda a: jnp.pad(a, ((0, 0), (0, tq - T), (0, 0)))
        q, qi, kis = padq(q), padq(qi), padq(kis)
    nq = q.shape[1] // tq
    s_total = k_b.shape[1]
    out = None
    for j0 in range(nq):
        reach = -(-(q_offset + (j0 + 1) * tq) // CHUNK) * CHUNK
        s_len = min(s_total, -(-min(reach, s_valid) // LANES) * LANES)
        out = _dsa(q, qi, kis, k_b, vt_b, ki_b, out, tq=tq, j0=j0, nj=1, s_len=s_len, s_valid=s_valid,
                   q_offset=q_offset, n_valid_q=n_valid_q, topk=topk)
    return out[:, :T]


def _ffn_kernel(x_ref, gdn_ref, att_ref, ssm_ref, mod_ref, wo_ref, g2_ref, wg_ref, wu_ref, cw_ref, cb_ref,
                wd_ref, prev_ref, fg_ref, *rest, nb, tt, nk, final):
    if final:
        xo_ref, y_ref, new_ref, h2_sc, acc_sc, buf_sc, carry_sc = rest
    else:
        xo_ref, new_ref, h2_sc, acc_sc, buf_sc, carry_sc = rest
        y_ref = None
    t = pl.program_id(1)
    k = pl.program_id(2)
    rows = nb * tt
    lo = SUBLANES - (FFN_CONV_W - 1)
    mod = mod_ref[...]

    @pl.when(k == 0)
    def _():
        mix = jnp.concatenate([gdn_ref[...], att_ref[...], ssm_ref[...]], axis=-1)
        proj = _dot(mix.reshape(rows, MIX_W).astype(BF16), wo_ref[...]).reshape(nb, tt, D_MODEL)
        x1 = x_ref[...] + mod[:, 2:3, :] * proj
        acc_sc[...] = x1
        ms = jnp.mean(x1 * x1, axis=-1, keepdims=True)
        h2 = x1 * lax.rsqrt(ms + EPS) * g2_ref[...]
        h2 = h2 * (1.0 + mod[:, 4:5, :]) + mod[:, 3:4, :]
        h2_sc[...] = h2.reshape(rows, D_MODEL).astype(BF16)

    @pl.when(t == 0)
    def _():
        carry_sc[k, :, lo:SUBLANES, :] = prev_ref[...]

    h2 = h2_sc[...]
    ag = _dot(h2, wg_ref[...])
    up = _dot(h2, wu_ref[...])
    tf = ag.shape[-1]
    buf_sc[:, lo:SUBLANES, :] = carry_sc[k, :, lo:SUBLANES, :]
    buf_sc[:, SUBLANES:SUBLANES + tt, :] = ag.reshape(nb, tt, tf)
    cw = cw_ref[...]
    conv = buf_sc[:, lo:lo + tt, :] * cw[0:1, :][None]
    for jj in range(1, FFN_CONV_W):
        conv = conv + buf_sc[:, lo + jj:lo + jj + tt, :] * cw[jj:jj + 1, :][None]
    tail = buf_sc[:, tt + lo:tt + SUBLANES, :]
    carry_sc[k, :, lo:SUBLANES, :] = tail
    new_ref[:, pl.ds(k, 1), :, :] = tail[:, None]
    act = _silu(conv + cb_ref[...][None]).reshape(rows, tf) * up
    y = _dot(act.astype(BF16), wd_ref[...]).reshape(nb, tt, D_MODEL)
    acc_sc[...] += mod[:, 5:6, :] * y

    @pl.when(k == nk - 1)
    def _():
        xo = acc_sc[...]
        xo_ref[...] = xo
        if final:
            ms = jnp.mean(xo * xo, axis=-1, keepdims=True)
            y_ref[...] = xo * lax.rsqrt(ms + EPS) * fg_ref[...]


def _ffn(x, gdn_o, att_o, ssm_o, mod, w_out, g2, w_gate, w_up, conv_w, conv_b, w_down, conv_prev, final_g,
         nb, tt, tf, final):
    B, T, _ = x.shape
    nk = D_FF // tf
    grid = (B // nb, T // tt, nk)
    weights_mode = pl.Buffered(1) if nk == 1 else None
    row_spec = lambda w: pl.BlockSpec((nb, tt, w), lambda b, t, k: (b, t, 0))
    in_specs = [row_spec(D_MODEL), row_spec(GDN_W), row_spec(ATT_W), row_spec(SSM_W),
                pl.BlockSpec((nb, 6, D_MODEL), lambda b, t, k: (b, 0, 0)),
                pl.BlockSpec((MIX_W, D_MODEL), lambda b, t, k: (0, 0), pipeline_mode=weights_mode),
                pl.BlockSpec((1, D_MODEL), lambda b, t, k: (0, 0)),
                pl.BlockSpec((D_MODEL, tf), lambda b, t, k: (0, k), pipeline_mode=weights_mode),
                pl.BlockSpec((D_MODEL, tf), lambda b, t, k: (0, k), pipeline_mode=weights_mode),
                pl.BlockSpec((FFN_CONV_W, tf), lambda b, t, k: (0, k)),
                pl.BlockSpec((1, tf), lambda b, t, k: (0, k)),
                pl.BlockSpec((tf, D_MODEL), lambda b, t, k: (k, 0), pipeline_mode=weights_mode),
                pl.BlockSpec((nb, FFN_CONV_W - 1, tf), lambda b, t, k: (b, 0, k)),
                pl.BlockSpec((1, D_MODEL), lambda b, t, k: (0, 0))]
    out_shape = [jax.ShapeDtypeStruct((B, T, D_MODEL), F32)]
    out_specs = [row_spec(D_MODEL)]
    if final:
        out_shape.append(jax.ShapeDtypeStruct((B, T, D_MODEL), F32))
        out_specs.append(row_spec(D_MODEL))
    out_shape.append(jax.ShapeDtypeStruct((B, nk, FFN_CONV_W - 1, tf), F32))
    out_specs.append(pl.BlockSpec((nb, nk, FFN_CONV_W - 1, tf), lambda b, t, k: (b, 0, 0, 0)))
    res = pl.pallas_call(
        functools.partial(_ffn_kernel, nb=nb, tt=tt, nk=nk, final=final),
        grid=grid,
        in_specs=in_specs,
        out_specs=tuple(out_specs),
        out_shape=tuple(out_shape),
        scratch_shapes=[pltpu.VMEM((nb * tt, D_MODEL), BF16),
                        pltpu.VMEM((nb, tt, D_MODEL), F32),
                        pltpu.VMEM((nb, SUBLANES + tt, tf), F32),
                        pltpu.VMEM((nk, nb, SUBLANES, tf), F32)],
        compiler_params=_params(("arbitrary", "arbitrary", "arbitrary")),
        name="ffn",
    )(x, gdn_o, att_o, ssm_o, mod, w_out, g2, w_gate, w_up, conv_w, conv_b.reshape(1, -1), w_down,
      conv_prev, final_g)
    tail = res[-1].transpose(0, 2, 1, 3).reshape(B, FFN_CONV_W - 1, D_FF)
    return (*res[:-1], tail)


def _rope_tables(pos):
    half = ATT_DH // 2
    inv_freq = ROPE_THETA ** (-jnp.arange(half, dtype=F32) / half)
    ang = pos.astype(F32)[:, None] * inv_freq[None, :]
    cos = jnp.cos(ang)
    sin = jnp.sin(ang)
    cos_h = jnp.concatenate([cos, cos], axis=-1)
    sin_h = jnp.concatenate([-sin, sin], axis=-1)
    ones = jnp.ones((pos.shape[0], LANES - ATT_DH), F32)
    cos_t = jnp.concatenate([jnp.tile(cos_h, (1, ATT_HEADS)), cos_h, ones], axis=-1)
    sin_t = jnp.concatenate([jnp.tile(sin_h, (1, ATT_HEADS)), sin_h, 0.0 * ones], axis=-1)
    return cos_t, sin_t


def _layer(x, mod, pos_tables, cache, states, lw, tiles, layer, stacks, final_g):
    final = layer == DEPTH - 1
    (norm1_g, w_perm, gdn_conv_w, gdn_A_log, gdn_dt_bias, gdn_norm_g, ssm_conv_w, ssm_conv_b, ssm_A_log,
     ssm_dt_bias, ssm_D, ssm_norm_g, w_out, norm2_g, w_gate, w_up, ffn_conv_w, ffn_conv_b, w_down) = lw
    gdn_conv_prev, gdn_s0, ssm_conv_prev, ssm_h0, ffn_conv_prev = states
    B, T, _ = x.shape
    nb, tt, nbb, ffn_tt, tf = tiles
    cos_t, sin_t = pos_tables
    proj = _inproj(x, mod, norm1_g.reshape(1, -1), w_perm, cos_t, sin_t, nb, tt, cache is None, layer, stacks)
    gqkv, ggate, q, k_stack, v_stack, qi, kis, z, xbc = proj[:len(OUT_SEGS)]
    stacks = (k_stack, v_stack, proj[len(OUT_SEGS)])

    gdn_o, gdn_conv_new, gdn_s = _gdn(gqkv, kis, ggate, gdn_conv_w, gdn_conv_prev, gdn_s0, gdn_A_log,
                                      gdn_dt_bias, gdn_norm_g, nbb)
    ssm_o, ssm_conv_new, ssm_h = _ssd(xbc, z, kis, ssm_conv_w, ssm_conv_b, ssm_conv_prev, ssm_h0, ssm_A_log,
                                      ssm_dt_bias, ssm_D, ssm_norm_g, nbb)
    if cache is None:
        k_b, ki_b, vt_b = proj[len(OUT_SEGS) + 1:]
        att_o = _dsa_group(q, qi, kis, k_b, vt_b, ki_b, s_valid=T, q_offset=0, topk=min(TOPK_MAX, T // 4))
    else:
        ck, cv, cki = cache
        P = ck.shape[1]
        att_o = _dsa_rows(q, qi, kis, k_stack[layer], v_stack[layer], ck.reshape(B, P, ATT_W).astype(BF16),
                          cv.reshape(B, P, ATT_W).astype(BF16), cki.astype(BF16), nbq=nbb,
                          topk=min(TOPK_MAX, (P + T) // 4))

    res = _ffn(x, gdn_o, att_o, ssm_o, mod, w_out, norm2_g.reshape(1, -1), w_gate, w_up, ffn_conv_w, ffn_conv_b,
               w_down, ffn_conv_prev, final_g.reshape(1, -1), ffn_tt[0], ffn_tt[1], tf, final)
    if final:
        x_new, y, ffn_conv_new = res
    else:
        (x_new, ffn_conv_new), y = res, None
    return x_new, y, (gdn_conv_new, gdn_s, ssm_conv_new, ssm_h, ffn_conv_new), stacks


def kernel(x_prompt, x_sample, c_prompt, c_sample, cache_k, cache_v, cache_kidx, state_gdn_conv, state_gdn,
           state_ssm_conv, state_ssm, state_ffn_conv, w_ada, b_ada, norm1_g, w_in, gdn_conv_w, gdn_A_log,
           gdn_dt_bias, gdn_norm_g, ssm_conv_w, ssm_conv_b, ssm_A_log, ssm_dt_bias, ssm_D, ssm_norm_g, w_out,
           norm2_g, w_gate, w_up, ffn_conv_w, ffn_conv_b, w_down, final_g):
    Bp, T, _ = x_prompt.shape
    Bs, Ts, _ = x_sample.shape
    P = cache_k.shape[2]

    c_all = jnp.concatenate([c_prompt, c_sample], axis=0)
    mod_all = _ada(c_all, w_ada, b_ada).reshape(DEPTH, Bp + Bs, 6, D_MODEL)

    tables_p = _rope_tables(jnp.arange(T))
    tables_s = _rope_tables(P + jnp.arange(Ts))

    w_perm = _permute_columns(w_in)
    w_out_b = w_out.astype(BF16)
    w_gate_b = w_gate.astype(BF16)
    w_up_b = w_up.astype(BF16)
    w_down_b = w_down.astype(BF16)

    zeros_p = (jnp.zeros((Bp, CONV_W - 1, GDN_CONV_C), F32),
               jnp.zeros((Bp, GDN_HEADS, GDN_DK, GDN_DV), F32),
               jnp.zeros((Bp, CONV_W - 1, SSM_CONV_C), F32),
               jnp.zeros((Bp, SSM_HEADS, SSM_P, SSM_N), F32),
               jnp.zeros((Bp, FFN_CONV_W - 1, D_FF), F32))

    tf = D_FF
    tiles_p = (1, min(256, T), math.gcd(Bp, 8), (1, min(256, T)), tf)
    tiles_s = (Bs, Ts, math.gcd(Bs, 4), (Bs, Ts), tf)

    xp, xs = x_prompt, x_sample
    new_p, new_s = [], []
    yp = ys = stacks_p = stacks_s = None
    for l in range(DEPTH):
        lw = (norm1_g[l], w_perm[l], gdn_conv_w[l], gdn_A_log[l], gdn_dt_bias[l], gdn_norm_g[l], ssm_conv_w[l],
              ssm_conv_b[l], ssm_A_log[l], ssm_dt_bias[l], ssm_D[l], ssm_norm_g[l], w_out_b[l], norm2_g[l],
              w_gate_b[l], w_up_b[l], ffn_conv_w[l], ffn_conv_b[l], w_down_b[l])
        xp, yp, st_p, stacks_p = _layer(xp, mod_all[l, :Bp], tables_p, None, zeros_p, lw, tiles_p, l, stacks_p,
                                        final_g)
        states_s = (state_gdn_conv[l], state_gdn[l], state_ssm_conv[l], state_ssm[l], state_ffn_conv[l])
        xs, ys, st_s, stacks_s = _layer(xs, mod_all[l, Bp:], tables_s, (cache_k[l], cache_v[l], cache_kidx[l]),
                                        states_s, lw, tiles_s, l, stacks_s, final_g)
        new_p.append(st_p)
        new_s.append(st_s)

    def outputs(stacks, per_layer):
        k_stack, v_stack, ki_stack = stacks
        heads = k_stack.shape[:3] + (ATT_HEADS, ATT_DH)
        return [k_stack.reshape(heads), v_stack.reshape(heads), ki_stack] + [
            jnp.stack([st[i] for st in per_layer]) for i in range(len(per_layer[0]))]

    return (yp, ys, *outputs(stacks_p, new_p), *outputs(stacks_s, new_s))
```

```python
import functools
import math

import jax
import jax.numpy as jnp
import numpy as np
from jax import lax
from jax.experimental import pallas as pl
from jax.experimental.pallas import tpu as pltpu

F32 = jnp.float32
BF16 = jnp.bfloat16
HI = lax.Precision.HIGHEST

D_MODEL = 1024
DEPTH = 2
CHUNK = 64
CONV_W = 4
FFN_CONV_W = 3
D_FF = 2816
ROPE_THETA = 10000.0
EPS = 1e-6
GDN_HEADS = 4
GDN_DK = 64
GDN_DV = 64
ATT_HEADS = 4
ATT_DH = 64
IDX_HEADS = 4
IDX_DH = 64
TOPK_MAX = 256
SSM_HEADS = 8
SSM_P = 64
SSM_GROUPS = 2
SSM_N = 128
GDN_W = GDN_HEADS * GDN_DV
ATT_W = ATT_HEADS * ATT_DH
SSM_W = SSM_HEADS * SSM_P
MIX_W = GDN_W + ATT_W + SSM_W
GDN_CONV_C = 2 * GDN_HEADS * GDN_DK + GDN_W
SSM_CONV_C = SSM_W + 2 * SSM_GROUPS * SSM_N
IN_SIZES = (GDN_CONV_C, GDN_HEADS, GDN_HEADS, GDN_W,
            ATT_W, ATT_W, ATT_W, IDX_HEADS * IDX_DH, IDX_DH, IDX_HEADS,
            SSM_W, SSM_CONV_C, SSM_HEADS)
IN_W = sum(IN_SIZES)

LANES = 128
SUBLANES = 8
VMEM_LIMIT = 56 * 1024 * 1024

KIS_A = IDX_DH
KIS_B = KIS_A + GDN_HEADS
KIS_WI = KIS_B + GDN_HEADS
KIS_DT = KIS_WI + IDX_HEADS
KIS_END = KIS_DT + SSM_HEADS
OUT_SEGS = (("gqkv", GDN_CONV_C), ("ggate", GDN_W), ("q", ATT_W), ("k", ATT_W), ("v", ATT_W),
            ("qi", IDX_HEADS * IDX_DH), ("kis", LANES), ("z", SSM_W), ("xbc", SSM_CONV_C))
PERM_W = sum(w for _, w in OUT_SEGS)
ROT_W = ATT_W + LANES


def _perm_columns():
    starts = np.concatenate([[0], np.cumsum(IN_SIZES)])
    (s_gqkv, s_ga, s_gb, s_gg, s_q, s_k, s_v, s_qi, s_ki, s_wi, s_z, s_xbc, s_dt) = starts[:-1]
    cols = []
    cols += list(range(s_gqkv, s_gqkv + GDN_CONV_C))
    cols += list(range(s_gg, s_gg + GDN_W))
    cols += list(range(s_q, s_q + ATT_W))
    cols += list(range(s_k, s_k + ATT_W))
    cols += list(range(s_v, s_v + ATT_W))
    cols += list(range(s_qi, s_qi + IDX_HEADS * IDX_DH))
    kis = (list(range(s_ki, s_ki + IDX_DH)) + list(range(s_ga, s_ga + GDN_HEADS))
           + list(range(s_gb, s_gb + GDN_HEADS)) + list(range(s_wi, s_wi + IDX_HEADS))
           + list(range(s_dt, s_dt + SSM_HEADS)))
    cols += kis + [-1] * (LANES - len(kis))
    cols += list(range(s_z, s_z + SSM_W))
    cols += list(range(s_xbc, s_xbc + SSM_CONV_C))
    assert len(cols) == PERM_W
    return np.asarray(cols, np.int32)


_PERM_COLS = _perm_columns()


def _perm_runs():
    runs = []
    start = 0
    for i in range(1, PERM_W + 1):
        prev = int(_PERM_COLS[i - 1])
        if i == PERM_W or not ((prev < 0 and _PERM_COLS[i] < 0) or (prev >= 0 and _PERM_COLS[i] == prev + 1)):
            runs.append((start, int(_PERM_COLS[start]), i - start))
            start = i
    return runs


_PERM_RUNS = _perm_runs()
WPREP_ROWS = 256


def _wprep_kernel(w_ref, o_ref):
    w = w_ref[0]
    for dst, src, n in _PERM_RUNS:
        if src < 0:
            o_ref[0, :, dst:dst + n] = jnp.zeros((w.shape[0], n), BF16)
        else:
            o_ref[0, :, dst:dst + n] = w[:, src:src + n].astype(BF16)


def _permute_columns(w_in):
    depth, rows, _ = w_in.shape
    return pl.pallas_call(
        _wprep_kernel,
        grid=(depth, rows // WPREP_ROWS),
        in_specs=[pl.BlockSpec((1, WPREP_ROWS, IN_W), lambda l, r: (l, r, 0))],
        out_specs=pl.BlockSpec((1, WPREP_ROWS, PERM_W), lambda l, r: (l, r, 0)),
        out_shape=jax.ShapeDtypeStruct((depth, rows, PERM_W), BF16),
        compiler_params=pltpu.CompilerParams(dimension_semantics=("arbitrary", "arbitrary"),
                                             vmem_limit_bytes=VMEM_LIMIT),
        name="wprep",
    )(w_in)


def _silu(x):
    return x * jax.nn.sigmoid(x)


def _softplus(x):
    return jnp.maximum(x, 0.0) + jnp.log1p(jnp.exp(-jnp.abs(x)))


def _dot(a, b, precision=None):
    return jnp.dot(a, b, preferred_element_type=F32, precision=precision)


def _dot_nt(a, b, precision=None):
    return lax.dot_general(a, b, (((1,), (1,)), ((), ())), preferred_element_type=F32,
                           precision=precision)


def _dot_tn(a, b, precision=None):
    return lax.dot_general(a, b, (((0,), (0,)), ((), ())), preferred_element_type=F32,
                           precision=precision)


def _cumsum_rows(x):
    c = x.shape[0]
    row = lax.broadcasted_iota(jnp.int32, x.shape, 0)
    s = 1
    while s < c:
        x = x + jnp.where(row >= s, pltpu.roll(x, s, axis=0), 0.0)
        s *= 2
    return x


def _mm(a, b):
    return _dot(a.astype(BF16), b.astype(BF16))


def _mm_nt(a, b):
    return _dot_nt(a.astype(BF16), b.astype(BF16))


def _split(a):
    hi = a.astype(BF16)
    return hi, (a - hi.astype(F32)).astype(BF16)


def _mm3(a, b):
    (ah, al), (bh, bl) = a, b
    return _dot(ah, bh) + (_dot(ah, bl) + _dot(al, bh))


def _lane_spread(vals, lane0, n_heads):
    width = n_heads * GDN_DV
    r = lax.broadcasted_iota(jnp.int32, (LANES, width), 0)
    cidx = lax.broadcasted_iota(jnp.int32, (LANES, width), 1)
    pick = (r == lane0 + cidx // GDN_DV).astype(BF16)
    p1 = vals.astype(BF16)
    r1 = vals - p1.astype(F32)
    p2 = r1.astype(BF16)
    p3 = (r1 - p2.astype(F32)).astype(BF16)
    return _dot(p1, pick) + (_dot(p2, pick) + _dot(p3, pick))


def _head_sums(x):
    r = lax.broadcasted_iota(jnp.int32, (LANES, LANES), 0) // GDN_DV
    cidx = lax.broadcasted_iota(jnp.int32, (LANES, LANES), 1) // GDN_DV
    ones = (r == cidx).astype(BF16)
    hi, lo = _split(x)
    cols = []
    for g in range(x.shape[1] // LANES):
        sl = slice(g * LANES, (g + 1) * LANES)
        cols.append(_dot(hi[:, sl], ones) + _dot(lo[:, sl], ones))
    return jnp.concatenate(cols, axis=1)


def _transpose_rows(vals):
    c = vals.shape[0]
    if c < LANES:
        vals = jnp.concatenate([vals, jnp.zeros((LANES - c, LANES), vals.dtype)], axis=0)
    return vals.T[:, :c]


def _lane_vector(vals, start):
    return jnp.zeros((1, LANES), F32).at[0, start:start + vals.shape[0]].set(vals.astype(F32))


def _params(sem):
    return pltpu.CompilerParams(dimension_semantics=sem, vmem_limit_bytes=VMEM_LIMIT)


def _ada_kernel(c_ref, w_ref, b_ref, o_ref):
    s = _silu(c_ref[...])
    o_ref[0] = _dot(s.astype(BF16), w_ref[0].astype(BF16)) + b_ref[0]


def _ada(c_all, w_ada, b_ada):
    rows = c_all.shape[0]
    n = w_ada.shape[2]
    tn = 1536
    return pl.pallas_call(
        _ada_kernel,
        grid=(DEPTH, n // tn),
        in_specs=[pl.BlockSpec((rows, D_MODEL), lambda l, j: (0, 0)),
                  pl.BlockSpec((1, D_MODEL, tn), lambda l, j: (l, 0, j)),
                  pl.BlockSpec((1, 1, tn), lambda l, j: (l, 0, j))],
        out_specs=pl.BlockSpec((1, rows, tn), lambda l, j: (l, 0, j)),
        out_shape=jax.ShapeDtypeStruct((DEPTH, rows, n), F32),
        compiler_params=_params(("arbitrary", "arbitrary")),
        name="ada",
    )(c_all, w_ada, b_ada.reshape(DEPTH, 1, n))


def _rotate(x, cos, sin_signed):
    w = x.shape[-1]
    lane = lax.broadcasted_iota(jnp.int32, x.shape, x.ndim - 1)
    first = (lane % ATT_DH) < (ATT_DH // 2)
    swapped = jnp.where(first, pltpu.roll(x, w - ATT_DH // 2, axis=x.ndim - 1),
                        pltpu.roll(x, ATT_DH // 2, axis=x.ndim - 1))
    return x * cos + swapped * sin_signed


def _inproj_kernel(x_ref, mod_ref, g_ref, w_ref, cos_ref, sin_ref, *refs, nb, tt, emit_keys, n_alias):
    out_refs = refs[n_alias:]
    x = x_ref[...]
    ms = jnp.mean(x * x, axis=-1, keepdims=True)
    xn = x * lax.rsqrt(ms + EPS) * g_ref[...]
    mod = mod_ref[...]
    h = xn * (1.0 + mod[:, 1:2, :]) + mod[:, 0:1, :]
    u = _dot(h.reshape(nb * tt, D_MODEL).astype(BF16), w_ref[...])
    cos = cos_ref[...]
    sin = sin_ref[...]
    off = 0
    segs = {}
    for (name, width), o_ref in zip(OUT_SEGS, out_refs):
        seg = u[:, off:off + width]
        if name in ("q", "k", "qi", "kis"):
            t0 = ATT_W if name == "kis" else 0
            c3 = cos[:, t0:t0 + width][None]
            s3 = sin[:, t0:t0 + width][None]
            if nb > 1:
                c3 = jnp.broadcast_to(c3, (nb, tt, width)).reshape(nb * tt, width)
                s3 = jnp.broadcast_to(s3, (nb, tt, width)).reshape(nb * tt, width)
            else:
                c3 = c3[0]
                s3 = s3[0]
            seg = _rotate(seg, c3, s3)
        o_ref[...] = seg.reshape(o_ref.shape)
        segs[name] = seg
        off += width
    ki_ref = out_refs[len(OUT_SEGS)]
    ki_ref[...] = segs["kis"][:, :IDX_DH].reshape(ki_ref.shape)
    if emit_keys:
        k_b_ref, ki_b_ref, vt_b_ref = out_refs[len(OUT_SEGS) + 1:]
        k_b_ref[0] = segs["k"].astype(BF16)
        ki_b_ref[0] = segs["kis"][:, :IDX_DH].astype(BF16)
        vt_b_ref[0] = segs["v"].T.astype(BF16)


STACKED = ("k", "v", "ki")


def _inproj(x, mod, g1, w_perm, cos, sin, nb, tt, emit_keys, layer, stacks):
    B, T, _ = x.shape
    grid = (B // nb, T // tt)
    names = [n for n, _ in OUT_SEGS] + ["ki"]
    widths = [w for _, w in OUT_SEGS] + [IDX_DH]
    out_shape, out_specs = (), ()
    for n, w in zip(names, widths):
        if n in STACKED:
            out_shape += (jax.ShapeDtypeStruct((DEPTH, B, T, w), F32),)
            out_specs += (pl.BlockSpec((1, nb, tt, w), lambda b, t: (layer, b, t, 0)),)
        else:
            out_shape += (jax.ShapeDtypeStruct((B, T, w), F32),)
            out_specs += (pl.BlockSpec((nb, tt, w), lambda b, t: (b, t, 0)),)
    alias_in = () if stacks is None else tuple(stacks)
    n_in = 6
    aliases = {n_in + i: names.index(n) for i, n in enumerate(STACKED)} if alias_in else {}
    if emit_keys:
        assert nb == 1
        out_shape += (jax.ShapeDtypeStruct((B, T, ATT_W), BF16), jax.ShapeDtypeStruct((B, T, IDX_DH), BF16),
                      jax.ShapeDtypeStruct((B, ATT_W, T), BF16))
        out_specs += (pl.BlockSpec((1, tt, ATT_W), lambda b, t: (b, t, 0)),
                      pl.BlockSpec((1, tt, IDX_DH), lambda b, t: (b, t, 0)),
                      pl.BlockSpec((1, ATT_W, tt), lambda b, t: (b, 0, t)))
    return pl.pallas_call(
        functools.partial(_inproj_kernel, nb=nb, tt=tt, emit_keys=emit_keys, n_alias=len(alias_in)),
        grid=grid,
        in_specs=[pl.BlockSpec((nb, tt, D_MODEL), lambda b, t: (b, t, 0)),
                  pl.BlockSpec((nb, 6, D_MODEL), lambda b, t: (b, 0, 0)),
                  pl.BlockSpec((1, D_MODEL), lambda b, t: (0, 0)),
                  pl.BlockSpec((D_MODEL, PERM_W), lambda b, t: (0, 0)),
                  pl.BlockSpec((tt, ROT_W), lambda b, t: (t, 0)),
                  pl.BlockSpec((tt, ROT_W), lambda b, t: (t, 0))]
        + [pl.BlockSpec(memory_space=pl.ANY)] * len(alias_in),
        out_specs=out_specs,
        out_shape=out_shape,
        input_output_aliases=aliases,
        compiler_params=_params(("arbitrary", "arbitrary")),
        name="inproj",
    )(x, mod, g1, w_perm, cos, sin, *alias_in)


def _short_conv(ext_ref, bi, u, w, c):
    lo = SUBLANES - (CONV_W - 1)
    ext_ref[bi, SUBLANES:SUBLANES + c, :] = u
    y = ext_ref[bi, lo:lo + c, :] * w[0:1, :]
    for j in range(1, CONV_W):
        y = y + ext_ref[bi, lo + j:lo + j + c, :] * w[j:j + 1, :]
    tail = ext_ref[bi, c + lo:c + SUBLANES, :]
    ext_ref[bi, lo:SUBLANES, :] = tail
    return y, tail


def _neumann_inverses(mats, c):
    eye = (lax.broadcasted_iota(jnp.int32, (c, c), 0)
           == lax.broadcasted_iota(jnp.int32, (c, c), 1)).astype(F32)
    ps = [eye - a for a in mats]
    sp = [_split(a) for a in mats]
    pws = [_mm3(s, s) for s in sp]
    n = 2
    while n < c:
        sp = [_split(pw) for pw in pws]
        ps = [p + _mm3(_split(p), s) for p, s in zip(ps, sp)]
        n *= 2
        if n < c:
            pws = [_mm3(s, s) for s in sp]
    return ps


def _gdn_kernel(qkv_ref, kis_ref, gate_ref, w_ref, prev_ref, s0_ref, alog_ref, dtb_ref, ng_ref,
                o_ref, new_ref, s_ref, ext_ref, s_sc, *, c, nbb):
    j = pl.program_id(1)
    lo = SUBLANES - (CONV_W - 1)

    @pl.when(j == 0)
    def _():
        s_sc[...] = s0_ref[...]
        ext_ref[:, lo:SUBLANES, :] = prev_ref[...]

    ng = ng_ref[...]
    w_conv = w_ref[...]
    ri = lax.broadcasted_iota(jnp.int32, (c, c), 0)
    ci = lax.broadcasted_iota(jnp.int32, (c, c), 1)
    incl = ri >= ci
    strict = ri > ci
    nkd = GDN_HEADS * GDN_DK
    ch = []
    for bi in range(nbb):
        y, tail = _short_conv(ext_ref, bi, qkv_ref[bi], w_conv, c)
        new_ref[bi] = tail
        y = _silu(y)
        small = kis_ref[bi]
        g = -jnp.exp(alog_ref[...]) * _softplus(small + dtb_ref[...])
        beta = jax.nn.sigmoid(small)
        gc = _cumsum_rows(g)
        gc_t = _transpose_rows(gc)
        qk_n = y[:, :2 * nkd]
        qk_n = qk_n * lax.rsqrt(_head_sums(qk_n * qk_n) + EPS)
        for h in range(GDN_HEADS):
            qh = qk_n[:, h * GDN_DK:(h + 1) * GDN_DK] * (GDN_DK ** -0.5)
            kh = qk_n[:, nkd + h * GDN_DK:nkd + (h + 1) * GDN_DK]
            vh = y[:, 2 * nkd + h * GDN_DV:2 * nkd + (h + 1) * GDN_DV]
            ch.append((qh, kh, vh, gc[:, KIS_A + h:KIS_A + h + 1], gc_t[KIS_A + h:KIS_A + h + 1, :],
                       beta[:, KIS_B + h:KIS_B + h + 1]))
    ids = [(bi, h) for bi in range(nbb) for h in range(GDN_HEADS)]
    decay = [jnp.where(incl, jnp.exp(jnp.where(incl, gcol - grow, 0.0)), 0.0)
             for (_, _, _, gcol, grow, _) in ch]
    kb = [kh * bcol for (_, kh, _, _, _, bcol) in ch]
    eg = [jnp.exp(gcol) for (_, _, _, gcol, _, _) in ch]
    grams = [_mm_nt(jnp.concatenate([kbi, x[0]], axis=0), x[1]) for kbi, x in zip(kb, ch)]
    t_inv = _neumann_inverses([jnp.where(strict, m[:c] * d, 0.0) for m, d in zip(grams, decay)], c)
    attn = [jnp.where(incl, m[c:] * d, 0.0) for m, d in zip(grams, decay)]
    rhs = [jnp.concatenate([x[2] * x[5], kbi * e], axis=1) for x, kbi, e in zip(ch, kb, eg)]
    sol = [_mm(t, r) for t, r in zip(t_inv, rhs)]
    st = [s_sc[bi, h] for bi, h in ids]
    ws = [_mm(sl[:, GDN_DV:], s) for sl, s in zip(sol, st)]
    qs = [_mm(x[0] * e, s) for x, e, s in zip(ch, eg, st)]
    v_new = [sl[:, :GDN_DV] - w for sl, w in zip(sol, ws)]
    av = [_mm(a, v) for a, v in zip(attn, v_new)]
    g_last = [x[3][c - 1:c, :] for x in ch]
    kv = [_dot_tn(x[1] * jnp.exp(gl - x[3]), v) for x, gl, v in zip(ch, g_last, v_new)]
    for (bi, h), s, gl, kvi in zip(ids, st, g_last, kv):
        s_new = s * jnp.exp(gl) + kvi
        s_sc[bi, h] = s_new
        s_ref[bi, h] = s_new
    for bi in range(nbb):
        o = jnp.concatenate([qs[bi * GDN_HEADS + h] + av[bi * GDN_HEADS + h] for h in range(GDN_HEADS)], axis=1)
        o = o * lax.rsqrt(_head_sums(o * o) * (1.0 / GDN_DV) + EPS) * ng
        o_ref[bi] = o * _silu(gate_ref[bi])


def _gdn(gqkv, kis, ggate, conv_w, conv_prev, s0, a_log, dt_bias, norm_g, nbb):
    B, T, _ = gqkv.shape
    c = math.gcd(T, CHUNK)
    return pl.pallas_call(
        functools.partial(_gdn_kernel, c=c, nbb=nbb),
        grid=(B // nbb, T // c),
        in_specs=[pl.BlockSpec((nbb, c, GDN_CONV_C), lambda b, j: (b, j, 0)),
                  pl.BlockSpec((nbb, c, LANES), lambda b, j: (b, j, 0)),
                  pl.BlockSpec((nbb, c, GDN_W), lambda b, j: (b, j, 0)),
                  pl.BlockSpec((CONV_W, GDN_CONV_C), lambda b, j: (0, 0)),
                  pl.BlockSpec((nbb, CONV_W - 1, GDN_CONV_C), lambda b, j: (b, 0, 0)),
                  pl.BlockSpec((nbb, GDN_HEADS, GDN_DK, GDN_DV), lambda b, j: (b, 0, 0, 0)),
                  pl.BlockSpec((1, LANES), lambda b, j: (0, 0)),
                  pl.BlockSpec((1, LANES), lambda b, j: (0, 0)),
                  pl.BlockSpec((1, GDN_W), lambda b, j: (0, 0))],
        out_specs=(pl.BlockSpec((nbb, c, GDN_W), lambda b, j: (b, j, 0)),
                   pl.BlockSpec((nbb, CONV_W - 1, GDN_CONV_C), lambda b, j: (b, 0, 0)),
                   pl.BlockSpec((nbb, GDN_HEADS, GDN_DK, GDN_DV), lambda b, j: (b, 0, 0, 0))),
        out_shape=(jax.ShapeDtypeStruct((B, T, GDN_W), F32),
                   jax.ShapeDtypeStruct((B, CONV_W - 1, GDN_CONV_C), F32),
                   jax.ShapeDtypeStruct((B, GDN_HEADS, GDN_DK, GDN_DV), F32)),
        scratch_shapes=[pltpu.VMEM((nbb, SUBLANES + c, GDN_CONV_C), F32),
                        pltpu.VMEM((nbb, GDN_HEADS, GDN_DK, GDN_DV), F32)],
        compiler_params=_params(("arbitrary", "arbitrary")),
        name="gdn",
    )(gqkv, kis, ggate, conv_w, conv_prev, s0, _lane_vector(a_log, KIS_A), _lane_vector(dt_bias, KIS_A),
      jnp.tile(norm_g, GDN_HEADS).reshape(1, -1))


def _ssd_kernel(xbc_ref, z_ref, kis_ref, w_ref, cb_ref, prev_ref, h0_ref, alog_ref, dtb_ref, dsk_ref,
                ng_ref, o_ref, new_ref, h_ref, ext_ref, h_sc, *, c, nbb):
    j = pl.program_id(1)
    lo = SUBLANES - (CONV_W - 1)

    @pl.when(j == 0)
    def _():
        h_sc[...] = h0_ref[...]
        ext_ref[:, lo:SUBLANES, :] = prev_ref[...]

    dsk = dsk_ref[...]
    ng = ng_ref[...]
    w_conv = w_ref[...]
    ri = lax.broadcasted_iota(jnp.int32, (c, c), 0)
    ci = lax.broadcasted_iota(jnp.int32, (c, c), 1)
    incl = ri >= ci
    gn = SSM_GROUPS * SSM_N
    rep = SSM_HEADS // SSM_GROUPS
    gw = SSM_W // SSM_GROUPS
    grp_in = []
    ch = []
    slot = []
    ids = []
    for bi in range(nbb):
        y, tail = _short_conv(ext_ref, bi, xbc_ref[bi], w_conv, c)
        new_ref[bi] = tail
        y = _silu(y + cb_ref[...])
        dts = _softplus(kis_ref[bi] + dtb_ref[...])
        a = dts * (-jnp.exp(alog_ref[...]))
        acum = _cumsum_rows(a)
        acum_t = _transpose_rows(acum)
        dtm = _lane_spread(dts, KIS_DT, SSM_HEADS)
        am = _lane_spread(acum, KIS_DT, SSM_HEADS)
        a_last = am[c - 1:c, :]
        xs = y[:, :SSM_W]
        xdt = xs * dtm
        xw = xs * (jnp.exp(a_last - am) * dtm)
        h_dec = jnp.exp(a_last)
        slot.append((xs, jnp.exp(am), _silu(z_ref[bi])))
        for grp in range(SSM_GROUPS):
            grp_in.append((y[:, SSM_W + grp * SSM_N:SSM_W + (grp + 1) * SSM_N],
                           y[:, SSM_W + gn + grp * SSM_N:SSM_W + gn + (grp + 1) * SSM_N]))
            for hh in range(rep):
                h = grp * rep + hh
                hsl = slice(h * SSM_P, (h + 1) * SSM_P)
                ch.append((xdt[:, hsl], xw[:, hsl], am[:, h * SSM_P:h * SSM_P + c], acum_t[KIS_DT + h:KIS_DT + h + 1, :],
                           jnp.concatenate([h_dec[:, hsl]] * (SSM_N // SSM_P), axis=1), len(grp_in) - 1))
                ids.append((bi, h))
    cb = [_mm_nt(cg, bg) for bg, cg in grp_in]
    seg = [jnp.where(incl, jnp.exp(jnp.where(incl, x[2] - x[3], 0.0)), 0.0) for x in ch]
    hs = [h_sc[bi, h] for bi, h in ids]
    y_in = [_mm(cb[x[5]] * sg, x[0]) for x, sg in zip(ch, seg)]
    y_st = [_mm_nt(grp_in[x[5]][1], hst) for x, hst in zip(ch, hs)]
    upd = [_dot_tn(x[1], grp_in[x[5]][0]) for x in ch]
    for (bi, h), x, hst, up in zip(ids, ch, hs, upd):
        h_new = hst * x[4] + up
        h_sc[bi, h] = h_new
        h_ref[bi, h] = h_new
    for bi, (xs, ea, zg) in enumerate(slot):
        lo_c, hi_c = bi * SSM_HEADS, (bi + 1) * SSM_HEADS
        yy = (jnp.concatenate(y_in[lo_c:hi_c], axis=1) + jnp.concatenate(y_st[lo_c:hi_c], axis=1) * ea
              + dsk * xs) * zg
        for grp in range(SSM_GROUPS):
            yg = yy[:, grp * gw:(grp + 1) * gw]
            yg = yg * lax.rsqrt(jnp.mean(yg * yg, axis=-1, keepdims=True) + EPS)
            o_ref[bi, :, grp * gw:(grp + 1) * gw] = yg * ng[:, grp * gw:(grp + 1) * gw]


def _ssd(xbc, z, kis, conv_w, conv_b, conv_prev, h0, a_log, dt_bias, d_skip, norm_g, nbb):
    B, T, _ = xbc.shape
    c = math.gcd(T, CHUNK)
    return pl.pallas_call(
        functools.partial(_ssd_kernel, c=c, nbb=nbb),
        grid=(B // nbb, T // c),
        in_specs=[pl.BlockSpec((nbb, c, SSM_CONV_C), lambda b, j: (b, j, 0)),
                  pl.BlockSpec((nbb, c, SSM_W), lambda b, j: (b, j, 0)),
                  pl.BlockSpec((nbb, c, LANES), lambda b, j: (b, j, 0)),
                  pl.BlockSpec((CONV_W, SSM_CONV_C), lambda b, j: (0, 0)),
                  pl.BlockSpec((1, SSM_CONV_C), lambda b, j: (0, 0)),
                  pl.BlockSpec((nbb, CONV_W - 1, SSM_CONV_C), lambda b, j: (b, 0, 0)),
                  pl.BlockSpec((nbb, SSM_HEADS, SSM_P, SSM_N), lambda b, j: (b, 0, 0, 0)),
                  pl.BlockSpec((1, LANES), lambda b, j: (0, 0)),
                  pl.BlockSpec((1, LANES), lambda b, j: (0, 0)),
                  pl.BlockSpec((1, SSM_W), lambda b, j: (0, 0)),
                  pl.BlockSpec((1, SSM_W), lambda b, j: (0, 0))],
        out_specs=(pl.BlockSpec((nbb, c, SSM_W), lambda b, j: (b, j, 0)),
                   pl.BlockSpec((nbb, CONV_W - 1, SSM_CONV_C), lambda b, j: (b, 0, 0)),
                   pl.BlockSpec((nbb, SSM_HEADS, SSM_P, SSM_N), lambda b, j: (b, 0, 0, 0))),
        out_shape=(jax.ShapeDtypeStruct((B, T, SSM_W), F32),
                   jax.ShapeDtypeStruct((B, CONV_W - 1, SSM_CONV_C), F32),
                   jax.ShapeDtypeStruct((B, SSM_HEADS, SSM_P, SSM_N), F32)),
        scratch_shapes=[pltpu.VMEM((nbb, SUBLANES + c, SSM_CONV_C), F32),
                        pltpu.VMEM((nbb, SSM_HEADS, SSM_P, SSM_N), F32)],
        compiler_params=_params(("arbitrary", "arbitrary")),
        name="ssd",
    )(xbc, z, kis, conv_w, conv_b.reshape(1, -1), conv_prev, h0, _lane_vector(a_log, KIS_DT),
      _lane_vector(dt_bias, KIS_DT), jnp.repeat(d_skip, SSM_P).reshape(1, -1), norm_g.reshape(1, -1))


IDX_BITS = 12


OPENING_PROBES = 14
ROUND_PROBES = 2
SEARCH_ROUNDS = 4096
FOLD_ROWS = 128


def _fold_keys(x, op, axis=0):
    s = x.shape[0]
    if axis == 0 and s % FOLD_ROWS == 0 and s > FOLD_ROWS:
        x = op(x.reshape(s // FOLD_ROWS, FOLD_ROWS, x.shape[1]), axis=0)
    return op(x, axis=axis, keepdims=True)


def _fold_rows(x, op):
    return _fold_keys(x, op, 0)


def _select_topk(score, adm, pos, skip, topk, axis):
    def count(mask):
        return _fold_keys(mask.astype(F32), jnp.sum, axis)

    s = jnp.where(adm, score, -jnp.inf)

    few = (count(adm) <= topk) | skip
    smax = _fold_keys(s, jnp.max, axis)
    smin = _fold_keys(jnp.where(adm, score, jnp.inf), jnp.min, axis)
    top_full = count(s >= smax) >= topk
    lo0 = jnp.where(few, 0.0, jnp.where(top_full, smax, smin))
    hi0 = jnp.where(few, 0.0, smax)
    done0 = (few | top_full).astype(F32)

    def probe(lo, hi, done):
        mid = 0.5 * lo + 0.5 * hi
        ok = (mid > lo) & (mid < hi) & (done < 0.5)
        take = count(s >= mid) >= topk
        return jnp.where(ok & take, mid, lo), jnp.where(ok & jnp.logical_not(take), mid, hi)

    def opening(lo, hi):
        for _ in range(OPENING_PROBES):
            lo, hi = probe(lo, hi, done0)
        return lo, hi

    lo1, hi1 = lax.cond(jnp.min(done0) < 0.5, opening, lambda lo, hi: (lo, hi), lo0, hi0)

    def search_cond(c):
        return (jnp.min(c[2]) < 0.5) & (c[3] < SEARCH_ROUNDS)

    def search_body(c):
        lo, hi, done, it = c
        last = _fold_keys(jnp.where(s < hi, s, -jnp.inf), jnp.max, axis)
        reached = count(s >= last) >= topk
        active = done < 0.5
        lo = jnp.where(active & reached, last, lo)
        hi = jnp.where(active & jnp.logical_not(reached), last, hi)
        done = jnp.maximum(done, reached.astype(F32))
        for _ in range(ROUND_PROBES):
            lo, hi = probe(lo, hi, done)
        return lo, hi, done, it + 1

    lo, _, _, _ = lax.while_loop(search_cond, search_body, (lo1, hi1, done0, jnp.int32(0)))
    t = jnp.where(few, -jnp.inf, lo)
    gt = s > t
    tie = s == t
    need = topk - count(gt)
    excess = jnp.where(few, 0.0, count(tie) - need)

    def tie_search():
        def ibody(i, m):
            cand = m | (jnp.int32(1) << (IDX_BITS - 1 - i))
            return jnp.where(count(tie & (pos < cand)) <= need, cand, m)

        return lax.fori_loop(0, IDX_BITS, ibody, jnp.zeros(few.shape, jnp.int32))

    m = lax.cond(jnp.max(excess) > 0.0, tie_search, lambda: jnp.full(few.shape, 2 ** IDX_BITS - 1, jnp.int32))
    return adm & (gt | (tie & (pos < m)))


def _dsa_kernel(q_ref, qi_ref, kis_ref, k_ref, vt_ref, ki_ref, *refs, tq, s_len, s_valid, q_offset, j0,
                n_valid_q, topk):
    o_ref = refs[-1]
    j = pl.program_id(1)
    lane = lax.broadcasted_iota(jnp.int32, (1, tq), 1)
    qpos = q_offset + (j0 + j) * tq + lane
    lim = jnp.minimum((qpos // CHUNK + 1) * CHUNK, s_valid)
    row = lax.broadcasted_iota(jnp.int32, (s_len, tq), 0)
    adm = row < lim

    qi_t = qi_ref[0].T.astype(BF16)
    kis_t = kis_ref[0].T
    ki = ki_ref[0]
    assert IDX_DH == 64 and IDX_HEADS == 4 and ATT_DH == 64
    wi = kis_t[KIS_WI:KIS_WI + IDX_HEADS, :] * (IDX_DH ** -0.5 * IDX_HEADS ** -0.5)
    qi_all = jnp.concatenate([qi_t[h * IDX_DH:(h + 1) * IDX_DH, :] for h in range(IDX_HEADS)], axis=1)
    rel_all = _dot(ki, qi_all)
    score = jnp.zeros((s_len, tq), F32)
    for h in range(IDX_HEADS):
        score = score + jnp.maximum(rel_all[:, h * tq:(h + 1) * tq], 0.0) * wi[h:h + 1, :]
    sel = _select_topk(score, adm, row, lane >= n_valid_q, topk, 0)

    q_t = q_ref[0].T * (ATT_DH ** -0.5)
    pair = LANES // ATT_DH
    head_in_pair = lax.broadcasted_iota(jnp.int32, (LANES, tq), 0) // ATT_DH
    logits = []
    for g in range(ATT_HEADS // pair):
        qg = q_t[g * LANES:(g + 1) * LANES, :]
        qm = jnp.concatenate([jnp.where(head_in_pair == i, qg, 0.0) for i in range(pair)], axis=1)
        lg = _dot(k_ref[0, :, g * LANES:(g + 1) * LANES], qm.astype(BF16))
        logits += [lg[:, i * tq:(i + 1) * tq] for i in range(pair)]
    probs, scales = [], []
    for lg in logits:
        lg = jnp.where(sel, lg, -jnp.inf)
        p = jnp.exp(lg - _fold_rows(lg, jnp.max))
        scales.append(1.0 / _fold_rows(p, jnp.sum))
        probs.append(p.astype(BF16))
    outs = [_dot(vt_ref[0, h * ATT_DH:(h + 1) * ATT_DH, :], probs[h]) * scales[h] for h in range(ATT_HEADS)]
    o_ref[0] = jnp.concatenate(outs, axis=0).T


def _dsa(q, qi, kis, k_b, vt_b, ki_b, out_prev, *, tq, j0, nj, s_len, s_valid, q_offset, n_valid_q, topk):
    B, t_all, _ = q.shape
    assert s_len < 2 ** IDX_BITS and s_len <= k_b.shape[1]
    alias_in = () if out_prev is None else (out_prev,)
    return pl.pallas_call(
        functools.partial(_dsa_kernel, tq=tq, s_len=s_len, s_valid=s_valid, q_offset=q_offset, j0=j0,
                          n_valid_q=n_valid_q, topk=topk),
        grid=(B, nj),
        in_specs=[pl.BlockSpec((1, tq, ATT_W), lambda b, j: (b, j0 + j, 0)),
                  pl.BlockSpec((1, tq, IDX_HEADS * IDX_DH), lambda b, j: (b, j0 + j, 0)),
                  pl.BlockSpec((1, tq, LANES), lambda b, j: (b, j0 + j, 0)),
                  pl.BlockSpec((1, s_len, ATT_W), lambda b, j: (b, 0, 0)),
                  pl.BlockSpec((1, ATT_W, s_len), lambda b, j: (b, 0, 0)),
                  pl.BlockSpec((1, s_len, IDX_DH), lambda b, j: (b, 0, 0))]
        + [pl.BlockSpec(memory_space=pl.ANY)] * len(alias_in),
        out_specs=pl.BlockSpec((1, tq, ATT_W), lambda b, j: (b, j0 + j, 0)),
        out_shape=jax.ShapeDtypeStruct((B, t_all, ATT_W), F32),
        input_output_aliases={6: 0} if alias_in else {},
        compiler_params=_params(("arbitrary", "arbitrary")),
        name="dsa",
    )(q, qi, kis, k_b, vt_b, ki_b, *alias_in)


def _dsa_rows_kernel(q_ref, qi_ref, kis_ref, k_ref, v_ref, ck_ref, cv_ref, cki_ref, o_ref, *, nbq, tq, s_valid,
                     q_offset, topk):
    rows = nbq * tq
    s_len = q_offset + LANES
    qpos = q_offset + lax.broadcasted_iota(jnp.int32, (rows, 1), 0) % tq
    lim = jnp.minimum((qpos // CHUNK + 1) * CHUNK, s_valid)
    col = lax.broadcasted_iota(jnp.int32, (rows, s_len), 1)
    adm = col < lim

    def with_new(cache, new):
        return jnp.concatenate([cache, new.astype(BF16),
                                jnp.zeros((LANES - tq, new.shape[1]), BF16)], axis=0)

    scores = []
    for b in range(nbq):
        qi = qi_ref[b]
        qi_st = jnp.concatenate([qi[:, h * IDX_DH:(h + 1) * IDX_DH] for h in range(IDX_HEADS)], axis=0)
        rel = _dot_nt(qi_st.astype(BF16), with_new(cki_ref[b], kis_ref[b][:, :IDX_DH]))
        wi = kis_ref[b][:, KIS_WI:KIS_WI + IDX_HEADS] * (IDX_DH ** -0.5 * IDX_HEADS ** -0.5)
        sc = jnp.maximum(rel[:tq, :], 0.0) * wi[:, 0:1]
        for h in range(1, IDX_HEADS):
            sc = sc + jnp.maximum(rel[h * tq:(h + 1) * tq, :], 0.0) * wi[:, h:h + 1]
        scores.append(sc)
    sel = _select_topk(jnp.concatenate(scores, axis=0), adm, col, qpos < 0, topk, 1)

    head_of_lane = lax.broadcasted_iota(jnp.int32, (tq, ATT_W), 1) // ATT_DH
    for b in range(nbq):
        q = q_ref[b] * (ATT_DH ** -0.5)
        q_st = jnp.concatenate([jnp.where(head_of_lane == h, q, 0.0) for h in range(ATT_HEADS)], axis=0)
        logits = _dot_nt(q_st.astype(BF16), with_new(ck_ref[b], k_ref[b]))
        logits = jnp.where(jnp.concatenate([sel[b * tq:(b + 1) * tq, :]] * ATT_HEADS, axis=0), logits, -jnp.inf)
        p = jnp.exp(logits - jnp.max(logits, axis=1, keepdims=True))
        scale = 1.0 / jnp.sum(p, axis=1, keepdims=True)
        pv = _dot(p.astype(BF16), with_new(cv_ref[b], v_ref[b])) * scale
        out = jnp.zeros((tq, ATT_W), F32)
        for h in range(ATT_HEADS):
            out = jnp.where(head_of_lane == h, pv[h * tq:(h + 1) * tq, :], out)
        o_ref[b] = out


def _dsa_rows(q, qi, kis, k, v, ck, cv, cki, *, nbq, topk):
    B, T, _ = q.shape
    P = ck.shape[1]
    assert P + LANES < 2 ** IDX_BITS and T % (2 * SUBLANES) == 0 and T <= LANES and P % (2 * SUBLANES) == 0
    assert B % nbq == 0
    new_spec = lambda w: pl.BlockSpec((nbq, T, w), lambda b: (b, 0, 0))
    cache_spec = lambda w: pl.BlockSpec((nbq, P, w), lambda b: (b, 0, 0))
    return pl.pallas_call(
        functools.partial(_dsa_rows_kernel, nbq=nbq, tq=T, s_valid=P + T, q_offset=P, topk=topk),
        grid=(B // nbq,),
        in_specs=[new_spec(ATT_W), new_spec(IDX_HEADS * IDX_DH), new_spec(LANES), new_spec(ATT_W), new_spec(ATT_W),
                  cache_spec(ATT_W), cache_spec(ATT_W), cache_spec(IDX_DH)],
        out_specs=new_spec(ATT_W),
        out_shape=jax.ShapeDtypeStruct((B, T, ATT_W), F32),
        compiler_params=_params(("arbitrary",)),
        name="dsa_rows",
    )(q, qi, kis, k, v, ck, cv, cki)


def _dsa_group(q, qi, kis, k_b, vt_b, ki_b, *, s_valid, q_offset, topk):
    B, T, _ = q.shape
    tq = 2 * LANES if T % (2 * LANES) == 0 else LANES
    n_valid_q = min(T, tq)
    if T < tq:
        padq = lambda a: jnp.pad(a, ((0, 0), (0, tq - T), (0, 0)))
        q, qi, kis = padq(q), padq(qi), padq(kis)
    nq = q.shape[1] // tq
    s_total = k_b.shape[1]
    out = jnp.zeros((B, q.shape[1], ATT_W), F32)
    for j0 in range(nq):
        reach = -(-(q_offset + (j0 + 1) * tq) // CHUNK) * CHUNK
        s_len = min(s_total, -(-min(reach, s_valid) // LANES) * LANES)
        out = _dsa(q, qi, kis, k_b, vt_b, ki_b, out, tq=tq, j0=j0, nj=1, s_len=s_len, s_valid=s_valid,
                   q_offset=q_offset, n_valid_q=n_valid_q, topk=topk)
    return out[:, :T]


def _ffn_kernel(x_ref, gdn_ref, att_ref, ssm_ref, mod_ref, wo_ref, g2_ref, wg_ref, wu_ref, cw_ref, cb_ref,
                wd_ref, prev_ref, fg_ref, *rest, nb, tt, nk, final):
    if final:
        xo_ref, y_ref, new_ref, h2_sc, acc_sc, buf_sc, carry_sc = rest
    else:
        xo_ref, new_ref, h2_sc, acc_sc, buf_sc, carry_sc = rest
        y_ref = None
    t = pl.program_id(1)
    k = pl.program_id(2)
    rows = nb * tt
    lo = SUBLANES - (FFN_CONV_W - 1)
    mod = mod_ref[...]

    @pl.when(k == 0)
    def _():
        mix = jnp.concatenate([gdn_ref[...], att_ref[...], ssm_ref[...]], axis=-1)
        proj = _dot(mix.reshape(rows, MIX_W).astype(BF16), wo_ref[...]).reshape(nb, tt, D_MODEL)
        x1 = x_ref[...] + mod[:, 2:3, :] * proj
        acc_sc[...] = x1
        ms = jnp.mean(x1 * x1, axis=-1, keepdims=True)
        h2 = x1 * lax.rsqrt(ms + EPS) * g2_ref[...]
        h2 = h2 * (1.0 + mod[:, 4:5, :]) + mod[:, 3:4, :]
        h2_sc[...] = h2.reshape(rows, D_MODEL).astype(BF16)

    @pl.when(t == 0)
    def _():
        carry_sc[k, :, lo:SUBLANES, :] = prev_ref[...]

    h2 = h2_sc[...]
    ag = _dot(h2, wg_ref[...])
    up = _dot(h2, wu_ref[...])
    tf = ag.shape[-1]
    buf_sc[:, lo:SUBLANES, :] = carry_sc[k, :, lo:SUBLANES, :]
    buf_sc[:, SUBLANES:SUBLANES + tt, :] = ag.reshape(nb, tt, tf)
    cw = cw_ref[...]
    conv = buf_sc[:, lo:lo + tt, :] * cw[0:1, :][None]
    for jj in range(1, FFN_CONV_W):
        conv = conv + buf_sc[:, lo + jj:lo + jj + tt, :] * cw[jj:jj + 1, :][None]
    tail = buf_sc[:, tt + lo:tt + SUBLANES, :]
    carry_sc[k, :, lo:SUBLANES, :] = tail
    new_ref[:, pl.ds(k, 1), :, :] = tail[:, None]
    act = _silu(conv + cb_ref[...][None]).reshape(rows, tf) * up
    y = _dot(act.astype(BF16), wd_ref[...]).reshape(nb, tt, D_MODEL)
    acc_sc[...] += mod[:, 5:6, :] * y

    @pl.when(k == nk - 1)
    def _():
        xo = acc_sc[...]
        xo_ref[...] = xo
        if final:
            ms = jnp.mean(xo * xo, axis=-1, keepdims=True)
            y_ref[...] = xo * lax.rsqrt(ms + EPS) * fg_ref[...]


def _ffn(x, gdn_o, att_o, ssm_o, mod, w_out, g2, w_gate, w_up, conv_w, conv_b, w_down, conv_prev, final_g,
         nb, tt, tf, final):
    B, T, _ = x.shape
    nk = D_FF // tf
    grid = (B // nb, T // tt, nk)
    weights_mode = pl.Buffered(1) if nk == 1 else None
    row_spec = lambda w: pl.BlockSpec((nb, tt, w), lambda b, t, k: (b, t, 0))
    in_specs = [row_spec(D_MODEL), row_spec(GDN_W), row_spec(ATT_W), row_spec(SSM_W),
                pl.BlockSpec((nb, 6, D_MODEL), lambda b, t, k: (b, 0, 0)),
                pl.BlockSpec((MIX_W, D_MODEL), lambda b, t, k: (0, 0), pipeline_mode=weights_mode),
                pl.BlockSpec((1, D_MODEL), lambda b, t, k: (0, 0)),
                pl.BlockSpec((D_MODEL, tf), lambda b, t, k: (0, k), pipeline_mode=weights_mode),
                pl.BlockSpec((D_MODEL, tf), lambda b, t, k: (0, k), pipeline_mode=weights_mode),
                pl.BlockSpec((FFN_CONV_W, tf), lambda b, t, k: (0, k)),
                pl.BlockSpec((1, tf), lambda b, t, k: (0, k)),
                pl.BlockSpec((tf, D_MODEL), lambda b, t, k: (k, 0), pipeline_mode=weights_mode),
                pl.BlockSpec((nb, FFN_CONV_W - 1, tf), lambda b, t, k: (b, 0, k)),
                pl.BlockSpec((1, D_MODEL), lambda b, t, k: (0, 0))]
    out_shape = [jax.ShapeDtypeStruct((B, T, D_MODEL), F32)]
    out_specs = [row_spec(D_MODEL)]
    if final:
        out_shape.append(jax.ShapeDtypeStruct((B, T, D_MODEL), F32))
        out_specs.append(row_spec(D_MODEL))
    out_shape.append(jax.ShapeDtypeStruct((B, nk, FFN_CONV_W - 1, tf), F32))
    out_specs.append(pl.BlockSpec((nb, nk, FFN_CONV_W - 1, tf), lambda b, t, k: (b, 0, 0, 0)))
    res = pl.pallas_call(
        functools.partial(_ffn_kernel, nb=nb, tt=tt, nk=nk, final=final),
        grid=grid,
        in_specs=in_specs,
        out_specs=tuple(out_specs),
        out_shape=tuple(out_shape),
        scratch_shapes=[pltpu.VMEM((nb * tt, D_MODEL), BF16),
                        pltpu.VMEM((nb, tt, D_MODEL), F32),
                        pltpu.VMEM((nb, SUBLANES + tt, tf), F32),
                        pltpu.VMEM((nk, nb, SUBLANES, tf), F32)],
        compiler_params=_params(("arbitrary", "arbitrary", "arbitrary")),
        name="ffn",
    )(x, gdn_o, att_o, ssm_o, mod, w_out, g2, w_gate, w_up, conv_w, conv_b.reshape(1, -1), w_down,
      conv_prev, final_g)
    tail = res[-1].transpose(0, 2, 1, 3).reshape(B, FFN_CONV_W - 1, D_FF)
    return (*res[:-1], tail)


def _rope_tables(pos):
    half = ATT_DH // 2
    inv_freq = ROPE_THETA ** (-jnp.arange(half, dtype=F32) / half)
    ang = pos.astype(F32)[:, None] * inv_freq[None, :]
    cos = jnp.cos(ang)
    sin = jnp.sin(ang)
    cos_h = jnp.concatenate([cos, cos], axis=-1)
    sin_h = jnp.concatenate([-sin, sin], axis=-1)
    ones = jnp.ones((pos.shape[0], LANES - ATT_DH), F32)
    cos_t = jnp.concatenate([jnp.tile(cos_h, (1, ATT_HEADS)), cos_h, ones], axis=-1)
    sin_t = jnp.concatenate([jnp.tile(sin_h, (1, ATT_HEADS)), sin_h, 0.0 * ones], axis=-1)
    return cos_t, sin_t


def _layer(x, mod, pos_tables, cache, states, lw, tiles, layer, stacks, final_g):
    final = layer == DEPTH - 1
    (norm1_g, w_perm, gdn_conv_w, gdn_A_log, gdn_dt_bias, gdn_norm_g, ssm_conv_w, ssm_conv_b, ssm_A_log,
     ssm_dt_bias, ssm_D, ssm_norm_g, w_out, norm2_g, w_gate, w_up, ffn_conv_w, ffn_conv_b, w_down) = lw
    gdn_conv_prev, gdn_s0, ssm_conv_prev, ssm_h0, ffn_conv_prev = states
    B, T, _ = x.shape
    nb, tt, nbb, ffn_tt, tf = tiles
    cos_t, sin_t = pos_tables
    if stacks is None:
        stacks = tuple(jnp.zeros((DEPTH, B, T, w), F32) for w in (ATT_W, ATT_W, IDX_DH))
    proj = _inproj(x, mod, norm1_g.reshape(1, -1), w_perm, cos_t, sin_t, nb, tt, cache is None, layer, stacks)
    gqkv, ggate, q, k_stack, v_stack, qi, kis, z, xbc = proj[:len(OUT_SEGS)]
    stacks = (k_stack, v_stack, proj[len(OUT_SEGS)])

    gdn_o, gdn_conv_new, gdn_s = _gdn(gqkv, kis, ggate, gdn_conv_w, gdn_conv_prev, gdn_s0, gdn_A_log,
                                      gdn_dt_bias, gdn_norm_g, nbb)
    ssm_o, ssm_conv_new, ssm_h = _ssd(xbc, z, kis, ssm_conv_w, ssm_conv_b, ssm_conv_prev, ssm_h0, ssm_A_log,
                                      ssm_dt_bias, ssm_D, ssm_norm_g, nbb)
    if cache is None:
        k_b, ki_b, vt_b = proj[len(OUT_SEGS) + 1:]
        att_o = _dsa_group(q, qi, kis, k_b, vt_b, ki_b, s_valid=T, q_offset=0, topk=min(TOPK_MAX, T // 4))
    else:
        ck, cv, cki = cache
        P = ck.shape[1]
        att_o = _dsa_rows(q, qi, kis, k_stack[layer], v_stack[layer], ck.reshape(B, P, ATT_W).astype(BF16),
                          cv.reshape(B, P, ATT_W).astype(BF16), cki.astype(BF16), nbq=nbb,
                          topk=min(TOPK_MAX, (P + T) // 4))

    res = _ffn(x, gdn_o, att_o, ssm_o, mod, w_out, norm2_g.reshape(1, -1), w_gate, w_up, ffn_conv_w, ffn_conv_b,
               w_down, ffn_conv_prev, final_g.reshape(1, -1), ffn_tt[0], ffn_tt[1], tf, final)
    if final:
        x_new, y, ffn_conv_new = res
    else:
        (x_new, ffn_conv_new), y = res, None
    return x_new, y, (gdn_conv_new, gdn_s, ssm_conv_new, ssm_h, ffn_conv_new), stacks


def kernel(x_prompt, x_sample, c_prompt, c_sample, cache_k, cache_v, cache_kidx, state_gdn_conv, state_gdn,
           state_ssm_conv, state_ssm, state_ffn_conv, w_ada, b_ada, norm1_g, w_in, gdn_conv_w, gdn_A_log,
           gdn_dt_bias, gdn_norm_g, ssm_conv_w, ssm_conv_b, ssm_A_log, ssm_dt_bias, ssm_D, ssm_norm_g, w_out,
           norm2_g, w_gate, w_up, ffn_conv_w, ffn_conv_b, w_down, final_g):
    Bp, T, _ = x_prompt.shape
    Bs, Ts, _ = x_sample.shape
    P = cache_k.shape[2]

    c_all = jnp.concatenate([c_prompt, c_sample], axis=0)
    mod_all = _ada(c_all, w_ada, b_ada).reshape(DEPTH, Bp + Bs, 6, D_MODEL)

    tables_p = _rope_tables(jnp.arange(T))
    tables_s = _rope_tables(P + jnp.arange(Ts))

    w_perm = _permute_columns(w_in)
    w_out_b = w_out.astype(BF16)
    w_gate_b = w_gate.astype(BF16)
    w_up_b = w_up.astype(BF16)
    w_down_b = w_down.astype(BF16)

    zeros_p = (jnp.zeros((Bp, CONV_W - 1, GDN_CONV_C), F32),
               jnp.zeros((Bp, GDN_HEADS, GDN_DK, GDN_DV), F32),
               jnp.zeros((Bp, CONV_W - 1, SSM_CONV_C), F32),
               jnp.zeros((Bp, SSM_HEADS, SSM_P, SSM_N), F32),
               jnp.zeros((Bp, FFN_CONV_W - 1, D_FF), F32))

    tf = D_FF
    tiles_p = (1, min(256, T), math.gcd(Bp, 8), (1, min(256, T)), tf)
    tiles_s = (Bs, Ts, math.gcd(Bs, 4), (Bs, Ts), tf)

    xp, xs = x_prompt, x_sample
    new_p, new_s = [], []
    yp = ys = stacks_p = stacks_s = None
    for l in range(DEPTH):
        lw = (norm1_g[l], w_perm[l], gdn_conv_w[l], gdn_A_log[l], gdn_dt_bias[l], gdn_norm_g[l], ssm_conv_w[l],
              ssm_conv_b[l], ssm_A_log[l], ssm_dt_bias[l], ssm_D[l], ssm_norm_g[l], w_out_b[l], norm2_g[l],
              w_gate_b[l], w_up_b[l], ffn_conv_w[l], ffn_conv_b[l], w_down_b[l])
        xp, yp, st_p, stacks_p = _layer(xp, mod_all[l, :Bp], tables_p, None, zeros_p, lw, tiles_p, l, stacks_p,
                                        final_g)
        states_s = (state_gdn_conv[l], state_gdn[l], state_ssm_conv[l], state_ssm[l], state_ffn_conv[l])
        xs, ys, st_s, stacks_s = _layer(xs, mod_all[l, Bp:], tables_s, (cache_k[l], cache_v[l], cache_kidx[l]),
                                        states_s, lw, tiles_s, l, stacks_s, final_g)
        new_p.append(st_p)
        new_s.append(st_s)

    def outputs(stacks, per_layer):
        k_stack, v_stack, ki_stack = stacks
        heads = k_stack.shape[:3] + (ATT_HEADS, ATT_DH)
        return [k_stack.reshape(heads), v_stack.reshape(heads), ki_stack] + [
            jnp.stack([st[i] for st in per_layer]) for i in range(len(per_layer[0]))]

    return (yp, ys, *outputs(stacks_p, new_p), *outputs(stacks_s, new_s))
```
